```python
import math
import jax, jax.numpy as jnp
from jax import lax
import numpy as np

D_MODEL = 1024
BATCH = 4
SEQ = 4096
DEPTH = 4
DEC_BATCH = 16
DEC_SEQ = 32
PAST_LEN = 4096

CHUNK = 64
Q_BLOCK = 128
N_MIXERS = 3
EPS = 1e-6
DIFF_HEADS = 8
DIFF_HEAD_DIM = 64
DIFF_QK_WIDTH = DIFF_HEADS * 2 * DIFF_HEAD_DIM
DIFF_SCALE = DIFF_HEAD_DIM ** -0.5
CONV_WIDTH = 31
FOX_HEADS = 16
FOX_HEAD_DIM = 64
FOX_WIDTH = FOX_HEADS * FOX_HEAD_DIM
FOX_SCALE = FOX_HEAD_DIM ** -0.5
FORGET_BIAS_INIT = 3.0
REL_BUCKETS = 32
REL_MAX_DISTANCE = 128
D_FF = 4 * D_MODEL

N_DIFF_LAYERS = len(range(0, DEPTH, N_MIXERS))
N_CONV_LAYERS = len(range(1, DEPTH, N_MIXERS))
N_FOX_LAYERS = len(range(2, DEPTH, N_MIXERS))

kernel_name = "hybrid_diffattn_conformer_fox_stream_step"


def _rmsnorm(x, g):
    xf = x.astype(jnp.float32)
    y = xf * lax.rsqrt(jnp.mean(xf * xf, axis=-1, keepdims=True) + EPS)
    return (y * g.astype(jnp.float32)).astype(x.dtype)


def _layernorm(x, g, b):
    xf = x.astype(jnp.float32)
    xc = xf - jnp.mean(xf, axis=-1, keepdims=True)
    y = xc * lax.rsqrt(jnp.mean(xc * xc, axis=-1, keepdims=True) + EPS)
    return (y * g.astype(jnp.float32) + b.astype(jnp.float32)).astype(x.dtype)


def _sq_relu_mlp(h, w1, w2):
    return jnp.square(jax.nn.relu(h @ w1)) @ w2


def _t5_bucket(rel):
    half = REL_BUCKETS // 2
    n = -rel
    offset = jnp.where(n < 0, half, 0)
    n = jnp.abs(n)
    max_exact = half // 2
    large = max_exact + (jnp.log(jnp.maximum(n, 1).astype(jnp.float32) / max_exact)
                         / math.log(REL_MAX_DISTANCE / max_exact)
                         * (half - max_exact)).astype(jnp.int32)
    large = jnp.minimum(large, half - 1)
    return offset + jnp.where(n < max_exact, n, large)


def _rel_bias(table, q_pos, k_pos):
    b = table.astype(jnp.float32)[_t5_bucket(k_pos[None, :] - q_pos[:, None])]
    return jnp.transpose(b, (2, 0, 1))


def _diff_qkv(h, w_in):
    B, T, _ = h.shape
    u = h @ w_in
    q = u[..., :DIFF_QK_WIDTH].reshape(B, T, DIFF_HEADS, 2, DIFF_HEAD_DIM)
    k = u[..., DIFF_QK_WIDTH:2 * DIFF_QK_WIDTH].reshape(B, T, DIFF_HEADS, 2, DIFF_HEAD_DIM)
    v = u[..., 2 * DIFF_QK_WIDTH:].reshape(B, T, DIFF_HEADS, 2 * DIFF_HEAD_DIM)
    return q, k, v


def _diff_attend(q, k, v, q_pos, k_pos, lam, rel_table):
    s = jnp.einsum('bqhmd,bkhmd->bhmqk', q, k, preferred_element_type=jnp.float32) * DIFF_SCALE
    s = s + _rel_bias(rel_table, q_pos, k_pos)[None, :, None]
    chunk_mask = (k_pos[None, :] // CHUNK) <= (q_pos[:, None] // CHUNK)
    p = jax.nn.softmax(jnp.where(chunk_mask, s, -jnp.inf), axis=-1)
    a = p[:, :, 0] - lam * p[:, :, 1]
    return jnp.einsum('bhqk,bkhe->bqhe', a.astype(v.dtype), v)


def _diff_out(o, subln_g, lam_init, w_out):
    B, T = o.shape[:2]
    o = _rmsnorm(o, subln_g) * (1.0 - lam_init)
    return o.reshape(B, T, DIFF_QK_WIDTH) @ w_out


def _fox_qkv(h, w_in, b_f):
    B, T, _ = h.shape
    u = h @ w_in
    q = u[..., :FOX_WIDTH].reshape(B, T, FOX_HEADS, FOX_HEAD_DIM)
    k = u[..., FOX_WIDTH:2 * FOX_WIDTH].reshape(B, T, FOX_HEADS, FOX_HEAD_DIM)
    v = u[..., 2 * FOX_WIDTH:3 * FOX_WIDTH].reshape(B, T, FOX_HEADS, FOX_HEAD_DIM)
    logf = jax.nn.log_sigmoid((u[..., 3 * FOX_WIDTH:] + b_f).astype(jnp.float32))
    return q, k, v, logf


def _fox_attend(q, k, v, cq, ck, q_pos, k_pos):
    s = jnp.einsum('bqhd,bkhd->bhqk', q, k, preferred_element_type=jnp.float32) * FOX_SCALE
    s = s + jnp.transpose(cq, (0, 2, 1))[:, :, :, None] - jnp.transpose(ck, (0, 2, 1))[:, :, None, :]
    p = jax.nn.softmax(jnp.where(k_pos[None, :] <= q_pos[:, None], s, -jnp.inf), axis=-1)
    return jnp.einsum('bhqk,bkhd->bqhd', p.astype(v.dtype), v)


def _conv_glu(h, w_pw1, b_pw1):
    a, g = jnp.split(h @ w_pw1 + b_pw1, 2, axis=-1)
    return a * jax.nn.sigmoid(g)


def _conv_tail(u_padded, w_dw, b_dw, ln_g, ln_b, w_pw2, b_pw2):
    y = lax.conv_general_dilated(u_padded, w_dw[:, None, :], window_strides=(1,), padding='VALID',
                                 dimension_numbers=('NWC', 'WIO', 'NWC'),
                                 feature_group_count=D_MODEL) + b_dw
    y = jax.nn.silu(_layernorm(y, ln_g, ln_b))
    return y @ w_pw2 + b_pw2


def setup_inputs(seed: int = 0) -> dict:
    key = jax.random.key(seed)
    ks = iter(jax.random.split(key, 40))

    def nrm(shape, scale=1.0):
        return scale * jax.random.normal(next(ks), shape, jnp.float32)

    D = D_MODEL
    return {
        "x_prompt": nrm((BATCH, SEQ, D)),
        "x_sample": nrm((DEC_BATCH, DEC_SEQ, D)),
        "cache_diff_k": nrm((N_DIFF_LAYERS, DEC_BATCH, PAST_LEN, DIFF_HEADS, 2, DIFF_HEAD_DIM)),
        "cache_diff_v": nrm((N_DIFF_LAYERS, DEC_BATCH, PAST_LEN, DIFF_HEADS, 2 * DIFF_HEAD_DIM)),
        "state_conv": nrm((N_CONV_LAYERS, DEC_BATCH, CONV_WIDTH - 1, D), 0.5),
        "cache_fox_k": nrm((N_FOX_LAYERS, DEC_BATCH, PAST_LEN, FOX_HEADS, FOX_HEAD_DIM)),
        "cache_fox_v": nrm((N_FOX_LAYERS, DEC_BATCH, PAST_LEN, FOX_HEADS, FOX_HEAD_DIM)),
        "cache_fox_logf": jax.nn.log_sigmoid(FORGET_BIAS_INIT + nrm((N_FOX_LAYERS, DEC_BATCH, PAST_LEN, FOX_HEADS))),
        "rel_bias": nrm((REL_BUCKETS, DIFF_HEADS), 0.5),
        "norm_g": 1.0 + nrm((DEPTH, 2, D), 0.02),
        "final_g": 1.0 + nrm((D,), 0.02),
        "diff_w_in": nrm((N_DIFF_LAYERS, D, 3 * DIFF_QK_WIDTH), D ** -0.5),
        "diff_w_out": nrm((N_DIFF_LAYERS, DIFF_QK_WIDTH, D), DIFF_QK_WIDTH ** -0.5),
        "diff_lq1": nrm((N_DIFF_LAYERS, DIFF_HEAD_DIM), 0.1),
        "diff_lk1": nrm((N_DIFF_LAYERS, DIFF_HEAD_DIM), 0.1),
        "diff_lq2": nrm((N_DIFF_LAYERS, DIFF_HEAD_DIM), 0.1),
        "diff_lk2": nrm((N_DIFF_LAYERS, DIFF_HEAD_DIM), 0.1),
        "diff_subln_g": 1.0 + nrm((N_DIFF_LAYERS, 2 * DIFF_HEAD_DIM), 0.02),
        "conv_w_pw1": nrm((N_CONV_LAYERS, D, 2 * D), D ** -0.5),
        "conv_b_pw1": nrm((N_CONV_LAYERS, 2 * D), 0.02),
        "conv_w_dw": nrm((N_CONV_LAYERS, CONV_WIDTH, D), CONV_WIDTH ** -0.5),
        "conv_b_dw": nrm((N_CONV_LAYERS, D), 0.02),
        "conv_ln_g": 1.0 + nrm((N_CONV_LAYERS, D), 0.02),
        "conv_ln_b": nrm((N_CONV_LAYERS, D), 0.02),
        "conv_w_pw2": nrm((N_CONV_LAYERS, D, D), D ** -0.5),
        "conv_b_pw2": nrm((N_CONV_LAYERS, D), 0.02),
        "fox_w_in": nrm((N_FOX_LAYERS, D, 3 * FOX_WIDTH + FOX_HEADS), D ** -0.5),
        "fox_b_f": FORGET_BIAS_INIT + nrm((N_FOX_LAYERS, FOX_HEADS), 0.1),
        "fox_w_out": nrm((N_FOX_LAYERS, FOX_WIDTH, D), FOX_WIDTH ** -0.5),
        "mlp_w1": nrm((DEPTH, D, D_FF), D ** -0.5),
        "mlp_w2": nrm((DEPTH, D_FF, D), D_FF ** -0.5),
    }


def reference(x_prompt, x_sample, cache_diff_k, cache_diff_v, state_conv, cache_fox_k, cache_fox_v,
              cache_fox_logf, rel_bias, norm_g, final_g, diff_w_in, diff_w_out, diff_lq1, diff_lk1,
              diff_lq2, diff_lk2, diff_subln_g, conv_w_pw1, conv_b_pw1, conv_w_dw, conv_b_dw,
              conv_ln_g, conv_ln_b, conv_w_pw2, conv_b_pw2, fox_w_in, fox_b_f, fox_w_out,
              mlp_w1, mlp_w2):
    B, S, _ = x_prompt.shape
    T = x_sample.shape[1]
    P = cache_diff_k.shape[2]
    pos_p = jnp.arange(S)
    q_pos_s = P + jnp.arange(T)
    k_pos_s = jnp.arange(P + T)
    q_starts = jnp.arange(0, S, Q_BLOCK)
    q_offsets = jnp.arange(Q_BLOCK)

    def sweep(block_fn):
        o = lax.map(block_fn, q_starts)
        return jnp.moveaxis(o, 0, 1).reshape((B, S) + o.shape[3:])

    xp, xs = x_prompt, x_sample
    dk_p, dv_p, dk_s, dv_s = [], [], [], []
    cv_p, cv_s = [], []
    fk_p, fv_p, fl_p, fk_s, fv_s, fl_s = [], [], [], [], [], []

    for i in range(DEPTH):
        kind, j = i % N_MIXERS, i // N_MIXERS
        hp = _rmsnorm(xp, norm_g[i, 0])
        hs = _rmsnorm(xs, norm_g[i, 0])
        if kind == 0:
            lam_init = 0.8 - 0.6 * math.exp(-0.3 * i)
            lam = (jnp.exp(jnp.sum(diff_lq1[j].astype(jnp.float32) * diff_lk1[j].astype(jnp.float32)))
                   - jnp.exp(jnp.sum(diff_lq2[j].astype(jnp.float32) * diff_lk2[j].astype(jnp.float32)))
                   + lam_init)
            qp, kp, vp = _diff_qkv(hp, diff_w_in[j])

            def diff_block(qs, qp=qp, kp=kp, vp=vp, lam=lam):
                qb = lax.dynamic_slice_in_dim(qp, qs, Q_BLOCK, axis=1)
                return _diff_attend(qb, kp, vp, qs + q_offsets, pos_p, lam, rel_bias)

            op = sweep(diff_block)
            qn, kn, vn = _diff_qkv(hs, diff_w_in[j])
            k_all = jnp.concatenate([cache_diff_k[j], kn], axis=1)
            v_all = jnp.concatenate([cache_diff_v[j], vn], axis=1)
            on = _diff_attend(qn, k_all, v_all, q_pos_s, k_pos_s, lam, rel_bias)
            mp = _diff_out(op, diff_subln_g[j], lam_init, diff_w_out[j])
            ms = _diff_out(on, diff_subln_g[j], lam_init, diff_w_out[j])
            dk_p.append(kp); dv_p.append(vp); dk_s.append(kn); dv_s.append(vn)
        elif kind == 1:
            up = _conv_glu(hp, conv_w_pw1[j], conv_b_pw1[j])
            un = _conv_glu(hs, conv_w_pw1[j], conv_b_pw1[j])
            up_pad = jnp.pad(up, ((0, 0), (CONV_WIDTH - 1, 0), (0, 0)))
            un_all = jnp.concatenate([state_conv[j], un], axis=1)
            mp = _conv_tail(up_pad, conv_w_dw[j], conv_b_dw[j], conv_ln_g[j], conv_ln_b[j],
                            conv_w_pw2[j], conv_b_pw2[j])
            ms = _conv_tail(un_all, conv_w_dw[j], conv_b_dw[j], conv_ln_g[j], conv_ln_b[j],
                            conv_w_pw2[j], conv_b_pw2[j])
            cv_p.append(up[:, S - (CONV_WIDTH - 1):]); cv_s.append(un_all[:, T:])
        else:
            qp, kp, vp, lfp = _fox_qkv(hp, fox_w_in[j], fox_b_f[j])
            cp = jnp.cumsum(lfp, axis=1)

            def fox_block(qs, qp=qp, kp=kp, vp=vp, cp=cp):
                qb = lax.dynamic_slice_in_dim(qp, qs, Q_BLOCK, axis=1)
                cb = lax.dynamic_slice_in_dim(cp, qs, Q_BLOCK, axis=1)
                return _fox_attend(qb, kp, vp, cb, cp, qs + q_offsets, pos_p)

            op = sweep(fox_block)
            qn, kn, vn, lfn = _fox_qkv(hs, fox_w_in[j], fox_b_f[j])
            c_all = jnp.cumsum(jnp.concatenate([cache_fox_logf[j].astype(jnp.float32), lfn], axis=1), axis=1)
            k_all = jnp.concatenate([cache_fox_k[j], kn], axis=1)
            v_all = jnp.concatenate([cache_fox_v[j], vn], axis=1)
            on = _fox_attend(qn, k_all, v_all, c_all[:, P:], c_all, q_pos_s, k_pos_s)
            mp = op.reshape(B, S, FOX_WIDTH) @ fox_w_out[j]
            ms = on.reshape(on.shape[0], T, FOX_WIDTH) @ fox_w_out[j]
            fk_p.append(kp); fv_p.append(vp); fl_p.append(lfp)
            fk_s.append(kn); fv_s.append(vn); fl_s.append(lfn)
        xp = xp + mp
        xs = xs + ms
        xp = xp + _sq_relu_mlp(_rmsnorm(xp, norm_g[i, 1]), mlp_w1[i], mlp_w2[i])
        xs = xs + _sq_relu_mlp(_rmsnorm(xs, norm_g[i, 1]), mlp_w1[i], mlp_w2[i])

    y_prompt = _rmsnorm(xp, final_g)
    y_sample = _rmsnorm(xs, final_g)
    new_diff_k_p = jnp.stack(dk_p)
    new_diff_v_p = jnp.stack(dv_p)
    new_conv_p = jnp.stack(cv_p)
    new_fox_k_p = jnp.stack(fk_p)
    new_fox_v_p = jnp.stack(fv_p)
    new_fox_logf_p = jnp.stack(fl_p)
    new_diff_k_s = jnp.stack(dk_s)
    new_diff_v_s = jnp.stack(dv_s)
    new_conv_s = jnp.stack(cv_s)
    new_fox_k_s = jnp.stack(fk_s)
    new_fox_v_s = jnp.stack(fv_s)
    new_fox_logf_s = jnp.stack(fl_s)
    return (y_prompt, y_sample, new_diff_k_p, new_diff_v_p, new_conv_p, new_fox_k_p, new_fox_v_p,
            new_fox_logf_p, new_diff_k_s, new_diff_v_s, new_conv_s, new_fox_k_s, new_fox_v_s,
            new_fox_logf_s)
```

```python
import functools
import math

import numpy as np
import jax
import jax.numpy as jnp
from jax import lax
from jax.experimental import pallas as pl
from jax.experimental.pallas import tpu as pltpu

F32 = jnp.float32
BF16 = jnp.bfloat16

D_MODEL = 1024
CHUNK = 64
EPS = 1e-6
DIFF_HEADS = 8
DIFF_HEAD_DIM = 64
DIFF_SCALE = DIFF_HEAD_DIM ** -0.5
CONV_WIDTH = 31
FOX_HEADS = 16
FOX_HEAD_DIM = 64
FOX_SCALE = FOX_HEAD_DIM ** -0.5
REL_BUCKETS = 32
N_MIXERS = 3

LANES = 128
NEG = -1e30
ROW_TILE = 512
ATT_TQ = 512
SAMPLE_TK = 1024
FF_CHUNK = 1024
CUM_BLOCK = 256
HALO = 32
CONV_RC = 32
CONV_LC = 256
VMEM_LIMIT = 56 * 1024 * 1024

_T5_STEPS = (12, 16, 23, 32, 46, 64, 91)


def _cparams(n_axes):
    return pltpu.CompilerParams(dimension_semantics=("arbitrary",) * n_axes, vmem_limit_bytes=VMEM_LIMIT)


def _rms(x, g):
    return x * lax.rsqrt(jnp.mean(x * x, axis=-1, keepdims=True) + EPS) * g


def _full_spec(shape):
    nd = len(shape)
    return pl.BlockSpec(shape, lambda *_: (0,) * nd)


def _diff_inproj_kernel(x_ref, g_ref, w_ref, q_ref, k_ref, v_ref, kb_ref, vb_ref):
    d = D_MODEL
    xn = _rms(x_ref[...], g_ref[...]).astype(BF16)
    q = jnp.dot(xn, w_ref[:, 0:d], preferred_element_type=F32)
    q_ref[...] = (q * DIFF_SCALE).astype(BF16)
    k = jnp.dot(xn, w_ref[:, d:2 * d], preferred_element_type=F32)
    k_ref[...] = k
    kb_ref[...] = k.astype(BF16)
    v = jnp.dot(xn, w_ref[:, 2 * d:3 * d], preferred_element_type=F32)
    v_ref[...] = v
    vb_ref[...] = v.astype(BF16)


def _diff_inproj(x, g, w):
    n, d = x.shape
    tm = ROW_TILE
    row = pl.BlockSpec((tm, d), lambda i: (i, 0))
    return pl.pallas_call(
        _diff_inproj_kernel,
        grid=(n // tm,),
        in_specs=[row, _full_spec((1, d)), _full_spec(w.shape)],
        out_specs=[row] * 5,
        out_shape=[jax.ShapeDtypeStruct((n, d), BF16), jax.ShapeDtypeStruct((n, d), F32),
                   jax.ShapeDtypeStruct((n, d), F32), jax.ShapeDtypeStruct((n, d), BF16),
                   jax.ShapeDtypeStruct((n, d), BF16)],
        compiler_params=_cparams(1),
        name="diff_inproj",
    )(x, g, w)


def _fox_inproj_kernel(x_ref, g_ref, w_ref, wf_ref, bf_ref, q_ref, k_ref, v_ref, kb_ref, vb_ref, lf_ref):
    d = D_MODEL
    xn = _rms(x_ref[...], g_ref[...]).astype(BF16)
    q = jnp.dot(xn, w_ref[:, 0:d], preferred_element_type=F32)
    q_ref[...] = (q * FOX_SCALE).astype(BF16)
    k = jnp.dot(xn, w_ref[:, d:2 * d], preferred_element_type=F32)
    k_ref[...] = k
    kb_ref[...] = k.astype(BF16)
    v = jnp.dot(xn, w_ref[:, 2 * d:3 * d], preferred_element_type=F32)
    v_ref[...] = v
    vb_ref[...] = v.astype(BF16)
    z = jnp.dot(xn, wf_ref[...], preferred_element_type=F32) + bf_ref[...]
    lf = jnp.minimum(z, 0.0) - jnp.log1p(jnp.exp(-jnp.abs(z)))
    lf_ref[...] = lf[:, 0:FOX_HEADS]


def _fox_inproj(x, g, w, wf, bf):
    n, d = x.shape
    tm = ROW_TILE
    row = pl.BlockSpec((tm, d), lambda i: (i, 0))
    return pl.pallas_call(
        _fox_inproj_kernel,
        grid=(n // tm,),
        in_specs=[row, _full_spec((1, d)), _full_spec(w.shape), _full_spec(wf.shape), _full_spec(bf.shape)],
        out_specs=[row] * 5 + [pl.BlockSpec((tm, FOX_HEADS), lambda i: (i, 0))],
        out_shape=[jax.ShapeDtypeStruct((n, d), BF16), jax.ShapeDtypeStruct((n, d), F32),
                   jax.ShapeDtypeStruct((n, d), F32), jax.ShapeDtypeStruct((n, d), BF16),
                   jax.ShapeDtypeStruct((n, d), BF16), jax.ShapeDtypeStruct((n, FOX_HEADS), F32)],
        compiler_params=_cparams(1),
        name="fox_inproj",
    )(x, g, w, wf, bf)


def _conv_inproj_kernel(x_ref, g_ref, w_ref, b_ref, u_ref):
    d = D_MODEL
    xn = _rms(x_ref[...], g_ref[...]).astype(BF16)
    a = jnp.dot(xn, w_ref[:, 0:d], preferred_element_type=F32) + b_ref[:, 0:d]
    gate = jnp.dot(xn, w_ref[:, d:2 * d], preferred_element_type=F32) + b_ref[:, d:2 * d]
    u_ref[...] = a * (1.0 / (1.0 + jnp.exp(-gate)))


def _conv_inproj(x, g, w, b):
    n, d = x.shape
    tm = ROW_TILE
    row = pl.BlockSpec((tm, d), lambda i: (i, 0))
    return pl.pallas_call(
        _conv_inproj_kernel,
        grid=(n // tm,),
        in_specs=[row, _full_spec((1, d)), _full_spec(w.shape), _full_spec(b.shape)],
        out_specs=row,
        out_shape=jax.ShapeDtypeStruct((n, d), F32),
        compiler_params=_cparams(1),
        name="conv_inproj",
    )(x, g, w, b)


def _mix_mlp_kernel(x_ref, o_ref, wo_ref, bo_ref, g_ref, w1_ref, w2_ref, gf_ref, out_ref, *, final):
    x1 = x_ref[...] + jnp.dot(o_ref[...], wo_ref[...], preferred_element_type=F32) + bo_ref[...]
    xn = _rms(x1, g_ref[...]).astype(BF16)
    acc = x1
    d_ff = w1_ref.shape[1]
    for c in range(d_ff // FF_CHUNK):
        h = jnp.dot(xn, w1_ref[:, c * FF_CHUNK:(c + 1) * FF_CHUNK], preferred_element_type=F32)
        h = jnp.square(jnp.maximum(h, 0.0)).astype(BF16)
        acc = acc + jnp.dot(h, w2_ref[c * FF_CHUNK:(c + 1) * FF_CHUNK, :], preferred_element_type=F32)
    if final:
        acc = _rms(acc, gf_ref[...])
    out_ref[...] = acc


def _mix_mlp(x, o, wo, bo, g, w1, w2, gf, final):
    n, d = x.shape
    tm = ROW_TILE
    row = pl.BlockSpec((tm, d), lambda i: (i, 0))
    resident = lambda a: pl.BlockSpec(a.shape, lambda i: (0,) * a.ndim, pipeline_mode=pl.Buffered(1))
    return pl.pallas_call(
        functools.partial(_mix_mlp_kernel, final=final),
        grid=(n // tm,),
        in_specs=[row, row, resident(wo), _full_spec(bo.shape), _full_spec(g.shape), resident(w1), resident(w2),
                  _full_spec(gf.shape)],
        out_specs=row,
        out_shape=jax.ShapeDtypeStruct((n, d), F32),
        compiler_params=_cparams(1),
        name="mix_mlp",
    )(x, o, wo, bo, g, w1, w2, gf)


def _t5_bias(tab_ref, h, qpos, kpos):
    half = REL_BUCKETS // 2
    n = qpos - kpos
    off = jnp.where(n < 0, half, 0)
    n = jnp.abs(n)
    large = jnp.full(n.shape, half // 2, jnp.int32)
    for t in _T5_STEPS:
        large = large + jnp.where(n >= t, 1, 0)
    bucket = off + jnp.where(n < half // 2, n, large)
    far = tab_ref[half - 1, h]
    out = jnp.zeros(n.shape, F32)
    for b in range(REL_BUCKETS):
        out = jnp.where(bucket == b, tab_ref[b, h] - far, out)
    visible = (kpos // CHUNK) <= (qpos // CHUNK)
    return jnp.where(visible, out, NEG)


def _prompt_tiles_kernel(tab_ref, bt_ref, cm_ref, *, tq):
    h = pl.program_id(0)
    for v in range(2):
        qpos = lax.broadcasted_iota(jnp.int32, (tq, 2 * tq), 0) + v * tq
        kpos = lax.broadcasted_iota(jnp.int32, (tq, 2 * tq), 1)
        bt_ref[0, v] = _t5_bias(tab_ref, h, qpos, kpos)
        cm_ref[0, v] = jnp.where(kpos <= qpos, 0.0, NEG)


def _prompt_tiles(table, tq):
    shp = (DIFF_HEADS, 2, tq, 2 * tq)
    spec = pl.BlockSpec((1, 2, tq, 2 * tq), lambda h: (h, 0, 0, 0))
    return pl.pallas_call(
        functools.partial(_prompt_tiles_kernel, tq=tq),
        grid=(DIFF_HEADS,),
        in_specs=[pl.BlockSpec(memory_space=pltpu.SMEM)],
        out_specs=[spec, spec],
        out_shape=[jax.ShapeDtypeStruct(shp, F32), jax.ShapeDtypeStruct(shp, F32)],
        compiler_params=_cparams(1),
        name="prompt_score_tiles",
    )(table)


def _sample_tiles_kernel(tab_ref, dl_ref, dn_ref, fn_ref, *, t_new, past, tk):
    rows = DIFF_HEADS * 2 * t_new
    for h in range(DIFF_HEADS):
        r0 = h * 2 * t_new
        q = lax.broadcasted_iota(jnp.int32, (2 * t_new, tk), 0) % t_new + past
        k = lax.broadcasted_iota(jnp.int32, (2 * t_new, tk), 1) + (past - tk)
        dl_ref[r0:r0 + 2 * t_new, :] = _t5_bias(tab_ref, h, q, k)
        q = lax.broadcasted_iota(jnp.int32, (2 * t_new, t_new), 0) % t_new + past
        k = lax.broadcasted_iota(jnp.int32, (2 * t_new, t_new), 1) + past
        dn_ref[r0:r0 + 2 * t_new, :] = _t5_bias(tab_ref, h, q, k)
    q = lax.broadcasted_iota(jnp.int32, (rows, t_new), 0) % t_new
    k = lax.broadcasted_iota(jnp.int32, (rows, t_new), 1)
    fn_ref[...] = jnp.where(k <= q, 0.0, NEG)


def _sample_tiles(table, t_new, past, tk):
    rows = DIFF_HEADS * 2 * t_new
    assert rows == FOX_HEADS * t_new
    return pl.pallas_call(
        functools.partial(_sample_tiles_kernel, t_new=t_new, past=past, tk=tk),
        in_specs=[pl.BlockSpec(memory_space=pltpu.SMEM)],
        out_shape=[jax.ShapeDtypeStruct((rows, tk), F32), jax.ShapeDtypeStruct((rows, t_new), F32),
                   jax.ShapeDtypeStruct((rows, t_new), F32)],
        name="sample_score_tiles",
    )(table)


def _flash_update(q2, k, v, bias, m_ref, l_ref, acc_ref):
    s = lax.dot_general(q2, k, (((1,), (1,)), ((), ())), preferred_element_type=F32)
    if bias is not None:
        if bias.shape[0] != s.shape[0]:
            r, w = s.shape
            s = (s.reshape(2, r // 2, w) + bias[None]).reshape(r, w)
        else:
            s = s + bias
    m_prev = m_ref[...]
    m_new = jnp.maximum(m_prev, jnp.max(s, axis=-1, keepdims=True))
    alpha = jnp.exp(m_prev - m_new)
    p = jnp.exp(s - m_new)
    l_ref[...] = alpha * l_ref[...] + jnp.sum(p, axis=-1, keepdims=True)
    acc_ref[...] = alpha * acc_ref[...] + jnp.dot(p.astype(BF16), v, preferred_element_type=F32)
    m_ref[...] = m_new


def _lambda(lqk_ref, lam_init):
    a = jnp.sum(lqk_ref[0:1, :] * lqk_ref[1:2, :], axis=-1, keepdims=True)
    b = jnp.sum(lqk_ref[2:3, :] * lqk_ref[3:4, :], axis=-1, keepdims=True)
    return jnp.exp(a) - jnp.exp(b) + lam_init


def _subln(o, g, lam_init):
    return _rms(o, g) * (1.0 - lam_init)


def _prompt_attn_kernel(*refs, fox, tq, seq, lam_init):
    if fox:
        (q_ref, qa_ref, k_ref, ka_ref, v_ref, bt_ref, o_ref, q2_ref, m_ref, l_ref, acc_ref) = refs
    else:
        (q_ref, k_ref, v_ref, bt_ref, lqk_ref, sg_ref, o_ref, q2_ref, m_ref, l_ref, acc_ref) = refs
    lane = lax.broadcasted_iota(jnp.int32, (tq, LANES), 1)
    lo = lane < LANES // 2
    if fox:
        c0 = 12 * pl.program_id(1)
        in_a = jnp.logical_and(lane >= c0, lane < c0 + 6)
        in_b = jnp.logical_and(lane >= c0 + 6, lane < c0 + 12)

    def keys(k0, w):
        if fox:
            return jnp.concatenate([k_ref[pl.ds(k0, w), :], ka_ref[pl.ds(k0, w), :]], axis=1)
        return k_ref[pl.ds(k0, w), :]

    def q_block(i, carry):
        r0 = pl.multiple_of(i * tq, tq)
        q = q_ref[pl.ds(r0, tq), :]
        zero = jnp.zeros_like(q)
        q2_ref[0:tq, 0:LANES] = jnp.where(lo, q, zero)
        q2_ref[tq:2 * tq, 0:LANES] = jnp.where(lo, zero, q)
        if fox:
            qa = qa_ref[pl.ds(r0, tq), :]
            q2_ref[0:tq, LANES:2 * LANES] = jnp.where(in_a, qa, zero)
            q2_ref[tq:2 * tq, LANES:2 * LANES] = jnp.where(in_b, qa, zero)
        m_ref[...] = jnp.full(m_ref.shape, NEG, F32)
        l_ref[...] = jnp.zeros(l_ref.shape, F32)
        acc_ref[...] = jnp.zeros(acc_ref.shape, F32)

        n_far = jnp.maximum(i - 1, 0)

        def far(t, c):
            k0 = pl.multiple_of(t * tq, tq)
            _flash_update(q2_ref[...], keys(k0, tq), v_ref[pl.ds(k0, tq), :], None, m_ref, l_ref, acc_ref)
            return c

        lax.fori_loop(0, n_far, far, 0)
        k0 = pl.multiple_of(n_far * tq, tq)
        bias = bt_ref[0, jnp.minimum(i, 1)]
        _flash_update(q2_ref[...], keys(k0, 2 * tq), v_ref[pl.ds(k0, 2 * tq), :], bias, m_ref, l_ref, acc_ref)

        top = acc_ref[0:tq, :] / l_ref[0:tq, :]
        bot = acc_ref[tq:2 * tq, :] / l_ref[tq:2 * tq, :]
        if fox:
            o = jnp.where(lo, top, bot)
        else:
            o = _subln(top - _lambda(lqk_ref, lam_init) * bot, sg_ref[...], lam_init)
        o_ref[pl.ds(r0, tq), :] = o.astype(BF16)
        return carry

    lax.fori_loop(0, seq // tq, q_block, 0)


def _prompt_attn(fox, batch, seq, q, k, v, bt, extra, lam_init=0.0):
    tq = ATT_TQ
    groups = D_MODEL // LANES
    blk = pl.BlockSpec((seq, LANES), lambda b, h: (b, h))
    if fox:
        qa, ka = extra
        aug = pl.BlockSpec((seq, LANES), lambda b, h: (b, 0))
        ins = [q, qa, k, ka, v, bt]
        in_specs = [blk, aug, blk, aug, blk, pl.BlockSpec((1, 2, tq, 2 * tq), lambda b, h: (0, 0, 0, 0))]
        kd = 2 * LANES
    else:
        lqk, sg = extra
        ins = [q, k, v, bt, lqk, sg]
        in_specs = [blk, blk, blk, pl.BlockSpec((1, 2, tq, 2 * tq), lambda b, h: (h, 0, 0, 0)),
                    _full_spec(lqk.shape), _full_spec(sg.shape)]
        kd = LANES
    return pl.pallas_call(
        functools.partial(_prompt_attn_kernel, fox=fox, tq=tq, seq=seq, lam_init=lam_init),
        grid=(batch, groups),
        in_specs=in_specs,
        out_specs=blk,
        out_shape=jax.ShapeDtypeStruct((batch * seq, D_MODEL), BF16),
        scratch_shapes=[pltpu.VMEM((2 * tq, kd), BF16), pltpu.VMEM((2 * tq, 1), F32),
                        pltpu.VMEM((2 * tq, 1), F32), pltpu.VMEM((2 * tq, LANES), F32)],
        compiler_params=_cparams(2),
        name="fox_prompt_attn" if fox else "diff_prompt_attn",
    )(*ins)


def _sample_attn_kernel(*refs, fox, t_new, n_tiles, lam_init):
    if fox:
        (q_ref, qa_ref, kc_ref, kac_ref, vc_ref, kn_ref, kan_ref, vn_ref, bl_ref, bn_ref,
         o_ref, qbd_ref, m_ref, l_ref, acc_ref) = refs
    else:
        (q_ref, kc_ref, vc_ref, kn_ref, vn_ref, bl_ref, bn_ref, lqk_ref, sg_ref,
         o_ref, qbd_ref, m_ref, l_ref, acc_ref) = refs
    t = pl.program_id(1)
    n_blocks = D_MODEL // 64

    @pl.when(t == 0)
    def _():
        q = q_ref[...]
        zero = jnp.zeros_like(q)
        cb = lax.broadcasted_iota(jnp.int32, q.shape, 1) // 64
        if fox:
            qa = qa_ref[0]
            la = lax.broadcasted_iota(jnp.int32, qa.shape, 1)
        for rb in range(n_blocks):
            qbd_ref[rb * t_new:(rb + 1) * t_new, 0:D_MODEL] = jnp.where(cb == rb, q, zero)
            if fox:
                sel = jnp.logical_and(la >= 6 * rb, la < 6 * rb + 6)
                qbd_ref[rb * t_new:(rb + 1) * t_new, D_MODEL:D_MODEL + LANES] = jnp.where(sel, qa, jnp.zeros_like(qa))
        m_ref[...] = jnp.full(m_ref.shape, NEG, F32)
        l_ref[...] = jnp.zeros(l_ref.shape, F32)
        acc_ref[...] = jnp.zeros(acc_ref.shape, F32)

    def cache_keys():
        k = kc_ref[0].astype(BF16)
        if fox:
            k = jnp.concatenate([k, kac_ref[0]], axis=1)
        return k

    @pl.when(t < n_tiles - 1)
    def _():
        _flash_update(qbd_ref[...], cache_keys(), vc_ref[0].astype(BF16), None, m_ref, l_ref, acc_ref)

    @pl.when(t == n_tiles - 1)
    def _():
        _flash_update(qbd_ref[...], cache_keys(), vc_ref[0].astype(BF16), bl_ref[...], m_ref, l_ref, acc_ref)
        kn = kn_ref[...]
        if fox:
            kn = jnp.concatenate([kn, kan_ref[0]], axis=1)
        _flash_update(qbd_ref[...], kn, vn_ref[...], bn_ref[...], m_ref, l_ref, acc_ref)
        lane = lax.broadcasted_iota(jnp.int32, (t_new, LANES), 1)
        lo = lane < LANES // 2
        for g in range(D_MODEL // LANES):
            ra, rb = 2 * g * t_new, (2 * g + 1) * t_new
            cols = slice(g * LANES, (g + 1) * LANES)
            top = acc_ref[ra:ra + t_new, cols] / l_ref[ra:ra + t_new, :]
            bot = acc_ref[rb:rb + t_new, cols] / l_ref[rb:rb + t_new, :]
            if fox:
                o = jnp.where(lo, top, bot)
            else:
                o = _subln(top - _lambda(lqk_ref, lam_init) * bot, sg_ref[...], lam_init)
            o_ref[:, cols] = o.astype(BF16)


def _sample_attn(fox, n_prompt, dec_batch, t_new, past, q, kc, vc, kn, vn, bias_last, bias_new, extra, lam_init=0.0):
    tk = SAMPLE_TK
    n_tiles = past // tk
    d = D_MODEL
    rows = (d // 64) * t_new
    s0 = n_prompt // t_new
    new = pl.BlockSpec((t_new, d), lambda b, t: (s0 + b, 0))
    cache = pl.BlockSpec((1, tk, d), lambda b, t: (b, t, 0))
    out = pl.BlockSpec((t_new, d), lambda b, t: (b, 0))
    if fox:
        qa, ka = extra
        qa_spec = pl.BlockSpec((1, t_new, LANES), lambda b, t: (b, past // t_new, 0))
        kac_spec = pl.BlockSpec((1, tk, LANES), lambda b, t: (b, t, 0))
        ins = [q, qa, kc, ka, vc, kn, ka, vn, bias_last, bias_new]
        in_specs = [new, qa_spec, cache, kac_spec, cache, new, qa_spec, new,
                    _full_spec(bias_last.shape), _full_spec(bias_new.shape)]
        kd = d + LANES
    else:
        lqk, sg = extra
        ins = [q, kc, vc, kn, vn, bias_last, bias_new, lqk, sg]
        in_specs = [new, cache, cache, new, new, _full_spec(bias_last.shape), _full_spec(bias_new.shape),
                    _full_spec(lqk.shape), _full_spec(sg.shape)]
        kd = d
    return pl.pallas_call(
        functools.partial(_sample_attn_kernel, fox=fox, t_new=t_new, n_tiles=n_tiles, lam_init=lam_init),
        grid=(dec_batch, n_tiles),
        in_specs=in_specs,
        out_specs=out,
        out_shape=jax.ShapeDtypeStruct((dec_batch * t_new, d), BF16),
        scratch_shapes=[pltpu.VMEM((rows, kd), BF16), pltpu.VMEM((rows, 1), F32),
                        pltpu.VMEM((rows, 1), F32), pltpu.VMEM((rows, d), F32)],
        compiler_params=_cparams(2),
        name="fox_sample_attn" if fox else "diff_sample_attn",
    )(*ins)


def _split3(c):
    hi = c.astype(BF16)
    r = c - hi.astype(F32)
    mid = r.astype(BF16)
    lo = (r - mid.astype(F32)).astype(BF16)
    return hi, mid, lo


def _decay_cols_kernel(lf_ref, eq_ref, ek_ref, qa_ref, ka_ref, *, length):
    h = FOX_HEADS
    carry = jnp.zeros((1, h), F32)
    for r in range(0, length, CUM_BLOCK):
        n = min(CUM_BLOCK, length - r)
        tri = (lax.broadcasted_iota(jnp.int32, (n, n), 1) <= lax.broadcasted_iota(jnp.int32, (n, n), 0))
        tri = jnp.where(tri, 1.0, 0.0).astype(BF16)
        parts = _split3(lf_ref[0, r:r + n, :])
        c = carry
        for p in parts:
            c = c + jnp.dot(tri, p, preferred_element_type=F32)
        carry = c[n - 1:n, :]
        chi, cmid, clo = _split3(c)
        ones = jnp.ones((n, h), BF16)
        qa = jnp.zeros((n, LANES), F32)
        ka = jnp.zeros((n, LANES), F32)
        for j, (pq, pk) in enumerate(((chi, -chi), (cmid, -cmid), (clo, -clo), (ones, ones))):
            qa = qa + jnp.dot(pq, eq_ref[j * h:(j + 1) * h, :], preferred_element_type=F32)
            ka = ka + jnp.dot(pk, ek_ref[j * h:(j + 1) * h, :], preferred_element_type=F32)
        qa_ref[0, r:r + n, :] = qa.astype(BF16)
        ka_ref[0, r:r + n, :] = ka.astype(BF16)


def _placement():
    h = FOX_HEADS
    eq = np.zeros((4 * h, LANES), np.float32)
    ek = np.zeros((4 * h, LANES), np.float32)
    for head in range(h):
        for part in range(3):
            eq[part * h + head, 6 * head + part] = 1.0
            ek[part * h + head, 6 * head + 3 + part] = 1.0
            eq[3 * h + head, 6 * head + 3 + part] = 1.0
            ek[3 * h + head, 6 * head + part] = 1.0
    return jnp.asarray(eq, BF16), jnp.asarray(ek, BF16)


def _decay_cols(lf):
    nb, length, h = lf.shape
    eq, ek = _placement()
    spec = pl.BlockSpec((1, length, LANES), lambda b: (b, 0, 0))
    return pl.pallas_call(
        functools.partial(_decay_cols_kernel, length=length),
        grid=(nb,),
        in_specs=[pl.BlockSpec((1, length, h), lambda b: (b, 0, 0)), _full_spec(eq.shape), _full_spec(ek.shape)],
        out_specs=[spec, spec],
        out_shape=[jax.ShapeDtypeStruct((nb, length, LANES), BF16)] * 2,
        compiler_params=_cparams(1),
        name="fox_decay_cols",
    )(lf, eq, ek)


def _conv_dw_kernel(u_ref, halo_ref, wdw_ref, bdw_ref, g_ref, b_ref, o_ref, ext_ref, y_ref, *, tm, steps_per_seq):
    halo = halo_ref[...]
    if steps_per_seq:
        halo = jnp.where(pl.program_id(0) % steps_per_seq == 0, 0.0, halo)
    ext_ref[0:HALO, :] = halo
    ext_ref[HALO:HALO + tm, :] = u_ref[...]
    d = u_ref.shape[1]
    first = HALO - (CONV_WIDTH - 1)
    for rc in range(tm // CONV_RC):
        for lc in range(d // CONV_LC):
            cols = slice(lc * CONV_LC, (lc + 1) * CONV_LC)
            acc = jnp.zeros((CONV_RC, CONV_LC), F32)
            for w in range(CONV_WIDTH):
                r = rc * CONV_RC + first + w
                acc = acc + ext_ref[r:r + CONV_RC, cols] * wdw_ref[w:w + 1, cols]
            y_ref[rc * CONV_RC:(rc + 1) * CONV_RC, cols] = acc
    y = y_ref[...] + bdw_ref[...]
    yc = y - jnp.mean(y, axis=-1, keepdims=True)
    yn = yc * lax.rsqrt(jnp.mean(yc * yc, axis=-1, keepdims=True) + EPS) * g_ref[...] + b_ref[...]
    o_ref[...] = (yn * (1.0 / (1.0 + jnp.exp(-yn)))).astype(BF16)


def _conv_dw(u, row0, n_rows, tm, halo, halo_index, steps_per_seq, wdw, bdw, g, b):
    d = u.shape[1]
    b0 = row0 // tm
    return pl.pallas_call(
        functools.partial(_conv_dw_kernel, tm=tm, steps_per_seq=steps_per_seq),
        grid=(n_rows // tm,),
        in_specs=[pl.BlockSpec((tm, d), lambda i: (b0 + i, 0)), pl.BlockSpec((HALO, d), lambda i: (halo_index(i), 0)),
                  _full_spec(wdw.shape), _full_spec(bdw.shape), _full_spec(g.shape), _full_spec(b.shape)],
        out_specs=pl.BlockSpec((tm, d), lambda i: (i, 0)),
        out_shape=jax.ShapeDtypeStruct((n_rows, d), BF16),
        scratch_shapes=[pltpu.VMEM((HALO + tm, d), F32), pltpu.VMEM((tm, d), F32)],
        compiler_params=_cparams(1),
        name="conv_dw",
    )(u, halo, wdw, bdw, g, b)


def kernel(x_prompt, x_sample, cache_diff_k, cache_diff_v, state_conv, cache_fox_k, cache_fox_v, cache_fox_logf, rel_bias, norm_g, final_g, diff_w_in, diff_w_out, diff_lq1, diff_lk1, diff_lq2, diff_lk2, diff_subln_g, conv_w_pw1, conv_b_pw1, conv_w_dw, conv_b_dw, conv_ln_g, conv_ln_b, conv_w_pw2, conv_b_pw2, fox_w_in, fox_b_f, fox_w_out, mlp_w1, mlp_w2):
    B, S, D = x_prompt.shape
    Bd, T, _ = x_sample.shape
    P = cache_diff_k.shape[2]
    depth = norm_g.shape[0]
    n_p, n_s = B * S, Bd * T
    assert D == D_MODEL and S % (2 * ATT_TQ) == 0 and P % SAMPLE_TK == 0 and n_p % ROW_TILE == 0
    assert n_s % ROW_TILE == 0 and T % 8 == 0 and T >= CONV_WIDTH - 1 and S % ROW_TILE == 0

    x = jnp.concatenate([x_prompt.reshape(n_p, D), x_sample.reshape(n_s, D)], axis=0)
    zero_bias = jnp.zeros((1, D), F32)
    final_gain = final_g.reshape(1, D)

    bt_diff, bt_fox = _prompt_tiles(rel_bias, ATT_TQ)
    sb_last, sb_new, sf_new = _sample_tiles(rel_bias, T, P, SAMPLE_TK)
    sf_last = jnp.zeros_like(sb_last)

    dk_p, dv_p, dk_s, dv_s = [], [], [], []
    cv_p, cv_s = [], []
    fk_p, fv_p, fl_p, fk_s, fv_s, fl_s = [], [], [], [], [], []

    for i in range(depth):
        kind, j = i % N_MIXERS, i // N_MIXERS
        g1 = norm_g[i, 0].reshape(1, D)
        g2 = norm_g[i, 1].reshape(1, D)
        if kind == 0:
            lam_init = 0.8 - 0.6 * math.exp(-0.3 * i)
            q, k, v, kb, vb = _diff_inproj(x, g1, diff_w_in[j].astype(BF16))
            lqk = jnp.stack([diff_lq1[j], diff_lk1[j], diff_lq2[j], diff_lk2[j]]).astype(F32)
            sg = diff_subln_g[j].reshape(1, 2 * DIFF_HEAD_DIM).astype(F32)
            o_p = _prompt_attn(False, B, S, q, kb, vb, bt_diff, (lqk, sg), lam_init)
            o_s = _sample_attn(False, n_p, Bd, T, P, q, cache_diff_k[j].reshape(Bd, P, D),
                               cache_diff_v[j].reshape(Bd, P, D), kb, vb, sb_last, sb_new, (lqk, sg), lam_init)
            o = jnp.concatenate([o_p, o_s], axis=0)
            wo, bo = diff_w_out[j].astype(BF16), zero_bias
            dk_p.append(k[:n_p].reshape(B, S, DIFF_HEADS, 2, DIFF_HEAD_DIM))
            dv_p.append(v[:n_p].reshape(B, S, DIFF_HEADS, 2 * DIFF_HEAD_DIM))
            dk_s.append(k[n_p:].reshape(Bd, T, DIFF_HEADS, 2, DIFF_HEAD_DIM))
            dv_s.append(v[n_p:].reshape(Bd, T, DIFF_HEADS, 2 * DIFF_HEAD_DIM))
        elif kind == 1:
            u = _conv_inproj(x, g1, conv_w_pw1[j].astype(BF16), conv_b_pw1[j].reshape(1, 2 * D))
            wdw = jnp.pad(conv_w_dw[j], ((0, HALO - CONV_WIDTH), (0, 0)))
            args = (wdw, conv_b_dw[j].reshape(1, D), conv_ln_g[j].reshape(1, D), conv_ln_b[j].reshape(1, D))
            per_tile = ROW_TILE // HALO
            o_p = _conv_dw(u, 0, n_p, ROW_TILE, u, lambda t: jnp.maximum(t * per_tile - 1, 0), S // ROW_TILE, *args)
            state = jnp.pad(state_conv[j], ((0, 0), (HALO - (CONV_WIDTH - 1), 0), (0, 0))).reshape(Bd * HALO, D)
            o_s = _conv_dw(u, n_p, n_s, T, state, lambda t: t, 0, *args)
            o = jnp.concatenate([o_p, o_s], axis=0)
            wo, bo = conv_w_pw2[j].astype(BF16), conv_b_pw2[j].reshape(1, D)
            keep = CONV_WIDTH - 1
            cv_p.append(u[:n_p].reshape(B, S, D)[:, S - keep:])
            cv_s.append(u[n_p:].reshape(Bd, T, D)[:, T - keep:])
        else:
            w_in = fox_w_in[j]
            wf = jnp.pad(w_in[:, 3 * D:], ((0, 0), (0, LANES - FOX_HEADS))).astype(BF16)
            bf = jnp.pad(fox_b_f[j].astype(F32), (0, LANES - FOX_HEADS)).reshape(1, LANES)
            q, k, v, kb, vb, lf = _fox_inproj(x, g1, w_in[:, :3 * D].astype(BF16), wf, bf)
            lf_p = lf[:n_p].reshape(B, S, FOX_HEADS)
            lf_s = lf[n_p:].reshape(Bd, T, FOX_HEADS)
            qa_p, ka_p = _decay_cols(lf_p)
            qa_s, ka_s = _decay_cols(jnp.concatenate([cache_fox_logf[j].astype(F32), lf_s], axis=1))
            o_p = _prompt_attn(True, B, S, q, kb, vb, bt_fox, (qa_p.reshape(n_p, LANES), ka_p.reshape(n_p, LANES)))
            o_s = _sample_attn(True, n_p, Bd, T, P, q, cache_fox_k[j].reshape(Bd, P, D),
                               cache_fox_v[j].reshape(Bd, P, D), kb, vb, sf_last, sf_new, (qa_s, ka_s))
            o = jnp.concatenate([o_p, o_s], axis=0)
            wo, bo = fox_w_out[j].astype(BF16), zero_bias
            fk_p.append(k[:n_p].reshape(B, S, FOX_HEADS, FOX_HEAD_DIM))
            fv_p.append(v[:n_p].reshape(B, S, FOX_HEADS, FOX_HEAD_DIM))
            fl_p.append(lf_p)
            fk_s.append(k[n_p:].reshape(Bd, T, FOX_HEADS, FOX_HEAD_DIM))
            fv_s.append(v[n_p:].reshape(Bd, T, FOX_HEADS, FOX_HEAD_DIM))
            fl_s.append(lf_s)
        x = _mix_mlp(x, o, wo, bo, g2, mlp_w1[i].astype(BF16), mlp_w2[i].astype(BF16), final_gain,
                     final=(i == depth - 1))

    y_prompt = x[:n_p].reshape(B, S, D)
    y_sample = x[n_p:].reshape(Bd, T, D)
    return (y_prompt, y_sample, jnp.stack(dk_p), jnp.stack(dv_p), jnp.stack(cv_p), jnp.stack(fk_p),
            jnp.stack(fv_p), jnp.stack(fl_p), jnp.stack(dk_s), jnp.stack(dv_s), jnp.stack(cv_s),
            jnp.stack(fk_s), jnp.stack(fv_s), jnp.stack(fl_s))
```

```python
import functools
import math

import numpy as np
import jax
import jax.numpy as jnp
from jax import lax
from jax.experimental import pallas as pl
from jax.experimental.pallas import tpu as pltpu

F32 = jnp.float32
BF16 = jnp.bfloat16

D_MODEL = 1024
CHUNK = 64
EPS = 1e-6
DIFF_HEADS = 8
DIFF_HEAD_DIM = 64
DIFF_SCALE = DIFF_HEAD_DIM ** -0.5
CONV_WIDTH = 31
FOX_HEADS = 16
FOX_HEAD_DIM = 64
FOX_SCALE = FOX_HEAD_DIM ** -0.5
REL_BUCKETS = 32
N_MIXERS = 3

LANES = 128
SUBLANES = 8
NEG = -1e30
ROW_TILE = 512
ATT_TQ = 512
ATT_STRIP = 512
REDUCE_ROWS = 64
SAMPLE_TK = 1024
FF_CHUNK = 1024
CUM_BLOCK = 256
HALO = 32
CONV_RC = 32
CONV_LC = 256
VMEM_LIMIT = 56 * 1024 * 1024

_T5_STEPS = (12, 16, 23, 32, 46, 64, 91)


def _cparams(n_axes):
    return pltpu.CompilerParams(dimension_semantics=("arbitrary",) * n_axes, vmem_limit_bytes=VMEM_LIMIT)


def _rms(x, g):
    return x * lax.rsqrt(jnp.mean(x * x, axis=-1, keepdims=True) + EPS) * g


def _full_spec(shape):
    nd = len(shape)
    return pl.BlockSpec(shape, lambda *_: (0,) * nd)


def _split_specs(n_prompt_tiles, tm, d):
    prompt = pl.BlockSpec((tm, d), lambda i: (jnp.minimum(i, n_prompt_tiles - 1), 0))
    sample = pl.BlockSpec((tm, d), lambda i: (jnp.maximum(i - n_prompt_tiles, 0), 0))
    return prompt, sample


def _split_store(n_prompt_tiles, value, prompt_ref, sample_ref):
    i = pl.program_id(0)

    @pl.when(i < n_prompt_tiles)
    def _():
        prompt_ref[...] = value

    @pl.when(i >= n_prompt_tiles)
    def _():
        sample_ref[...] = value


def _qkv_proj(x_ref, g_ref, w_ref, scale, n_pt, q_ref, kp_ref, ks_ref, vp_ref, vs_ref, kb_ref, vb_ref):
    d = D_MODEL
    xn = _rms(x_ref[...], g_ref[...]).astype(BF16)
    q = jnp.dot(xn, w_ref[:, 0:d], preferred_element_type=F32)
    q_ref[...] = (q * scale).astype(BF16)
    k = jnp.dot(xn, w_ref[:, d:2 * d], preferred_element_type=F32)
    _split_store(n_pt, k, kp_ref, ks_ref)
    kb_ref[...] = k.astype(BF16)
    v = jnp.dot(xn, w_ref[:, 2 * d:3 * d], preferred_element_type=F32)
    _split_store(n_pt, v, vp_ref, vs_ref)
    vb_ref[...] = v.astype(BF16)
    return xn


def _diff_inproj_kernel(x_ref, g_ref, w_ref, q_ref, kp_ref, ks_ref, vp_ref, vs_ref, kb_ref, vb_ref, *, n_pt):
    _qkv_proj(x_ref, g_ref, w_ref, DIFF_SCALE, n_pt, q_ref, kp_ref, ks_ref, vp_ref, vs_ref, kb_ref, vb_ref)


def _fox_inproj_kernel(x_ref, g_ref, w_ref, wf_ref, bf_ref, q_ref, kp_ref, ks_ref, vp_ref, vs_ref, kb_ref, vb_ref,
                       lf_ref, *, n_pt):
    xn = _qkv_proj(x_ref, g_ref, w_ref, FOX_SCALE, n_pt, q_ref, kp_ref, ks_ref, vp_ref, vs_ref, kb_ref, vb_ref)
    z = jnp.dot(xn, wf_ref[...], preferred_element_type=F32) + bf_ref[...]
    lf = jnp.minimum(z, 0.0) - jnp.log1p(jnp.exp(-jnp.abs(z)))
    lf_ref[...] = lf[:, 0:FOX_HEADS]


def _qkv_inproj(x, n_p, g, w, forget=None):
    n, d = x.shape
    tm = ROW_TILE
    n_pt = n_p // tm
    row = pl.BlockSpec((tm, d), lambda i: (i, 0))
    prompt, sample = _split_specs(n_pt, tm, d)
    f32_p = jax.ShapeDtypeStruct((n_p, d), F32)
    f32_s = jax.ShapeDtypeStruct((n - n_p, d), F32)
    bf_all = jax.ShapeDtypeStruct((n, d), BF16)
    ins = [x, g, w]
    in_specs = [row, _full_spec((1, d)), _full_spec(w.shape)]
    out_specs = [row, prompt, sample, prompt, sample, row, row]
    out_shape = [bf_all, f32_p, f32_s, f32_p, f32_s, bf_all, bf_all]
    if forget is None:
        body, name = _diff_inproj_kernel, "diff_inproj"
    else:
        body, name = _fox_inproj_kernel, "fox_inproj"
        ins += list(forget)
        in_specs += [_full_spec(a.shape) for a in forget]
        out_specs.append(pl.BlockSpec((tm, FOX_HEADS), lambda i: (i, 0)))
        out_shape.append(jax.ShapeDtypeStruct((n, FOX_HEADS), F32))
    return pl.pallas_call(
        functools.partial(body, n_pt=n_pt),
        grid=(n // tm,),
        in_specs=in_specs,
        out_specs=out_specs,
        out_shape=out_shape,
        compiler_params=_cparams(1),
        name=name,
    )(*ins)


def _conv_inproj_kernel(x_ref, g_ref, w_ref, b_ref, u_ref):
    d = D_MODEL
    xn = _rms(x_ref[...], g_ref[...]).astype(BF16)
    a = jnp.dot(xn, w_ref[:, 0:d], preferred_element_type=F32) + b_ref[:, 0:d]
    gate = jnp.dot(xn, w_ref[:, d:2 * d], preferred_element_type=F32) + b_ref[:, d:2 * d]
    u_ref[...] = a * (1.0 / (1.0 + jnp.exp(-gate)))


def _conv_inproj(x, g, w, b):
    n, d = x.shape
    tm = ROW_TILE
    row = pl.BlockSpec((tm, d), lambda i: (i, 0))
    return pl.pallas_call(
        _conv_inproj_kernel,
        grid=(n // tm,),
        in_specs=[row, _full_spec((1, d)), _full_spec(w.shape), _full_spec(b.shape)],
        out_specs=row,
        out_shape=jax.ShapeDtypeStruct((n, d), F32),
        compiler_params=_cparams(1),
        name="conv_inproj",
    )(x, g, w, b)


def _mix_mlp_kernel(x_ref, op_ref, os_ref, wo_ref, bo_ref, g_ref, w1_ref, w2_ref, gf_ref, *out_refs, n_pt, final):
    o = jnp.where(pl.program_id(0) < n_pt, op_ref[...], os_ref[...])
    x1 = x_ref[...] + jnp.dot(o, wo_ref[...], preferred_element_type=F32) + bo_ref[...]
    xn = _rms(x1, g_ref[...]).astype(BF16)
    acc = x1
    d_ff = w1_ref.shape[1]
    for c in range(d_ff // FF_CHUNK):
        h = jnp.dot(xn, w1_ref[:, c * FF_CHUNK:(c + 1) * FF_CHUNK], preferred_element_type=F32)
        h = jnp.square(jnp.maximum(h, 0.0)).astype(BF16)
        acc = acc + jnp.dot(h, w2_ref[c * FF_CHUNK:(c + 1) * FF_CHUNK, :], preferred_element_type=F32)
    if final:
        _split_store(n_pt, _rms(acc, gf_ref[...]), *out_refs)
    else:
        out_refs[0][...] = acc


def _mix_mlp(x, o_p, o_s, wo, bo, g, w1, w2, gf, final):
    n, d = x.shape
    tm = ROW_TILE
    n_p = o_p.shape[0]
    n_pt = n_p // tm
    row = pl.BlockSpec((tm, d), lambda i: (i, 0))
    prompt, sample = _split_specs(n_pt, tm, d)
    resident = lambda a: pl.BlockSpec(a.shape, lambda i: (0,) * a.ndim, pipeline_mode=pl.Buffered(1))
    if final:
        out_specs = [prompt, sample]
        out_shape = [jax.ShapeDtypeStruct((n_p, d), F32), jax.ShapeDtypeStruct((n - n_p, d), F32)]
    else:
        out_specs = [row]
        out_shape = [jax.ShapeDtypeStruct((n, d), F32)]
    return pl.pallas_call(
        functools.partial(_mix_mlp_kernel, n_pt=n_pt, final=final),
        grid=(n // tm,),
        in_specs=[row, prompt, sample, resident(wo), _full_spec(bo.shape), _full_spec(g.shape), resident(w1),
                  resident(w2), _full_spec(gf.shape)],
        out_specs=out_specs,
        out_shape=out_shape,
        compiler_params=_cparams(1),
        name="mix_mlp",
    )(x, o_p, o_s, wo, bo, g, w1, w2, gf)


def _t5_bias(tab_ref, h, qpos, kpos):
    half = REL_BUCKETS // 2
    n = qpos - kpos
    off = jnp.where(n < 0, half, 0)
    n = jnp.abs(n)
    large = jnp.full(n.shape, half // 2, jnp.int32)
    for t in _T5_STEPS:
        large = large + jnp.where(n >= t, 1, 0)
    bucket = off + jnp.where(n < half // 2, n, large)
    far = tab_ref[half - 1, h]
    out = jnp.zeros(n.shape, F32)
    for b in range(REL_BUCKETS):
        out = jnp.where(bucket == b, tab_ref[b, h] - far, out)
    visible = (kpos // CHUNK) <= (qpos // CHUNK)
    return jnp.where(visible, out, NEG)


def _prompt_tiles_kernel(tab_ref, bt_ref, cm_ref, *, tq):
    h = pl.program_id(0)
    for v in range(2):
        kpos = lax.broadcasted_iota(jnp.int32, (tq, tq), 0)
        qpos = lax.broadcasted_iota(jnp.int32, (tq, tq), 1) + (1 - v) * tq
        bt_ref[0, v] = _t5_bias(tab_ref, h, qpos, kpos)

    @pl.when(h == 0)
    def _():
        kpos = lax.broadcasted_iota(jnp.int32, (tq, tq), 0)
        qpos = lax.broadcasted_iota(jnp.int32, (tq, tq), 1)
        cm_ref[0, 0] = jnp.zeros((tq, tq), F32)
        cm_ref[0, 1] = jnp.where(kpos <= qpos, 0.0, NEG)


def _prompt_tiles(table, tq):
    return pl.pallas_call(
        functools.partial(_prompt_tiles_kernel, tq=tq),
        grid=(DIFF_HEADS,),
        in_specs=[pl.BlockSpec(memory_space=pltpu.SMEM)],
        out_specs=[pl.BlockSpec((1, 2, tq, tq), lambda h: (h, 0, 0, 0)),
                   pl.BlockSpec((1, 2, tq, tq), lambda h: (0, 0, 0, 0))],
        out_shape=[jax.ShapeDtypeStruct((DIFF_HEADS, 2, tq, tq), F32), jax.ShapeDtypeStruct((1, 2, tq, tq), F32)],
        compiler_params=_cparams(1),
        name="prompt_score_tiles",
    )(table)


def _sample_tiles_kernel(tab_ref, dl_ref, dn_ref, fn_ref, *, t_new, past, tk):
    rows = DIFF_HEADS * 2 * t_new
    for h in range(DIFF_HEADS):
        r0 = h * 2 * t_new
        q = lax.broadcasted_iota(jnp.int32, (2 * t_new, tk), 0) % t_new + past
        k = lax.broadcasted_iota(jnp.int32, (2 * t_new, tk), 1) + (past - tk)
        dl_ref[r0:r0 + 2 * t_new, :] = _t5_bias(tab_ref, h, q, k)
        q = lax.broadcasted_iota(jnp.int32, (2 * t_new, t_new), 0) % t_new + past
        k = lax.broadcasted_iota(jnp.int32, (2 * t_new, t_new), 1) + past
        dn_ref[r0:r0 + 2 * t_new, :] = _t5_bias(tab_ref, h, q, k)
    q = lax.broadcasted_iota(jnp.int32, (rows, t_new), 0) % t_new
    k = lax.broadcasted_iota(jnp.int32, (rows, t_new), 1)
    fn_ref[...] = jnp.where(k <= q, 0.0, NEG)


def _sample_tiles(table, t_new, past, tk):
    rows = DIFF_HEADS * 2 * t_new
    assert rows == FOX_HEADS * t_new
    return pl.pallas_call(
        functools.partial(_sample_tiles_kernel, t_new=t_new, past=past, tk=tk),
        in_specs=[pl.BlockSpec(memory_space=pltpu.SMEM)],
        out_shape=[jax.ShapeDtypeStruct((rows, tk), F32), jax.ShapeDtypeStruct((rows, t_new), F32),
                   jax.ShapeDtypeStruct((rows, t_new), F32)],
        name="sample_score_tiles",
    )(table)


def _lambda(lqk_ref, lam_init):
    a = jnp.sum(lqk_ref[0:1, :] * lqk_ref[1:2, :], axis=-1, keepdims=True)
    b = jnp.sum(lqk_ref[2:3, :] * lqk_ref[3:4, :], axis=-1, keepdims=True)
    return jnp.exp(a) - jnp.exp(b) + lam_init


def _scores_t(k, q2_ref, s_out):
    for c0 in range(0, q2_ref.shape[0], ATT_STRIP):
        cols = slice(c0, c0 + ATT_STRIP)
        s_out[:, cols] = lax.dot_general(k, q2_ref[cols, :], (((1,), (1,)), ((), ())), preferred_element_type=F32)


def _reduce_rows(op, x):
    w, n = x.shape
    part = op(x.reshape(w // REDUCE_ROWS, REDUCE_ROWS, n), axis=0)
    return op(part, axis=0, keepdims=True)


def _softmax_pv_t(s_in, v, bias_t, m_ref, l_ref, acc_ref):
    for c0 in range(0, s_in.shape[1], ATT_STRIP):
        cols = slice(c0, c0 + ATT_STRIP)
        s = s_in[:, cols]
        if bias_t is not None:
            b0 = c0 % bias_t.shape[1]
            s = s + bias_t[:, b0:b0 + ATT_STRIP]
        m_prev = m_ref[:, cols]
        m_new = jnp.maximum(m_prev, _reduce_rows(jnp.max, s))
        alpha = jnp.exp(m_prev - m_new)
        p = jnp.exp(s - m_new)
        l_ref[:, cols] = alpha * l_ref[:, cols] + _reduce_rows(jnp.sum, p)
        pv = lax.dot_general(v, p.astype(BF16), (((0,), (0,)), ((), ())), preferred_element_type=F32)
        acc_ref[:, cols] = alpha * acc_ref[:, cols] + pv
        m_ref[:, cols] = m_new


def _prompt_attn_kernel(*refs, fox, tq, seq, lam_init):
    if fox:
        (q_ref, qa_ref, k_ref, ka_ref, v_ref, bt_ref, o_ref, q2_ref, s_ref, m_ref, l_ref, acc_ref) = refs
    else:
        (q_ref, k_ref, v_ref, bt_ref, lqk_ref, sg_ref, o_ref, q2_ref, s_ref, m_ref, l_ref, acc_ref) = refs
    lane = lax.broadcasted_iota(jnp.int32, (tq, LANES), 1)
    lo = lane < LANES // 2
    first_rows = lax.broadcasted_iota(jnp.int32, (LANES, tq), 0) < LANES // 2
    if fox:
        c0 = 12 * pl.program_id(1)
        in_a = jnp.logical_and(lane >= c0, lane < c0 + 6)
        in_b = jnp.logical_and(lane >= c0 + 6, lane < c0 + 12)

    def tile(j, bias_t):
        k0 = pl.multiple_of(j * tq, tq)
        k = k_ref[pl.ds(k0, tq), :]
        if fox:
            k = jnp.concatenate([k, ka_ref[pl.ds(k0, tq), :]], axis=1)
        _scores_t(k, q2_ref, s_ref)
        _softmax_pv_t(s_ref, v_ref[pl.ds(k0, tq), :], bias_t, m_ref, l_ref, acc_ref)

    def q_block(i, carry):
        r0 = pl.multiple_of(i * tq, tq)
        q = q_ref[pl.ds(r0, tq), :]
        zero = jnp.zeros_like(q)
        q2_ref[0:tq, 0:LANES] = jnp.where(lo, q, zero)
        q2_ref[tq:2 * tq, 0:LANES] = jnp.where(lo, zero, q)
        if fox:
            qa = qa_ref[pl.ds(r0, tq), :]
            q2_ref[0:tq, LANES:2 * LANES] = jnp.where(in_a, qa, zero)
            q2_ref[tq:2 * tq, LANES:2 * LANES] = jnp.where(in_b, qa, zero)
        m_ref[...] = jnp.full(m_ref.shape, NEG, F32)
        l_ref[...] = jnp.zeros(l_ref.shape, F32)
        acc_ref[...] = jnp.zeros(acc_ref.shape, F32)

        def far(j, c):
            tile(j, None)
            return c

        lax.fori_loop(0, i - 1, far, 0)

        @pl.when(i >= 1)
        def _():
            tile(i - 1, bt_ref.at[0, 0])

        tile(i, bt_ref.at[0, 1])

        top = acc_ref[:, 0:tq] / l_ref[:, 0:tq]
        bot = acc_ref[:, tq:2 * tq] / l_ref[:, tq:2 * tq]
        if fox:
            o_t = jnp.where(first_rows, top, bot)
        else:
            o_t = top - _lambda(lqk_ref, lam_init) * bot
            ms = jnp.mean(o_t * o_t, axis=0, keepdims=True)
            o_t = o_t * lax.rsqrt(ms + EPS) * sg_ref[...] * (1.0 - lam_init)
        o_ref[pl.ds(r0, tq), :] = o_t.T.astype(BF16)
        return carry

    lax.fori_loop(0, seq // tq, q_block, 0)


def _prompt_attn(fox, batch, seq, q, k, v, bt, extra, lam_init=0.0):
    tq = ATT_TQ
    groups = D_MODEL // LANES
    blk = pl.BlockSpec((seq, LANES), lambda b, h: (b, h))
    if fox:
        qa, ka = extra
        aug = pl.BlockSpec((seq, LANES), lambda b, h: (b, 0))
        ins = [q, qa, k, ka, v, bt]
        in_specs = [blk, aug, blk, aug, blk, pl.BlockSpec((1, 2, tq, tq), lambda b, h: (0, 0, 0, 0))]
        kd = 2 * LANES
    else:
        lqk, sg = extra
        ins = [q, k, v, bt, lqk, sg]
        in_specs = [blk, blk, blk, pl.BlockSpec((1, 2, tq, tq), lambda b, h: (h, 0, 0, 0)),
                    _full_spec(lqk.shape), _full_spec(sg.shape)]
        kd = LANES
    return pl.pallas_call(
        functools.partial(_prompt_attn_kernel, fox=fox, tq=tq, seq=seq, lam_init=lam_init),
        grid=(batch, groups),
        in_specs=in_specs,
        out_specs=blk,
        out_shape=jax.ShapeDtypeStruct((batch * seq, D_MODEL), BF16),
        scratch_shapes=[pltpu.VMEM((2 * tq, kd), BF16), pltpu.VMEM((tq, 2 * tq), F32), pltpu.VMEM((1, 2 * tq), F32),
                        pltpu.VMEM((1, 2 * tq), F32), pltpu.VMEM((LANES, 2 * tq), F32)],
        compiler_params=_cparams(2),
        name="fox_prompt_attn" if fox else "diff_prompt_attn",
    )(*ins)


def _online_update(s, pv_fn, m_ref, l_ref, acc_ref):
    m_prev = m_ref[...]
    m_new = jnp.maximum(m_prev, jnp.max(s, axis=-1, keepdims=True))
    alpha = jnp.exp(m_prev - m_new)
    p = jnp.exp(s - m_new)
    l_ref[...] = alpha * l_ref[...] + jnp.sum(p, axis=-1, keepdims=True)
    acc_ref[...] = alpha * acc_ref[...] + pv_fn(p.astype(BF16))
    m_ref[...] = m_new


def _sample_attn_kernel(*refs, fox, t_new, tk, n_tiles, lam_init):
    if fox:
        (q_ref, qa_ref, kt_ref, kat_ref, vt_ref, kn_ref, kan_ref, vn_ref, bn_ref,
         o_ref, qbd_ref, m_ref, l_ref, acc_ref) = refs
    else:
        (q_ref, kt_ref, v_ref, kn_ref, vn_ref, bl_ref, bn_ref, lqk_ref, sg_ref,
         o_ref, qbd_ref, m_ref, l_ref, acc_ref) = refs
    t = pl.program_id(1)
    n_blocks = D_MODEL // 64
    rows_per_head = 2 * t_new

    @pl.when(t == 0)
    def _():
        q = q_ref[...]
        zero = jnp.zeros_like(q)
        cb = lax.broadcasted_iota(jnp.int32, q.shape, 1) // 64
        if fox:
            qa = qa_ref[0]
            la = lax.broadcasted_iota(jnp.int32, qa.shape, 1)
        for rb in range(n_blocks):
            qbd_ref[rb * t_new:(rb + 1) * t_new, 0:D_MODEL] = jnp.where(cb == rb, q, zero)
            if fox:
                sel = jnp.logical_and(la >= 6 * rb, la < 6 * rb + 6)
                qbd_ref[rb * t_new:(rb + 1) * t_new, D_MODEL:D_MODEL + LANES] = jnp.where(sel, qa, jnp.zeros_like(qa))
        m_ref[...] = jnp.full(m_ref.shape, NEG, F32)
        l_ref[...] = jnp.zeros(l_ref.shape, F32)
        acc_ref[...] = jnp.zeros(acc_ref.shape, F32)

    def cache_scores():
        kt = kt_ref[0, 0].astype(BF16)
        if fox:
            kt = jnp.concatenate([kt, kat_ref[0]], axis=0)
        return jnp.dot(qbd_ref[...], kt, preferred_element_type=F32)

    if fox:
        def cache_pv(p):
            return lax.dot_general(p, vt_ref[0, 0].astype(BF16), (((1,), (1,)), ((), ())), preferred_element_type=F32)

        def new_pv(p):
            return jnp.dot(p, vn_ref[...], preferred_element_type=F32)
    else:
        def cache_pv(p):
            return jnp.concatenate(
                [jnp.dot(p[h * rows_per_head:(h + 1) * rows_per_head, :],
                         v_ref[0, 0, pl.ds(h, tk, stride=DIFF_HEADS), :].astype(BF16), preferred_element_type=F32)
                 for h in range(DIFF_HEADS)], axis=0)

        def new_pv(p):
            return jnp.concatenate(
                [jnp.dot(p[h * rows_per_head:(h + 1) * rows_per_head, :], vn_ref[:, h * LANES:(h + 1) * LANES],
                         preferred_element_type=F32) for h in range(DIFF_HEADS)], axis=0)

    @pl.when(t < n_tiles - 1)
    def _():
        _online_update(cache_scores(), cache_pv, m_ref, l_ref, acc_ref)

    @pl.when(t == n_tiles - 1)
    def _():
        s = cache_scores()
        if not fox:
            s = s + bl_ref[...]
        _online_update(s, cache_pv, m_ref, l_ref, acc_ref)
        kn = kn_ref[...]
        if fox:
            kn = jnp.concatenate([kn, kan_ref[0]], axis=1)
        s = lax.dot_general(qbd_ref[...], kn, (((1,), (1,)), ((), ())), preferred_element_type=F32) + bn_ref[...]
        _online_update(s, new_pv, m_ref, l_ref, acc_ref)

        lane = lax.broadcasted_iota(jnp.int32, (t_new, LANES), 1)
        lo = lane < LANES // 2
        for g in range(D_MODEL // LANES):
            ra, rb = 2 * g * t_new, (2 * g + 1) * t_new
            cols = slice(g * LANES, (g + 1) * LANES)
            acc_cols = cols if fox else slice(0, LANES)
            top = acc_ref[ra:ra + t_new, acc_cols] / l_ref[ra:ra + t_new, :]
            bot = acc_ref[rb:rb + t_new, acc_cols] / l_ref[rb:rb + t_new, :]
            if fox:
                o = jnp.where(lo, top, bot)
            else:
                o = _rms(top - _lambda(lqk_ref, lam_init) * bot, sg_ref[...]) * (1.0 - lam_init)
            o_ref[:, cols] = o.astype(BF16)


def _sample_attn(fox, layer, n_prompt, dec_batch, t_new, past, q, kt, v, kn, vn, bias_last, bias_new, extra,
                 lam_init=0.0):
    tk = SAMPLE_TK
    n_tiles = past // tk
    d = D_MODEL
    rows = (d // 64) * t_new
    s0 = n_prompt // t_new
    new = pl.BlockSpec((t_new, d), lambda b, t: (s0 + b, 0))
    dims_keys = pl.BlockSpec((1, 1, d, tk), lambda b, t: (layer, b, 0, t))
    out = pl.BlockSpec((t_new, d), lambda b, t: (b, 0))
    if fox:
        qa, ka, kat = extra
        new_aug = pl.BlockSpec((1, t_new, LANES), lambda b, t: (b, past // t_new, 0))
        ins = [q, qa, kt, kat, v, kn, ka, vn, bias_new]
        in_specs = [new, new_aug, dims_keys, pl.BlockSpec((1, LANES, tk), lambda b, t: (b, 0, t)), dims_keys,
                    new, new_aug, new, _full_spec(bias_new.shape)]
        kd, acc_w = d + LANES, d
    else:
        lqk, sg = extra
        ins = [q, kt, v, kn, vn, bias_last, bias_new, lqk, sg]
        in_specs = [new, dims_keys, pl.BlockSpec((1, 1, tk * DIFF_HEADS, LANES), lambda b, t: (layer, b, t, 0)),
                    new, new, _full_spec(bias_last.shape), _full_spec(bias_new.shape),
                    _full_spec(lqk.shape), _full_spec(sg.shape)]
        kd, acc_w = d, LANES
    return pl.pallas_call(
        functools.partial(_sample_attn_kernel, fox=fox, t_new=t_new, tk=tk, n_tiles=n_tiles, lam_init=lam_init),
        grid=(dec_batch, n_tiles),
        in_specs=in_specs,
        out_specs=out,
        out_shape=jax.ShapeDtypeStruct((dec_batch * t_new, d), BF16),
        scratch_shapes=[pltpu.VMEM((rows, kd), BF16), pltpu.VMEM((rows, 1), F32),
                        pltpu.VMEM((rows, 1), F32), pltpu.VMEM((rows, acc_w), F32)],
        compiler_params=_cparams(2),
        name="fox_sample_attn" if fox else "diff_sample_attn",
    )(*ins)


def _split3(c):
    hi = c.astype(BF16)
    r = c - hi.astype(F32)
    mid = r.astype(BF16)
    lo = (r - mid.astype(F32)).astype(BF16)
    return hi, mid, lo


def _decay_cols_kernel(lf_ref, eq_ref, ek_ref, ekt_ref, qa_ref, ka_ref, *maybe_kat_ref, length):
    h = FOX_HEADS
    carry = jnp.zeros((1, h), F32)
    for r in range(0, length, CUM_BLOCK):
        n = min(CUM_BLOCK, length - r)
        tri = (lax.broadcasted_iota(jnp.int32, (n, n), 1) <= lax.broadcasted_iota(jnp.int32, (n, n), 0))
        tri = jnp.where(tri, 1.0, 0.0).astype(BF16)
        parts = _split3(lf_ref[0, r:r + n, :])
        c = carry
        for p in parts:
            c = c + jnp.dot(tri, p, preferred_element_type=F32)
        carry = c[n - 1:n, :]
        chi, cmid, clo = _split3(c)
        ones = jnp.ones((n, h), BF16)
        qa = jnp.zeros((n, LANES), F32)
        ka = jnp.zeros((n, LANES), F32)
        kat = jnp.zeros((LANES, n), F32)
        for j, (pq, pk) in enumerate(((chi, -chi), (cmid, -cmid), (clo, -clo), (ones, ones))):
            qa = qa + jnp.dot(pq, eq_ref[j * h:(j + 1) * h, :], preferred_element_type=F32)
            ka = ka + jnp.dot(pk, ek_ref[j * h:(j + 1) * h, :], preferred_element_type=F32)
            if maybe_kat_ref:
                kat = kat + lax.dot_general(ekt_ref[:, j * h:(j + 1) * h], pk, (((1,), (1,)), ((), ())),
                                            preferred_element_type=F32)
        qa_ref[0, r:r + n, :] = qa.astype(BF16)
        ka_ref[0, r:r + n, :] = ka.astype(BF16)
        if maybe_kat_ref:
            maybe_kat_ref[0][0, :, r:r + n] = kat.astype(BF16)


def _placement():
    h = FOX_HEADS
    eq = np.zeros((4 * h, LANES), np.float32)
    ek = np.zeros((4 * h, LANES), np.float32)
    for head in range(h):
        for part in range(3):
            eq[part * h + head, 6 * head + part] = 1.0
            ek[part * h + head, 6 * head + 3 + part] = 1.0
            eq[3 * h + head, 6 * head + 3 + part] = 1.0
            ek[3 * h + head, 6 * head + part] = 1.0
    return jnp.asarray(eq, BF16), jnp.asarray(ek, BF16), jnp.asarray(ek.T, BF16)


def _decay_cols(lf, keys_transposed):
    nb, length, h = lf.shape
    eq, ek, ekt = _placement()
    spec = pl.BlockSpec((1, length, LANES), lambda b: (b, 0, 0))
    out_specs = [spec, spec]
    out_shape = [jax.ShapeDtypeStruct((nb, length, LANES), BF16)] * 2
    if keys_transposed:
        out_specs.append(pl.BlockSpec((1, LANES, length), lambda b: (b, 0, 0)))
        out_shape.append(jax.ShapeDtypeStruct((nb, LANES, length), BF16))
    return pl.pallas_call(
        functools.partial(_decay_cols_kernel, length=length),
        grid=(nb,),
        in_specs=[pl.BlockSpec((1, length, h), lambda b: (b, 0, 0)), _full_spec(eq.shape), _full_spec(ek.shape),
                  _full_spec(ekt.shape)],
        out_specs=out_specs,
        out_shape=out_shape,
        compiler_params=_cparams(1),
        name="fox_decay_cols",
    )(lf, eq, ek, ekt)


def _conv_dw_kernel(u_ref, halo_ref, wdw_ref, bdw_ref, g_ref, b_ref, o_ref, ext_ref, y_ref, *, tm, steps_per_seq):
    halo = halo_ref[...]
    if steps_per_seq:
        halo = jnp.where(pl.program_id(0) % steps_per_seq == 0, 0.0, halo)
    ext_ref[0:HALO, :] = halo
    ext_ref[HALO:HALO + tm, :] = u_ref[...]
    d = u_ref.shape[1]
    first = HALO - (CONV_WIDTH - 1)
    for rc in range(tm // CONV_RC):
        for lc in range(d // CONV_LC):
            cols = slice(lc * CONV_LC, (lc + 1) * CONV_LC)
            acc = jnp.zeros((CONV_RC, CONV_LC), F32)
            for w in range(CONV_WIDTH):
                r = rc * CONV_RC + first + w
                acc = acc + ext_ref[r:r + CONV_RC, cols] * wdw_ref[w:w + 1, cols]
            y_ref[rc * CONV_RC:(rc + 1) * CONV_RC, cols] = acc
    y = y_ref[...] + bdw_ref[...]
    yc = y - jnp.mean(y, axis=-1, keepdims=True)
    yn = yc * lax.rsqrt(jnp.mean(yc * yc, axis=-1, keepdims=True) + EPS) * g_ref[...] + b_ref[...]
    o_ref[...] = (yn * (1.0 / (1.0 + jnp.exp(-yn)))).astype(BF16)


def _conv_dw(u, row0, n_rows, tm, halo, halo_index, steps_per_seq, wdw, bdw, g, b):
    d = u.shape[1]
    b0 = row0 // tm
    return pl.pallas_call(
        functools.partial(_conv_dw_kernel, tm=tm, steps_per_seq=steps_per_seq),
        grid=(n_rows // tm,),
        in_specs=[pl.BlockSpec((tm, d), lambda i: (b0 + i, 0)), pl.BlockSpec((HALO, d), lambda i: (halo_index(i), 0)),
                  _full_spec(wdw.shape), _full_spec(bdw.shape), _full_spec(g.shape), _full_spec(b.shape)],
        out_specs=pl.BlockSpec((tm, d), lambda i: (i, 0)),
        out_shape=jax.ShapeDtypeStruct((n_rows, d), BF16),
        scratch_shapes=[pltpu.VMEM((HALO + tm, d), F32), pltpu.VMEM((tm, d), F32)],
        compiler_params=_cparams(1),
        name="conv_dw",
    )(u, halo, wdw, bdw, g, b)


def _dims_keys(cache):
    nd = cache.ndim
    t = jnp.transpose(cache, (0, 1) + tuple(range(3, nd)) + (2,))
    return t.reshape(cache.shape[0], cache.shape[1], D_MODEL, cache.shape[2])


def kernel(x_prompt, x_sample, cache_diff_k, cache_diff_v, state_conv, cache_fox_k, cache_fox_v, cache_fox_logf, rel_bias, norm_g, final_g, diff_w_in, diff_w_out, diff_lq1, diff_lk1, diff_lq2, diff_lk2, diff_subln_g, conv_w_pw1, conv_b_pw1, conv_w_dw, conv_b_dw, conv_ln_g, conv_ln_b, conv_w_pw2, conv_b_pw2, fox_w_in, fox_b_f, fox_w_out, mlp_w1, mlp_w2):
    B, S, D = x_prompt.shape
    Bd, T, _ = x_sample.shape
    P = cache_diff_k.shape[2]
    depth = norm_g.shape[0]
    n_p, n_s = B * S, Bd * T
    assert D == D_MODEL and S % (2 * ATT_TQ) == 0 and P % SAMPLE_TK == 0 and n_p % ROW_TILE == 0
    assert n_s % ROW_TILE == 0 and T % 8 == 0 and T >= CONV_WIDTH - 1 and S % ROW_TILE == 0

    x = jnp.concatenate([x_prompt.reshape(n_p, D), x_sample.reshape(n_s, D)], axis=0)
    zero_bias = jnp.zeros((1, D), F32)
    final_gain = final_g.reshape(1, D)

    bt_diff, bt_fox = _prompt_tiles(rel_bias, ATT_TQ)
    sb_last, sb_new, sf_new = _sample_tiles(rel_bias, T, P, SAMPLE_TK)

    diff_kt = _dims_keys(cache_diff_k)
    diff_v = cache_diff_v.reshape(cache_diff_v.shape[0], Bd, P * DIFF_HEADS, 2 * DIFF_HEAD_DIM)
    fox_kt = _dims_keys(cache_fox_k)
    fox_vt = _dims_keys(cache_fox_v)

    dk_p, dv_p, dk_s, dv_s = [], [], [], []
    cv_p, cv_s = [], []
    fk_p, fv_p, fl_p, fk_s, fv_s, fl_s = [], [], [], [], [], []
    y = None

    for i in range(depth):
        kind, j = i % N_MIXERS, i // N_MIXERS
        g1 = norm_g[i, 0].reshape(1, D)
        g2 = norm_g[i, 1].reshape(1, D)
        if kind == 0:
            lam_init = 0.8 - 0.6 * math.exp(-0.3 * i)
            q, k_p, k_s, v_p, v_s, kb, vb = _qkv_inproj(x, n_p, g1, diff_w_in[j].astype(BF16))
            lqk = jnp.stack([diff_lq1[j], diff_lk1[j], diff_lq2[j], diff_lk2[j]]).astype(F32)
            sg = diff_subln_g[j].reshape(1, 2 * DIFF_HEAD_DIM).astype(F32)
            o_p = _prompt_attn(False, B, S, q, kb, vb, bt_diff, (lqk, sg.reshape(2 * DIFF_HEAD_DIM, 1)), lam_init)
            o_s = _sample_attn(False, j, n_p, Bd, T, P, q, diff_kt, diff_v, kb, vb, sb_last, sb_new, (lqk, sg),
                               lam_init)
            wo, bo = diff_w_out[j].astype(BF16), zero_bias
            dk_p.append(k_p.reshape(B, S, DIFF_HEADS, 2, DIFF_HEAD_DIM))
            dv_p.append(v_p.reshape(B, S, DIFF_HEADS, 2 * DIFF_HEAD_DIM))
            dk_s.append(k_s.reshape(Bd, T, DIFF_HEADS, 2, DIFF_HEAD_DIM))
            dv_s.append(v_s.reshape(Bd, T, DIFF_HEADS, 2 * DIFF_HEAD_DIM))
        elif kind == 1:
            u = _conv_inproj(x, g1, conv_w_pw1[j].astype(BF16), conv_b_pw1[j].reshape(1, 2 * D))
            wdw = jnp.pad(conv_w_dw[j], ((0, HALO - CONV_WIDTH), (0, 0)))
            args = (wdw, conv_b_dw[j].reshape(1, D), conv_ln_g[j].reshape(1, D), conv_ln_b[j].reshape(1, D))
            per_tile = ROW_TILE // HALO
            o_p = _conv_dw(u, 0, n_p, ROW_TILE, u, lambda t: jnp.maximum(t * per_tile - 1, 0), S // ROW_TILE, *args)
            state = jnp.pad(state_conv[j], ((0, 0), (HALO - (CONV_WIDTH - 1), 0), (0, 0))).reshape(Bd * HALO, D)
            o_s = _conv_dw(u, n_p, n_s, T, state, lambda t: t, 0, *args)
            wo, bo = conv_w_pw2[j].astype(BF16), conv_b_pw2[j].reshape(1, D)
            keep = CONV_WIDTH - 1
            cv_p.append(jnp.stack([u[(b + 1) * S - keep:(b + 1) * S] for b in range(B)]))
            cv_s.append(u[n_p:].reshape(Bd, T, D)[:, T - keep:])
        else:
            w_in = fox_w_in[j]
            wf = jnp.pad(w_in[:, 3 * D:], ((0, 0), (0, LANES - FOX_HEADS))).astype(BF16)
            bf = jnp.pad(fox_b_f[j].astype(F32), (0, LANES - FOX_HEADS)).reshape(1, LANES)
            q, k_p, k_s, v_p, v_s, kb, vb, lf = _qkv_inproj(x, n_p, g1, w_in[:, :3 * D].astype(BF16), (wf, bf))
            lf_p = lf[:n_p].reshape(B, S, FOX_HEADS)
            lf_s = lf[n_p:].reshape(Bd, T, FOX_HEADS)
            qa_p, ka_p = _decay_cols(lf_p, False)
            decay_s = _decay_cols(jnp.concatenate([cache_fox_logf[j].astype(F32), lf_s], axis=1), True)
            o_p = _prompt_attn(True, B, S, q, kb, vb, bt_fox, (qa_p.reshape(n_p, LANES), ka_p.reshape(n_p, LANES)))
            o_s = _sample_attn(True, j, n_p, Bd, T, P, q, fox_kt, fox_vt, kb, vb, None, sf_new, decay_s)
            wo, bo = fox_w_out[j].astype(BF16), zero_bias
            fk_p.append(k_p.reshape(B, S, FOX_HEADS, FOX_HEAD_DIM))
            fv_p.append(v_p.reshape(B, S, FOX_HEADS, FOX_HEAD_DIM))
            fl_p.append(lf_p)
            fk_s.append(k_s.reshape(Bd, T, FOX_HEADS, FOX_HEAD_DIM))
            fv_s.append(v_s.reshape(Bd, T, FOX_HEADS, FOX_HEAD_DIM))
            fl_s.append(lf_s)
        final = i == depth - 1
        out = _mix_mlp(x, o_p, o_s, wo, bo, g2, mlp_w1[i].astype(BF16), mlp_w2[i].astype(BF16), final_gain, final)
        if final:
            y = out
        else:
            x = out[0]

    y_prompt = y[0].reshape(B, S, D)
    y_sample = y[1].reshape(Bd, T, D)
    return (y_prompt, y_sample, jnp.stack(dk_p), jnp.stack(dv_p), jnp.stack(cv_p), jnp.stack(fk_p),
            jnp.stack(fv_p), jnp.stack(fl_p), jnp.stack(dk_s), jnp.stack(dv_s), jnp.stack(cv_s),
            jnp.stack(fk_s), jnp.stack(fv_s), jnp.stack(fl_s))
```

```python
import functools
import math

import numpy as np
import jax
import jax.numpy as jnp
from jax import lax
from jax.experimental import pallas as pl
from jax.experimental.pallas import tpu as pltpu

F32 = jnp.float32
BF16 = jnp.bfloat16

D_MODEL = 1024
CHUNK = 64
EPS = 1e-6
DIFF_HEADS = 8
DIFF_HEAD_DIM = 64
LOG2E = math.log2(math.e)
DIFF_SCALE = DIFF_HEAD_DIM ** -0.5
CONV_WIDTH = 31
FOX_HEADS = 16
FOX_HEAD_DIM = 64
FOX_SCALE = FOX_HEAD_DIM ** -0.5
REL_BUCKETS = 32
N_MIXERS = 3

LANES = 128
SUBLANES = 8
NEG = -1e30
ROW_TILE = 512
ATT_TQ = 512
ATT_STRIP = 512
REDUCE_ROWS = 64
SAMPLE_TK = 1024
FF_CHUNK = 1024
CUM_BLOCK = 256
HALO = 32
CONV_RC = 32
CONV_LC = 256
VMEM_LIMIT = 56 * 1024 * 1024

_T5_STEPS = (12, 16, 23, 32, 46, 64, 91)


def _cparams(n_axes):
    return pltpu.CompilerParams(dimension_semantics=("arbitrary",) * n_axes, vmem_limit_bytes=VMEM_LIMIT)


def _rms(x, g):
    return x * lax.rsqrt(jnp.mean(x * x, axis=-1, keepdims=True) + EPS) * g


def _full_spec(shape):
    nd = len(shape)
    return pl.BlockSpec(shape, lambda *_: (0,) * nd)


def _split_specs(n_prompt_tiles, tm, d):
    prompt = pl.BlockSpec((tm, d), lambda i: (jnp.minimum(i, n_prompt_tiles - 1), 0))
    sample = pl.BlockSpec((tm, d), lambda i: (jnp.maximum(i - n_prompt_tiles, 0), 0))
    return prompt, sample


def _split_store(n_prompt_tiles, value, prompt_ref, sample_ref):
    i = pl.program_id(0)

    @pl.when(i < n_prompt_tiles)
    def _():
        prompt_ref[...] = value

    @pl.when(i >= n_prompt_tiles)
    def _():
        sample_ref[...] = value


def _store_new_cache(n_pt, value, prompt_ref, sample_ref):
    i = pl.program_id(0)

    @pl.when(i < n_pt)
    def _():
        if len(prompt_ref.shape) == 3:
            prompt_ref[0] = value
        else:
            heads, width = prompt_ref.shape[2:]
            for h in range(heads):
                prompt_ref[0, :, h, :] = value[:, h * width:(h + 1) * width]

    @pl.when(i >= n_pt)
    def _():
        sample_ref[...] = value


def _qkv_inproj_kernel(*refs, n_pt, scale, n_prev, forget):
    x_ref, g_ref, w_ref = refs[:3]
    refs = refs[3:]
    if forget:
        wf_ref, bf_ref = refs[:2]
        refs = refs[2:]
    refs = refs[n_prev:]
    q_ref, kp_ref, ks_ref, vp_ref, vs_ref, kb_ref, vb_ref = refs[:7]
    d = D_MODEL
    xn = _rms(x_ref[...], g_ref[...]).astype(BF16)
    q = jnp.dot(xn, w_ref[:, 0:d], preferred_element_type=F32)
    q_ref[...] = (q * (scale * LOG2E)).astype(BF16)
    k = jnp.dot(xn, w_ref[:, d:2 * d], preferred_element_type=F32)
    _store_new_cache(n_pt, k, kp_ref, ks_ref)
    kb_ref[...] = k.astype(BF16)
    v = jnp.dot(xn, w_ref[:, 2 * d:3 * d], preferred_element_type=F32)
    _store_new_cache(n_pt, v, vp_ref, vs_ref)
    vb_ref[...] = v.astype(BF16)
    if forget:
        z = jnp.dot(xn, wf_ref[...], preferred_element_type=F32) + bf_ref[...]
        lf = jnp.minimum(z, 0.0) - jnp.log1p(jnp.exp(-jnp.abs(z)))
        refs[7][...] = lf[:, 0:FOX_HEADS]


def _qkv_inproj(x, n_p, g, w, scale, layer, k_shape, v_shape, prev=(), forget=()):
    n, d = x.shape
    tm = ROW_TILE
    n_pt = n_p // tm
    row = pl.BlockSpec((tm, d), lambda i: (i, 0))
    _, sample = _split_specs(n_pt, tm, d)

    def slab(shape):
        rest = tuple(shape[2:])
        zeros = (0,) * len(rest)
        return pl.BlockSpec((1, tm) + rest, lambda i: (layer, jnp.minimum(i, n_pt - 1)) + zeros)

    f32_s = jax.ShapeDtypeStruct((n - n_p, d), F32)
    bf_all = jax.ShapeDtypeStruct((n, d), BF16)
    ins = [x, g, w] + list(forget) + list(prev)
    in_specs = ([row, _full_spec((1, d)), _full_spec(w.shape)] + [_full_spec(a.shape) for a in forget]
                + [pl.BlockSpec(memory_space=pl.ANY)] * len(prev))
    out_specs = [row, slab(k_shape), sample, slab(v_shape), sample, row, row]
    out_shape = [bf_all, jax.ShapeDtypeStruct(k_shape, F32), f32_s, jax.ShapeDtypeStruct(v_shape, F32), f32_s,
                 bf_all, bf_all]
    if forget:
        out_specs.append(pl.BlockSpec((tm, FOX_HEADS), lambda i: (i, 0)))
        out_shape.append(jax.ShapeDtypeStruct((n, FOX_HEADS), F32))
    first_prev = 3 + len(forget)
    aliases = {first_prev: 1, first_prev + 1: 3} if prev else {}
    return pl.pallas_call(
        functools.partial(_qkv_inproj_kernel, n_pt=n_pt, scale=scale, n_prev=len(prev), forget=bool(forget)),
        grid=(n // tm,),
        in_specs=in_specs,
        out_specs=out_specs,
        out_shape=out_shape,
        input_output_aliases=aliases,
        compiler_params=_cparams(1),
        name="fox_inproj" if forget else "diff_inproj",
    )(*ins)


def _conv_inproj_kernel(x_ref, g_ref, w_ref, b_ref, u_ref):
    d = D_MODEL
    xn = _rms(x_ref[...], g_ref[...]).astype(BF16)
    a = jnp.dot(xn, w_ref[:, 0:d], preferred_element_type=F32) + b_ref[:, 0:d]
    gate = jnp.dot(xn, w_ref[:, d:2 * d], preferred_element_type=F32) + b_ref[:, d:2 * d]
    u_ref[...] = a * (1.0 / (1.0 + jnp.exp(-gate)))


def _conv_inproj(x, g, w, b):
    n, d = x.shape
    tm = ROW_TILE
    row = pl.BlockSpec((tm, d), lambda i: (i, 0))
    return pl.pallas_call(
        _conv_inproj_kernel,
        grid=(n // tm,),
        in_specs=[row, _full_spec((1, d)), _full_spec(w.shape), _full_spec(b.shape)],
        out_specs=row,
        out_shape=jax.ShapeDtypeStruct((n, d), F32),
        compiler_params=_cparams(1),
        name="conv_inproj",
    )(x, g, w, b)


def _mix_mlp_kernel(x_ref, op_ref, os_ref, wo_ref, bo_ref, g_ref, w1_ref, w2_ref, gf_ref, *out_refs, n_pt, final):
    o = jnp.where(pl.program_id(0) < n_pt, op_ref[...], os_ref[...])
    x1 = x_ref[...] + jnp.dot(o, wo_ref[...], preferred_element_type=F32) + bo_ref[...]
    xn = _rms(x1, g_ref[...]).astype(BF16)
    acc = x1
    d_ff = w1_ref.shape[1]
    for c in range(d_ff // FF_CHUNK):
        h = jnp.dot(xn, w1_ref[:, c * FF_CHUNK:(c + 1) * FF_CHUNK], preferred_element_type=F32)
        h = jnp.square(jnp.maximum(h, 0.0)).astype(BF16)
        acc = acc + jnp.dot(h, w2_ref[c * FF_CHUNK:(c + 1) * FF_CHUNK, :], preferred_element_type=F32)
    if final:
        _split_store(n_pt, _rms(acc, gf_ref[...]), *out_refs)
    else:
        out_refs[0][...] = acc


def _mix_mlp(x, o_p, o_s, wo, bo, g, w1, w2, gf, final):
    n, d = x.shape
    tm = ROW_TILE
    n_p = o_p.shape[0]
    n_pt = n_p // tm
    row = pl.BlockSpec((tm, d), lambda i: (i, 0))
    prompt, sample = _split_specs(n_pt, tm, d)
    resident = lambda a: pl.BlockSpec(a.shape, lambda i: (0,) * a.ndim, pipeline_mode=pl.Buffered(1))
    if final:
        out_specs = [prompt, sample]
        out_shape = [jax.ShapeDtypeStruct((n_p, d), F32), jax.ShapeDtypeStruct((n - n_p, d), F32)]
    else:
        out_specs = [row]
        out_shape = [jax.ShapeDtypeStruct((n, d), F32)]
    return pl.pallas_call(
        functools.partial(_mix_mlp_kernel, n_pt=n_pt, final=final),
        grid=(n // tm,),
        in_specs=[row, prompt, sample, resident(wo), _full_spec(bo.shape), _full_spec(g.shape), resident(w1),
                  resident(w2), _full_spec(gf.shape)],
        out_specs=out_specs,
        out_shape=out_shape,
        compiler_params=_cparams(1),
        name="mix_mlp",
    )(x, o_p, o_s, wo, bo, g, w1, w2, gf)


def _t5_bias(tab_ref, h, qpos, kpos):
    half = REL_BUCKETS // 2
    n = qpos - kpos
    off = jnp.where(n < 0, half, 0)
    n = jnp.abs(n)
    large = jnp.full(n.shape, half // 2, jnp.int32)
    for t in _T5_STEPS:
        large = large + jnp.where(n >= t, 1, 0)
    bucket = off + jnp.where(n < half // 2, n, large)
    far = tab_ref[half - 1, h]
    out = jnp.zeros(n.shape, F32)
    for b in range(REL_BUCKETS):
        out = jnp.where(bucket == b, (tab_ref[b, h] - far) * LOG2E, out)
    visible = (kpos // CHUNK) <= (qpos // CHUNK)
    return jnp.where(visible, out, NEG)


def _prompt_tiles_kernel(tab_ref, bt_ref, cm_ref, *, tq):
    h = pl.program_id(0)
    for v in range(2):
        kpos = lax.broadcasted_iota(jnp.int32, (tq, tq), 0)
        qpos = lax.broadcasted_iota(jnp.int32, (tq, tq), 1) + (1 - v) * tq
        bt_ref[0, v] = _t5_bias(tab_ref, h, qpos, kpos)

    @pl.when(h == 0)
    def _():
        kpos = lax.broadcasted_iota(jnp.int32, (tq, tq), 0)
        qpos = lax.broadcasted_iota(jnp.int32, (tq, tq), 1)
        cm_ref[0, 0] = jnp.zeros((tq, tq), F32)
        cm_ref[0, 1] = jnp.where(kpos <= qpos, 0.0, NEG)


def _prompt_tiles(table, tq):
    return pl.pallas_call(
        functools.partial(_prompt_tiles_kernel, tq=tq),
        grid=(DIFF_HEADS,),
        in_specs=[pl.BlockSpec(memory_space=pltpu.SMEM)],
        out_specs=[pl.BlockSpec((1, 2, tq, tq), lambda h: (h, 0, 0, 0)),
                   pl.BlockSpec((1, 2, tq, tq), lambda h: (0, 0, 0, 0))],
        out_shape=[jax.ShapeDtypeStruct((DIFF_HEADS, 2, tq, tq), F32), jax.ShapeDtypeStruct((1, 2, tq, tq), F32)],
        compiler_params=_cparams(1),
        name="prompt_score_tiles",
    )(table)


def _sample_tiles_kernel(tab_ref, dl_ref, dn_ref, fn_ref, *, t_new, past, tk):
    rows = DIFF_HEADS * 2 * t_new
    for h in range(DIFF_HEADS):
        r0 = h * 2 * t_new
        q = lax.broadcasted_iota(jnp.int32, (2 * t_new, tk), 0) % t_new + past
        k = lax.broadcasted_iota(jnp.int32, (2 * t_new, tk), 1) + (past - tk)
        dl_ref[r0:r0 + 2 * t_new, :] = _t5_bias(tab_ref, h, q, k)
        q = lax.broadcasted_iota(jnp.int32, (2 * t_new, t_new), 0) % t_new + past
        k = lax.broadcasted_iota(jnp.int32, (2 * t_new, t_new), 1) + past
        dn_ref[r0:r0 + 2 * t_new, :] = _t5_bias(tab_ref, h, q, k)
    q = lax.broadcasted_iota(jnp.int32, (rows, t_new), 0) % t_new
    k = lax.broadcasted_iota(jnp.int32, (rows, t_new), 1)
    fn_ref[...] = jnp.where(k <= q, 0.0, NEG)


def _sample_tiles(table, t_new, past, tk):
    rows = DIFF_HEADS * 2 * t_new
    assert rows == FOX_HEADS * t_new
    return pl.pallas_call(
        functools.partial(_sample_tiles_kernel, t_new=t_new, past=past, tk=tk),
        in_specs=[pl.BlockSpec(memory_space=pltpu.SMEM)],
        out_shape=[jax.ShapeDtypeStruct((rows, tk), F32), jax.ShapeDtypeStruct((rows, t_new), F32),
                   jax.ShapeDtypeStruct((rows, t_new), F32)],
        name="sample_score_tiles",
    )(table)


def _lambda(lqk_ref, lam_init):
    a = jnp.sum(lqk_ref[0:1, :] * lqk_ref[1:2, :], axis=-1, keepdims=True)
    b = jnp.sum(lqk_ref[2:3, :] * lqk_ref[3:4, :], axis=-1, keepdims=True)
    return jnp.exp(a) - jnp.exp(b) + lam_init


def _scores_t(k, q2_ref, s_out):
    for c0 in range(0, q2_ref.shape[0], ATT_STRIP):
        cols = slice(c0, c0 + ATT_STRIP)
        s_out[:, cols] = lax.dot_general(k, q2_ref[cols, :], (((1,), (1,)), ((), ())), preferred_element_type=F32)


def _reduce_rows(op, x):
    w, n = x.shape
    part = op(x.reshape(w // REDUCE_ROWS, REDUCE_ROWS, n), axis=0)
    return op(part, axis=0, keepdims=True)


def _softmax_pv_t(s_in, v, bias_t, m_ref, l_ref, acc_ref):
    for c0 in range(0, s_in.shape[1], ATT_STRIP):
        cols = slice(c0, c0 + ATT_STRIP)
        s = s_in[:, cols]
        if bias_t is not None:
            b0 = c0 % bias_t.shape[1]
            s = s + bias_t[:, b0:b0 + ATT_STRIP]
        m_prev = m_ref[:, cols]
        m_new = jnp.maximum(m_prev, _reduce_rows(jnp.max, s))
        alpha = jnp.exp2(m_prev - m_new)
        p = jnp.exp2(s - m_new)
        l_ref[:, cols] = alpha * l_ref[:, cols] + _reduce_rows(jnp.sum, p)
        pv = lax.dot_general(v, p.astype(BF16), (((0,), (0,)), ((), ())), preferred_element_type=F32)
        acc_ref[:, cols] = alpha * acc_ref[:, cols] + pv
        m_ref[:, cols] = m_new


def _prompt_attn_kernel(*refs, fox, tq, seq, lam_init):
    if fox:
        (q_ref, qa_ref, k_ref, ka_ref, v_ref, bt_ref, o_ref, q2_ref, s0_ref, s1_ref, m_ref, l_ref, acc_ref) = refs
    else:
        (q_ref, k_ref, v_ref, bt_ref, lqk_ref, sg_ref, o_ref, q2_ref, s0_ref, s1_ref, m_ref, l_ref, acc_ref) = refs
    s_refs = (s0_ref, s1_ref)
    lane = lax.broadcasted_iota(jnp.int32, (tq, LANES), 1)
    lo = lane < LANES // 2
    first_rows = lax.broadcasted_iota(jnp.int32, (LANES, tq), 0) < LANES // 2
    if fox:
        c0 = 12 * pl.program_id(1)
        in_a = jnp.logical_and(lane >= c0, lane < c0 + 6)
        in_b = jnp.logical_and(lane >= c0 + 6, lane < c0 + 12)

    def scores(j, slot):
        k0 = pl.multiple_of(jnp.maximum(j, 0) * tq, tq)
        k = k_ref[pl.ds(k0, tq), :]
        if fox:
            k = jnp.concatenate([k, ka_ref[pl.ds(k0, tq), :]], axis=1)
        _scores_t(k, q2_ref, s_refs[slot])

    def softmax_pv(j, slot, bias_t):
        k0 = pl.multiple_of(j * tq, tq)
        _softmax_pv_t(s_refs[slot], v_ref[pl.ds(k0, tq), :], bias_t, m_ref, l_ref, acc_ref)

    def q_block(i, carry):
        r0 = pl.multiple_of(i * tq, tq)
        q = q_ref[pl.ds(r0, tq), :]
        zero = jnp.zeros_like(q)
        q2_ref[0:tq, 0:LANES] = jnp.where(lo, q, zero)
        q2_ref[tq:2 * tq, 0:LANES] = jnp.where(lo, zero, q)
        if fox:
            qa = qa_ref[pl.ds(r0, tq), :]
            q2_ref[0:tq, LANES:2 * LANES] = jnp.where(in_a, qa, zero)
            q2_ref[tq:2 * tq, LANES:2 * LANES] = jnp.where(in_b, qa, zero)
        m_ref[...] = jnp.full(m_ref.shape, NEG, F32)
        l_ref[...] = jnp.zeros(l_ref.shape, F32)
        acc_ref[...] = jnp.zeros(acc_ref.shape, F32)

        scores(i, 0)

        @pl.when(i == 0)
        def _():
            softmax_pv(i, 0, bt_ref.at[0, 1])

        @pl.when(i >= 1)
        def _():
            scores(i - 1, 1)
            softmax_pv(i, 0, bt_ref.at[0, 1])
            scores(i - 2, 0)
            softmax_pv(i - 1, 1, bt_ref.at[0, 0])
            n_far = i - 1

            def pair(u, c):
                a = i - 2 - 2 * u
                scores(a - 1, 1)
                softmax_pv(a, 0, None)
                scores(a - 2, 0)
                softmax_pv(a - 1, 1, None)
                return c

            lax.fori_loop(0, n_far // 2, pair, 0)

            @pl.when(n_far % 2 == 1)
            def _():
                softmax_pv(0, 0, None)

        top = acc_ref[:, 0:tq] / l_ref[:, 0:tq]
        bot = acc_ref[:, tq:2 * tq] / l_ref[:, tq:2 * tq]
        if fox:
            o_t = jnp.where(first_rows, top, bot)
        else:
            o_t = top - _lambda(lqk_ref, lam_init) * bot
            ms = jnp.mean(o_t * o_t, axis=0, keepdims=True)
            o_t = o_t * lax.rsqrt(ms + EPS) * sg_ref[...] * (1.0 - lam_init)
        o_ref[pl.ds(r0, tq), :] = o_t.T.astype(BF16)
        return carry

    lax.fori_loop(0, seq // tq, q_block, 0)


def _prompt_attn(fox, batch, seq, q, k, v, bt, extra, lam_init=0.0):
    tq = ATT_TQ
    groups = D_MODEL // LANES
    blk = pl.BlockSpec((seq, LANES), lambda b, h: (b, h))
    if fox:
        qa, ka = extra
        aug = pl.BlockSpec((seq, LANES), lambda b, h: (b, 0))
        ins = [q, qa, k, ka, v, bt]
        in_specs = [blk, aug, blk, aug, blk, pl.BlockSpec((1, 2, tq, tq), lambda b, h: (0, 0, 0, 0))]
        kd = 2 * LANES
    else:
        lqk, sg = extra
        ins = [q, k, v, bt, lqk, sg]
        in_specs = [blk, blk, blk, pl.BlockSpec((1, 2, tq, tq), lambda b, h: (h, 0, 0, 0)),
                    _full_spec(lqk.shape), _full_spec(sg.shape)]
        kd = LANES
    return pl.pallas_call(
        functools.partial(_prompt_attn_kernel, fox=fox, tq=tq, seq=seq, lam_init=lam_init),
        grid=(batch, groups),
        in_specs=in_specs,
        out_specs=blk,
        out_shape=jax.ShapeDtypeStruct((batch * seq, D_MODEL), BF16),
        scratch_shapes=[pltpu.VMEM((2 * tq, kd), BF16), pltpu.VMEM((tq, 2 * tq), F32), pltpu.VMEM((tq, 2 * tq), F32),
                        pltpu.VMEM((1, 2 * tq), F32), pltpu.VMEM((1, 2 * tq), F32), pltpu.VMEM((LANES, 2 * tq), F32)],
        compiler_params=_cparams(2),
        name="fox_prompt_attn" if fox else "diff_prompt_attn",
    )(*ins)


def _online_update(s, pv_fn, m_ref, l_ref, acc_ref):
    m_prev = m_ref[...]
    m_new = jnp.maximum(m_prev, jnp.max(s, axis=-1, keepdims=True))
    alpha = jnp.exp2(m_prev - m_new)
    p = jnp.exp2(s - m_new)
    l_ref[...] = alpha * l_ref[...] + jnp.sum(p, axis=-1, keepdims=True)
    acc_ref[...] = alpha * acc_ref[...] + pv_fn(p.astype(BF16))
    m_ref[...] = m_new


def _sample_attn_kernel(*refs, fox, t_new, tk, n_tiles, lam_init):
    if fox:
        (q_ref, qa_ref, kt_ref, kat_ref, vt_ref, kn_ref, kan_ref, vn_ref, bn_ref,
         o_ref, qbd_ref, m_ref, l_ref, acc_ref) = refs
    else:
        (q_ref, kt_ref, v_ref, kn_ref, vn_ref, bl_ref, bn_ref, lqk_ref, sg_ref,
         o_ref, qbd_ref, m_ref, l_ref, acc_ref) = refs
    t = pl.program_id(1)
    n_blocks = D_MODEL // 64
    rows_per_head = 2 * t_new

    @pl.when(t == 0)
    def _():
        q = q_ref[...]
        zero = jnp.zeros_like(q)
        cb = lax.broadcasted_iota(jnp.int32, q.shape, 1) // 64
        if fox:
            qa = qa_ref[0]
            la = lax.broadcasted_iota(jnp.int32, qa.shape, 1)
        for rb in range(n_blocks):
            qbd_ref[rb * t_new:(rb + 1) * t_new, 0:D_MODEL] = jnp.where(cb == rb, q, zero)
            if fox:
                sel = jnp.logical_and(la >= 6 * rb, la < 6 * rb + 6)
                qbd_ref[rb * t_new:(rb + 1) * t_new, D_MODEL:D_MODEL + LANES] = jnp.where(sel, qa, jnp.zeros_like(qa))
        m_ref[...] = jnp.full(m_ref.shape, NEG, F32)
        l_ref[...] = jnp.zeros(l_ref.shape, F32)
        acc_ref[...] = jnp.zeros(acc_ref.shape, F32)

    def cache_scores():
        kt = kt_ref[0, 0].astype(BF16)
        if fox:
            kt = jnp.concatenate([kt, kat_ref[0]], axis=0)
        return jnp.dot(qbd_ref[...], kt, preferred_element_type=F32)

    if fox:
        def cache_pv(p):
            return lax.dot_general(p, vt_ref[0, 0].astype(BF16), (((1,), (1,)), ((), ())), preferred_element_type=F32)

        def new_pv(p):
            return jnp.dot(p, vn_ref[...], preferred_element_type=F32)
    else:
        def cache_pv(p):
            return jnp.concatenate(
                [jnp.dot(p[h * rows_per_head:(h + 1) * rows_per_head, :],
                         v_ref[0, 0, pl.ds(h, tk, stride=DIFF_HEADS), :].astype(BF16), preferred_element_type=F32)
                 for h in range(DIFF_HEADS)], axis=0)

        def new_pv(p):
            return jnp.concatenate(
                [jnp.dot(p[h * rows_per_head:(h + 1) * rows_per_head, :], vn_ref[:, h * LANES:(h + 1) * LANES],
                         preferred_element_type=F32) for h in range(DIFF_HEADS)], axis=0)

    @pl.when(t < n_tiles - 1)
    def _():
        _online_update(cache_scores(), cache_pv, m_ref, l_ref, acc_ref)

    @pl.when(t == n_tiles - 1)
    def _():
        s = cache_scores()
        if not fox:
            s = s + bl_ref[...]
        _online_update(s, cache_pv, m_ref, l_ref, acc_ref)
        kn = kn_ref[...]
        if fox:
            kn = jnp.concatenate([kn, kan_ref[0]], axis=1)
        s = lax.dot_general(qbd_ref[...], kn, (((1,), (1,)), ((), ())), preferred_element_type=F32) + bn_ref[...]
        _online_update(s, new_pv, m_ref, l_ref, acc_ref)

        lane = lax.broadcasted_iota(jnp.int32, (t_new, LANES), 1)
        lo = lane < LANES // 2
        for g in range(D_MODEL // LANES):
            ra, rb = 2 * g * t_new, (2 * g + 1) * t_new
            cols = slice(g * LANES, (g + 1) * LANES)
            acc_cols = cols if fox else slice(0, LANES)
            top = acc_ref[ra:ra + t_new, acc_cols] / l_ref[ra:ra + t_new, :]
            bot = acc_ref[rb:rb + t_new, acc_cols] / l_ref[rb:rb + t_new, :]
            if fox:
                o = jnp.where(lo, top, bot)
            else:
                o = _rms(top - _lambda(lqk_ref, lam_init) * bot, sg_ref[...]) * (1.0 - lam_init)
            o_ref[:, cols] = o.astype(BF16)


def _sample_attn(fox, layer, n_prompt, dec_batch, t_new, past, q, kt, v, kn, vn, bias_last, bias_new, extra,
                 lam_init=0.0):
    tk = SAMPLE_TK
    n_tiles = past // tk
    d = D_MODEL
    rows = (d // 64) * t_new
    s0 = n_prompt // t_new
    new = pl.BlockSpec((t_new, d), lambda b, t: (s0 + b, 0))
    dims_keys = pl.BlockSpec((1, 1, d, tk), lambda b, t: (layer, b, 0, t))
    out = pl.BlockSpec((t_new, d), lambda b, t: (b, 0))
    if fox:
        qa, ka, kat = extra
        new_aug = pl.BlockSpec((1, t_new, LANES), lambda b, t: (b, past // t_new, 0))
        ins = [q, qa, kt, kat, v, kn, ka, vn, bias_new]
        in_specs = [new, new_aug, dims_keys, pl.BlockSpec((1, LANES, tk), lambda b, t: (b, 0, t)), dims_keys,
                    new, new_aug, new, _full_spec(bias_new.shape)]
        kd, acc_w = d + LANES, d
    else:
        lqk, sg = extra
        ins = [q, kt, v, kn, vn, bias_last, bias_new, lqk, sg]
        in_specs = [new, dims_keys, pl.BlockSpec((1, 1, tk * DIFF_HEADS, LANES), lambda b, t: (layer, b, t, 0)),
                    new, new, _full_spec(bias_last.shape), _full_spec(bias_new.shape),
                    _full_spec(lqk.shape), _full_spec(sg.shape)]
        kd, acc_w = d, LANES
    return pl.pallas_call(
        functools.partial(_sample_attn_kernel, fox=fox, t_new=t_new, tk=tk, n_tiles=n_tiles, lam_init=lam_init),
        grid=(dec_batch, n_tiles),
        in_specs=in_specs,
        out_specs=out,
        out_shape=jax.ShapeDtypeStruct((dec_batch * t_new, d), BF16),
        scratch_shapes=[pltpu.VMEM((rows, kd), BF16), pltpu.VMEM((rows, 1), F32),
                        pltpu.VMEM((rows, 1), F32), pltpu.VMEM((rows, acc_w), F32)],
        compiler_params=_cparams(2),
        name="fox_sample_attn" if fox else "diff_sample_attn",
    )(*ins)


def _split3(c):
    hi = c.astype(BF16)
    r = c - hi.astype(F32)
    mid = r.astype(BF16)
    lo = (r - mid.astype(F32)).astype(BF16)
    return hi, mid, lo


def _decay_cols_kernel(lf_ref, eq_ref, ek_ref, ekt_ref, qa_ref, ka_ref, *maybe_kat_ref, length):
    h = FOX_HEADS
    carry = jnp.zeros((1, h), F32)
    for r in range(0, length, CUM_BLOCK):
        n = min(CUM_BLOCK, length - r)
        tri = (lax.broadcasted_iota(jnp.int32, (n, n), 1) <= lax.broadcasted_iota(jnp.int32, (n, n), 0))
        tri = jnp.where(tri, 1.0, 0.0).astype(BF16)
        parts = _split3(lf_ref[0, r:r + n, :])
        c = carry
        for p in parts:
            c = c + jnp.dot(tri, p, preferred_element_type=F32)
        carry = c[n - 1:n, :]
        chi, cmid, clo = _split3(c * LOG2E)
        ones = jnp.ones((n, h), BF16)
        qa = jnp.zeros((n, LANES), F32)
        ka = jnp.zeros((n, LANES), F32)
        kat = jnp.zeros((LANES, n), F32)
        for j, (pq, pk) in enumerate(((chi, -chi), (cmid, -cmid), (clo, -clo), (ones, ones))):
            qa = qa + jnp.dot(pq, eq_ref[j * h:(j + 1) * h, :], preferred_element_type=F32)
            ka = ka + jnp.dot(pk, ek_ref[j * h:(j + 1) * h, :], preferred_element_type=F32)
            if maybe_kat_ref:
                kat = kat + lax.dot_general(ekt_ref[:, j * h:(j + 1) * h], pk, (((1,), (1,)), ((), ())),
                                            preferred_element_type=F32)
        qa_ref[0, r:r + n, :] = qa.astype(BF16)
        ka_ref[0, r:r + n, :] = ka.astype(BF16)
        if maybe_kat_ref:
            maybe_kat_ref[0][0, :, r:r + n] = kat.astype(BF16)


def _placement():
    h = FOX_HEADS
    eq = np.zeros((4 * h, LANES), np.float32)
    ek = np.zeros((4 * h, LANES), np.float32)
    for head in range(h):
        for part in range(3):
            eq[part * h + head, 6 * head + part] = 1.0
            ek[part * h + head, 6 * head + 3 + part] = 1.0
            eq[3 * h + head, 6 * head + 3 + part] = 1.0
            ek[3 * h + head, 6 * head + part] = 1.0
    return jnp.asarray(eq, BF16), jnp.asarray(ek, BF16), jnp.asarray(ek.T, BF16)


def _decay_cols(lf, keys_transposed):
    nb, length, h = lf.shape
    eq, ek, ekt = _placement()
    spec = pl.BlockSpec((1, length, LANES), lambda b: (b, 0, 0))
    out_specs = [spec, spec]
    out_shape = [jax.ShapeDtypeStruct((nb, length, LANES), BF16)] * 2
    if keys_transposed:
        out_specs.append(pl.BlockSpec((1, LANES, length), lambda b: (b, 0, 0)))
        out_shape.append(jax.ShapeDtypeStruct((nb, LANES, length), BF16))
    return pl.pallas_call(
        functools.partial(_decay_cols_kernel, length=length),
        grid=(nb,),
        in_specs=[pl.BlockSpec((1, length, h), lambda b: (b, 0, 0)), _full_spec(eq.shape), _full_spec(ek.shape),
                  _full_spec(ekt.shape)],
        out_specs=out_specs,
        out_shape=out_shape,
        compiler_params=_cparams(1),
        name="fox_decay_cols",
    )(lf, eq, ek, ekt)


def _conv_dw_kernel(u_ref, halo_ref, wdw_ref, bdw_ref, g_ref, b_ref, o_ref, ext_ref, sh_ref, y_ref, *, tm,
                    steps_per_seq):
    halo = halo_ref[...]
    if steps_per_seq:
        halo = jnp.where(pl.program_id(0) % steps_per_seq == 0, 0.0, halo)
    ext_ref[0:HALO, :] = halo
    ext_ref[HALO:HALO + tm, :] = u_ref[...]
    span = sh_ref.shape[1]
    for s in range(1, SUBLANES):
        sh_ref[s - 1] = ext_ref[s:s + span, :]
    d = u_ref.shape[1]
    first = HALO - (CONV_WIDTH - 1)
    for rc in range(tm // CONV_RC):
        for lc in range(d // CONV_LC):
            cols = slice(lc * CONV_LC, (lc + 1) * CONV_LC)
            acc = jnp.zeros((CONV_RC, CONV_LC), F32)
            for w in range(CONV_WIDTH):
                a, s = divmod(first + w, SUBLANES)
                src = ext_ref if s == 0 else sh_ref.at[s - 1]
                r = rc * CONV_RC + a * SUBLANES
                acc = acc + src[r:r + CONV_RC, cols] * wdw_ref[w:w + 1, cols]
            y_ref[rc * CONV_RC:(rc + 1) * CONV_RC, cols] = acc
    y = y_ref[...] + bdw_ref[...]
    yc = y - jnp.mean(y, axis=-1, keepdims=True)
    yn = yc * lax.rsqrt(jnp.mean(yc * yc, axis=-1, keepdims=True) + EPS) * g_ref[...] + b_ref[...]
    o_ref[...] = (yn * (1.0 / (1.0 + jnp.exp(-yn)))).astype(BF16)


def _conv_dw(u, row0, n_rows, tm, halo, halo_index, steps_per_seq, wdw, bdw, g, b):
    d = u.shape[1]
    b0 = row0 // tm
    return pl.pallas_call(
        functools.partial(_conv_dw_kernel, tm=tm, steps_per_seq=steps_per_seq),
        grid=(n_rows // tm,),
        in_specs=[pl.BlockSpec((tm, d), lambda i: (b0 + i, 0)), pl.BlockSpec((HALO, d), lambda i: (halo_index(i), 0)),
                  _full_spec(wdw.shape), _full_spec(bdw.shape), _full_spec(g.shape), _full_spec(b.shape)],
        out_specs=pl.BlockSpec((tm, d), lambda i: (i, 0)),
        out_shape=jax.ShapeDtypeStruct((n_rows, d), BF16),
        scratch_shapes=[pltpu.VMEM((HALO + tm, d), F32), pltpu.VMEM((SUBLANES - 1, HALO + tm - SUBLANES, d), F32),
                        pltpu.VMEM((tm, d), F32)],
        compiler_params=_cparams(1),
        name="conv_dw",
    )(u, halo, wdw, bdw, g, b)


def _dims_keys(cache):
    nd = cache.ndim
    t = jnp.transpose(cache, (0, 1) + tuple(range(3, nd)) + (2,))
    return t.reshape(cache.shape[0], cache.shape[1], D_MODEL, cache.shape[2])


def kernel(x_prompt, x_sample, cache_diff_k, cache_diff_v, state_conv, cache_fox_k, cache_fox_v, cache_fox_logf, rel_bias, norm_g, final_g, diff_w_in, diff_w_out, diff_lq1, diff_lk1, diff_lq2, diff_lk2, diff_subln_g, conv_w_pw1, conv_b_pw1, conv_w_dw, conv_b_dw, conv_ln_g, conv_ln_b, conv_w_pw2, conv_b_pw2, fox_w_in, fox_b_f, fox_w_out, mlp_w1, mlp_w2):
    B, S, D = x_prompt.shape
    Bd, T, _ = x_sample.shape
    P = cache_diff_k.shape[2]
    depth = norm_g.shape[0]
    n_p, n_s = B * S, Bd * T
    assert D == D_MODEL and S % (2 * ATT_TQ) == 0 and P % SAMPLE_TK == 0 and n_p % ROW_TILE == 0
    assert n_s % ROW_TILE == 0 and T % 8 == 0 and T >= CONV_WIDTH - 1 and S % ROW_TILE == 0

    x = jnp.concatenate([x_prompt.reshape(n_p, D), x_sample.reshape(n_s, D)], axis=0)
    zero_bias = jnp.zeros((1, D), F32)
    final_gain = final_g.reshape(1, D)

    bt_diff, bt_fox = _prompt_tiles(rel_bias, ATT_TQ)
    sb_last, sb_new, sf_new = _sample_tiles(rel_bias, T, P, SAMPLE_TK)

    diff_kt = _dims_keys(cache_diff_k)
    diff_v = cache_diff_v.reshape(cache_diff_v.shape[0], Bd, P * DIFF_HEADS, 2 * DIFF_HEAD_DIM)
    fox_kt = _dims_keys(cache_fox_k)
    fox_vt = _dims_keys(cache_fox_v)

    n_diff, n_fox = diff_w_in.shape[0], fox_w_in.shape[0]
    dk_shape = (n_diff, n_p, D)
    dv_shape = (n_diff, n_p, DIFF_HEADS, 2 * DIFF_HEAD_DIM)
    fkv_shape = (n_fox, n_p, FOX_HEADS, FOX_HEAD_DIM)
    diff_new, fox_new = (), ()
    dk_s, dv_s = [], []
    cv_p, cv_s = [], []
    fl_p, fk_s, fv_s, fl_s = [], [], [], []
    y = None

    for i in range(depth):
        kind, j = i % N_MIXERS, i // N_MIXERS
        g1 = norm_g[i, 0].reshape(1, D)
        g2 = norm_g[i, 1].reshape(1, D)
        if kind == 0:
            lam_init = 0.8 - 0.6 * math.exp(-0.3 * i)
            q, dk_new, k_s, dv_new, v_s, kb, vb = _qkv_inproj(x, n_p, g1, diff_w_in[j].astype(BF16), DIFF_SCALE, j,
                                                              dk_shape, dv_shape, prev=diff_new)
            diff_new = (dk_new, dv_new)
            lqk = jnp.stack([diff_lq1[j], diff_lk1[j], diff_lq2[j], diff_lk2[j]]).astype(F32)
            sg = diff_subln_g[j].reshape(1, 2 * DIFF_HEAD_DIM).astype(F32)
            o_p = _prompt_attn(False, B, S, q, kb, vb, bt_diff, (lqk, sg.reshape(2 * DIFF_HEAD_DIM, 1)), lam_init)
            o_s = _sample_attn(False, j, n_p, Bd, T, P, q, diff_kt, diff_v, kb, vb, sb_last, sb_new, (lqk, sg),
                               lam_init)
            wo, bo = diff_w_out[j].astype(BF16), zero_bias
            dk_s.append(k_s.reshape(Bd, T, DIFF_HEADS, 2, DIFF_HEAD_DIM))
            dv_s.append(v_s.reshape(Bd, T, DIFF_HEADS, 2 * DIFF_HEAD_DIM))
        elif kind == 1:
            u = _conv_inproj(x, g1, conv_w_pw1[j].astype(BF16), conv_b_pw1[j].reshape(1, 2 * D))
            wdw = jnp.pad(conv_w_dw[j], ((0, HALO - CONV_WIDTH), (0, 0)))
            args = (wdw, conv_b_dw[j].reshape(1, D), conv_ln_g[j].reshape(1, D), conv_ln_b[j].reshape(1, D))
            per_tile = ROW_TILE // HALO
            o_p = _conv_dw(u, 0, n_p, ROW_TILE, u, lambda t: jnp.maximum(t * per_tile - 1, 0), S // ROW_TILE, *args)
            state = jnp.pad(state_conv[j], ((0, 0), (HALO - (CONV_WIDTH - 1), 0), (0, 0))).reshape(Bd * HALO, D)
            o_s = _conv_dw(u, n_p, n_s, T, state, lambda t: t, 0, *args)
            wo, bo = conv_w_pw2[j].astype(BF16), conv_b_pw2[j].reshape(1, D)
            keep = CONV_WIDTH - 1
            cv_p.append(jnp.stack([u[(b + 1) * S - keep:(b + 1) * S] for b in range(B)]))
            cv_s.append(u[n_p:].reshape(Bd, T, D)[:, T - keep:])
        else:
            w_in = fox_w_in[j]
            wf = jnp.pad(w_in[:, 3 * D:], ((0, 0), (0, LANES - FOX_HEADS))).astype(BF16)
            bf = jnp.pad(fox_b_f[j].astype(F32), (0, LANES - FOX_HEADS)).reshape(1, LANES)
            q, fk_new, k_s, fv_new, v_s, kb, vb, lf = _qkv_inproj(x, n_p, g1, w_in[:, :3 * D].astype(BF16), FOX_SCALE, j,
                                                                  fkv_shape, fkv_shape, prev=fox_new, forget=(wf, bf))
            fox_new = (fk_new, fv_new)
            lf_p = lf[:n_p].reshape(B, S, FOX_HEADS)
            lf_s = lf[n_p:].reshape(Bd, T, FOX_HEADS)
            qa_p, ka_p = _decay_cols(lf_p, False)
            decay_s = _decay_cols(jnp.concatenate([cache_fox_logf[j].astype(F32), lf_s], axis=1), True)
            o_p = _prompt_attn(True, B, S, q, kb, vb, bt_fox, (qa_p.reshape(n_p, LANES), ka_p.reshape(n_p, LANES)))
            o_s = _sample_attn(True, j, n_p, Bd, T, P, q, fox_kt, fox_vt, kb, vb, None, sf_new, decay_s)
            wo, bo = fox_w_out[j].astype(BF16), zero_bias
            fl_p.append(lf_p)
            fk_s.append(k_s.reshape(Bd, T, FOX_HEADS, FOX_HEAD_DIM))
            fv_s.append(v_s.reshape(Bd, T, FOX_HEADS, FOX_HEAD_DIM))
            fl_s.append(lf_s)
        final = i == depth - 1
        out = _mix_mlp(x, o_p, o_s, wo, bo, g2, mlp_w1[i].astype(BF16), mlp_w2[i].astype(BF16), final_gain, final)
        if final:
            y = out
        else:
            x = out[0]

    y_prompt = y[0].reshape(B, S, D)
    y_sample = y[1].reshape(Bd, T, D)
    new_diff_k_p = diff_new[0].reshape(n_diff, B, S, DIFF_HEADS, 2, DIFF_HEAD_DIM)
    new_diff_v_p = diff_new[1].reshape(n_diff, B, S, DIFF_HEADS, 2 * DIFF_HEAD_DIM)
    new_fox_k_p = fox_new[0].reshape(n_fox, B, S, FOX_HEADS, FOX_HEAD_DIM)
    new_fox_v_p = fox_new[1].reshape(n_fox, B, S, FOX_HEADS, FOX_HEAD_DIM)
    return (y_prompt, y_sample, new_diff_k_p, new_diff_v_p, jnp.stack(cv_p), new_fox_k_p,
            new_fox_v_p, jnp.stack(fl_p), jnp.stack(dk_s), jnp.stack(dv_s), jnp.stack(cv_s),
            jnp.stack(fk_s), jnp.stack(fv_s), jnp.stack(fl_s))
```

```python
import functools
import math

import numpy as np
import jax
import jax.numpy as jnp
from jax import lax
from jax.experimental import pallas as pl
from jax.experimental.pallas import tpu as pltpu

F32 = jnp.float32
BF16 = jnp.bfloat16

D_MODEL = 1024
CHUNK = 64
EPS = 1e-6
DIFF_HEADS = 8
DIFF_HEAD_DIM = 64
LOG2E = math.log2(math.e)
DIFF_SCALE = DIFF_HEAD_DIM ** -0.5
CONV_WIDTH = 31
FOX_HEADS = 16
FOX_HEAD_DIM = 64
FOX_SCALE = FOX_HEAD_DIM ** -0.5
REL_BUCKETS = 32
N_MIXERS = 3

LANES = 128
SUBLANES = 8
NEG = -1e30
ROW_TILE = 512
ATT_TQ = 512
ATT_STRIP = 512
REDUCE_ROWS = 64
SAMPLE_TK = 1024
SAMPLE_GROUP = 256
FF_CHUNK = 1024
CUM_BLOCK = 256
HALO = 32
CONV_RC = 32
CONV_LC = 256
VMEM_LIMIT = 56 * 1024 * 1024

_T5_STEPS = (12, 16, 23, 32, 46, 64, 91)


def _cparams(n_axes):
    return pltpu.CompilerParams(dimension_semantics=("arbitrary",) * n_axes, vmem_limit_bytes=VMEM_LIMIT)


def _rms(x, g):
    return x * lax.rsqrt(jnp.mean(x * x, axis=-1, keepdims=True) + EPS) * g


def _full_spec(shape):
    nd = len(shape)
    return pl.BlockSpec(shape, lambda *_: (0,) * nd)


def _split_specs(n_prompt_tiles, tm, d):
    prompt = pl.BlockSpec((tm, d), lambda i: (jnp.minimum(i, n_prompt_tiles - 1), 0))
    sample = pl.BlockSpec((tm, d), lambda i: (jnp.maximum(i - n_prompt_tiles, 0), 0))
    return prompt, sample


def _split_store(n_prompt_tiles, value, prompt_ref, sample_ref):
    i = pl.program_id(0)

    @pl.when(i < n_prompt_tiles)
    def _():
        prompt_ref[...] = value

    @pl.when(i >= n_prompt_tiles)
    def _():
        sample_ref[...] = value


def _store_new_cache(n_pt, value, prompt_ref, sample_ref, rotated_fn=None):
    i = pl.program_id(0)

    @pl.when(i < n_pt)
    def _():
        if len(prompt_ref.shape) == 3:
            prompt_ref[0] = value
            return
        heads, width = prompt_ref.shape[2:]
        if width == LANES:
            for h in range(heads):
                prompt_ref[0, :, h, :] = value[:, h * width:(h + 1) * width]
        else:
            rotated = rotated_fn()
            for h in range(heads):
                src = value if h % 2 == 0 else rotated
                prompt_ref[0, :, h, :] = src[:, (h // 2) * LANES:(h // 2) * LANES + width]

    @pl.when(i >= n_pt)
    def _():
        sample_ref[...] = value


def _qkv_inproj_kernel(*refs, n_pt, scale, n_prev, forget):
    x_ref, g_ref, w_ref = refs[:3]
    refs = refs[3:]
    if forget:
        wf_ref, bf_ref = refs[:2]
        refs = refs[2:]
    refs = refs[n_prev:]
    q_ref, kp_ref, ks_ref, vp_ref, vs_ref, kb_ref, vb_ref = refs[:7]
    d = D_MODEL
    xn = _rms(x_ref[...], g_ref[...]).astype(BF16)
    q = jnp.dot(xn, w_ref[:, 0:d], preferred_element_type=F32)
    q_ref[...] = (q * (scale * LOG2E)).astype(BF16)
    k = jnp.dot(xn, w_ref[:, d:2 * d], preferred_element_type=F32)
    _store_new_cache(n_pt, k, kp_ref, ks_ref, lambda: jnp.dot(xn, w_ref[:, 3 * d:4 * d], preferred_element_type=F32))
    kb_ref[...] = k.astype(BF16)
    v = jnp.dot(xn, w_ref[:, 2 * d:3 * d], preferred_element_type=F32)
    _store_new_cache(n_pt, v, vp_ref, vs_ref, lambda: jnp.dot(xn, w_ref[:, 4 * d:5 * d], preferred_element_type=F32))
    vb_ref[...] = v.astype(BF16)
    if forget:
        z = jnp.dot(xn, wf_ref[...], preferred_element_type=F32) + bf_ref[...]
        lf = jnp.minimum(z, 0.0) - jnp.log1p(jnp.exp(-jnp.abs(z)))
        refs[7][...] = lf[:, 0:FOX_HEADS]


def _qkv_inproj(x, n_p, g, w, scale, k_slab, v_slab, prev=None, forget=()):
    n, d = x.shape
    tm = ROW_TILE
    n_pt = n_p // tm
    row = pl.BlockSpec((tm, d), lambda i: (i, 0))
    _, sample = _split_specs(n_pt, tm, d)
    prev = prev or {}

    def slab(shape, layer):
        rest = tuple(shape[2:])
        zeros = (0,) * len(rest)
        return pl.BlockSpec((1, tm) + rest, lambda i: (layer, jnp.minimum(i, n_pt - 1)) + zeros)

    f32_s = jax.ShapeDtypeStruct((n - n_p, d), F32)
    bf_all = jax.ShapeDtypeStruct((n, d), BF16)
    ins = [x, g, w] + list(forget) + list(prev.values())
    w_spec = pl.BlockSpec(w.shape, lambda i: (0, 0), pipeline_mode=pl.Buffered(1))
    in_specs = ([row, _full_spec((1, d)), w_spec] + [_full_spec(a.shape) for a in forget]
                + [pl.BlockSpec(memory_space=pl.ANY)] * len(prev))
    out_specs = [row, slab(*k_slab), sample, slab(*v_slab), sample, row, row]
    out_shape = [bf_all, jax.ShapeDtypeStruct(k_slab[0], F32), f32_s, jax.ShapeDtypeStruct(v_slab[0], F32), f32_s,
                 bf_all, bf_all]
    if forget:
        out_specs.append(pl.BlockSpec((tm, FOX_HEADS), lambda i: (i, 0)))
        out_shape.append(jax.ShapeDtypeStruct((n, FOX_HEADS), F32))
    first_prev = 3 + len(forget)
    aliases = {first_prev + pos: out_index for pos, out_index in enumerate(prev)}
    return pl.pallas_call(
        functools.partial(_qkv_inproj_kernel, n_pt=n_pt, scale=scale, n_prev=len(prev), forget=bool(forget)),
        grid=(n // tm,),
        in_specs=in_specs,
        out_specs=out_specs,
        out_shape=out_shape,
        input_output_aliases=aliases,
        compiler_params=_cparams(1),
        name="fox_inproj" if forget else "diff_inproj",
    )(*ins)


def _conv_inproj_kernel(x_ref, g_ref, w_ref, b_ref, u_ref):
    d = D_MODEL
    xn = _rms(x_ref[...], g_ref[...]).astype(BF16)
    a = jnp.dot(xn, w_ref[:, 0:d], preferred_element_type=F32) + b_ref[:, 0:d]
    gate = jnp.dot(xn, w_ref[:, d:2 * d], preferred_element_type=F32) + b_ref[:, d:2 * d]
    u_ref[...] = a * (1.0 / (1.0 + jnp.exp(-gate)))


def _conv_inproj(x, g, w, b):
    n, d = x.shape
    tm = ROW_TILE
    row = pl.BlockSpec((tm, d), lambda i: (i, 0))
    return pl.pallas_call(
        _conv_inproj_kernel,
        grid=(n // tm,),
        in_specs=[row, _full_spec((1, d)), _full_spec(w.shape), _full_spec(b.shape)],
        out_specs=row,
        out_shape=jax.ShapeDtypeStruct((n, d), F32),
        compiler_params=_cparams(1),
        name="conv_inproj",
    )(x, g, w, b)


def _mix_mlp_kernel(x_ref, op_ref, os_ref, wo_ref, bo_ref, g_ref, w1_ref, w2_ref, gf_ref, *out_refs, n_pt, final):
    o = jnp.where(pl.program_id(0) < n_pt, op_ref[...], os_ref[...])
    x1 = x_ref[...] + jnp.dot(o, wo_ref[...], preferred_element_type=F32) + bo_ref[...]
    xn = _rms(x1, g_ref[...]).astype(BF16)
    acc = x1
    d_ff = w1_ref.shape[1]
    for c in range(d_ff // FF_CHUNK):
        h = jnp.dot(xn, w1_ref[:, c * FF_CHUNK:(c + 1) * FF_CHUNK], preferred_element_type=F32)
        h = jnp.square(jnp.maximum(h, 0.0)).astype(BF16)
        acc = acc + jnp.dot(h, w2_ref[c * FF_CHUNK:(c + 1) * FF_CHUNK, :], preferred_element_type=F32)
    if final:
        _split_store(n_pt, _rms(acc, gf_ref[...]), *out_refs)
    else:
        out_refs[0][...] = acc


def _mix_mlp(x, o_p, o_s, wo, bo, g, w1, w2, gf, final):
    n, d = x.shape
    tm = ROW_TILE
    n_p = o_p.shape[0]
    n_pt = n_p // tm
    row = pl.BlockSpec((tm, d), lambda i: (i, 0))
    prompt, sample = _split_specs(n_pt, tm, d)
    resident = lambda a: pl.BlockSpec(a.shape, lambda i: (0,) * a.ndim, pipeline_mode=pl.Buffered(1))
    if final:
        out_specs = [prompt, sample]
        out_shape = [jax.ShapeDtypeStruct((n_p, d), F32), jax.ShapeDtypeStruct((n - n_p, d), F32)]
    else:
        out_specs = [row]
        out_shape = [jax.ShapeDtypeStruct((n, d), F32)]
    return pl.pallas_call(
        functools.partial(_mix_mlp_kernel, n_pt=n_pt, final=final),
        grid=(n // tm,),
        in_specs=[row, prompt, sample, resident(wo), _full_spec(bo.shape), _full_spec(g.shape), resident(w1),
                  resident(w2), _full_spec(gf.shape)],
        out_specs=out_specs,
        out_shape=out_shape,
        compiler_params=_cparams(1),
        name="mix_mlp",
    )(x, o_p, o_s, wo, bo, g, w1, w2, gf)


def _t5_bias(tab_ref, h, qpos, kpos):
    half = REL_BUCKETS // 2
    n = qpos - kpos
    off = jnp.where(n < 0, half, 0)
    n = jnp.abs(n)
    large = jnp.full(n.shape, half // 2, jnp.int32)
    for t in _T5_STEPS:
        large = large + jnp.where(n >= t, 1, 0)
    bucket = off + jnp.where(n < half // 2, n, large)
    far = tab_ref[half - 1, h]
    out = jnp.zeros(n.shape, F32)
    for b in range(REL_BUCKETS):
        out = jnp.where(bucket == b, (tab_ref[b, h] - far) * LOG2E, out)
    visible = (kpos // CHUNK) <= (qpos // CHUNK)
    return jnp.where(visible, out, NEG)


def _prompt_tiles_kernel(tab_ref, bt_ref, cm_ref, *, tq):
    h = pl.program_id(0)
    for v in range(2):
        kpos = lax.broadcasted_iota(jnp.int32, (tq, tq), 0)
        qpos = lax.broadcasted_iota(jnp.int32, (tq, tq), 1) + (1 - v) * tq
        bt_ref[0, v] = _t5_bias(tab_ref, h, qpos, kpos)

    @pl.when(h == 0)
    def _():
        kpos = lax.broadcasted_iota(jnp.int32, (tq, tq), 0)
        qpos = lax.broadcasted_iota(jnp.int32, (tq, tq), 1)
        cm_ref[0, 0] = jnp.zeros((tq, tq), F32)
        cm_ref[0, 1] = jnp.where(kpos <= qpos, 0.0, NEG)


def _prompt_tiles(table, tq):
    return pl.pallas_call(
        functools.partial(_prompt_tiles_kernel, tq=tq),
        grid=(DIFF_HEADS,),
        in_specs=[pl.BlockSpec(memory_space=pltpu.SMEM)],
        out_specs=[pl.BlockSpec((1, 2, tq, tq), lambda h: (h, 0, 0, 0)),
                   pl.BlockSpec((1, 2, tq, tq), lambda h: (0, 0, 0, 0))],
        out_shape=[jax.ShapeDtypeStruct((DIFF_HEADS, 2, tq, tq), F32), jax.ShapeDtypeStruct((1, 2, tq, tq), F32)],
        compiler_params=_cparams(1),
        name="prompt_score_tiles",
    )(table)


def _sample_tiles_kernel(tab_ref, dl_ref, dn_ref, fn_ref, *, t_new, past, tk):
    rows = DIFF_HEADS * 2 * t_new
    for h in range(DIFF_HEADS):
        r0 = h * 2 * t_new
        q = lax.broadcasted_iota(jnp.int32, (2 * t_new, tk), 0) % t_new + past
        k = lax.broadcasted_iota(jnp.int32, (2 * t_new, tk), 1) + (past - tk)
        dl_ref[r0:r0 + 2 * t_new, :] = _t5_bias(tab_ref, h, q, k)
        q = lax.broadcasted_iota(jnp.int32, (2 * t_new, t_new), 0) % t_new + past
        k = lax.broadcasted_iota(jnp.int32, (2 * t_new, t_new), 1) + past
        dn_ref[r0:r0 + 2 * t_new, :] = _t5_bias(tab_ref, h, q, k)
    q = lax.broadcasted_iota(jnp.int32, (rows, t_new), 0) % t_new
    k = lax.broadcasted_iota(jnp.int32, (rows, t_new), 1)
    fn_ref[...] = jnp.where(k <= q, 0.0, NEG)


def _sample_tiles(table, t_new, past, tk):
    rows = DIFF_HEADS * 2 * t_new
    assert rows == FOX_HEADS * t_new
    return pl.pallas_call(
        functools.partial(_sample_tiles_kernel, t_new=t_new, past=past, tk=tk),
        in_specs=[pl.BlockSpec(memory_space=pltpu.SMEM)],
        out_shape=[jax.ShapeDtypeStruct((rows, tk), F32), jax.ShapeDtypeStruct((rows, t_new), F32),
                   jax.ShapeDtypeStruct((rows, t_new), F32)],
        name="sample_score_tiles",
    )(table)


def _lambda(lqk_ref, lam_init):
    a = jnp.sum(lqk_ref[0:1, :] * lqk_ref[1:2, :], axis=-1, keepdims=True)
    b = jnp.sum(lqk_ref[2:3, :] * lqk_ref[3:4, :], axis=-1, keepdims=True)
    return jnp.exp(a) - jnp.exp(b) + lam_init


def _scores_t(k, q2_ref, s_out):
    for c0 in range(0, q2_ref.shape[0], ATT_STRIP):
        cols = slice(c0, c0 + ATT_STRIP)
        s_out[:, cols] = lax.dot_general(k, q2_ref[cols, :], (((1,), (1,)), ((), ())), preferred_element_type=F32)


def _reduce_rows(op, x):
    w, n = x.shape
    part = op(x.reshape(w // REDUCE_ROWS, REDUCE_ROWS, n), axis=0)
    return op(part, axis=0, keepdims=True)


def _softmax_pv_t(s_in, v, bias_t, m_ref, l_ref, acc_ref):
    for c0 in range(0, s_in.shape[1], ATT_STRIP):
        cols = slice(c0, c0 + ATT_STRIP)
        s = s_in[:, cols]
        if bias_t is not None:
            b0 = c0 % bias_t.shape[1]
            s = s + bias_t[:, b0:b0 + ATT_STRIP]
        m_prev = m_ref[:, cols]
        m_new = jnp.maximum(m_prev, _reduce_rows(jnp.max, s))
        alpha = jnp.exp2(m_prev - m_new)
        p = jnp.exp2(s - m_new)
        l_ref[:, cols] = alpha * l_ref[:, cols] + _reduce_rows(jnp.sum, p)
        pv = lax.dot_general(v, p.astype(BF16), (((0,), (0,)), ((), ())), preferred_element_type=F32)
        acc_ref[:, cols] = alpha * acc_ref[:, cols] + pv
        m_ref[:, cols] = m_new


def _prompt_attn_kernel(*refs, fox, tq, seq, lam_init):
    if fox:
        (q_ref, qa_ref, k_ref, ka_ref, v_ref, bt_ref, o_ref, q2_ref, s0_ref, s1_ref, m_ref, l_ref, acc_ref) = refs
    else:
        (q_ref, k_ref, v_ref, bt_ref, lqk_ref, sg_ref, o_ref, q2_ref, s0_ref, s1_ref, m_ref, l_ref, acc_ref) = refs
    s_refs = (s0_ref, s1_ref)
    lane = lax.broadcasted_iota(jnp.int32, (tq, LANES), 1)
    lo = lane < LANES // 2
    first_rows = lax.broadcasted_iota(jnp.int32, (LANES, tq), 0) < LANES // 2
    if fox:
        c0 = 12 * pl.program_id(1)
        in_a = jnp.logical_and(lane >= c0, lane < c0 + 6)
        in_b = jnp.logical_and(lane >= c0 + 6, lane < c0 + 12)

    def scores(j, slot):
        k0 = pl.multiple_of(jnp.maximum(j, 0) * tq, tq)
        k = k_ref[pl.ds(k0, tq), :]
        if fox:
            k = jnp.concatenate([k, ka_ref[pl.ds(k0, tq), :]], axis=1)
        _scores_t(k, q2_ref, s_refs[slot])

    def softmax_pv(j, slot, bias_t):
        k0 = pl.multiple_of(j * tq, tq)
        _softmax_pv_t(s_refs[slot], v_ref[pl.ds(k0, tq), :], bias_t, m_ref, l_ref, acc_ref)

    def q_block(i, carry):
        r0 = pl.multiple_of(i * tq, tq)
        q = q_ref[pl.ds(r0, tq), :]
        zero = jnp.zeros_like(q)
        q2_ref[0:tq, 0:LANES] = jnp.where(lo, q, zero)
        q2_ref[tq:2 * tq, 0:LANES] = jnp.where(lo, zero, q)
        if fox:
            qa = qa_ref[pl.ds(r0, tq), :]
            q2_ref[0:tq, LANES:2 * LANES] = jnp.where(in_a, qa, zero)
            q2_ref[tq:2 * tq, LANES:2 * LANES] = jnp.where(in_b, qa, zero)
        m_ref[...] = jnp.full(m_ref.shape, NEG, F32)
        l_ref[...] = jnp.zeros(l_ref.shape, F32)
        acc_ref[...] = jnp.zeros(acc_ref.shape, F32)

        scores(i, 0)

        @pl.when(i == 0)
        def _():
            softmax_pv(i, 0, bt_ref.at[0, 1])

        @pl.when(i >= 1)
        def _():
            scores(i - 1, 1)
            softmax_pv(i, 0, bt_ref.at[0, 1])
            scores(i - 2, 0)
            softmax_pv(i - 1, 1, bt_ref.at[0, 0])
            n_far = i - 1

            def pair(u, c):
                a = i - 2 - 2 * u
                scores(a - 1, 1)
                softmax_pv(a, 0, None)
                scores(a - 2, 0)
                softmax_pv(a - 1, 1, None)
                return c

            lax.fori_loop(0, n_far // 2, pair, 0)

            @pl.when(n_far % 2 == 1)
            def _():
                softmax_pv(0, 0, None)

        top = acc_ref[:, 0:tq] / l_ref[:, 0:tq]
        bot = acc_ref[:, tq:2 * tq] / l_ref[:, tq:2 * tq]
        if fox:
            o_t = jnp.where(first_rows, top, bot)
        else:
            o_t = top - _lambda(lqk_ref, lam_init) * bot
            ms = jnp.mean(o_t * o_t, axis=0, keepdims=True)
            o_t = o_t * lax.rsqrt(ms + EPS) * sg_ref[...] * (1.0 - lam_init)
        o_ref[pl.ds(r0, tq), :] = o_t.T.astype(BF16)
        return carry

    lax.fori_loop(0, seq // tq, q_block, 0)


def _prompt_attn(fox, batch, seq, q, k, v, bt, extra, lam_init=0.0):
    tq = ATT_TQ
    groups = D_MODEL // LANES
    blk = pl.BlockSpec((seq, LANES), lambda b, h: (b, h))
    if fox:
        qa, ka = extra
        aug = pl.BlockSpec((seq, LANES), lambda b, h: (b, 0))
        ins = [q, qa, k, ka, v, bt]
        in_specs = [blk, aug, blk, aug, blk, pl.BlockSpec((1, 2, tq, tq), lambda b, h: (0, 0, 0, 0))]
        kd = 2 * LANES
    else:
        lqk, sg = extra
        ins = [q, k, v, bt, lqk, sg]
        in_specs = [blk, blk, blk, pl.BlockSpec((1, 2, tq, tq), lambda b, h: (h, 0, 0, 0)),
                    _full_spec(lqk.shape), _full_spec(sg.shape)]
        kd = LANES
    return pl.pallas_call(
        functools.partial(_prompt_attn_kernel, fox=fox, tq=tq, seq=seq, lam_init=lam_init),
        grid=(batch, groups),
        in_specs=in_specs,
        out_specs=blk,
        out_shape=jax.ShapeDtypeStruct((batch * seq, D_MODEL), BF16),
        scratch_shapes=[pltpu.VMEM((2 * tq, kd), BF16), pltpu.VMEM((tq, 2 * tq), F32), pltpu.VMEM((tq, 2 * tq), F32),
                        pltpu.VMEM((1, 2 * tq), F32), pltpu.VMEM((1, 2 * tq), F32), pltpu.VMEM((LANES, 2 * tq), F32)],
        compiler_params=_cparams(2),
        name="fox_prompt_attn" if fox else "diff_prompt_attn",
    )(*ins)


def _online_update(s, pv_fn, m_ref, l_ref, acc_ref):
    m_prev = m_ref[...]
    m_new = jnp.maximum(m_prev, jnp.max(s, axis=-1, keepdims=True))
    alpha = jnp.exp2(m_prev - m_new)
    p = jnp.exp2(s - m_new)
    l_ref[...] = alpha * l_ref[...] + jnp.sum(p, axis=-1, keepdims=True)
    acc_ref[...] = alpha * acc_ref[...] + pv_fn(p.astype(BF16))
    m_ref[...] = m_new


def _sample_attn_kernel(*refs, fox, t_new, tk, n_tiles, lam_init):
    if fox:
        (q_ref, qa_ref, kt_ref, kat_ref, vt_ref, kn_ref, kan_ref, vn_ref, bn_ref,
         o_ref, qbd_ref, m_ref, l_ref, acc_ref) = refs
    else:
        (q_ref, kt_ref, v_ref, kn_ref, vn_ref, bl_ref, bn_ref, lqk_ref, sg_ref,
         o_ref, qbd_ref, m_ref, l_ref, acc_ref) = refs
    t = pl.program_id(1)
    n_blocks = D_MODEL // 64
    rows_per_head = 2 * t_new

    @pl.when(t == 0)
    def _():
        q = q_ref[...]
        zero = jnp.zeros_like(q)
        cb = lax.broadcasted_iota(jnp.int32, q.shape, 1) // 64
        if fox:
            qa = qa_ref[0]
            la = lax.broadcasted_iota(jnp.int32, qa.shape, 1)
        for rb in range(n_blocks):
            qbd_ref[rb * t_new:(rb + 1) * t_new, 0:D_MODEL] = jnp.where(cb == rb, q, zero)
            if fox:
                sel = jnp.logical_and(la >= 6 * rb, la < 6 * rb + 6)
                qbd_ref[rb * t_new:(rb + 1) * t_new, D_MODEL:D_MODEL + LANES] = jnp.where(sel, qa, jnp.zeros_like(qa))
        m_ref[...] = jnp.full(m_ref.shape, NEG, F32)
        l_ref[...] = jnp.zeros(l_ref.shape, F32)
        acc_ref[...] = jnp.zeros(acc_ref.shape, F32)

    n_groups = D_MODEL // SAMPLE_GROUP
    group_rows = (SAMPLE_GROUP // 64) * t_new
    groups = [(slice(g * group_rows, (g + 1) * group_rows), slice(g * SAMPLE_GROUP, (g + 1) * SAMPLE_GROUP))
              for g in range(n_groups)]

    def cache_scores():
        s = jnp.concatenate([jnp.dot(qbd_ref[rows, dims], kt_ref[0, 0, dims, :].astype(BF16),
                                     preferred_element_type=F32) for rows, dims in groups], axis=0)
        if fox:
            s = s + jnp.dot(qbd_ref[:, D_MODEL:D_MODEL + LANES], kat_ref[0], preferred_element_type=F32)
        return s

    if fox:
        def cache_pv(p):
            return jnp.concatenate(
                [lax.dot_general(p[rows, :], vt_ref[0, 0, dims, :].astype(BF16), (((1,), (1,)), ((), ())),
                                 preferred_element_type=F32) for rows, dims in groups], axis=0)

        def new_pv(p):
            return jnp.concatenate([jnp.dot(p[rows, :], vn_ref[:, dims], preferred_element_type=F32)
                                    for rows, dims in groups], axis=0)
    else:
        def cache_pv(p):
            return jnp.concatenate(
                [jnp.dot(p[h * rows_per_head:(h + 1) * rows_per_head, :],
                         v_ref[0, 0, pl.ds(h, tk, stride=DIFF_HEADS), :].astype(BF16), preferred_element_type=F32)
                 for h in range(DIFF_HEADS)], axis=0)

        def new_pv(p):
            return jnp.concatenate(
                [jnp.dot(p[h * rows_per_head:(h + 1) * rows_per_head, :], vn_ref[:, h * LANES:(h + 1) * LANES],
                         preferred_element_type=F32) for h in range(DIFF_HEADS)], axis=0)

    @pl.when(t < n_tiles - 1)
    def _():
        _online_update(cache_scores(), cache_pv, m_ref, l_ref, acc_ref)

    @pl.when(t == n_tiles - 1)
    def _():
        s = cache_scores()
        if not fox:
            s = s + bl_ref[...]
        _online_update(s, cache_pv, m_ref, l_ref, acc_ref)
        kn = kn_ref[...]
        if fox:
            kn = jnp.concatenate([kn, kan_ref[0]], axis=1)
        s = lax.dot_general(qbd_ref[...], kn, (((1,), (1,)), ((), ())), preferred_element_type=F32) + bn_ref[...]
        _online_update(s, new_pv, m_ref, l_ref, acc_ref)

        lane = lax.broadcasted_iota(jnp.int32, (t_new, LANES), 1)
        lo = lane < LANES // 2
        for g in range(D_MODEL // LANES):
            ra, rb = 2 * g * t_new, (2 * g + 1) * t_new
            cols = slice(g * LANES, (g + 1) * LANES)
            c0 = (g * LANES) % SAMPLE_GROUP if fox else 0
            acc_cols = slice(c0, c0 + LANES)
            top = acc_ref[ra:ra + t_new, acc_cols] / l_ref[ra:ra + t_new, :]
            bot = acc_ref[rb:rb + t_new, acc_cols] / l_ref[rb:rb + t_new, :]
            if fox:
                o = jnp.where(lo, top, bot)
            else:
                o = _rms(top - _lambda(lqk_ref, lam_init) * bot, sg_ref[...]) * (1.0 - lam_init)
            o_ref[:, cols] = o.astype(BF16)


def _sample_attn(fox, layer, n_prompt, dec_batch, t_new, past, q, kt, v, kn, vn, bias_last, bias_new, extra,
                 lam_init=0.0):
    tk = SAMPLE_TK
    n_tiles = past // tk
    d = D_MODEL
    rows = (d // 64) * t_new
    s0 = n_prompt // t_new
    new = pl.BlockSpec((t_new, d), lambda b, t: (s0 + b, 0))
    dims_keys = pl.BlockSpec((1, 1, d, tk), lambda b, t: (layer, b, 0, t))
    out = pl.BlockSpec((t_new, d), lambda b, t: (b, 0))
    if fox:
        qa, ka, kat = extra
        new_aug = pl.BlockSpec((1, t_new, LANES), lambda b, t: (b, past // t_new, 0))
        ins = [q, qa, kt, kat, v, kn, ka, vn, bias_new]
        in_specs = [new, new_aug, dims_keys, pl.BlockSpec((1, LANES, tk), lambda b, t: (b, 0, t)), dims_keys,
                    new, new_aug, new, _full_spec(bias_new.shape)]
        kd, acc_w = d + LANES, SAMPLE_GROUP
    else:
        lqk, sg = extra
        ins = [q, kt, v, kn, vn, bias_last, bias_new, lqk, sg]
        in_specs = [new, dims_keys, pl.BlockSpec((1, 1, tk * DIFF_HEADS, LANES), lambda b, t: (layer, b, t, 0)),
                    new, new, _full_spec(bias_last.shape), _full_spec(bias_new.shape),
                    _full_spec(lqk.shape), _full_spec(sg.shape)]
        kd, acc_w = d, LANES
    return pl.pallas_call(
        functools.partial(_sample_attn_kernel, fox=fox, t_new=t_new, tk=tk, n_tiles=n_tiles, lam_init=lam_init),
        grid=(dec_batch, n_tiles),
        in_specs=in_specs,
        out_specs=out,
        out_shape=jax.ShapeDtypeStruct((dec_batch * t_new, d), BF16),
        scratch_shapes=[pltpu.VMEM((rows, kd), BF16), pltpu.VMEM((rows, 1), F32),
                        pltpu.VMEM((rows, 1), F32), pltpu.VMEM((rows, acc_w), F32)],
        compiler_params=_cparams(2),
        name="fox_sample_attn" if fox else "diff_sample_attn",
    )(*ins)


def _split3(c):
    hi = c.astype(BF16)
    r = c - hi.astype(F32)
    mid = r.astype(BF16)
    lo = (r - mid.astype(F32)).astype(BF16)
    return hi, mid, lo


def _decay_cols_kernel(lf_ref, eq_ref, ek_ref, ekt_ref, qa_ref, ka_ref, *maybe_kat_ref, length):
    h = FOX_HEADS
    carry = jnp.zeros((1, h), F32)
    for r in range(0, length, CUM_BLOCK):
        n = min(CUM_BLOCK, length - r)
        tri = (lax.broadcasted_iota(jnp.int32, (n, n), 1) <= lax.broadcasted_iota(jnp.int32, (n, n), 0))
        tri = jnp.where(tri, 1.0, 0.0).astype(BF16)
        parts = _split3(lf_ref[0, r:r + n, :])
        c = carry
        for p in parts:
            c = c + jnp.dot(tri, p, preferred_element_type=F32)
        carry = c[n - 1:n, :]
        chi, cmid, clo = _split3(c * LOG2E)
        ones = jnp.ones((n, h), BF16)
        qa = jnp.zeros((n, LANES), F32)
        ka = jnp.zeros((n, LANES), F32)
        kat = jnp.zeros((LANES, n), F32)
        for j, (pq, pk) in enumerate(((chi, -chi), (cmid, -cmid), (clo, -clo), (ones, ones))):
            qa = qa + jnp.dot(pq, eq_ref[j * h:(j + 1) * h, :], preferred_element_type=F32)
            ka = ka + jnp.dot(pk, ek_ref[j * h:(j + 1) * h, :], preferred_element_type=F32)
            if maybe_kat_ref:
                kat = kat + lax.dot_general(ekt_ref[:, j * h:(j + 1) * h], pk, (((1,), (1,)), ((), ())),
                                            preferred_element_type=F32)
        qa_ref[0, r:r + n, :] = qa.astype(BF16)
        ka_ref[0, r:r + n, :] = ka.astype(BF16)
        if maybe_kat_ref:
            maybe_kat_ref[0][0, :, r:r + n] = kat.astype(BF16)


def _placement():
    h = FOX_HEADS
    eq = np.zeros((4 * h, LANES), np.float32)
    ek = np.zeros((4 * h, LANES), np.float32)
    for head in range(h):
        for part in range(3):
            eq[part * h + head, 6 * head + part] = 1.0
            ek[part * h + head, 6 * head + 3 + part] = 1.0
            eq[3 * h + head, 6 * head + 3 + part] = 1.0
            ek[3 * h + head, 6 * head + part] = 1.0
    return jnp.asarray(eq, BF16), jnp.asarray(ek, BF16), jnp.asarray(ek.T, BF16)


def _decay_cols(lf, keys_transposed):
    nb, length, h = lf.shape
    eq, ek, ekt = _placement()
    spec = pl.BlockSpec((1, length, LANES), lambda b: (b, 0, 0))
    out_specs = [spec, spec]
    out_shape = [jax.ShapeDtypeStruct((nb, length, LANES), BF16)] * 2
    if keys_transposed:
        out_specs.append(pl.BlockSpec((1, LANES, length), lambda b: (b, 0, 0)))
        out_shape.append(jax.ShapeDtypeStruct((nb, LANES, length), BF16))
    return pl.pallas_call(
        functools.partial(_decay_cols_kernel, length=length),
        grid=(nb,),
        in_specs=[pl.BlockSpec((1, length, h), lambda b: (b, 0, 0)), _full_spec(eq.shape), _full_spec(ek.shape),
                  _full_spec(ekt.shape)],
        out_specs=out_specs,
        out_shape=out_shape,
        compiler_params=_cparams(1),
        name="fox_decay_cols",
    )(lf, eq, ek, ekt)


def _conv_dw_kernel(u_ref, halo_ref, wdw_ref, bdw_ref, g_ref, b_ref, o_ref, ext_ref, sh_ref, y_ref, *, tm,
                    steps_per_seq):
    halo = halo_ref[...]
    if steps_per_seq:
        halo = jnp.where(pl.program_id(0) % steps_per_seq == 0, 0.0, halo)
    ext_ref[0:HALO, :] = halo
    ext_ref[HALO:HALO + tm, :] = u_ref[...]
    span = sh_ref.shape[1]
    for s in range(1, SUBLANES):
        sh_ref[s - 1] = ext_ref[s:s + span, :]
    d = u_ref.shape[1]
    first = HALO - (CONV_WIDTH - 1)
    for rc in range(tm // CONV_RC):
        for lc in range(d // CONV_LC):
            cols = slice(lc * CONV_LC, (lc + 1) * CONV_LC)
            acc = jnp.zeros((CONV_RC, CONV_LC), F32)
            for w in range(CONV_WIDTH):
                a, s = divmod(first + w, SUBLANES)
                src = ext_ref if s == 0 else sh_ref.at[s - 1]
                r = rc * CONV_RC + a * SUBLANES
                acc = acc + src[r:r + CONV_RC, cols] * wdw_ref[w:w + 1, cols]
            y_ref[rc * CONV_RC:(rc + 1) * CONV_RC, cols] = acc
    y = y_ref[...] + bdw_ref[...]
    yc = y - jnp.mean(y, axis=-1, keepdims=True)
    yn = yc * lax.rsqrt(jnp.mean(yc * yc, axis=-1, keepdims=True) + EPS) * g_ref[...] + b_ref[...]
    o_ref[...] = (yn * (1.0 / (1.0 + jnp.exp(-yn)))).astype(BF16)


def _conv_dw(u, row0, n_rows, tm, halo, halo_index, steps_per_seq, wdw, bdw, g, b):
    d = u.shape[1]
    b0 = row0 // tm
    return pl.pallas_call(
        functools.partial(_conv_dw_kernel, tm=tm, steps_per_seq=steps_per_seq),
        grid=(n_rows // tm,),
        in_specs=[pl.BlockSpec((tm, d), lambda i: (b0 + i, 0)), pl.BlockSpec((HALO, d), lambda i: (halo_index(i), 0)),
                  _full_spec(wdw.shape), _full_spec(bdw.shape), _full_spec(g.shape), _full_spec(b.shape)],
        out_specs=pl.BlockSpec((tm, d), lambda i: (i, 0)),
        out_shape=jax.ShapeDtypeStruct((n_rows, d), BF16),
        scratch_shapes=[pltpu.VMEM((HALO + tm, d), F32), pltpu.VMEM((SUBLANES - 1, HALO + tm - SUBLANES, d), F32),
                        pltpu.VMEM((tm, d), F32)],
        compiler_params=_cparams(1),
        name="conv_dw",
    )(u, halo, wdw, bdw, g, b)


def _dims_keys(cache):
    nd = cache.ndim
    t = jnp.transpose(cache, (0, 1) + tuple(range(3, nd)) + (2,))
    return t.reshape(cache.shape[0], cache.shape[1], D_MODEL, cache.shape[2])


def kernel(x_prompt, x_sample, cache_diff_k, cache_diff_v, state_conv, cache_fox_k, cache_fox_v, cache_fox_logf, rel_bias, norm_g, final_g, diff_w_in, diff_w_out, diff_lq1, diff_lk1, diff_lq2, diff_lk2, diff_subln_g, conv_w_pw1, conv_b_pw1, conv_w_dw, conv_b_dw, conv_ln_g, conv_ln_b, conv_w_pw2, conv_b_pw2, fox_w_in, fox_b_f, fox_w_out, mlp_w1, mlp_w2):
    B, S, D = x_prompt.shape
    Bd, T, _ = x_sample.shape
    P = cache_diff_k.shape[2]
    depth = norm_g.shape[0]
    n_p, n_s = B * S, Bd * T
    assert D == D_MODEL and S % (2 * ATT_TQ) == 0 and P % SAMPLE_TK == 0 and n_p % ROW_TILE == 0
    assert n_s % ROW_TILE == 0 and T % 8 == 0 and T >= CONV_WIDTH - 1 and S % ROW_TILE == 0

    x = jnp.concatenate([x_prompt.reshape(n_p, D), x_sample.reshape(n_s, D)], axis=0)
    zero_bias = jnp.zeros((1, D), F32)
    final_gain = final_g.reshape(1, D)

    bt_diff, bt_fox = _prompt_tiles(rel_bias, ATT_TQ)
    sb_last, sb_new, sf_new = _sample_tiles(rel_bias, T, P, SAMPLE_TK)

    diff_kt = _dims_keys(cache_diff_k)
    diff_v = cache_diff_v.reshape(cache_diff_v.shape[0], Bd, P * DIFF_HEADS, 2 * DIFF_HEAD_DIM)
    fox_kt = _dims_keys(cache_fox_k)
    fox_vt = _dims_keys(cache_fox_v)

    n_diff, n_fox = diff_w_in.shape[0], fox_w_in.shape[0]
    dk_shape = (1, n_p, D)
    dv_shape = (n_diff, n_p, DIFF_HEADS, 2 * DIFF_HEAD_DIM)
    fkv_shape = (n_fox, n_p, FOX_HEADS, FOX_HEAD_DIM)
    dv_new, fk_new, fv_new = None, None, None
    dk_p, dk_s, dv_s = [], [], []
    cv_p, cv_s = [], []
    fl_p, fk_s, fv_s, fl_s = [], [], [], []
    y = None

    for i in range(depth):
        kind, j = i % N_MIXERS, i // N_MIXERS
        g1 = norm_g[i, 0].reshape(1, D)
        g2 = norm_g[i, 1].reshape(1, D)
        if kind == 0:
            lam_init = 0.8 - 0.6 * math.exp(-0.3 * i)
            q, k_p, k_s, dv_new, v_s, kb, vb = _qkv_inproj(
                x, n_p, g1, diff_w_in[j].astype(BF16), DIFF_SCALE, (dk_shape, 0), (dv_shape, j),
                prev=None if dv_new is None else {3: dv_new})
            dk_p.append(k_p.reshape(B, S, DIFF_HEADS, 2, DIFF_HEAD_DIM))
            lqk = jnp.stack([diff_lq1[j], diff_lk1[j], diff_lq2[j], diff_lk2[j]]).astype(F32)
            sg = diff_subln_g[j].reshape(1, 2 * DIFF_HEAD_DIM).astype(F32)
            o_p = _prompt_attn(False, B, S, q, kb, vb, bt_diff, (lqk, sg.reshape(2 * DIFF_HEAD_DIM, 1)), lam_init)
            o_s = _sample_attn(False, j, n_p, Bd, T, P, q, diff_kt, diff_v, kb, vb, sb_last, sb_new, (lqk, sg),
                               lam_init)
            wo, bo = diff_w_out[j].astype(BF16), zero_bias
            dk_s.append(k_s.reshape(Bd, T, DIFF_HEADS, 2, DIFF_HEAD_DIM))
            dv_s.append(v_s.reshape(Bd, T, DIFF_HEADS, 2 * DIFF_HEAD_DIM))
        elif kind == 1:
            u = _conv_inproj(x, g1, conv_w_pw1[j].astype(BF16), conv_b_pw1[j].reshape(1, 2 * D))
            wdw = jnp.pad(conv_w_dw[j], ((0, HALO - CONV_WIDTH), (0, 0)))
            args = (wdw, conv_b_dw[j].reshape(1, D), conv_ln_g[j].reshape(1, D), conv_ln_b[j].reshape(1, D))
            per_tile = ROW_TILE // HALO
            o_p = _conv_dw(u, 0, n_p, ROW_TILE, u, lambda t: jnp.maximum(t * per_tile - 1, 0), S // ROW_TILE, *args)
            state = jnp.pad(state_conv[j], ((0, 0), (HALO - (CONV_WIDTH - 1), 0), (0, 0))).reshape(Bd * HALO, D)
            o_s = _conv_dw(u, n_p, n_s, T, state, lambda t: t, 0, *args)
            wo, bo = conv_w_pw2[j].astype(BF16), conv_b_pw2[j].reshape(1, D)
            keep = CONV_WIDTH - 1
            cv_p.append(jnp.stack([u[(b + 1) * S - keep:(b + 1) * S] for b in range(B)]))
            cv_s.append(u[n_p:].reshape(Bd, T, D)[:, T - keep:])
        else:
            w_in = fox_w_in[j]
            wf = jnp.pad(w_in[:, 3 * D:], ((0, 0), (0, LANES - FOX_HEADS))).astype(BF16)
            bf = jnp.pad(fox_b_f[j].astype(F32), (0, LANES - FOX_HEADS)).reshape(1, LANES)
            w_qkv = w_in[:, :3 * D].astype(BF16)
            w_rot = [jnp.roll(w_qkv[:, c * D:(c + 1) * D], -FOX_HEAD_DIM, axis=1) for c in (1, 2)]
            q, fk_new, k_s, fv_new, v_s, kb, vb, lf = _qkv_inproj(
                x, n_p, g1, jnp.concatenate([w_qkv] + w_rot, axis=1), FOX_SCALE, (fkv_shape, j), (fkv_shape, j),
                prev=None if fk_new is None else {1: fk_new, 3: fv_new}, forget=(wf, bf))
            lf_p = lf[:n_p].reshape(B, S, FOX_HEADS)
            lf_s = lf[n_p:].reshape(Bd, T, FOX_HEADS)
            qa_p, ka_p = _decay_cols(lf_p, False)
            decay_s = _decay_cols(jnp.concatenate([cache_fox_logf[j].astype(F32), lf_s], axis=1), True)
            o_p = _prompt_attn(True, B, S, q, kb, vb, bt_fox, (qa_p.reshape(n_p, LANES), ka_p.reshape(n_p, LANES)))
            o_s = _sample_attn(True, j, n_p, Bd, T, P, q, fox_kt, fox_vt, kb, vb, None, sf_new, decay_s)
            wo, bo = fox_w_out[j].astype(BF16), zero_bias
            fl_p.append(lf_p)
            fk_s.append(k_s.reshape(Bd, T, FOX_HEADS, FOX_HEAD_DIM))
            fv_s.append(v_s.reshape(Bd, T, FOX_HEADS, FOX_HEAD_DIM))
            fl_s.append(lf_s)
        final = i == depth - 1
        out = _mix_mlp(x, o_p, o_s, wo, bo, g2, mlp_w1[i].astype(BF16), mlp_w2[i].astype(BF16), final_gain, final)
        if final:
            y = out
        else:
            x = out[0]

    y_prompt = y[0].reshape(B, S, D)
    y_sample = y[1].reshape(Bd, T, D)
    new_diff_k_p = jnp.stack(dk_p)
    new_diff_v_p = dv_new.reshape(n_diff, B, S, DIFF_HEADS, 2 * DIFF_HEAD_DIM)
    new_fox_k_p = fk_new.reshape(n_fox, B, S, FOX_HEADS, FOX_HEAD_DIM)
    new_fox_v_p = fv_new.reshape(n_fox, B, S, FOX_HEADS, FOX_HEAD_DIM)
    return (y_prompt, y_sample, new_diff_k_p, new_diff_v_p, jnp.stack(cv_p), new_fox_k_p,
            new_fox_v_p, jnp.stack(fl_p), jnp.stack(dk_s), jnp.stack(dv_s), jnp.stack(cv_s),
            jnp.stack(fk_s), jnp.stack(fv_s), jnp.stack(fl_s))
```

```python
import functools
import math

import numpy as np
import jax
import jax.numpy as jnp
from jax import lax
from jax.experimental import pallas as pl
from jax.experimental.pallas import tpu as pltpu

F32 = jnp.float32
BF16 = jnp.bfloat16

D_MODEL = 1024
CHUNK = 64
EPS = 1e-6
DIFF_HEADS = 8
DIFF_HEAD_DIM = 64
LOG2E = math.log2(math.e)
DIFF_SCALE = DIFF_HEAD_DIM ** -0.5
CONV_WIDTH = 31
FOX_HEADS = 16
FOX_HEAD_DIM = 64
FOX_SCALE = FOX_HEAD_DIM ** -0.5
REL_BUCKETS = 32
N_MIXERS = 3

LANES = 128
SUBLANES = 8
NEG = -1e30
ROW_TILE = 512
ATT_TQ = 512
ATT_STRIP = 512
ATT_GROUPS = 2
REDUCE_ROWS = 64
SAMPLE_TK = 1024
SAMPLE_GROUP = 256
FF_CHUNK = 1024
CUM_BLOCK = 256
HALO = 32
CONV_RC = 32
CONV_LC = 256
VMEM_LIMIT = 56 * 1024 * 1024

_T5_STEPS = (12, 16, 23, 32, 46, 64, 91)


def _cparams(n_axes):
    return pltpu.CompilerParams(dimension_semantics=("arbitrary",) * n_axes, vmem_limit_bytes=VMEM_LIMIT)


def _rms(x, g):
    return x * lax.rsqrt(jnp.mean(x * x, axis=-1, keepdims=True) + EPS) * g


def _full_spec(shape):
    nd = len(shape)
    return pl.BlockSpec(shape, lambda *_: (0,) * nd)


def _split_specs(n_prompt_tiles, tm, d):
    prompt = pl.BlockSpec((tm, d), lambda i: (jnp.minimum(i, n_prompt_tiles - 1), 0))
    sample = pl.BlockSpec((tm, d), lambda i: (jnp.maximum(i - n_prompt_tiles, 0), 0))
    return prompt, sample


def _split_store(n_prompt_tiles, value, prompt_ref, sample_ref):
    i = pl.program_id(0)

    @pl.when(i < n_prompt_tiles)
    def _():
        prompt_ref[...] = value

    @pl.when(i >= n_prompt_tiles)
    def _():
        sample_ref[...] = value


def _store_prompt_cache(value, prompt_ref, rotated_fn):
    if len(prompt_ref.shape) == 3:
        prompt_ref[0] = value
        return
    heads, width = prompt_ref.shape[2:]
    if width == LANES:
        for h in range(heads):
            prompt_ref[0, :, h, :] = value[:, h * width:(h + 1) * width]
    else:
        rotated = rotated_fn()
        for h in range(heads):
            src = value if h % 2 == 0 else rotated
            prompt_ref[0, :, h, :] = src[:, (h // 2) * LANES:(h // 2) * LANES + width]


def _qkv_inproj_kernel(*refs, n_pt, scale, n_prev, forget):
    x_ref, g_ref, w_ref = refs[:3]
    refs = refs[3:]
    if forget:
        wf_ref, bf_ref = refs[:2]
        refs = refs[2:]
    refs = refs[n_prev:]
    q_ref, kp_ref, ks_ref, vp_ref, vs_ref, kb_ref, vb_ref = refs[:7]
    d = D_MODEL

    def project(prompt):
        xn = _rms(x_ref[...], g_ref[...]).astype(BF16)
        q = jnp.dot(xn, w_ref[:, 0:d], preferred_element_type=F32)
        q_ref[...] = (q * (scale * LOG2E)).astype(BF16)
        for c, new_p, new_s, bf_ref_out in ((1, kp_ref, ks_ref, kb_ref), (2, vp_ref, vs_ref, vb_ref)):
            val = jnp.dot(xn, w_ref[:, c * d:(c + 1) * d], preferred_element_type=F32)
            if prompt:
                _store_prompt_cache(val, new_p, lambda c=c: jnp.dot(xn, w_ref[:, (c + 2) * d:(c + 3) * d],
                                                                   preferred_element_type=F32))
            else:
                new_s[...] = val
            bf_ref_out[...] = val.astype(BF16)
        if forget:
            z = jnp.dot(xn, wf_ref[...], preferred_element_type=F32) + bf_ref[...]
            lf = jnp.minimum(z, 0.0) - jnp.log1p(jnp.exp(-jnp.abs(z)))
            refs[7][...] = lf[:, 0:FOX_HEADS]

    pl.when(pl.program_id(0) < n_pt)(lambda: project(True))
    pl.when(pl.program_id(0) >= n_pt)(lambda: project(False))


def _qkv_inproj(x, n_p, g, w, scale, k_slab, v_slab, prev=None, forget=()):
    n, d = x.shape
    tm = ROW_TILE
    n_pt = n_p // tm
    row = pl.BlockSpec((tm, d), lambda i: (i, 0))
    _, sample = _split_specs(n_pt, tm, d)
    prev = prev or {}

    def slab(shape, layer):
        rest = tuple(shape[2:])
        zeros = (0,) * len(rest)
        return pl.BlockSpec((1, tm) + rest, lambda i: (layer, jnp.minimum(i, n_pt - 1)) + zeros)

    f32_s = jax.ShapeDtypeStruct((n - n_p, d), F32)
    bf_all = jax.ShapeDtypeStruct((n, d), BF16)
    ins = [x, g, w] + list(forget) + list(prev.values())
    w_spec = pl.BlockSpec(w.shape, lambda i: (0, 0), pipeline_mode=pl.Buffered(1))
    in_specs = ([row, _full_spec((1, d)), w_spec] + [_full_spec(a.shape) for a in forget]
                + [pl.BlockSpec(memory_space=pl.ANY)] * len(prev))
    out_specs = [row, slab(*k_slab), sample, slab(*v_slab), sample, row, row]
    out_shape = [bf_all, jax.ShapeDtypeStruct(k_slab[0], F32), f32_s, jax.ShapeDtypeStruct(v_slab[0], F32), f32_s,
                 bf_all, bf_all]
    if forget:
        out_specs.append(pl.BlockSpec((tm, FOX_HEADS), lambda i: (i, 0)))
        out_shape.append(jax.ShapeDtypeStruct((n, FOX_HEADS), F32))
    first_prev = 3 + len(forget)
    aliases = {first_prev + pos: out_index for pos, out_index in enumerate(prev)}
    return pl.pallas_call(
        functools.partial(_qkv_inproj_kernel, n_pt=n_pt, scale=scale, n_prev=len(prev), forget=bool(forget)),
        grid=(n // tm,),
        in_specs=in_specs,
        out_specs=out_specs,
        out_shape=out_shape,
        input_output_aliases=aliases,
        compiler_params=_cparams(1),
        name="fox_inproj" if forget else "diff_inproj",
    )(*ins)


def _conv_inproj_kernel(x_ref, g_ref, w_ref, b_ref, u_ref):
    d = D_MODEL
    xn = _rms(x_ref[...], g_ref[...]).astype(BF16)
    a = jnp.dot(xn, w_ref[:, 0:d], preferred_element_type=F32) + b_ref[:, 0:d]
    gate = jnp.dot(xn, w_ref[:, d:2 * d], preferred_element_type=F32) + b_ref[:, d:2 * d]
    u_ref[...] = a * (1.0 / (1.0 + jnp.exp(-gate)))


def _conv_inproj(x, g, w, b):
    n, d = x.shape
    tm = ROW_TILE
    row = pl.BlockSpec((tm, d), lambda i: (i, 0))
    return pl.pallas_call(
        _conv_inproj_kernel,
        grid=(n // tm,),
        in_specs=[row, _full_spec((1, d)), _full_spec(w.shape), _full_spec(b.shape)],
        out_specs=row,
        out_shape=jax.ShapeDtypeStruct((n, d), F32),
        compiler_params=_cparams(1),
        name="conv_inproj",
    )(x, g, w, b)


def _mix_mlp_kernel(x_ref, op_ref, os_ref, wo_ref, bo_ref, g_ref, w1_ref, w2_ref, gf_ref, *out_refs, n_pt, final):
    o = jnp.where(pl.program_id(0) < n_pt, op_ref[...], os_ref[...])
    x1 = x_ref[...] + jnp.dot(o, wo_ref[...], preferred_element_type=F32) + bo_ref[...]
    xn = _rms(x1, g_ref[...]).astype(BF16)
    acc = x1
    d_ff = w1_ref.shape[1]
    for c in range(d_ff // FF_CHUNK):
        h = jnp.dot(xn, w1_ref[:, c * FF_CHUNK:(c + 1) * FF_CHUNK], preferred_element_type=F32)
        h = jnp.square(jnp.maximum(h, 0.0)).astype(BF16)
        acc = acc + jnp.dot(h, w2_ref[c * FF_CHUNK:(c + 1) * FF_CHUNK, :], preferred_element_type=F32)
    if final:
        _split_store(n_pt, _rms(acc, gf_ref[...]), *out_refs)
    else:
        out_refs[0][...] = acc


def _mix_mlp(x, o_p, o_s, wo, bo, g, w1, w2, gf, final):
    n, d = x.shape
    tm = ROW_TILE
    n_p = o_p.shape[0]
    n_pt = n_p // tm
    row = pl.BlockSpec((tm, d), lambda i: (i, 0))
    prompt, sample = _split_specs(n_pt, tm, d)
    resident = lambda a: pl.BlockSpec(a.shape, lambda i: (0,) * a.ndim, pipeline_mode=pl.Buffered(1))
    if final:
        out_specs = [prompt, sample]
        out_shape = [jax.ShapeDtypeStruct((n_p, d), F32), jax.ShapeDtypeStruct((n - n_p, d), F32)]
    else:
        out_specs = [row]
        out_shape = [jax.ShapeDtypeStruct((n, d), F32)]
    return pl.pallas_call(
        functools.partial(_mix_mlp_kernel, n_pt=n_pt, final=final),
        grid=(n // tm,),
        in_specs=[row, prompt, sample, resident(wo), _full_spec(bo.shape), _full_spec(g.shape), resident(w1),
                  resident(w2), _full_spec(gf.shape)],
        out_specs=out_specs,
        out_shape=out_shape,
        compiler_params=_cparams(1),
        name="mix_mlp",
    )(x, o_p, o_s, wo, bo, g, w1, w2, gf)


def _t5_bias(tab_ref, h, qpos, kpos):
    half = REL_BUCKETS // 2
    n = qpos - kpos
    off = jnp.where(n < 0, half, 0)
    n = jnp.abs(n)
    large = jnp.full(n.shape, half // 2, jnp.int32)
    for t in _T5_STEPS:
        large = large + jnp.where(n >= t, 1, 0)
    bucket = off + jnp.where(n < half // 2, n, large)
    far = tab_ref[half - 1, h]
    out = jnp.zeros(n.shape, F32)
    for b in range(REL_BUCKETS):
        out = jnp.where(bucket == b, (tab_ref[b, h] - far) * LOG2E, out)
    visible = (kpos // CHUNK) <= (qpos // CHUNK)
    return jnp.where(visible, out, NEG)


def _prompt_tiles_kernel(tab_ref, bt_ref, cm_ref, *, tq):
    h = pl.program_id(0)
    for v in range(2):
        kpos = lax.broadcasted_iota(jnp.int32, (tq, tq), 0)
        qpos = lax.broadcasted_iota(jnp.int32, (tq, tq), 1) + (1 - v) * tq
        bt_ref[0, v] = _t5_bias(tab_ref, h, qpos, kpos)

    @pl.when(h == 0)
    def _():
        kpos = lax.broadcasted_iota(jnp.int32, (tq, tq), 0)
        qpos = lax.broadcasted_iota(jnp.int32, (tq, tq), 1)
        cm_ref[0, 0] = jnp.zeros((tq, tq), F32)
        cm_ref[0, 1] = jnp.where(kpos <= qpos, 0.0, NEG)


def _prompt_tiles(table, tq):
    return pl.pallas_call(
        functools.partial(_prompt_tiles_kernel, tq=tq),
        grid=(DIFF_HEADS,),
        in_specs=[pl.BlockSpec(memory_space=pltpu.SMEM)],
        out_specs=[pl.BlockSpec((1, 2, tq, tq), lambda h: (h, 0, 0, 0)),
                   pl.BlockSpec((1, 2, tq, tq), lambda h: (0, 0, 0, 0))],
        out_shape=[jax.ShapeDtypeStruct((DIFF_HEADS, 2, tq, tq), F32), jax.ShapeDtypeStruct((1, 2, tq, tq), F32)],
        compiler_params=_cparams(1),
        name="prompt_score_tiles",
    )(table)


def _sample_tiles_kernel(tab_ref, dl_ref, dn_ref, fn_ref, *, t_new, past, tk):
    rows = DIFF_HEADS * 2 * t_new
    for h in range(DIFF_HEADS):
        r0 = h * 2 * t_new
        q = lax.broadcasted_iota(jnp.int32, (2 * t_new, tk), 0) % t_new + past
        k = lax.broadcasted_iota(jnp.int32, (2 * t_new, tk), 1) + (past - tk)
        dl_ref[r0:r0 + 2 * t_new, :] = _t5_bias(tab_ref, h, q, k)
        q = lax.broadcasted_iota(jnp.int32, (2 * t_new, t_new), 0) % t_new + past
        k = lax.broadcasted_iota(jnp.int32, (2 * t_new, t_new), 1) + past
        dn_ref[r0:r0 + 2 * t_new, :] = _t5_bias(tab_ref, h, q, k)
    q = lax.broadcasted_iota(jnp.int32, (rows, t_new), 0) % t_new
    k = lax.broadcasted_iota(jnp.int32, (rows, t_new), 1)
    fn_ref[...] = jnp.where(k <= q, 0.0, NEG)


def _sample_tiles(table, t_new, past, tk):
    rows = DIFF_HEADS * 2 * t_new
    assert rows == FOX_HEADS * t_new
    return pl.pallas_call(
        functools.partial(_sample_tiles_kernel, t_new=t_new, past=past, tk=tk),
        in_specs=[pl.BlockSpec(memory_space=pltpu.SMEM)],
        out_shape=[jax.ShapeDtypeStruct((rows, tk), F32), jax.ShapeDtypeStruct((rows, t_new), F32),
                   jax.ShapeDtypeStruct((rows, t_new), F32)],
        name="sample_score_tiles",
    )(table)


def _lambda(lqk_ref, lam_init):
    a = jnp.sum(lqk_ref[0:1, :] * lqk_ref[1:2, :], axis=-1, keepdims=True)
    b = jnp.sum(lqk_ref[2:3, :] * lqk_ref[3:4, :], axis=-1, keepdims=True)
    return jnp.exp(a) - jnp.exp(b) + lam_init


def _scores_t(k, q2_ref, s_out):
    for c0 in range(0, q2_ref.shape[0], ATT_STRIP):
        cols = slice(c0, c0 + ATT_STRIP)
        s_out[:, cols] = lax.dot_general(k, q2_ref[cols, :], (((1,), (1,)), ((), ())), preferred_element_type=F32)


def _reduce_rows(op, x):
    w, n = x.shape
    part = op(x.reshape(w // REDUCE_ROWS, REDUCE_ROWS, n), axis=0)
    return op(part, axis=0, keepdims=True)


def _softmax_pv_t(s_in, v, bias_t, m_ref, l_ref, acc_ref):
    for c0 in range(0, s_in.shape[1], ATT_STRIP):
        cols = slice(c0, c0 + ATT_STRIP)
        s = s_in[:, cols]
        if bias_t is not None:
            b0 = c0 % bias_t.shape[1]
            s = s + bias_t[:, b0:b0 + ATT_STRIP]
        m_prev = m_ref[:, cols]
        m_new = jnp.maximum(m_prev, _reduce_rows(jnp.max, s))
        alpha = jnp.exp2(m_prev - m_new)
        p = jnp.exp2(s - m_new)
        l_ref[:, cols] = alpha * l_ref[:, cols] + _reduce_rows(jnp.sum, p)
        pv = lax.dot_general(v, p.astype(BF16), (((0,), (0,)), ((), ())), preferred_element_type=F32)
        acc_ref[:, cols] = alpha * acc_ref[:, cols] + pv
        m_ref[:, cols] = m_new


def _prompt_attn_kernel(*refs, fox, tq, seq, lam_init):
    if fox:
        (q_ref, qa_ref, k_ref, ka_ref, v_ref, bt_ref, o_ref, q2_ref, s0_ref, s1_ref, m_ref, l_ref, acc_ref) = refs
    else:
        (q_ref, k_ref, v_ref, bt_ref, lqk_ref, sg_ref, o_ref, q2_ref, s0_ref, s1_ref, m_ref, l_ref, acc_ref) = refs
    s_refs = (s0_ref, s1_ref)
    n_groups = q_ref.shape[1] // LANES
    lane = lax.broadcasted_iota(jnp.int32, (tq, LANES), 1)
    lo = lane < LANES // 2
    first_rows = lax.broadcasted_iota(jnp.int32, (LANES, tq), 0) < LANES // 2
    lanes = [slice(g * LANES, (g + 1) * LANES) for g in range(n_groups)]
    if fox:
        in_a, in_b = [], []
        for g in range(n_groups):
            c0 = 12 * (pl.program_id(1) * n_groups + g)
            in_a.append(jnp.logical_and(lane >= c0, lane < c0 + 6))
            in_b.append(jnp.logical_and(lane >= c0 + 6, lane < c0 + 12))

    def scores_g(g, j, slot):
        k0 = pl.multiple_of(jnp.maximum(j, 0) * tq, tq)
        k = k_ref[pl.ds(k0, tq), lanes[g]]
        if fox:
            k = jnp.concatenate([k, ka_ref[pl.ds(k0, tq), :]], axis=1)
        _scores_t(k, q2_ref.at[g], s_refs[slot].at[g])

    def softmax_pv_g(g, j, slot, bias_index):
        k0 = pl.multiple_of(j * tq, tq)
        bias_t = None if bias_index is None else bt_ref.at[0 if fox else g, bias_index]
        _softmax_pv_t(s_refs[slot].at[g], v_ref[pl.ds(k0, tq), lanes[g]], bias_t, m_ref.at[g], l_ref.at[g],
                      acc_ref.at[g])

    def scores(j, slot):
        for g in range(n_groups):
            scores_g(g, j, slot)

    def softmax_pv(j, slot, bias_index):
        for g in range(n_groups):
            softmax_pv_g(g, j, slot, bias_index)

    def step(j_next, slot_next, j, slot, bias_index):
        for g in range(n_groups):
            scores_g(g, j_next, slot_next)
            softmax_pv_g(g, j, slot, bias_index)

    def q_block(i, carry):
        r0 = pl.multiple_of(i * tq, tq)
        for g in range(n_groups):
            q = q_ref[pl.ds(r0, tq), lanes[g]]
            zero = jnp.zeros_like(q)
            q2_ref[g, 0:tq, 0:LANES] = jnp.where(lo, q, zero)
            q2_ref[g, tq:2 * tq, 0:LANES] = jnp.where(lo, zero, q)
            if fox:
                qa = qa_ref[pl.ds(r0, tq), :]
                q2_ref[g, 0:tq, LANES:2 * LANES] = jnp.where(in_a[g], qa, zero)
                q2_ref[g, tq:2 * tq, LANES:2 * LANES] = jnp.where(in_b[g], qa, zero)
        m_ref[...] = jnp.full(m_ref.shape, NEG, F32)
        l_ref[...] = jnp.zeros(l_ref.shape, F32)
        acc_ref[...] = jnp.zeros(acc_ref.shape, F32)

        @pl.when(i == 0)
        def _():
            scores(i, 0)
            softmax_pv(i, 0, 1)

        @pl.when(i >= 1)
        def _():
            scores(i, 0)
            step(i - 1, 1, i, 0, 1)
            step(i - 2, 0, i - 1, 1, 0)
            n_far = i - 1

            def pair(u, c):
                a = i - 2 - 2 * u
                step(a - 1, 1, a, 0, None)
                step(a - 2, 0, a - 1, 1, None)
                return c

            lax.fori_loop(0, n_far // 2, pair, 0)

            @pl.when(n_far % 2 == 1)
            def _():
                softmax_pv(0, 0, None)

        for g in range(n_groups):
            top = acc_ref[g, :, 0:tq] / l_ref[g, :, 0:tq]
            bot = acc_ref[g, :, tq:2 * tq] / l_ref[g, :, tq:2 * tq]
            if fox:
                o_t = jnp.where(first_rows, top, bot)
            else:
                o_t = top - _lambda(lqk_ref, lam_init) * bot
                ms = jnp.mean(o_t * o_t, axis=0, keepdims=True)
                o_t = o_t * lax.rsqrt(ms + EPS) * sg_ref[...] * (1.0 - lam_init)
            o_ref[pl.ds(r0, tq), lanes[g]] = o_t.T.astype(BF16)
        return carry

    lax.fori_loop(0, seq // tq, q_block, 0)


def _prompt_attn(fox, batch, seq, q, k, v, bt, extra, lam_init=0.0):
    tq = ATT_TQ
    ng = ATT_GROUPS
    steps = D_MODEL // (LANES * ng)
    blk = pl.BlockSpec((seq, ng * LANES), lambda b, h: (b, h))
    if fox:
        qa, ka = extra
        aug = pl.BlockSpec((seq, LANES), lambda b, h: (b, 0))
        ins = [q, qa, k, ka, v, bt]
        in_specs = [blk, aug, blk, aug, blk, pl.BlockSpec((1, 2, tq, tq), lambda b, h: (0, 0, 0, 0))]
        kd = 2 * LANES
    else:
        lqk, sg = extra
        ins = [q, k, v, bt, lqk, sg]
        in_specs = [blk, blk, blk, pl.BlockSpec((ng, 2, tq, tq), lambda b, h: (h, 0, 0, 0)),
                    _full_spec(lqk.shape), _full_spec(sg.shape)]
        kd = LANES
    return pl.pallas_call(
        functools.partial(_prompt_attn_kernel, fox=fox, tq=tq, seq=seq, lam_init=lam_init),
        grid=(batch, steps),
        in_specs=in_specs,
        out_specs=blk,
        out_shape=jax.ShapeDtypeStruct((batch * seq, D_MODEL), BF16),
        scratch_shapes=[pltpu.VMEM((ng, 2 * tq, kd), BF16), pltpu.VMEM((ng, tq, 2 * tq), F32),
                        pltpu.VMEM((ng, tq, 2 * tq), F32), pltpu.VMEM((ng, 1, 2 * tq), F32),
                        pltpu.VMEM((ng, 1, 2 * tq), F32), pltpu.VMEM((ng, LANES, 2 * tq), F32)],
        compiler_params=_cparams(2),
        name="fox_prompt_attn" if fox else "diff_prompt_attn",
    )(*ins)


def _online_update(s, pv_fn, m_ref, l_ref, acc_ref):
    m_prev = m_ref[...]
    m_new = jnp.maximum(m_prev, jnp.max(s, axis=-1, keepdims=True))
    alpha = jnp.exp2(m_prev - m_new)
    p = jnp.exp2(s - m_new)
    l_ref[...] = alpha * l_ref[...] + jnp.sum(p, axis=-1, keepdims=True)
    acc_ref[...] = alpha * acc_ref[...] + pv_fn(p.astype(BF16))
    m_ref[...] = m_new


def _sample_attn_kernel(*refs, fox, t_new, tk, n_tiles, lam_init):
    if fox:
        (q_ref, qa_ref, kt_ref, kat_ref, vt_ref, kn_ref, kan_ref, vn_ref, bn_ref,
         o_ref, qbd_ref, m_ref, l_ref, acc_ref) = refs
    else:
        (q_ref, kt_ref, v_ref, kn_ref, vn_ref, bl_ref, bn_ref, lqk_ref, sg_ref,
         o_ref, qbd_ref, m_ref, l_ref, acc_ref) = refs
    t = pl.program_id(1)
    n_blocks = D_MODEL // 64
    rows_per_head = 2 * t_new

    @pl.when(t == 0)
    def _():
        q = q_ref[...]
        zero = jnp.zeros_like(q)
        cb = lax.broadcasted_iota(jnp.int32, q.shape, 1) // 64
        if fox:
            qa = qa_ref[0]
            la = lax.broadcasted_iota(jnp.int32, qa.shape, 1)
        for rb in range(n_blocks):
            qbd_ref[rb * t_new:(rb + 1) * t_new, 0:D_MODEL] = jnp.where(cb == rb, q, zero)
            if fox:
                sel = jnp.logical_and(la >= 6 * rb, la < 6 * rb + 6)
                qbd_ref[rb * t_new:(rb + 1) * t_new, D_MODEL:D_MODEL + LANES] = jnp.where(sel, qa, jnp.zeros_like(qa))
        m_ref[...] = jnp.full(m_ref.shape, NEG, F32)
        l_ref[...] = jnp.zeros(l_ref.shape, F32)
        acc_ref[...] = jnp.zeros(acc_ref.shape, F32)

    n_groups = D_MODEL // SAMPLE_GROUP
    group_rows = (SAMPLE_GROUP // 64) * t_new
    groups = [(slice(g * group_rows, (g + 1) * group_rows), slice(g * SAMPLE_GROUP, (g + 1) * SAMPLE_GROUP))
              for g in range(n_groups)]

    def cache_scores():
        s = jnp.concatenate([jnp.dot(qbd_ref[rows, dims], kt_ref[0, 0, dims, :].astype(BF16),
                                     preferred_element_type=F32) for rows, dims in groups], axis=0)
        if fox:
            s = s + jnp.dot(qbd_ref[:, D_MODEL:D_MODEL + LANES], kat_ref[0], preferred_element_type=F32)
        return s

    if fox:
        def cache_pv(p):
            return jnp.concatenate(
                [lax.dot_general(p[rows, :], vt_ref[0, 0, dims, :].astype(BF16), (((1,), (1,)), ((), ())),
                                 preferred_element_type=F32) for rows, dims in groups], axis=0)

        def new_pv(p):
            return jnp.concatenate([jnp.dot(p[rows, :], vn_ref[:, dims], preferred_element_type=F32)
                                    for rows, dims in groups], axis=0)
    else:
        def cache_pv(p):
            return jnp.concatenate(
                [jnp.dot(p[h * rows_per_head:(h + 1) * rows_per_head, :],
                         v_ref[0, 0, pl.ds(h, tk, stride=DIFF_HEADS), :].astype(BF16), preferred_element_type=F32)
                 for h in range(DIFF_HEADS)], axis=0)

        def new_pv(p):
            return jnp.concatenate(
                [jnp.dot(p[h * rows_per_head:(h + 1) * rows_per_head, :], vn_ref[:, h * LANES:(h + 1) * LANES],
                         preferred_element_type=F32) for h in range(DIFF_HEADS)], axis=0)

    @pl.when(t < n_tiles - 1)
    def _():
        _online_update(cache_scores(), cache_pv, m_ref, l_ref, acc_ref)

    @pl.when(t == n_tiles - 1)
    def _():
        s = cache_scores()
        if not fox:
            s = s + bl_ref[...]
        _online_update(s, cache_pv, m_ref, l_ref, acc_ref)
        kn = kn_ref[...]
        if fox:
            kn = jnp.concatenate([kn, kan_ref[0]], axis=1)
        s = lax.dot_general(qbd_ref[...], kn, (((1,), (1,)), ((), ())), preferred_element_type=F32) + bn_ref[...]
        _online_update(s, new_pv, m_ref, l_ref, acc_ref)

        lane = lax.broadcasted_iota(jnp.int32, (t_new, LANES), 1)
        lo = lane < LANES // 2
        for g in range(D_MODEL // LANES):
            ra, rb = 2 * g * t_new, (2 * g + 1) * t_new
            cols = slice(g * LANES, (g + 1) * LANES)
            c0 = (g * LANES) % SAMPLE_GROUP if fox else 0
            acc_cols = slice(c0, c0 + LANES)
            top = acc_ref[ra:ra + t_new, acc_cols] / l_ref[ra:ra + t_new, :]
            bot = acc_ref[rb:rb + t_new, acc_cols] / l_ref[rb:rb + t_new, :]
            if fox:
                o = jnp.where(lo, top, bot)
            else:
                o = _rms(top - _lambda(lqk_ref, lam_init) * bot, sg_ref[...]) * (1.0 - lam_init)
            o_ref[:, cols] = o.astype(BF16)


def _sample_attn(fox, layer, n_prompt, dec_batch, t_new, past, q, kt, v, kn, vn, bias_last, bias_new, extra,
                 lam_init=0.0):
    tk = SAMPLE_TK
    n_tiles = past // tk
    d = D_MODEL
    rows = (d // 64) * t_new
    s0 = n_prompt // t_new
    new = pl.BlockSpec((t_new, d), lambda b, t: (s0 + b, 0))
    dims_keys = pl.BlockSpec((1, 1, d, tk), lambda b, t: (layer, b, 0, t))
    out = pl.BlockSpec((t_new, d), lambda b, t: (b, 0))
    if fox:
        qa, ka, kat = extra
        new_aug = pl.BlockSpec((1, t_new, LANES), lambda b, t: (b, past // t_new, 0))
        ins = [q, qa, kt, kat, v, kn, ka, vn, bias_new]
        in_specs = [new, new_aug, dims_keys, pl.BlockSpec((1, LANES, tk), lambda b, t: (b, 0, t)), dims_keys,
                    new, new_aug, new, _full_spec(bias_new.shape)]
        kd, acc_w = d + LANES, SAMPLE_GROUP
    else:
        lqk, sg = extra
        ins = [q, kt, v, kn, vn, bias_last, bias_new, lqk, sg]
        in_specs = [new, dims_keys, pl.BlockSpec((1, 1, tk * DIFF_HEADS, LANES), lambda b, t: (layer, b, t, 0)),
                    new, new, _full_spec(bias_last.shape), _full_spec(bias_new.shape),
                    _full_spec(lqk.shape), _full_spec(sg.shape)]
        kd, acc_w = d, LANES
    return pl.pallas_call(
        functools.partial(_sample_attn_kernel, fox=fox, t_new=t_new, tk=tk, n_tiles=n_tiles, lam_init=lam_init),
        grid=(dec_batch, n_tiles),
        in_specs=in_specs,
        out_specs=out,
        out_shape=jax.ShapeDtypeStruct((dec_batch * t_new, d), BF16),
        scratch_shapes=[pltpu.VMEM((rows, kd), BF16), pltpu.VMEM((rows, 1), F32),
                        pltpu.VMEM((rows, 1), F32), pltpu.VMEM((rows, acc_w), F32)],
        compiler_params=_cparams(2),
        name="fox_sample_attn" if fox else "diff_sample_attn",
    )(*ins)


def _split3(c):
    hi = c.astype(BF16)
    r = c - hi.astype(F32)
    mid = r.astype(BF16)
    lo = (r - mid.astype(F32)).astype(BF16)
    return hi, mid, lo


def _decay_cols_kernel(lf_ref, eq_ref, ek_ref, ekt_ref, qa_ref, ka_ref, *maybe_kat_ref, length):
    h = FOX_HEADS
    carry = jnp.zeros((1, h), F32)
    for r in range(0, length, CUM_BLOCK):
        n = min(CUM_BLOCK, length - r)
        tri = (lax.broadcasted_iota(jnp.int32, (n, n), 1) <= lax.broadcasted_iota(jnp.int32, (n, n), 0))
        tri = jnp.where(tri, 1.0, 0.0).astype(BF16)
        parts = _split3(lf_ref[0, r:r + n, :])
        c = carry
        for p in parts:
            c = c + jnp.dot(tri, p, preferred_element_type=F32)
        carry = c[n - 1:n, :]
        chi, cmid, clo = _split3(c * LOG2E)
        ones = jnp.ones((n, h), BF16)
        qa = jnp.zeros((n, LANES), F32)
        ka = jnp.zeros((n, LANES), F32)
        kat = jnp.zeros((LANES, n), F32)
        for j, (pq, pk) in enumerate(((chi, -chi), (cmid, -cmid), (clo, -clo), (ones, ones))):
            qa = qa + jnp.dot(pq, eq_ref[j * h:(j + 1) * h, :], preferred_element_type=F32)
            ka = ka + jnp.dot(pk, ek_ref[j * h:(j + 1) * h, :], preferred_element_type=F32)
            if maybe_kat_ref:
                kat = kat + lax.dot_general(ekt_ref[:, j * h:(j + 1) * h], pk, (((1,), (1,)), ((), ())),
                                            preferred_element_type=F32)
        qa_ref[0, r:r + n, :] = qa.astype(BF16)
        ka_ref[0, r:r + n, :] = ka.astype(BF16)
        if maybe_kat_ref:
            maybe_kat_ref[0][0, :, r:r + n] = kat.astype(BF16)


def _placement():
    h = FOX_HEADS
    eq = np.zeros((4 * h, LANES), np.float32)
    ek = np.zeros((4 * h, LANES), np.float32)
    for head in range(h):
        for part in range(3):
            eq[part * h + head, 6 * head + part] = 1.0
            ek[part * h + head, 6 * head + 3 + part] = 1.0
            eq[3 * h + head, 6 * head + 3 + part] = 1.0
            ek[3 * h + head, 6 * head + part] = 1.0
    return jnp.asarray(eq, BF16), jnp.asarray(ek, BF16), jnp.asarray(ek.T, BF16)


def _decay_cols(lf, keys_transposed):
    nb, length, h = lf.shape
    eq, ek, ekt = _placement()
    spec = pl.BlockSpec((1, length, LANES), lambda b: (b, 0, 0))
    out_specs = [spec, spec]
    out_shape = [jax.ShapeDtypeStruct((nb, length, LANES), BF16)] * 2
    if keys_transposed:
        out_specs.append(pl.BlockSpec((1, LANES, length), lambda b: (b, 0, 0)))
        out_shape.append(jax.ShapeDtypeStruct((nb, LANES, length), BF16))
    return pl.pallas_call(
        functools.partial(_decay_cols_kernel, length=length),
        grid=(nb,),
        in_specs=[pl.BlockSpec((1, length, h), lambda b: (b, 0, 0)), _full_spec(eq.shape), _full_spec(ek.shape),
                  _full_spec(ekt.shape)],
        out_specs=out_specs,
        out_shape=out_shape,
        compiler_params=_cparams(1),
        name="fox_decay_cols",
    )(lf, eq, ek, ekt)


def _conv_dw_kernel(u_ref, halo_ref, wdw_ref, bdw_ref, g_ref, b_ref, o_ref, ext_ref, sh_ref, y_ref, *, tm,
                    steps_per_seq):
    halo = halo_ref[...]
    if steps_per_seq:
        halo = jnp.where(pl.program_id(0) % steps_per_seq == 0, 0.0, halo)
    ext_ref[0:HALO, :] = halo
    ext_ref[HALO:HALO + tm, :] = u_ref[...]
    span = sh_ref.shape[1]
    for s in range(1, SUBLANES):
        sh_ref[s - 1] = ext_ref[s:s + span, :]
    d = u_ref.shape[1]
    first = HALO - (CONV_WIDTH - 1)
    for rc in range(tm // CONV_RC):
        for lc in range(d // CONV_LC):
            cols = slice(lc * CONV_LC, (lc + 1) * CONV_LC)
            acc = jnp.zeros((CONV_RC, CONV_LC), F32)
            for w in range(CONV_WIDTH):
                a, s = divmod(first + w, SUBLANES)
                src = ext_ref if s == 0 else sh_ref.at[s - 1]
                r = rc * CONV_RC + a * SUBLANES
                acc = acc + src[r:r + CONV_RC, cols] * wdw_ref[w:w + 1, cols]
            y_ref[rc * CONV_RC:(rc + 1) * CONV_RC, cols] = acc
    y = y_ref[...] + bdw_ref[...]
    yc = y - jnp.mean(y, axis=-1, keepdims=True)
    yn = yc * lax.rsqrt(jnp.mean(yc * yc, axis=-1, keepdims=True) + EPS) * g_ref[...] + b_ref[...]
    o_ref[...] = (yn * (1.0 / (1.0 + jnp.exp(-yn)))).astype(BF16)


def _conv_dw(u, row0, n_rows, tm, halo, halo_index, steps_per_seq, wdw, bdw, g, b):
    d = u.shape[1]
    b0 = row0 // tm
    return pl.pallas_call(
        functools.partial(_conv_dw_kernel, tm=tm, steps_per_seq=steps_per_seq),
        grid=(n_rows // tm,),
        in_specs=[pl.BlockSpec((tm, d), lambda i: (b0 + i, 0)), pl.BlockSpec((HALO, d), lambda i: (halo_index(i), 0)),
                  _full_spec(wdw.shape), _full_spec(bdw.shape), _full_spec(g.shape), _full_spec(b.shape)],
        out_specs=pl.BlockSpec((tm, d), lambda i: (i, 0)),
        out_shape=jax.ShapeDtypeStruct((n_rows, d), BF16),
        scratch_shapes=[pltpu.VMEM((HALO + tm, d), F32), pltpu.VMEM((SUBLANES - 1, HALO + tm - SUBLANES, d), F32),
                        pltpu.VMEM((tm, d), F32)],
        compiler_params=_cparams(1),
        name="conv_dw",
    )(u, halo, wdw, bdw, g, b)


def _dims_keys(cache):
    nd = cache.ndim
    t = jnp.transpose(cache, (0, 1) + tuple(range(3, nd)) + (2,))
    return t.reshape(cache.shape[0], cache.shape[1], D_MODEL, cache.shape[2])


def kernel(x_prompt, x_sample, cache_diff_k, cache_diff_v, state_conv, cache_fox_k, cache_fox_v, cache_fox_logf, rel_bias, norm_g, final_g, diff_w_in, diff_w_out, diff_lq1, diff_lk1, diff_lq2, diff_lk2, diff_subln_g, conv_w_pw1, conv_b_pw1, conv_w_dw, conv_b_dw, conv_ln_g, conv_ln_b, conv_w_pw2, conv_b_pw2, fox_w_in, fox_b_f, fox_w_out, mlp_w1, mlp_w2):
    B, S, D = x_prompt.shape
    Bd, T, _ = x_sample.shape
    P = cache_diff_k.shape[2]
    depth = norm_g.shape[0]
    n_p, n_s = B * S, Bd * T
    assert D == D_MODEL and S % (2 * ATT_TQ) == 0 and P % SAMPLE_TK == 0 and n_p % ROW_TILE == 0
    assert n_s % ROW_TILE == 0 and T % 8 == 0 and T >= CONV_WIDTH - 1 and S % ROW_TILE == 0

    x = jnp.concatenate([x_prompt.reshape(n_p, D), x_sample.reshape(n_s, D)], axis=0)
    zero_bias = jnp.zeros((1, D), F32)
    final_gain = final_g.reshape(1, D)

    bt_diff, bt_fox = _prompt_tiles(rel_bias, ATT_TQ)
    sb_last, sb_new, sf_new = _sample_tiles(rel_bias, T, P, SAMPLE_TK)

    diff_kt = _dims_keys(cache_diff_k)
    diff_v = cache_diff_v.reshape(cache_diff_v.shape[0], Bd, P * DIFF_HEADS, 2 * DIFF_HEAD_DIM)
    fox_kt = _dims_keys(cache_fox_k)
    fox_vt = _dims_keys(cache_fox_v)

    n_diff, n_fox = diff_w_in.shape[0], fox_w_in.shape[0]
    dk_shape = (1, n_p, D)
    dv_shape = (n_diff, n_p, DIFF_HEADS, 2 * DIFF_HEAD_DIM)
    fkv_shape = (n_fox, n_p, FOX_HEADS, FOX_HEAD_DIM)
    dv_new, fk_new, fv_new = None, None, None
    dk_p, dk_s, dv_s = [], [], []
    cv_p, cv_s = [], []
    fl_p, fk_s, fv_s, fl_s = [], [], [], []
    y = None

    for i in range(depth):
        kind, j = i % N_MIXERS, i // N_MIXERS
        g1 = norm_g[i, 0].reshape(1, D)
        g2 = norm_g[i, 1].reshape(1, D)
        if kind == 0:
            lam_init = 0.8 - 0.6 * math.exp(-0.3 * i)
            q, k_p, k_s, dv_new, v_s, kb, vb = _qkv_inproj(
                x, n_p, g1, diff_w_in[j].astype(BF16), DIFF_SCALE, (dk_shape, 0), (dv_shape, j),
                prev=None if dv_new is None else {3: dv_new})
            dk_p.append(k_p.reshape(B, S, DIFF_HEADS, 2, DIFF_HEAD_DIM))
            lqk = jnp.stack([diff_lq1[j], diff_lk1[j], diff_lq2[j], diff_lk2[j]]).astype(F32)
            sg = diff_subln_g[j].reshape(1, 2 * DIFF_HEAD_DIM).astype(F32)
            o_p = _prompt_attn(False, B, S, q, kb, vb, bt_diff, (lqk, sg.reshape(2 * DIFF_HEAD_DIM, 1)), lam_init)
            o_s = _sample_attn(False, j, n_p, Bd, T, P, q, diff_kt, diff_v, kb, vb, sb_last, sb_new, (lqk, sg),
                               lam_init)
            wo, bo = diff_w_out[j].astype(BF16), zero_bias
            dk_s.append(k_s.reshape(Bd, T, DIFF_HEADS, 2, DIFF_HEAD_DIM))
            dv_s.append(v_s.reshape(Bd, T, DIFF_HEADS, 2 * DIFF_HEAD_DIM))
        elif kind == 1:
            u = _conv_inproj(x, g1, conv_w_pw1[j].astype(BF16), conv_b_pw1[j].reshape(1, 2 * D))
            wdw = jnp.pad(conv_w_dw[j], ((0, HALO - CONV_WIDTH), (0, 0)))
            args = (wdw, conv_b_dw[j].reshape(1, D), conv_ln_g[j].reshape(1, D), conv_ln_b[j].reshape(1, D))
            per_tile = ROW_TILE // HALO
            o_p = _conv_dw(u, 0, n_p, ROW_TILE, u, lambda t: jnp.maximum(t * per_tile - 1, 0), S // ROW_TILE, *args)
            state = jnp.pad(state_conv[j], ((0, 0), (HALO - (CONV_WIDTH - 1), 0), (0, 0))).reshape(Bd * HALO, D)
            o_s = _conv_dw(u, n_p, n_s, T, state, lambda t: t, 0, *args)
            wo, bo = conv_w_pw2[j].astype(BF16), conv_b_pw2[j].reshape(1, D)
            keep = CONV_WIDTH - 1
            cv_p.append(jnp.stack([u[(b + 1) * S - keep:(b + 1) * S] for b in range(B)]))
            cv_s.append(u[n_p:].reshape(Bd, T, D)[:, T - keep:])
        else:
            w_in = fox_w_in[j]
            wf = jnp.pad(w_in[:, 3 * D:], ((0, 0), (0, LANES - FOX_HEADS))).astype(BF16)
            bf = jnp.pad(fox_b_f[j].astype(F32), (0, LANES - FOX_HEADS)).reshape(1, LANES)
            w_qkv = w_in[:, :3 * D].astype(BF16)
            w_rot = [jnp.roll(w_qkv[:, c * D:(c + 1) * D], -FOX_HEAD_DIM, axis=1) for c in (1, 2)]
            q, fk_new, k_s, fv_new, v_s, kb, vb, lf = _qkv_inproj(
                x, n_p, g1, jnp.concatenate([w_qkv] + w_rot, axis=1), FOX_SCALE, (fkv_shape, j), (fkv_shape, j),
                prev=None if fk_new is None else {1: fk_new, 3: fv_new}, forget=(wf, bf))
            lf_p = lf[:n_p].reshape(B, S, FOX_HEADS)
            lf_s = lf[n_p:].reshape(Bd, T, FOX_HEADS)
            qa_p, ka_p = _decay_cols(lf_p, False)
            decay_s = _decay_cols(jnp.concatenate([cache_fox_logf[j].astype(F32), lf_s], axis=1), True)
            o_p = _prompt_attn(True, B, S, q, kb, vb, bt_fox, (qa_p.reshape(n_p, LANES), ka_p.reshape(n_p, LANES)))
            o_s = _sample_attn(True, j, n_p, Bd, T, P, q, fox_kt, fox_vt, kb, vb, None, sf_new, decay_s)
            wo, bo = fox_w_out[j].astype(BF16), zero_bias
            fl_p.append(lf_p)
            fk_s.append(k_s.reshape(Bd, T, FOX_HEADS, FOX_HEAD_DIM))
            fv_s.append(v_s.reshape(Bd, T, FOX_HEADS, FOX_HEAD_DIM))
            fl_s.append(lf_s)
        final = i == depth - 1
        out = _mix_mlp(x, o_p, o_s, wo, bo, g2, mlp_w1[i].astype(BF16), mlp_w2[i].astype(BF16), final_gain, final)
        if final:
            y = out
        else:
            x = out[0]

    y_prompt = y[0].reshape(B, S, D)
    y_sample = y[1].reshape(Bd, T, D)
    new_diff_k_p = jnp.stack(dk_p)
    new_diff_v_p = dv_new.reshape(n_diff, B, S, DIFF_HEADS, 2 * DIFF_HEAD_DIM)
    new_fox_k_p = fk_new.reshape(n_fox, B, S, FOX_HEADS, FOX_HEAD_DIM)
    new_fox_v_p = fv_new.reshape(n_fox, B, S, FOX_HEADS, FOX_HEAD_DIM)
    return (y_prompt, y_sample, new_diff_k_p, new_diff_v_p, jnp.stack(cv_p), new_fox_k_p,
            new_fox_v_p, jnp.stack(fl_p), jnp.stack(dk_s), jnp.stack(dv_s), jnp.stack(cv_s),
            jnp.stack(fk_s), jnp.stack(fv_s), jnp.stack(fl_s))
```

```python
import functools
import math

import numpy as np
import jax
import jax.numpy as jnp
from jax import lax
from jax.experimental import pallas as pl
from jax.experimental.pallas import tpu as pltpu

F32 = jnp.float32
BF16 = jnp.bfloat16

D_MODEL = 1024
CHUNK = 64
EPS = 1e-6
DIFF_HEADS = 8
DIFF_HEAD_DIM = 64
LOG2E = math.log2(math.e)
DIFF_SCALE = DIFF_HEAD_DIM ** -0.5
CONV_WIDTH = 31
FOX_HEADS = 16
FOX_HEAD_DIM = 64
FOX_SCALE = FOX_HEAD_DIM ** -0.5
REL_BUCKETS = 32
N_MIXERS = 3

LANES = 128
SUBLANES = 8
NEG = -1e30
ROW_TILE = 512
ATT_TQ = 512
ATT_STRIP = 512
ATT_GROUPS = 2
REDUCE_ROWS = 64
SAMPLE_TK = 2048
SAMPLE_GROUP = 256
FF_CHUNK = 1024
CUM_BLOCK = 256
DECAY_COLS = 6
HALO = 32
CONV_RC = 32
CONV_LC = 256
VMEM_LIMIT = 56 * 1024 * 1024

_T5_STEPS = (12, 16, 23, 32, 46, 64, 91)


def _cparams(n_axes):
    return pltpu.CompilerParams(dimension_semantics=("arbitrary",) * n_axes, vmem_limit_bytes=VMEM_LIMIT)


def _rms(x, g):
    return x * lax.rsqrt(jnp.mean(x * x, axis=-1, keepdims=True) + EPS) * g


def _full_spec(shape):
    nd = len(shape)
    return pl.BlockSpec(shape, lambda *_: (0,) * nd)


def _split_specs(n_prompt_tiles, tm, d):
    prompt = pl.BlockSpec((tm, d), lambda i: (jnp.minimum(i, n_prompt_tiles - 1), 0))
    sample = pl.BlockSpec((tm, d), lambda i: (jnp.maximum(i - n_prompt_tiles, 0), 0))
    return prompt, sample


def _split_store(n_prompt_tiles, value, prompt_ref, sample_ref):
    i = pl.program_id(0)

    @pl.when(i < n_prompt_tiles)
    def _():
        prompt_ref[...] = value

    @pl.when(i >= n_prompt_tiles)
    def _():
        sample_ref[...] = value


def _store_prompt_cache(value, prompt_ref, rotated_fn):
    if len(prompt_ref.shape) == 3:
        prompt_ref[0] = value
        return
    heads, width = prompt_ref.shape[2:]
    if width == LANES:
        for h in range(heads):
            prompt_ref[0, :, h, :] = value[:, h * width:(h + 1) * width]
    else:
        rotated = rotated_fn()
        for h in range(heads):
            src = value if h % 2 == 0 else rotated
            prompt_ref[0, :, h, :] = src[:, (h // 2) * LANES:(h // 2) * LANES + width]


def _qkv_inproj_kernel(*refs, n_pt, scale, n_prev, forget):
    x_ref, g_ref, w_ref = refs[:3]
    refs = refs[3:]
    if forget:
        wf_ref, bf_ref = refs[:2]
        refs = refs[2:]
    refs = refs[n_prev:]
    q_ref, kp_ref, ks_ref, vp_ref, vs_ref, kb_ref, vb_ref = refs[:7]
    d = D_MODEL

    def project(prompt):
        xn = _rms(x_ref[...], g_ref[...]).astype(BF16)
        q = jnp.dot(xn, w_ref[:, 0:d], preferred_element_type=F32)
        q_ref[...] = (q * (scale * LOG2E)).astype(BF16)
        for c, new_p, new_s, bf_ref_out in ((1, kp_ref, ks_ref, kb_ref), (2, vp_ref, vs_ref, vb_ref)):
            val = jnp.dot(xn, w_ref[:, c * d:(c + 1) * d], preferred_element_type=F32)
            if prompt:
                _store_prompt_cache(val, new_p, lambda c=c: jnp.dot(xn, w_ref[:, (c + 2) * d:(c + 3) * d],
                                                                   preferred_element_type=F32))
            else:
                new_s[...] = val
            bf_ref_out[...] = val.astype(BF16)
        if forget:
            z = jnp.dot(xn, wf_ref[...], preferred_element_type=F32) + bf_ref[...]
            lf = jnp.minimum(z, 0.0) - jnp.log1p(jnp.exp(-jnp.abs(z)))
            refs[7][...] = lf

    pl.when(pl.program_id(0) < n_pt)(lambda: project(True))
    pl.when(pl.program_id(0) >= n_pt)(lambda: project(False))


def _qkv_inproj(x, n_p, g, w, scale, k_slab, v_slab, prev=None, forget=()):
    n, d = x.shape
    tm = ROW_TILE
    n_pt = n_p // tm
    row = pl.BlockSpec((tm, d), lambda i: (i, 0))
    _, sample = _split_specs(n_pt, tm, d)
    prev = prev or {}

    def slab(shape, layer):
        rest = tuple(shape[2:])
        zeros = (0,) * len(rest)
        return pl.BlockSpec((1, tm) + rest, lambda i: (layer, jnp.minimum(i, n_pt - 1)) + zeros)

    f32_s = jax.ShapeDtypeStruct((n - n_p, d), F32)
    bf_all = jax.ShapeDtypeStruct((n, d), BF16)
    ins = [x, g, w] + list(forget) + list(prev.values())
    w_spec = pl.BlockSpec(w.shape, lambda i: (0, 0), pipeline_mode=pl.Buffered(1))
    in_specs = ([row, _full_spec((1, d)), w_spec] + [_full_spec(a.shape) for a in forget]
                + [pl.BlockSpec(memory_space=pl.ANY)] * len(prev))
    out_specs = [row, slab(*k_slab), sample, slab(*v_slab), sample, row, row]
    out_shape = [bf_all, jax.ShapeDtypeStruct(k_slab[0], F32), f32_s, jax.ShapeDtypeStruct(v_slab[0], F32), f32_s,
                 bf_all, bf_all]
    if forget:
        out_specs.append(pl.BlockSpec((tm, LANES), lambda i: (i, 0)))
        out_shape.append(jax.ShapeDtypeStruct((n, LANES), F32))
    first_prev = 3 + len(forget)
    aliases = {first_prev + pos: out_index for pos, out_index in enumerate(prev)}
    return pl.pallas_call(
        functools.partial(_qkv_inproj_kernel, n_pt=n_pt, scale=scale, n_prev=len(prev), forget=bool(forget)),
        grid=(n // tm,),
        in_specs=in_specs,
        out_specs=out_specs,
        out_shape=out_shape,
        input_output_aliases=aliases,
        compiler_params=_cparams(1),
        name="fox_inproj" if forget else "diff_inproj",
    )(*ins)


def _conv_inproj_kernel(x_ref, g_ref, w_ref, b_ref, u_ref):
    d = D_MODEL
    xn = _rms(x_ref[...], g_ref[...]).astype(BF16)
    a = jnp.dot(xn, w_ref[:, 0:d], preferred_element_type=F32) + b_ref[:, 0:d]
    gate = jnp.dot(xn, w_ref[:, d:2 * d], preferred_element_type=F32) + b_ref[:, d:2 * d]
    u_ref[...] = a * (1.0 / (1.0 + jnp.exp(-gate)))


def _conv_inproj(x, g, w, b):
    n, d = x.shape
    tm = ROW_TILE
    row = pl.BlockSpec((tm, d), lambda i: (i, 0))
    return pl.pallas_call(
        _conv_inproj_kernel,
        grid=(n // tm,),
        in_specs=[row, _full_spec((1, d)), _full_spec(w.shape), _full_spec(b.shape)],
        out_specs=row,
        out_shape=jax.ShapeDtypeStruct((n, d), F32),
        compiler_params=_cparams(1),
        name="conv_inproj",
    )(x, g, w, b)


def _mix_mlp_kernel(x_ref, op_ref, os_ref, wo_ref, bo_ref, g_ref, w1_ref, w2_ref, gf_ref, *out_refs, n_pt, final):
    o = jnp.where(pl.program_id(0) < n_pt, op_ref[...], os_ref[...])
    x1 = x_ref[...] + jnp.dot(o, wo_ref[...], preferred_element_type=F32) + bo_ref[...]
    xn = _rms(x1, g_ref[...]).astype(BF16)
    acc = x1
    d_ff = w1_ref.shape[1]
    for c in range(d_ff // FF_CHUNK):
        h = jnp.dot(xn, w1_ref[:, c * FF_CHUNK:(c + 1) * FF_CHUNK], preferred_element_type=F32)
        h = jnp.square(jnp.maximum(h, 0.0)).astype(BF16)
        acc = acc + jnp.dot(h, w2_ref[c * FF_CHUNK:(c + 1) * FF_CHUNK, :], preferred_element_type=F32)
    if final:
        _split_store(n_pt, _rms(acc, gf_ref[...]), *out_refs)
    else:
        out_refs[0][...] = acc


def _mix_mlp(x, o_p, o_s, wo, bo, g, w1, w2, gf, final):
    n, d = x.shape
    tm = ROW_TILE
    n_p = o_p.shape[0]
    n_pt = n_p // tm
    row = pl.BlockSpec((tm, d), lambda i: (i, 0))
    prompt, sample = _split_specs(n_pt, tm, d)
    resident = lambda a: pl.BlockSpec(a.shape, lambda i: (0,) * a.ndim, pipeline_mode=pl.Buffered(1))
    if final:
        out_specs = [prompt, sample]
        out_shape = [jax.ShapeDtypeStruct((n_p, d), F32), jax.ShapeDtypeStruct((n - n_p, d), F32)]
    else:
        out_specs = [row]
        out_shape = [jax.ShapeDtypeStruct((n, d), F32)]
    return pl.pallas_call(
        functools.partial(_mix_mlp_kernel, n_pt=n_pt, final=final),
        grid=(n // tm,),
        in_specs=[row, prompt, sample, resident(wo), _full_spec(bo.shape), _full_spec(g.shape), resident(w1),
                  resident(w2), _full_spec(gf.shape)],
        out_specs=out_specs,
        out_shape=out_shape,
        compiler_params=_cparams(1),
        name="mix_mlp",
    )(x, o_p, o_s, wo, bo, g, w1, w2, gf)


def _t5_bias(tab_ref, h, qpos, kpos):
    half = REL_BUCKETS // 2
    n = qpos - kpos
    off = jnp.where(n < 0, half, 0)
    n = jnp.abs(n)
    large = jnp.full(n.shape, half // 2, jnp.int32)
    for t in _T5_STEPS:
        large = large + jnp.where(n >= t, 1, 0)
    bucket = off + jnp.where(n < half // 2, n, large)
    far = tab_ref[half - 1, h]
    out = jnp.zeros(n.shape, F32)
    for b in range(REL_BUCKETS):
        out = jnp.where(bucket == b, (tab_ref[b, h] - far) * LOG2E, out)
    visible = (kpos // CHUNK) <= (qpos // CHUNK)
    return jnp.where(visible, out, NEG)


def _prompt_tiles_kernel(tab_ref, bt_ref, cm_ref, *, tq):
    h = pl.program_id(0)
    for v in range(2):
        kpos = lax.broadcasted_iota(jnp.int32, (tq, tq), 0)
        qpos = lax.broadcasted_iota(jnp.int32, (tq, tq), 1) + (1 - v) * tq
        bt_ref[0, v] = _t5_bias(tab_ref, h, qpos, kpos)

    @pl.when(h == 0)
    def _():
        kpos = lax.broadcasted_iota(jnp.int32, (tq, tq), 0)
        qpos = lax.broadcasted_iota(jnp.int32, (tq, tq), 1)
        cm_ref[0, 0] = jnp.zeros((tq, tq), F32)
        cm_ref[0, 1] = jnp.where(kpos <= qpos, 0.0, NEG)


def _prompt_tiles(table, tq):
    return pl.pallas_call(
        functools.partial(_prompt_tiles_kernel, tq=tq),
        grid=(DIFF_HEADS,),
        in_specs=[pl.BlockSpec(memory_space=pltpu.SMEM)],
        out_specs=[pl.BlockSpec((1, 2, tq, tq), lambda h: (h, 0, 0, 0)),
                   pl.BlockSpec((1, 2, tq, tq), lambda h: (0, 0, 0, 0))],
        out_shape=[jax.ShapeDtypeStruct((DIFF_HEADS, 2, tq, tq), F32), jax.ShapeDtypeStruct((1, 2, tq, tq), F32)],
        compiler_params=_cparams(1),
        name="prompt_score_tiles",
    )(table)


def _sample_tiles_kernel(tab_ref, dl_ref, dn_ref, fn_ref, *, t_new, past, tk):
    rows = DIFF_HEADS * 2 * t_new
    for h in range(DIFF_HEADS):
        r0 = h * 2 * t_new
        q = lax.broadcasted_iota(jnp.int32, (2 * t_new, tk), 0) % t_new + past
        k = lax.broadcasted_iota(jnp.int32, (2 * t_new, tk), 1) + (past - tk)
        dl_ref[r0:r0 + 2 * t_new, :] = _t5_bias(tab_ref, h, q, k)
        q = lax.broadcasted_iota(jnp.int32, (2 * t_new, t_new), 0) % t_new + past
        k = lax.broadcasted_iota(jnp.int32, (2 * t_new, t_new), 1) + past
        dn_ref[r0:r0 + 2 * t_new, :] = _t5_bias(tab_ref, h, q, k)
    q = lax.broadcasted_iota(jnp.int32, (rows, t_new), 0) % t_new
    k = lax.broadcasted_iota(jnp.int32, (rows, t_new), 1)
    fn_ref[...] = jnp.where(k <= q, 0.0, NEG)


def _sample_tiles(table, t_new, past, tk):
    rows = DIFF_HEADS * 2 * t_new
    assert rows == FOX_HEADS * t_new
    return pl.pallas_call(
        functools.partial(_sample_tiles_kernel, t_new=t_new, past=past, tk=tk),
        in_specs=[pl.BlockSpec(memory_space=pltpu.SMEM)],
        out_shape=[jax.ShapeDtypeStruct((rows, tk), F32), jax.ShapeDtypeStruct((rows, t_new), F32),
                   jax.ShapeDtypeStruct((rows, t_new), F32)],
        name="sample_score_tiles",
    )(table)


def _lambda(lqk_ref, lam_init):
    a = jnp.sum(lqk_ref[0:1, :] * lqk_ref[1:2, :], axis=-1, keepdims=True)
    b = jnp.sum(lqk_ref[2:3, :] * lqk_ref[3:4, :], axis=-1, keepdims=True)
    return jnp.exp(a) - jnp.exp(b) + lam_init


def _scores_t(k, q2_ref, s_out):
    for c0 in range(0, q2_ref.shape[0], ATT_STRIP):
        cols = slice(c0, c0 + ATT_STRIP)
        s_out[:, cols] = lax.dot_general(k, q2_ref[cols, :], (((1,), (1,)), ((), ())), preferred_element_type=F32)


def _reduce_rows(op, x):
    w, n = x.shape
    part = op(x.reshape(w // REDUCE_ROWS, REDUCE_ROWS, n), axis=0)
    return op(part, axis=0, keepdims=True)


def _softmax_pv_t(s_in, v, bias_t, m_ref, l_ref, acc_ref):
    for c0 in range(0, s_in.shape[1], ATT_STRIP):
        cols = slice(c0, c0 + ATT_STRIP)
        s = s_in[:, cols]
        if bias_t is not None:
            b0 = c0 % bias_t.shape[1]
            s = s + bias_t[:, b0:b0 + ATT_STRIP]
        m_prev = m_ref[:, cols]
        m_new = jnp.maximum(m_prev, _reduce_rows(jnp.max, s))
        alpha = jnp.exp2(m_prev - m_new)
        p = jnp.exp2(s - m_new)
        l_ref[:, cols] = alpha * l_ref[:, cols] + _reduce_rows(jnp.sum, p)
        pv = lax.dot_general(v, p.astype(BF16), (((0,), (0,)), ((), ())), preferred_element_type=F32)
        acc_ref[:, cols] = alpha * acc_ref[:, cols] + pv
        m_ref[:, cols] = m_new


def _prompt_attn_kernel(*refs, fox, tq, seq, lam_init):
    if fox:
        (q_ref, qa_ref, k_ref, ka_ref, v_ref, bt_ref, o_ref, q2_ref, s0_ref, s1_ref, m_ref, l_ref, acc_ref) = refs
    else:
        (q_ref, k_ref, v_ref, bt_ref, lqk_ref, sg_ref, o_ref, q2_ref, s0_ref, s1_ref, m_ref, l_ref, acc_ref) = refs
    s_refs = (s0_ref, s1_ref)
    n_groups = q_ref.shape[1] // LANES
    lane = lax.broadcasted_iota(jnp.int32, (tq, LANES), 1)
    lo = lane < LANES // 2
    first_rows = lax.broadcasted_iota(jnp.int32, (LANES, tq), 0) < LANES // 2
    lanes = [slice(g * LANES, (g + 1) * LANES) for g in range(n_groups)]
    if fox:
        in_a, in_b = [], []
        for g in range(n_groups):
            c0 = 12 * (pl.program_id(1) * n_groups + g)
            in_a.append(jnp.logical_and(lane >= c0, lane < c0 + 6))
            in_b.append(jnp.logical_and(lane >= c0 + 6, lane < c0 + 12))

    def scores_g(g, j, slot):
        k0 = pl.multiple_of(jnp.maximum(j, 0) * tq, tq)
        k = k_ref[pl.ds(k0, tq), lanes[g]]
        if fox:
            k = jnp.concatenate([k, ka_ref[pl.ds(k0, tq), :]], axis=1)
        _scores_t(k, q2_ref.at[g], s_refs[slot].at[g])

    def softmax_pv_g(g, j, slot, bias_index):
        k0 = pl.multiple_of(j * tq, tq)
        bias_t = None if bias_index is None else bt_ref.at[0 if fox else g, bias_index]
        _softmax_pv_t(s_refs[slot].at[g], v_ref[pl.ds(k0, tq), lanes[g]], bias_t, m_ref.at[g], l_ref.at[g],
                      acc_ref.at[g])

    def scores(j, slot):
        for g in range(n_groups):
            scores_g(g, j, slot)

    def softmax_pv(j, slot, bias_index):
        for g in range(n_groups):
            softmax_pv_g(g, j, slot, bias_index)

    def step(j_next, slot_next, j, slot, bias_index):
        for g in range(n_groups):
            scores_g(g, j_next, slot_next)
            softmax_pv_g(g, j, slot, bias_index)

    def q_block(i, carry):
        r0 = pl.multiple_of(i * tq, tq)
        for g in range(n_groups):
            q = q_ref[pl.ds(r0, tq), lanes[g]]
            zero = jnp.zeros_like(q)
            q2_ref[g, 0:tq, 0:LANES] = jnp.where(lo, q, zero)
            q2_ref[g, tq:2 * tq, 0:LANES] = jnp.where(lo, zero, q)
            if fox:
                qa = qa_ref[pl.ds(r0, tq), :]
                q2_ref[g, 0:tq, LANES:2 * LANES] = jnp.where(in_a[g], qa, zero)
                q2_ref[g, tq:2 * tq, LANES:2 * LANES] = jnp.where(in_b[g], qa, zero)
        m_ref[...] = jnp.full(m_ref.shape, NEG, F32)
        l_ref[...] = jnp.zeros(l_ref.shape, F32)
        acc_ref[...] = jnp.zeros(acc_ref.shape, F32)

        @pl.when(i == 0)
        def _():
            scores(i, 0)
            softmax_pv(i, 0, 1)

        @pl.when(i >= 1)
        def _():
            scores(i, 0)
            step(i - 1, 1, i, 0, 1)
            step(i - 2, 0, i - 1, 1, 0)
            n_far = i - 1

            def pair(u, c):
                a = i - 2 - 2 * u
                step(a - 1, 1, a, 0, None)
                step(a - 2, 0, a - 1, 1, None)
                return c

            lax.fori_loop(0, n_far // 2, pair, 0)

            @pl.when(n_far % 2 == 1)
            def _():
                softmax_pv(0, 0, None)

        for g in range(n_groups):
            top = acc_ref[g, :, 0:tq] / l_ref[g, :, 0:tq]
            bot = acc_ref[g, :, tq:2 * tq] / l_ref[g, :, tq:2 * tq]
            if fox:
                o_t = jnp.where(first_rows, top, bot)
            else:
                o_t = top - _lambda(lqk_ref, lam_init) * bot
                ms = jnp.mean(o_t * o_t, axis=0, keepdims=True)
                o_t = o_t * lax.rsqrt(ms + EPS) * sg_ref[...] * (1.0 - lam_init)
            o_ref[pl.ds(r0, tq), lanes[g]] = o_t.T.astype(BF16)
        return carry

    lax.fori_loop(0, seq // tq, q_block, 0)


def _prompt_attn(fox, batch, seq, q, k, v, bt, extra, lam_init=0.0):
    tq = ATT_TQ
    ng = ATT_GROUPS
    steps = D_MODEL // (LANES * ng)
    blk = pl.BlockSpec((seq, ng * LANES), lambda b, h: (b, h))
    if fox:
        qa, ka = extra
        aug = pl.BlockSpec((seq, LANES), lambda b, h: (b, 0))
        ins = [q, qa, k, ka, v, bt]
        in_specs = [blk, aug, blk, aug, blk, pl.BlockSpec((1, 2, tq, tq), lambda b, h: (0, 0, 0, 0))]
        kd = 2 * LANES
    else:
        lqk, sg = extra
        ins = [q, k, v, bt, lqk, sg]
        in_specs = [blk, blk, blk, pl.BlockSpec((ng, 2, tq, tq), lambda b, h: (h, 0, 0, 0)),
                    _full_spec(lqk.shape), _full_spec(sg.shape)]
        kd = LANES
    return pl.pallas_call(
        functools.partial(_prompt_attn_kernel, fox=fox, tq=tq, seq=seq, lam_init=lam_init),
        grid=(batch, steps),
        in_specs=in_specs,
        out_specs=blk,
        out_shape=jax.ShapeDtypeStruct((batch * seq, D_MODEL), BF16),
        scratch_shapes=[pltpu.VMEM((ng, 2 * tq, kd), BF16), pltpu.VMEM((ng, tq, 2 * tq), F32),
                        pltpu.VMEM((ng, tq, 2 * tq), F32), pltpu.VMEM((ng, 1, 2 * tq), F32),
                        pltpu.VMEM((ng, 1, 2 * tq), F32), pltpu.VMEM((ng, LANES, 2 * tq), F32)],
        compiler_params=_cparams(2),
        name="fox_prompt_attn" if fox else "diff_prompt_attn",
    )(*ins)


def _online_update(s, pv_fn, m_ref, l_ref, acc_ref):
    m_prev = m_ref[...]
    m_new = jnp.maximum(m_prev, jnp.max(s, axis=-1, keepdims=True))
    alpha = jnp.exp2(m_prev - m_new)
    p = jnp.exp2(s - m_new)
    l_ref[...] = alpha * l_ref[...] + jnp.sum(p, axis=-1, keepdims=True)
    acc_ref[...] = alpha * acc_ref[...] + pv_fn(p.astype(BF16))
    m_ref[...] = m_new


def _sample_attn_kernel(*refs, fox, t_new, tk, n_tiles, lam_init):
    if fox:
        (q_ref, qa_ref, kt_ref, kat_ref, vt_ref, kn_ref, kan_ref, vn_ref, bn_ref,
         o_ref, qbd_ref, m_ref, l_ref, acc_ref) = refs
    else:
        (q_ref, kt_ref, v_ref, kn_ref, vn_ref, bl_ref, bn_ref, lqk_ref, sg_ref,
         o_ref, qbd_ref, m_ref, l_ref, acc_ref) = refs
    t = pl.program_id(1)
    n_blocks = D_MODEL // 64
    rows_per_head = 2 * t_new

    @pl.when(t == 0)
    def _():
        q = q_ref[...]
        zero = jnp.zeros_like(q)
        cb = lax.broadcasted_iota(jnp.int32, q.shape, 1) // 64
        if fox:
            qa = qa_ref[0]
            la = lax.broadcasted_iota(jnp.int32, qa.shape, 1)
        for rb in range(n_blocks):
            qbd_ref[rb * t_new:(rb + 1) * t_new, 0:D_MODEL] = jnp.where(cb == rb, q, zero)
            if fox:
                sel = jnp.logical_and(la >= 6 * rb, la < 6 * rb + 6)
                qbd_ref[rb * t_new:(rb + 1) * t_new, D_MODEL:D_MODEL + LANES] = jnp.where(sel, qa, jnp.zeros_like(qa))
        m_ref[...] = jnp.full(m_ref.shape, NEG, F32)
        l_ref[...] = jnp.zeros(l_ref.shape, F32)
        acc_ref[...] = jnp.zeros(acc_ref.shape, F32)

    n_groups = D_MODEL // SAMPLE_GROUP
    group_rows = (SAMPLE_GROUP // 64) * t_new
    groups = [(slice(g * group_rows, (g + 1) * group_rows), slice(g * SAMPLE_GROUP, (g + 1) * SAMPLE_GROUP))
              for g in range(n_groups)]

    def cache_scores():
        s = jnp.concatenate([jnp.dot(qbd_ref[rows, dims], kt_ref[0, 0, dims, :].astype(BF16),
                                     preferred_element_type=F32) for rows, dims in groups], axis=0)
        if fox:
            s = s + jnp.dot(qbd_ref[:, D_MODEL:D_MODEL + LANES], kat_ref[0], preferred_element_type=F32)
        return s

    if fox:
        def cache_pv(p):
            return jnp.concatenate(
                [lax.dot_general(p[rows, :], vt_ref[0, 0, dims, :].astype(BF16), (((1,), (1,)), ((), ())),
                                 preferred_element_type=F32) for rows, dims in groups], axis=0)

        def new_pv(p):
            return jnp.concatenate([jnp.dot(p[rows, :], vn_ref[:, dims], preferred_element_type=F32)
                                    for rows, dims in groups], axis=0)
    else:
        def cache_pv(p):
            return jnp.concatenate(
                [jnp.dot(p[h * rows_per_head:(h + 1) * rows_per_head, :],
                         v_ref[0, 0, pl.ds(h, tk, stride=DIFF_HEADS), :].astype(BF16), preferred_element_type=F32)
                 for h in range(DIFF_HEADS)], axis=0)

        def new_pv(p):
            return jnp.concatenate(
                [jnp.dot(p[h * rows_per_head:(h + 1) * rows_per_head, :], vn_ref[:, h * LANES:(h + 1) * LANES],
                         preferred_element_type=F32) for h in range(DIFF_HEADS)], axis=0)

    @pl.when(t < n_tiles - 1)
    def _():
        _online_update(cache_scores(), cache_pv, m_ref, l_ref, acc_ref)

    @pl.when(t == n_tiles - 1)
    def _():
        s = cache_scores()
        if not fox:
            s = s + bl_ref[...]
        _online_update(s, cache_pv, m_ref, l_ref, acc_ref)
        kn = kn_ref[...]
        if fox:
            kn = jnp.concatenate([kn, kan_ref[0]], axis=1)
        s = lax.dot_general(qbd_ref[...], kn, (((1,), (1,)), ((), ())), preferred_element_type=F32) + bn_ref[...]
        _online_update(s, new_pv, m_ref, l_ref, acc_ref)

        lane = lax.broadcasted_iota(jnp.int32, (t_new, LANES), 1)
        lo = lane < LANES // 2
        for g in range(D_MODEL // LANES):
            ra, rb = 2 * g * t_new, (2 * g + 1) * t_new
            cols = slice(g * LANES, (g + 1) * LANES)
            c0 = (g * LANES) % SAMPLE_GROUP if fox else 0
            acc_cols = slice(c0, c0 + LANES)
            top = acc_ref[ra:ra + t_new, acc_cols] / l_ref[ra:ra + t_new, :]
            bot = acc_ref[rb:rb + t_new, acc_cols] / l_ref[rb:rb + t_new, :]
            if fox:
                o = jnp.where(lo, top, bot)
            else:
                o = _rms(top - _lambda(lqk_ref, lam_init) * bot, sg_ref[...]) * (1.0 - lam_init)
            o_ref[:, cols] = o.astype(BF16)


def _sample_attn(fox, layer, n_prompt, dec_batch, t_new, past, q, kt, v, kn, vn, bias_last, bias_new, extra,
                 lam_init=0.0):
    tk = SAMPLE_TK
    n_tiles = past // tk
    d = D_MODEL
    rows = (d // 64) * t_new
    s0 = n_prompt // t_new
    new = pl.BlockSpec((t_new, d), lambda b, t: (s0 + b, 0))
    dims_keys = pl.BlockSpec((1, 1, d, tk), lambda b, t: (layer, b, 0, t))
    out = pl.BlockSpec((t_new, d), lambda b, t: (b, 0))
    if fox:
        qa, ka, kat = extra
        new_aug = pl.BlockSpec((1, t_new, LANES), lambda b, t: (b, past // t_new, 0))
        ins = [q, qa, kt, kat, v, kn, ka, vn, bias_new]
        in_specs = [new, new_aug, dims_keys, pl.BlockSpec((1, LANES, tk), lambda b, t: (b, 0, t)), dims_keys,
                    new, new_aug, new, _full_spec(bias_new.shape)]
        kd, acc_w = d + LANES, SAMPLE_GROUP
    else:
        lqk, sg = extra
        ins = [q, kt, v, kn, vn, bias_last, bias_new, lqk, sg]
        in_specs = [new, dims_keys, pl.BlockSpec((1, 1, tk * DIFF_HEADS, LANES), lambda b, t: (layer, b, t, 0)),
                    new, new, _full_spec(bias_last.shape), _full_spec(bias_new.shape),
                    _full_spec(lqk.shape), _full_spec(sg.shape)]
        kd, acc_w = d, LANES
    return pl.pallas_call(
        functools.partial(_sample_attn_kernel, fox=fox, t_new=t_new, tk=tk, n_tiles=n_tiles, lam_init=lam_init),
        grid=(dec_batch, n_tiles),
        in_specs=in_specs,
        out_specs=out,
        out_shape=jax.ShapeDtypeStruct((dec_batch * t_new, d), BF16),
        scratch_shapes=[pltpu.VMEM((rows, kd), BF16), pltpu.VMEM((rows, 1), F32),
                        pltpu.VMEM((rows, 1), F32), pltpu.VMEM((rows, acc_w), F32)],
        compiler_params=_cparams(2),
        name="fox_sample_attn" if fox else "diff_sample_attn",
    )(*ins)


def _split3(c):
    hi = c.astype(BF16)
    r = c - hi.astype(F32)
    mid = r.astype(BF16)
    lo = (r - mid.astype(F32)).astype(BF16)
    return hi, mid, lo


def _decay_cols_kernel(lf_ref, part_ref, qa_ref, ka_ref, *maybe_kat_ref, length):
    part = part_ref[0:1, :]
    which = part_ref[1:2, :]
    carry = jnp.zeros((1, LANES), F32)
    for r in range(0, length, CUM_BLOCK):
        n = min(CUM_BLOCK, length - r)
        tri = (lax.broadcasted_iota(jnp.int32, (n, n), 1) <= lax.broadcasted_iota(jnp.int32, (n, n), 0))
        tri = jnp.where(tri, 1.0, 0.0).astype(BF16)
        c = carry
        for p in _split3(lf_ref[0, r:r + n, :]):
            c = c + jnp.dot(tri, p, preferred_element_type=F32)
        carry = c[n - 1:n, :]
        hi, mid, lo = (t.astype(F32) for t in _split3(c * LOG2E))
        terms = jnp.where(which == 0, hi, jnp.where(which == 1, mid, lo))
        qa = jnp.where(part < 0, 0.0, jnp.where(part < 3, terms, 1.0))
        ka = jnp.where(part < 0, 0.0, jnp.where(part < 3, 1.0, -terms))
        qa_ref[0, r:r + n, :] = qa.astype(BF16)
        ka_ref[0, r:r + n, :] = ka.astype(BF16)
        if maybe_kat_ref:
            maybe_kat_ref[0][0, :, r:r + n] = ka.T.astype(BF16)


def _decay_layout():
    part = np.full((2, LANES), -1, np.int32)
    used = np.arange(FOX_HEADS * DECAY_COLS) % DECAY_COLS
    part[0, :used.size] = used
    part[1, :used.size] = used % 3
    return jnp.asarray(part)


def _spread_heads(a):
    rep = jnp.repeat(a, DECAY_COLS, axis=-1)
    return jnp.pad(rep, [(0, 0)] * (a.ndim - 1) + [(0, LANES - rep.shape[-1])])


def _decay_cols(lf, keys_transposed):
    nb, length, _ = lf.shape
    part = _decay_layout()
    spec = pl.BlockSpec((1, length, LANES), lambda b: (b, 0, 0))
    out_specs = [spec, spec]
    out_shape = [jax.ShapeDtypeStruct((nb, length, LANES), BF16)] * 2
    if keys_transposed:
        out_specs.append(pl.BlockSpec((1, LANES, length), lambda b: (b, 0, 0)))
        out_shape.append(jax.ShapeDtypeStruct((nb, LANES, length), BF16))
    return pl.pallas_call(
        functools.partial(_decay_cols_kernel, length=length),
        grid=(nb,),
        in_specs=[spec, _full_spec(part.shape)],
        out_specs=out_specs,
        out_shape=out_shape,
        compiler_params=_cparams(1),
        name="fox_decay_cols",
    )(lf, part)


def _conv_dw_kernel(u_ref, halo_ref, wdw_ref, bdw_ref, g_ref, b_ref, o_ref, ext_ref, sh_ref, y_ref, *, tm,
                    steps_per_seq):
    halo = halo_ref[...]
    if steps_per_seq:
        halo = jnp.where(pl.program_id(0) % steps_per_seq == 0, 0.0, halo)
    ext_ref[0:HALO, :] = halo
    ext_ref[HALO:HALO + tm, :] = u_ref[...]
    span = sh_ref.shape[1]
    for s in range(1, SUBLANES):
        sh_ref[s - 1] = ext_ref[s:s + span, :]
    d = u_ref.shape[1]
    first = HALO - (CONV_WIDTH - 1)
    for rc in range(tm // CONV_RC):
        for lc in range(d // CONV_LC):
            cols = slice(lc * CONV_LC, (lc + 1) * CONV_LC)
            acc = jnp.zeros((CONV_RC, CONV_LC), F32)
            for w in range(CONV_WIDTH):
                a, s = divmod(first + w, SUBLANES)
                src = ext_ref if s == 0 else sh_ref.at[s - 1]
                r = rc * CONV_RC + a * SUBLANES
                acc = acc + src[r:r + CONV_RC, cols] * wdw_ref[w:w + 1, cols]
            y_ref[rc * CONV_RC:(rc + 1) * CONV_RC, cols] = acc
    y = y_ref[...] + bdw_ref[...]
    yc = y - jnp.mean(y, axis=-1, keepdims=True)
    yn = yc * lax.rsqrt(jnp.mean(yc * yc, axis=-1, keepdims=True) + EPS) * g_ref[...] + b_ref[...]
    o_ref[...] = (yn * (1.0 / (1.0 + jnp.exp(-yn)))).astype(BF16)


def _conv_dw(u, row0, n_rows, tm, halo, halo_index, steps_per_seq, wdw, bdw, g, b):
    d = u.shape[1]
    b0 = row0 // tm
    return pl.pallas_call(
        functools.partial(_conv_dw_kernel, tm=tm, steps_per_seq=steps_per_seq),
        grid=(n_rows // tm,),
        in_specs=[pl.BlockSpec((tm, d), lambda i: (b0 + i, 0)), pl.BlockSpec((HALO, d), lambda i: (halo_index(i), 0)),
                  _full_spec(wdw.shape), _full_spec(bdw.shape), _full_spec(g.shape), _full_spec(b.shape)],
        out_specs=pl.BlockSpec((tm, d), lambda i: (i, 0)),
        out_shape=jax.ShapeDtypeStruct((n_rows, d), BF16),
        scratch_shapes=[pltpu.VMEM((HALO + tm, d), F32), pltpu.VMEM((SUBLANES - 1, HALO + tm - SUBLANES, d), F32),
                        pltpu.VMEM((tm, d), F32)],
        compiler_params=_cparams(1),
        name="conv_dw",
    )(u, halo, wdw, bdw, g, b)


def _dims_keys(cache):
    nd = cache.ndim
    t = jnp.transpose(cache, (0, 1) + tuple(range(3, nd)) + (2,))
    return t.reshape(cache.shape[0], cache.shape[1], D_MODEL, cache.shape[2])


def kernel(x_prompt, x_sample, cache_diff_k, cache_diff_v, state_conv, cache_fox_k, cache_fox_v, cache_fox_logf, rel_bias, norm_g, final_g, diff_w_in, diff_w_out, diff_lq1, diff_lk1, diff_lq2, diff_lk2, diff_subln_g, conv_w_pw1, conv_b_pw1, conv_w_dw, conv_b_dw, conv_ln_g, conv_ln_b, conv_w_pw2, conv_b_pw2, fox_w_in, fox_b_f, fox_w_out, mlp_w1, mlp_w2):
    B, S, D = x_prompt.shape
    Bd, T, _ = x_sample.shape
    P = cache_diff_k.shape[2]
    depth = norm_g.shape[0]
    n_p, n_s = B * S, Bd * T
    assert D == D_MODEL and S % (2 * ATT_TQ) == 0 and P % SAMPLE_TK == 0 and n_p % ROW_TILE == 0
    assert n_s % ROW_TILE == 0 and T % 8 == 0 and T >= CONV_WIDTH - 1 and S % ROW_TILE == 0

    x = jnp.concatenate([x_prompt.reshape(n_p, D), x_sample.reshape(n_s, D)], axis=0)
    zero_bias = jnp.zeros((1, D), F32)
    final_gain = final_g.reshape(1, D)

    bt_diff, bt_fox = _prompt_tiles(rel_bias, ATT_TQ)
    sb_last, sb_new, sf_new = _sample_tiles(rel_bias, T, P, SAMPLE_TK)

    diff_kt = _dims_keys(cache_diff_k)
    diff_v = cache_diff_v.reshape(cache_diff_v.shape[0], Bd, P * DIFF_HEADS, 2 * DIFF_HEAD_DIM)
    fox_kt = _dims_keys(cache_fox_k)
    fox_vt = _dims_keys(cache_fox_v)

    n_diff, n_fox = diff_w_in.shape[0], fox_w_in.shape[0]
    dk_shape = (1, n_p, D)
    dv_shape = (n_diff, n_p, DIFF_HEADS, 2 * DIFF_HEAD_DIM)
    fkv_shape = (n_fox, n_p, FOX_HEADS, FOX_HEAD_DIM)
    dv_new, fk_new, fv_new = None, None, None
    dk_p, dk_s, dv_s = [], [], []
    cv_p, cv_s = [], []
    fl_p, fk_s, fv_s, fl_s = [], [], [], []
    y = None

    for i in range(depth):
        kind, j = i % N_MIXERS, i // N_MIXERS
        g1 = norm_g[i, 0].reshape(1, D)
        g2 = norm_g[i, 1].reshape(1, D)
        if kind == 0:
            lam_init = 0.8 - 0.6 * math.exp(-0.3 * i)
            q, k_p, k_s, dv_new, v_s, kb, vb = _qkv_inproj(
                x, n_p, g1, diff_w_in[j].astype(BF16), DIFF_SCALE, (dk_shape, 0), (dv_shape, j),
                prev=None if dv_new is None else {3: dv_new})
            dk_p.append(k_p.reshape(B, S, DIFF_HEADS, 2, DIFF_HEAD_DIM))
            lqk = jnp.stack([diff_lq1[j], diff_lk1[j], diff_lq2[j], diff_lk2[j]]).astype(F32)
            sg = diff_subln_g[j].reshape(1, 2 * DIFF_HEAD_DIM).astype(F32)
            o_p = _prompt_attn(False, B, S, q, kb, vb, bt_diff, (lqk, sg.reshape(2 * DIFF_HEAD_DIM, 1)), lam_init)
            o_s = _sample_attn(False, j, n_p, Bd, T, P, q, diff_kt, diff_v, kb, vb, sb_last, sb_new, (lqk, sg),
                               lam_init)
            wo, bo = diff_w_out[j].astype(BF16), zero_bias
            dk_s.append(k_s.reshape(Bd, T, DIFF_HEADS, 2, DIFF_HEAD_DIM))
            dv_s.append(v_s.reshape(Bd, T, DIFF_HEADS, 2 * DIFF_HEAD_DIM))
        elif kind == 1:
            u = _conv_inproj(x, g1, conv_w_pw1[j].astype(BF16), conv_b_pw1[j].reshape(1, 2 * D))
            wdw = jnp.pad(conv_w_dw[j], ((0, HALO - CONV_WIDTH), (0, 0)))
            args = (wdw, conv_b_dw[j].reshape(1, D), conv_ln_g[j].reshape(1, D), conv_ln_b[j].reshape(1, D))
            per_tile = ROW_TILE // HALO
            o_p = _conv_dw(u, 0, n_p, ROW_TILE, u, lambda t: jnp.maximum(t * per_tile - 1, 0), S // ROW_TILE, *args)
            state = jnp.pad(state_conv[j], ((0, 0), (HALO - (CONV_WIDTH - 1), 0), (0, 0))).reshape(Bd * HALO, D)
            o_s = _conv_dw(u, n_p, n_s, T, state, lambda t: t, 0, *args)
            wo, bo = conv_w_pw2[j].astype(BF16), conv_b_pw2[j].reshape(1, D)
            keep = CONV_WIDTH - 1
            cv_p.append(jnp.stack([u[(b + 1) * S - keep:(b + 1) * S] for b in range(B)]))
            cv_s.append(u[n_p:].reshape(Bd, T, D)[:, T - keep:])
        else:
            w_in = fox_w_in[j]
            wf = _spread_heads(w_in[:, 3 * D:]).astype(BF16)
            bf = _spread_heads(fox_b_f[j].astype(F32)).reshape(1, LANES)
            w_qkv = w_in[:, :3 * D].astype(BF16)
            w_rot = [jnp.roll(w_qkv[:, c * D:(c + 1) * D], -FOX_HEAD_DIM, axis=1) for c in (1, 2)]
            q, fk_new, k_s, fv_new, v_s, kb, vb, lf = _qkv_inproj(
                x, n_p, g1, jnp.concatenate([w_qkv] + w_rot, axis=1), FOX_SCALE, (fkv_shape, j), (fkv_shape, j),
                prev=None if fk_new is None else {1: fk_new, 3: fv_new}, forget=(wf, bf))
            lf_p_cols = lf[:n_p].reshape(B, S, LANES)
            lf_s_cols = lf[n_p:].reshape(Bd, T, LANES)
            lf_p = lf_p_cols[:, :, 0:FOX_HEADS * DECAY_COLS:DECAY_COLS]
            lf_s = lf_s_cols[:, :, 0:FOX_HEADS * DECAY_COLS:DECAY_COLS]
            qa_p, ka_p = _decay_cols(lf_p_cols, False)
            cache_cols = _spread_heads(cache_fox_logf[j].astype(F32))
            decay_s = _decay_cols(jnp.concatenate([cache_cols, lf_s_cols], axis=1), True)
            o_p = _prompt_attn(True, B, S, q, kb, vb, bt_fox, (qa_p.reshape(n_p, LANES), ka_p.reshape(n_p, LANES)))
            o_s = _sample_attn(True, j, n_p, Bd, T, P, q, fox_kt, fox_vt, kb, vb, None, sf_new, decay_s)
            wo, bo = fox_w_out[j].astype(BF16), zero_bias
            fl_p.append(lf_p)
            fk_s.append(k_s.reshape(Bd, T, FOX_HEADS, FOX_HEAD_DIM))
            fv_s.append(v_s.reshape(Bd, T, FOX_HEADS, FOX_HEAD_DIM))
            fl_s.append(lf_s)
        final = i == depth - 1
        out = _mix_mlp(x, o_p, o_s, wo, bo, g2, mlp_w1[i].astype(BF16), mlp_w2[i].astype(BF16), final_gain, final)
        if final:
            y = out
        else:
            x = out[0]

    y_prompt = y[0].reshape(B, S, D)
    y_sample = y[1].reshape(Bd, T, D)
    new_diff_k_p = jnp.stack(dk_p)
    new_diff_v_p = dv_new.reshape(n_diff, B, S, DIFF_HEADS, 2 * DIFF_HEAD_DIM)
    new_fox_k_p = fk_new.reshape(n_fox, B, S, FOX_HEADS, FOX_HEAD_DIM)
    new_fox_v_p = fv_new.reshape(n_fox, B, S, FOX_HEADS, FOX_HEAD_DIM)
    return (y_prompt, y_sample, new_diff_k_p, new_diff_v_p, jnp.stack(cv_p), new_fox_k_p,
            new_fox_v_p, jnp.stack(fl_p), jnp.stack(dk_s), jnp.stack(dv_s), jnp.stack(cv_s),
            jnp.stack(fk_s), jnp.stack(fv_s), jnp.stack(fl_s))
```

```python
import functools
import math

import numpy as np
import jax
import jax.numpy as jnp
from jax import lax
from jax.experimental import pallas as pl
from jax.experimental.pallas import tpu as pltpu

F32 = jnp.float32
BF16 = jnp.bfloat16

D_MODEL = 1024
CHUNK = 64
EPS = 1e-6
DIFF_HEADS = 8
DIFF_HEAD_DIM = 64
LOG2E = math.log2(math.e)
DIFF_SCALE = DIFF_HEAD_DIM ** -0.5
CONV_WIDTH = 31
FOX_HEADS = 16
FOX_HEAD_DIM = 64
FOX_SCALE = FOX_HEAD_DIM ** -0.5
REL_BUCKETS = 32
N_MIXERS = 3

LANES = 128
SUBLANES = 8
NEG = -1e30
ROW_TILE = 512
ATT_TQ = 512
ATT_STRIP = 512
ATT_GROUPS = 2
REDUCE_ROWS = 64
SAMPLE_TK = 2048
SAMPLE_GROUP = 256
FF_CHUNK = 1024
CUM_BLOCK = 256
DECAY_COLS = 6
HALO = 32
CONV_RC = 32
CONV_LC = 256
VMEM_LIMIT = 56 * 1024 * 1024

_T5_STEPS = (12, 16, 23, 32, 46, 64, 91)
T5_NEAR = 128
assert _T5_STEPS[-1] <= T5_NEAR


def _cparams(n_axes):
    return pltpu.CompilerParams(dimension_semantics=("arbitrary",) * n_axes, vmem_limit_bytes=VMEM_LIMIT)


def _rms(x, g):
    return x * lax.rsqrt(jnp.mean(x * x, axis=-1, keepdims=True) + EPS) * g


def _full_spec(shape):
    nd = len(shape)
    return pl.BlockSpec(shape, lambda *_: (0,) * nd)


def _split_specs(n_prompt_tiles, tm, d):
    prompt = pl.BlockSpec((tm, d), lambda i: (jnp.minimum(i, n_prompt_tiles - 1), 0))
    sample = pl.BlockSpec((tm, d), lambda i: (jnp.maximum(i - n_prompt_tiles, 0), 0))
    return prompt, sample


def _split_store(n_prompt_tiles, value, prompt_ref, sample_ref):
    i = pl.program_id(0)

    @pl.when(i < n_prompt_tiles)
    def _():
        prompt_ref[...] = value

    @pl.when(i >= n_prompt_tiles)
    def _():
        sample_ref[...] = value


def _store_prompt_cache(value, prompt_ref, layer, rotated_fn):
    layer = layer if prompt_ref.shape[0] > 1 else 0
    for other in range(prompt_ref.shape[0]):
        if other != layer:
            prompt_ref[other] = jnp.zeros(prompt_ref.shape[1:], prompt_ref.dtype)
    if len(prompt_ref.shape) == 3:
        prompt_ref[layer] = value
        return
    heads, width = prompt_ref.shape[2:]
    if width == LANES:
        for h in range(heads):
            prompt_ref[layer, :, h, :] = value[:, h * width:(h + 1) * width]
    else:
        rotated = rotated_fn()
        for h in range(heads):
            src = value if h % 2 == 0 else rotated
            prompt_ref[layer, :, h, :] = src[:, (h // 2) * LANES:(h // 2) * LANES + width]


def _qkv_inproj_kernel(*refs, n_pt, scale, n_prev, forget, layers):
    x_ref, g_ref, w_ref = refs[:3]
    refs = refs[3:]
    if forget:
        wf_ref, bf_ref = refs[:2]
        refs = refs[2:]
    refs = refs[n_prev:]
    q_ref, kp_ref, ks_ref, vp_ref, vs_ref, kb_ref, vb_ref = refs[:7]
    d = D_MODEL

    def project(prompt):
        xn = _rms(x_ref[...], g_ref[...]).astype(BF16)
        q = jnp.dot(xn, w_ref[:, 0:d], preferred_element_type=F32)
        q_ref[...] = (q * (scale * LOG2E)).astype(BF16)
        for c, new_p, new_s, bf_ref_out in ((1, kp_ref, ks_ref, kb_ref), (2, vp_ref, vs_ref, vb_ref)):
            val = jnp.dot(xn, w_ref[:, c * d:(c + 1) * d], preferred_element_type=F32)
            if prompt:
                _store_prompt_cache(val, new_p, layers[c - 1],
                                    lambda c=c: jnp.dot(xn, w_ref[:, (c + 2) * d:(c + 3) * d],
                                                        preferred_element_type=F32))
            else:
                new_s[...] = val
            bf_ref_out[...] = val.astype(BF16)
        if forget:
            z = jnp.dot(xn, wf_ref[...], preferred_element_type=F32) + bf_ref[...]
            lf = jnp.minimum(z, 0.0) - jnp.log1p(jnp.exp(-jnp.abs(z)))
            refs[7][...] = lf[:, 0:LANES]
            refs[8][...] = lf[:, LANES:LANES + FOX_HEADS]

    pl.when(pl.program_id(0) < n_pt)(lambda: project(True))
    pl.when(pl.program_id(0) >= n_pt)(lambda: project(False))


def _qkv_inproj(x, n_p, g, w, scale, k_slab, v_slab, prev=None, forget=()):
    n, d = x.shape
    tm = ROW_TILE
    n_pt = n_p // tm
    row = pl.BlockSpec((tm, d), lambda i: (i, 0))
    _, sample = _split_specs(n_pt, tm, d)
    prev = prev or {}

    def slab(out_index, shape, layer):
        rest = tuple(shape[2:])
        zeros = (0,) * len(rest)
        if out_index in prev:
            return pl.BlockSpec((1, tm) + rest, lambda i: (layer, jnp.minimum(i, n_pt - 1)) + zeros)
        return pl.BlockSpec((shape[0], tm) + rest, lambda i: (0, jnp.minimum(i, n_pt - 1)) + zeros)

    f32_s = jax.ShapeDtypeStruct((n - n_p, d), F32)
    bf_all = jax.ShapeDtypeStruct((n, d), BF16)
    ins = [x, g, w] + list(forget) + list(prev.values())
    w_spec = pl.BlockSpec(w.shape, lambda i: (0, 0), pipeline_mode=pl.Buffered(1))
    in_specs = ([row, _full_spec((1, d)), w_spec] + [_full_spec(a.shape) for a in forget]
                + [pl.BlockSpec(memory_space=pl.ANY)] * len(prev))
    out_specs = [row, slab(1, *k_slab), sample, slab(3, *v_slab), sample, row, row]
    out_shape = [bf_all, jax.ShapeDtypeStruct(k_slab[0], F32), f32_s, jax.ShapeDtypeStruct(v_slab[0], F32), f32_s,
                 bf_all, bf_all]
    if forget:
        out_specs += [pl.BlockSpec((tm, LANES), lambda i: (i, 0)), pl.BlockSpec((tm, FOX_HEADS), lambda i: (i, 0))]
        out_shape += [jax.ShapeDtypeStruct((n, LANES), F32), jax.ShapeDtypeStruct((n, FOX_HEADS), F32)]
    first_prev = 3 + len(forget)
    aliases = {first_prev + pos: out_index for pos, out_index in enumerate(prev)}
    return pl.pallas_call(
        functools.partial(_qkv_inproj_kernel, n_pt=n_pt, scale=scale, n_prev=len(prev), forget=bool(forget),
                          layers=(k_slab[1], v_slab[1])),
        grid=(n // tm,),
        in_specs=in_specs,
        out_specs=out_specs,
        out_shape=out_shape,
        input_output_aliases=aliases,
        compiler_params=_cparams(1),
        name="fox_inproj" if forget else "diff_inproj",
    )(*ins)


def _conv_inproj_kernel(x_ref, g_ref, w_ref, b_ref, u_ref):
    d = D_MODEL
    xn = _rms(x_ref[...], g_ref[...]).astype(BF16)
    a = jnp.dot(xn, w_ref[:, 0:d], preferred_element_type=F32) + b_ref[:, 0:d]
    gate = jnp.dot(xn, w_ref[:, d:2 * d], preferred_element_type=F32) + b_ref[:, d:2 * d]
    u_ref[...] = a * (1.0 / (1.0 + jnp.exp(-gate)))


def _conv_inproj(x, g, w, b):
    n, d = x.shape
    tm = ROW_TILE
    row = pl.BlockSpec((tm, d), lambda i: (i, 0))
    return pl.pallas_call(
        _conv_inproj_kernel,
        grid=(n // tm,),
        in_specs=[row, _full_spec((1, d)), _full_spec(w.shape), _full_spec(b.shape)],
        out_specs=row,
        out_shape=jax.ShapeDtypeStruct((n, d), F32),
        compiler_params=_cparams(1),
        name="conv_inproj",
    )(x, g, w, b)


def _mix_mlp_kernel(x_ref, op_ref, os_ref, wo_ref, bo_ref, g_ref, w1_ref, w2_ref, gf_ref, *out_refs, n_pt, final):
    o = jnp.where(pl.program_id(0) < n_pt, op_ref[...], os_ref[...])
    x1 = x_ref[...] + jnp.dot(o, wo_ref[...], preferred_element_type=F32) + bo_ref[...]
    xn = _rms(x1, g_ref[...]).astype(BF16)
    acc = x1
    d_ff = w1_ref.shape[1]
    for c in range(d_ff // FF_CHUNK):
        h = jnp.dot(xn, w1_ref[:, c * FF_CHUNK:(c + 1) * FF_CHUNK], preferred_element_type=F32)
        h = jnp.square(jnp.maximum(h, 0.0)).astype(BF16)
        acc = acc + jnp.dot(h, w2_ref[c * FF_CHUNK:(c + 1) * FF_CHUNK, :], preferred_element_type=F32)
    if final:
        _split_store(n_pt, _rms(acc, gf_ref[...]), *out_refs)
    else:
        out_refs[0][...] = acc


def _mix_mlp(x, o_p, o_s, wo, bo, g, w1, w2, gf, final):
    n, d = x.shape
    tm = ROW_TILE
    n_p = o_p.shape[0]
    n_pt = n_p // tm
    row = pl.BlockSpec((tm, d), lambda i: (i, 0))
    prompt, sample = _split_specs(n_pt, tm, d)
    resident = lambda a: pl.BlockSpec(a.shape, lambda i: (0,) * a.ndim, pipeline_mode=pl.Buffered(1))
    if final:
        out_specs = [prompt, sample]
        out_shape = [jax.ShapeDtypeStruct((n_p, d), F32), jax.ShapeDtypeStruct((n - n_p, d), F32)]
    else:
        out_specs = [row]
        out_shape = [jax.ShapeDtypeStruct((n, d), F32)]
    return pl.pallas_call(
        functools.partial(_mix_mlp_kernel, n_pt=n_pt, final=final),
        grid=(n // tm,),
        in_specs=[row, prompt, sample, resident(wo), _full_spec(bo.shape), _full_spec(g.shape), resident(w1),
                  resident(w2), _full_spec(gf.shape)],
        out_specs=out_specs,
        out_shape=out_shape,
        compiler_params=_cparams(1),
        name="mix_mlp",
    )(x, o_p, o_s, wo, bo, g, w1, w2, gf)


def _t5_bias(tab_ref, h, qpos, kpos):
    half = REL_BUCKETS // 2
    n = qpos - kpos
    off = jnp.where(n < 0, half, 0)
    n = jnp.abs(n)
    large = jnp.full(n.shape, half // 2, jnp.int32)
    for t in _T5_STEPS:
        large = large + jnp.where(n >= t, 1, 0)
    bucket = off + jnp.where(n < half // 2, n, large)
    far = tab_ref[half - 1, h]
    out = jnp.zeros(n.shape, F32)
    for b in range(REL_BUCKETS):
        out = jnp.where(bucket == b, (tab_ref[b, h] - far) * LOG2E, out)
    visible = (kpos // CHUNK) <= (qpos // CHUNK)
    return jnp.where(visible, out, NEG)


def _prompt_tiles_kernel(tab_ref, bt_ref, cm_ref, *, tq):
    h = pl.program_id(0)
    for v in range(2):
        kpos = lax.broadcasted_iota(jnp.int32, (tq, tq), 0)
        qpos = lax.broadcasted_iota(jnp.int32, (tq, tq), 1) + (1 - v) * tq
        bt_ref[0, v] = _t5_bias(tab_ref, h, qpos, kpos)

    @pl.when(h == 0)
    def _():
        kpos = lax.broadcasted_iota(jnp.int32, (tq, tq), 0)
        qpos = lax.broadcasted_iota(jnp.int32, (tq, tq), 1)
        cm_ref[0, 0] = jnp.zeros((tq, tq), F32)
        cm_ref[0, 1] = jnp.where(kpos <= qpos, 0.0, NEG)


def _prompt_tiles(table, tq):
    return pl.pallas_call(
        functools.partial(_prompt_tiles_kernel, tq=tq),
        grid=(DIFF_HEADS,),
        in_specs=[pl.BlockSpec(memory_space=pltpu.SMEM)],
        out_specs=[pl.BlockSpec((1, 2, tq, tq), lambda h: (h, 0, 0, 0)),
                   pl.BlockSpec((1, 2, tq, tq), lambda h: (0, 0, 0, 0))],
        out_shape=[jax.ShapeDtypeStruct((DIFF_HEADS, 2, tq, tq), F32), jax.ShapeDtypeStruct((1, 2, tq, tq), F32)],
        compiler_params=_cparams(1),
        name="prompt_score_tiles",
    )(table)


def _sample_tiles_kernel(tab_ref, dl_ref, dn_ref, fn_ref, *, t_new, past, tk):
    rows = DIFF_HEADS * 2 * t_new
    for h in range(DIFF_HEADS):
        r0 = h * 2 * t_new
        q = lax.broadcasted_iota(jnp.int32, (2 * t_new, tk), 0) % t_new + past
        k = lax.broadcasted_iota(jnp.int32, (2 * t_new, tk), 1) + (past - tk)
        dl_ref[r0:r0 + 2 * t_new, :] = _t5_bias(tab_ref, h, q, k)
        q = lax.broadcasted_iota(jnp.int32, (2 * t_new, t_new), 0) % t_new + past
        k = lax.broadcasted_iota(jnp.int32, (2 * t_new, t_new), 1) + past
        dn_ref[r0:r0 + 2 * t_new, :] = _t5_bias(tab_ref, h, q, k)
    q = lax.broadcasted_iota(jnp.int32, (rows, t_new), 0) % t_new
    k = lax.broadcasted_iota(jnp.int32, (rows, t_new), 1)
    fn_ref[...] = jnp.where(k <= q, 0.0, NEG)


def _sample_tiles(table, t_new, past, tk):
    rows = DIFF_HEADS * 2 * t_new
    assert rows == FOX_HEADS * t_new
    return pl.pallas_call(
        functools.partial(_sample_tiles_kernel, t_new=t_new, past=past, tk=tk),
        in_specs=[pl.BlockSpec(memory_space=pltpu.SMEM)],
        out_shape=[jax.ShapeDtypeStruct((rows, tk), F32), jax.ShapeDtypeStruct((rows, t_new), F32),
                   jax.ShapeDtypeStruct((rows, t_new), F32)],
        name="sample_score_tiles",
    )(table)


def _lambda(lqk_ref, lam_init):
    a = jnp.sum(lqk_ref[0:1, :] * lqk_ref[1:2, :], axis=-1, keepdims=True)
    b = jnp.sum(lqk_ref[2:3, :] * lqk_ref[3:4, :], axis=-1, keepdims=True)
    return jnp.exp(a) - jnp.exp(b) + lam_init


def _scores_t(k, q2_ref, s_out):
    for c0 in range(0, q2_ref.shape[0], ATT_STRIP):
        cols = slice(c0, c0 + ATT_STRIP)
        s_out[:, cols] = lax.dot_general(k, q2_ref[cols, :], (((1,), (1,)), ((), ())), preferred_element_type=F32)


def _reduce_rows(op, x):
    w, n = x.shape
    part = op(x.reshape(w // REDUCE_ROWS, REDUCE_ROWS, n), axis=0)
    return op(part, axis=0, keepdims=True)


def _softmax_pv_t(s_in, v, bias_t, bias_rows, m_ref, l_ref, acc_ref):
    w = s_in.shape[0]
    for c0 in range(0, s_in.shape[1], ATT_STRIP):
        cols = slice(c0, c0 + ATT_STRIP)
        s = s_in[:, cols]
        if bias_t is not None:
            b0 = c0 % bias_t.shape[1]
            r0 = w - bias_rows
            biased = s[r0:, :] + bias_t[r0:, b0:b0 + ATT_STRIP]
            s = biased if r0 == 0 else jnp.concatenate([s[:r0, :], biased], axis=0)
        m_prev = m_ref[:, cols]
        m_new = jnp.maximum(m_prev, _reduce_rows(jnp.max, s))
        alpha = jnp.exp2(m_prev - m_new)
        p = jnp.exp2(s - m_new)
        l_ref[:, cols] = alpha * l_ref[:, cols] + _reduce_rows(jnp.sum, p)
        pv = lax.dot_general(v, p.astype(BF16), (((0,), (0,)), ((), ())), preferred_element_type=F32)
        acc_ref[:, cols] = alpha * acc_ref[:, cols] + pv
        m_ref[:, cols] = m_new


def _prompt_attn_kernel(*refs, fox, tq, seq, lam_init):
    if fox:
        (q_ref, qa_ref, k_ref, ka_ref, v_ref, bt_ref, o_ref, q2_ref, s0_ref, s1_ref, m_ref, l_ref, acc_ref) = refs
    else:
        (q_ref, k_ref, v_ref, bt_ref, lqk_ref, sg_ref, o_ref, q2_ref, s0_ref, s1_ref, m_ref, l_ref, acc_ref) = refs
    s_refs = (s0_ref, s1_ref)
    n_groups = q_ref.shape[1] // LANES
    lane = lax.broadcasted_iota(jnp.int32, (tq, LANES), 1)
    lo = lane < LANES // 2
    first_rows = lax.broadcasted_iota(jnp.int32, (LANES, tq), 0) < LANES // 2
    lanes = [slice(g * LANES, (g + 1) * LANES) for g in range(n_groups)]
    if fox:
        in_a, in_b = [], []
        for g in range(n_groups):
            c0 = 12 * (pl.program_id(1) * n_groups + g)
            in_a.append(jnp.logical_and(lane >= c0, lane < c0 + 6))
            in_b.append(jnp.logical_and(lane >= c0 + 6, lane < c0 + 12))

    def scores_g(g, j, slot):
        k0 = pl.multiple_of(jnp.maximum(j, 0) * tq, tq)
        k = k_ref[pl.ds(k0, tq), lanes[g]]
        if fox:
            k = jnp.concatenate([k, ka_ref[pl.ds(k0, tq), :]], axis=1)
        _scores_t(k, q2_ref.at[g], s_refs[slot].at[g])

    def softmax_pv_g(g, j, slot, bias_index):
        k0 = pl.multiple_of(j * tq, tq)
        if bias_index is None or (fox and bias_index == 0):
            bias_t, bias_rows = None, 0
        else:
            bias_t = bt_ref.at[0 if fox else g, bias_index]
            bias_rows = tq if bias_index == 1 else T5_NEAR
        _softmax_pv_t(s_refs[slot].at[g], v_ref[pl.ds(k0, tq), lanes[g]], bias_t, bias_rows, m_ref.at[g],
                      l_ref.at[g], acc_ref.at[g])

    def scores(j, slot):
        for g in range(n_groups):
            scores_g(g, j, slot)

    def softmax_pv(j, slot, bias_index):
        for g in range(n_groups):
            softmax_pv_g(g, j, slot, bias_index)

    def step(j_next, slot_next, j, slot, bias_index):
        for g in range(n_groups):
            scores_g(g, j_next, slot_next)
            softmax_pv_g(g, j, slot, bias_index)

    def q_block(i, carry):
        r0 = pl.multiple_of(i * tq, tq)
        for g in range(n_groups):
            q = q_ref[pl.ds(r0, tq), lanes[g]]
            zero = jnp.zeros_like(q)
            q2_ref[g, 0:tq, 0:LANES] = jnp.where(lo, q, zero)
            q2_ref[g, tq:2 * tq, 0:LANES] = jnp.where(lo, zero, q)
            if fox:
                qa = qa_ref[pl.ds(r0, tq), :]
                q2_ref[g, 0:tq, LANES:2 * LANES] = jnp.where(in_a[g], qa, zero)
                q2_ref[g, tq:2 * tq, LANES:2 * LANES] = jnp.where(in_b[g], qa, zero)
        m_ref[...] = jnp.full(m_ref.shape, NEG, F32)
        l_ref[...] = jnp.zeros(l_ref.shape, F32)
        acc_ref[...] = jnp.zeros(acc_ref.shape, F32)

        @pl.when(i == 0)
        def _():
            scores(i, 0)
            softmax_pv(i, 0, 1)

        @pl.when(i >= 1)
        def _():
            scores(i, 0)
            step(i - 1, 1, i, 0, 1)
            step(i - 2, 0, i - 1, 1, 0)
            n_far = i - 1

            def pair(u, c):
                a = i - 2 - 2 * u
                step(a - 1, 1, a, 0, None)
                step(a - 2, 0, a - 1, 1, None)
                return c

            lax.fori_loop(0, n_far // 2, pair, 0)

            @pl.when(n_far % 2 == 1)
            def _():
                softmax_pv(0, 0, None)

        for g in range(n_groups):
            top = acc_ref[g, :, 0:tq] / l_ref[g, :, 0:tq]
            bot = acc_ref[g, :, tq:2 * tq] / l_ref[g, :, tq:2 * tq]
            if fox:
                o_t = jnp.where(first_rows, top, bot)
            else:
                o_t = top - _lambda(lqk_ref, lam_init) * bot
                ms = jnp.mean(o_t * o_t, axis=0, keepdims=True)
                o_t = o_t * lax.rsqrt(ms + EPS) * sg_ref[...] * (1.0 - lam_init)
            o_ref[pl.ds(r0, tq), lanes[g]] = o_t.T.astype(BF16)
        return carry

    lax.fori_loop(0, seq // tq, q_block, 0)


def _prompt_attn(fox, batch, seq, q, k, v, bt, extra, lam_init=0.0):
    tq = ATT_TQ
    ng = ATT_GROUPS
    steps = D_MODEL // (LANES * ng)
    blk = pl.BlockSpec((seq, ng * LANES), lambda b, h: (b, h))
    if fox:
        qa, ka = extra
        aug = pl.BlockSpec((seq, LANES), lambda b, h: (b, 0))
        ins = [q, qa, k, ka, v, bt]
        in_specs = [blk, aug, blk, aug, blk, pl.BlockSpec((1, 2, tq, tq), lambda b, h: (0, 0, 0, 0))]
        kd = 2 * LANES
    else:
        lqk, sg = extra
        ins = [q, k, v, bt, lqk, sg]
        in_specs = [blk, blk, blk, pl.BlockSpec((ng, 2, tq, tq), lambda b, h: (h, 0, 0, 0)),
                    _full_spec(lqk.shape), _full_spec(sg.shape)]
        kd = LANES
    return pl.pallas_call(
        functools.partial(_prompt_attn_kernel, fox=fox, tq=tq, seq=seq, lam_init=lam_init),
        grid=(batch, steps),
        in_specs=in_specs,
        out_specs=blk,
        out_shape=jax.ShapeDtypeStruct((batch * seq, D_MODEL), BF16),
        scratch_shapes=[pltpu.VMEM((ng, 2 * tq, kd), BF16), pltpu.VMEM((ng, tq, 2 * tq), F32),
                        pltpu.VMEM((ng, tq, 2 * tq), F32), pltpu.VMEM((ng, 1, 2 * tq), F32),
                        pltpu.VMEM((ng, 1, 2 * tq), F32), pltpu.VMEM((ng, LANES, 2 * tq), F32)],
        compiler_params=_cparams(2),
        name="fox_prompt_attn" if fox else "diff_prompt_attn",
    )(*ins)


def _online_update(s, pv_fn, m_ref, l_ref, acc_ref):
    m_prev = m_ref[...]
    m_new = jnp.maximum(m_prev, jnp.max(s, axis=-1, keepdims=True))
    alpha = jnp.exp2(m_prev - m_new)
    p = jnp.exp2(s - m_new)
    l_ref[...] = alpha * l_ref[...] + jnp.sum(p, axis=-1, keepdims=True)
    acc_ref[...] = alpha * acc_ref[...] + pv_fn(p.astype(BF16))
    m_ref[...] = m_new


def _sample_attn_kernel(*refs, fox, t_new, tk, n_tiles, lam_init):
    if fox:
        (q_ref, qa_ref, kt_ref, kat_ref, vt_ref, kn_ref, kan_ref, vn_ref, bn_ref,
         o_ref, qbd_ref, m_ref, l_ref, acc_ref) = refs
    else:
        (q_ref, kt_ref, v_ref, kn_ref, vn_ref, bl_ref, bn_ref, lqk_ref, sg_ref,
         o_ref, qbd_ref, m_ref, l_ref, acc_ref) = refs
    t = pl.program_id(1)
    n_blocks = D_MODEL // 64
    rows_per_head = 2 * t_new

    @pl.when(t == 0)
    def _():
        q = q_ref[...]
        zero = jnp.zeros_like(q)
        cb = lax.broadcasted_iota(jnp.int32, q.shape, 1) // 64
        if fox:
            qa = qa_ref[0]
            la = lax.broadcasted_iota(jnp.int32, qa.shape, 1)
        for rb in range(n_blocks):
            qbd_ref[rb * t_new:(rb + 1) * t_new, 0:D_MODEL] = jnp.where(cb == rb, q, zero)
            if fox:
                sel = jnp.logical_and(la >= 6 * rb, la < 6 * rb + 6)
                qbd_ref[rb * t_new:(rb + 1) * t_new, D_MODEL:D_MODEL + LANES] = jnp.where(sel, qa, jnp.zeros_like(qa))
        m_ref[...] = jnp.full(m_ref.shape, NEG, F32)
        l_ref[...] = jnp.zeros(l_ref.shape, F32)
        acc_ref[...] = jnp.zeros(acc_ref.shape, F32)

    n_groups = D_MODEL // SAMPLE_GROUP
    group_rows = (SAMPLE_GROUP // 64) * t_new
    groups = [(slice(g * group_rows, (g + 1) * group_rows), slice(g * SAMPLE_GROUP, (g + 1) * SAMPLE_GROUP))
              for g in range(n_groups)]

    def cache_scores():
        s = jnp.concatenate([jnp.dot(qbd_ref[rows, dims], kt_ref[0, 0, dims, :].astype(BF16),
                                     preferred_element_type=F32) for rows, dims in groups], axis=0)
        if fox:
            s = s + jnp.dot(qbd_ref[:, D_MODEL:D_MODEL + LANES], kat_ref[0], preferred_element_type=F32)
        return s

    if fox:
        def cache_pv(p):
            return jnp.concatenate(
                [lax.dot_general(p[rows, :], vt_ref[0, 0, dims, :].astype(BF16), (((1,), (1,)), ((), ())),
                                 preferred_element_type=F32) for rows, dims in groups], axis=0)

        def new_pv(p):
            return jnp.concatenate([jnp.dot(p[rows, :], vn_ref[:, dims], preferred_element_type=F32)
                                    for rows, dims in groups], axis=0)
    else:
        def cache_pv(p):
            return jnp.concatenate(
                [jnp.dot(p[h * rows_per_head:(h + 1) * rows_per_head, :],
                         v_ref[0, 0, pl.ds(h, tk, stride=DIFF_HEADS), :].astype(BF16), preferred_element_type=F32)
                 for h in range(DIFF_HEADS)], axis=0)

        def new_pv(p):
            return jnp.concatenate(
                [jnp.dot(p[h * rows_per_head:(h + 1) * rows_per_head, :], vn_ref[:, h * LANES:(h + 1) * LANES],
                         preferred_element_type=F32) for h in range(DIFF_HEADS)], axis=0)

    @pl.when(t < n_tiles - 1)
    def _():
        _online_update(cache_scores(), cache_pv, m_ref, l_ref, acc_ref)

    @pl.when(t == n_tiles - 1)
    def _():
        s = cache_scores()
        if not fox:
            s = s + bl_ref[...]
        _online_update(s, cache_pv, m_ref, l_ref, acc_ref)
        kn = kn_ref[...]
        if fox:
            kn = jnp.concatenate([kn, kan_ref[0]], axis=1)
        s = lax.dot_general(qbd_ref[...], kn, (((1,), (1,)), ((), ())), preferred_element_type=F32) + bn_ref[...]
        _online_update(s, new_pv, m_ref, l_ref, acc_ref)

        lane = lax.broadcasted_iota(jnp.int32, (t_new, LANES), 1)
        lo = lane < LANES // 2
        for g in range(D_MODEL // LANES):
            ra, rb = 2 * g * t_new, (2 * g + 1) * t_new
            cols = slice(g * LANES, (g + 1) * LANES)
            c0 = (g * LANES) % SAMPLE_GROUP if fox else 0
            acc_cols = slice(c0, c0 + LANES)
            top = acc_ref[ra:ra + t_new, acc_cols] / l_ref[ra:ra + t_new, :]
            bot = acc_ref[rb:rb + t_new, acc_cols] / l_ref[rb:rb + t_new, :]
            if fox:
                o = jnp.where(lo, top, bot)
            else:
                o = _rms(top - _lambda(lqk_ref, lam_init) * bot, sg_ref[...]) * (1.0 - lam_init)
            o_ref[:, cols] = o.astype(BF16)


def _sample_attn(fox, layer, n_prompt, dec_batch, t_new, past, q, kt, v, kn, vn, bias_last, bias_new, extra,
                 lam_init=0.0):
    tk = SAMPLE_TK
    n_tiles = past // tk
    d = D_MODEL
    rows = (d // 64) * t_new
    s0 = n_prompt // t_new
    new = pl.BlockSpec((t_new, d), lambda b, t: (s0 + b, 0))
    dims_keys = pl.BlockSpec((1, 1, d, tk), lambda b, t: (layer, b, 0, t))
    out = pl.BlockSpec((t_new, d), lambda b, t: (b, 0))
    if fox:
        qa, ka, kat = extra
        new_aug = pl.BlockSpec((1, t_new, LANES), lambda b, t: (b, past // t_new, 0))
        ins = [q, qa, kt, kat, v, kn, ka, vn, bias_new]
        in_specs = [new, new_aug, dims_keys, pl.BlockSpec((1, LANES, tk), lambda b, t: (b, 0, t)), dims_keys,
                    new, new_aug, new, _full_spec(bias_new.shape)]
        kd, acc_w = d + LANES, SAMPLE_GROUP
    else:
        lqk, sg = extra
        ins = [q, kt, v, kn, vn, bias_last, bias_new, lqk, sg]
        in_specs = [new, dims_keys, pl.BlockSpec((1, 1, tk * DIFF_HEADS, LANES), lambda b, t: (layer, b, t, 0)),
                    new, new, _full_spec(bias_last.shape), _full_spec(bias_new.shape),
                    _full_spec(lqk.shape), _full_spec(sg.shape)]
        kd, acc_w = d, LANES
    return pl.pallas_call(
        functools.partial(_sample_attn_kernel, fox=fox, t_new=t_new, tk=tk, n_tiles=n_tiles, lam_init=lam_init),
        grid=(dec_batch, n_tiles),
        in_specs=in_specs,
        out_specs=out,
        out_shape=jax.ShapeDtypeStruct((dec_batch * t_new, d), BF16),
        scratch_shapes=[pltpu.VMEM((rows, kd), BF16), pltpu.VMEM((rows, 1), F32),
                        pltpu.VMEM((rows, 1), F32), pltpu.VMEM((rows, acc_w), F32)],
        compiler_params=_cparams(2),
        name="fox_sample_attn" if fox else "diff_sample_attn",
    )(*ins)


def _split3(c):
    hi = c.astype(BF16)
    r = c - hi.astype(F32)
    mid = r.astype(BF16)
    lo = (r - mid.astype(F32)).astype(BF16)
    return hi, mid, lo


def _decay_cols_kernel(lf_ref, part_ref, qa_ref, ka_ref, *maybe_kat_ref, length):
    part = part_ref[0:1, :]
    which = part_ref[1:2, :]
    carry = jnp.zeros((1, LANES), F32)
    for r in range(0, length, CUM_BLOCK):
        n = min(CUM_BLOCK, length - r)
        tri = (lax.broadcasted_iota(jnp.int32, (n, n), 1) <= lax.broadcasted_iota(jnp.int32, (n, n), 0))
        tri = jnp.where(tri, 1.0, 0.0).astype(BF16)
        c = carry
        for p in _split3(lf_ref[0, r:r + n, :]):
            c = c + jnp.dot(tri, p, preferred_element_type=F32)
        carry = c[n - 1:n, :]
        hi, mid, lo = (t.astype(F32) for t in _split3(c * LOG2E))
        terms = jnp.where(which == 0, hi, jnp.where(which == 1, mid, lo))
        qa = jnp.where(part < 0, 0.0, jnp.where(part < 3, terms, 1.0))
        ka = jnp.where(part < 0, 0.0, jnp.where(part < 3, 1.0, -terms))
        qa_ref[0, r:r + n, :] = qa.astype(BF16)
        ka_ref[0, r:r + n, :] = ka.astype(BF16)
        if maybe_kat_ref:
            maybe_kat_ref[0][0, :, r:r + n] = ka.T.astype(BF16)


def _decay_layout():
    part = np.full((2, LANES), -1, np.int32)
    used = np.arange(FOX_HEADS * DECAY_COLS) % DECAY_COLS
    part[0, :used.size] = used
    part[1, :used.size] = used % 3
    return jnp.asarray(part)


def _spread_heads(a):
    rep = jnp.repeat(a, DECAY_COLS, axis=-1)
    return jnp.pad(rep, [(0, 0)] * (a.ndim - 1) + [(0, LANES - rep.shape[-1])])


def _decay_cols(lf, keys_transposed):
    nb, length, _ = lf.shape
    part = _decay_layout()
    spec = pl.BlockSpec((1, length, LANES), lambda b: (b, 0, 0))
    out_specs = [spec, spec]
    out_shape = [jax.ShapeDtypeStruct((nb, length, LANES), BF16)] * 2
    if keys_transposed:
        out_specs.append(pl.BlockSpec((1, LANES, length), lambda b: (b, 0, 0)))
        out_shape.append(jax.ShapeDtypeStruct((nb, LANES, length), BF16))
    return pl.pallas_call(
        functools.partial(_decay_cols_kernel, length=length),
        grid=(nb,),
        in_specs=[spec, _full_spec(part.shape)],
        out_specs=out_specs,
        out_shape=out_shape,
        compiler_params=_cparams(1),
        name="fox_decay_cols",
    )(lf, part)


def _conv_dw_kernel(u_ref, halo_ref, wdw_ref, bdw_ref, g_ref, b_ref, o_ref, ext_ref, sh_ref, y_ref, *, tm,
                    steps_per_seq):
    halo = halo_ref[...]
    if steps_per_seq:
        halo = jnp.where(pl.program_id(0) % steps_per_seq == 0, 0.0, halo)
    ext_ref[0:HALO, :] = halo
    ext_ref[HALO:HALO + tm, :] = u_ref[...]
    span = sh_ref.shape[1]
    for s in range(1, SUBLANES):
        sh_ref[s - 1] = ext_ref[s:s + span, :]
    d = u_ref.shape[1]
    first = HALO - (CONV_WIDTH - 1)
    for rc in range(tm // CONV_RC):
        for lc in range(d // CONV_LC):
            cols = slice(lc * CONV_LC, (lc + 1) * CONV_LC)
            acc = jnp.zeros((CONV_RC, CONV_LC), F32)
            for w in range(CONV_WIDTH):
                a, s = divmod(first + w, SUBLANES)
                src = ext_ref if s == 0 else sh_ref.at[s - 1]
                r = rc * CONV_RC + a * SUBLANES
                acc = acc + src[r:r + CONV_RC, cols] * wdw_ref[w:w + 1, cols]
            y_ref[rc * CONV_RC:(rc + 1) * CONV_RC, cols] = acc
    y = y_ref[...] + bdw_ref[...]
    yc = y - jnp.mean(y, axis=-1, keepdims=True)
    yn = yc * lax.rsqrt(jnp.mean(yc * yc, axis=-1, keepdims=True) + EPS) * g_ref[...] + b_ref[...]
    o_ref[...] = (yn * (1.0 / (1.0 + jnp.exp(-yn)))).astype(BF16)


def _conv_dw(u, row0, n_rows, tm, halo, halo_index, steps_per_seq, wdw, bdw, g, b):
    d = u.shape[1]
    b0 = row0 // tm
    return pl.pallas_call(
        functools.partial(_conv_dw_kernel, tm=tm, steps_per_seq=steps_per_seq),
        grid=(n_rows // tm,),
        in_specs=[pl.BlockSpec((tm, d), lambda i: (b0 + i, 0)), pl.BlockSpec((HALO, d), lambda i: (halo_index(i), 0)),
                  _full_spec(wdw.shape), _full_spec(bdw.shape), _full_spec(g.shape), _full_spec(b.shape)],
        out_specs=pl.BlockSpec((tm, d), lambda i: (i, 0)),
        out_shape=jax.ShapeDtypeStruct((n_rows, d), BF16),
        scratch_shapes=[pltpu.VMEM((HALO + tm, d), F32), pltpu.VMEM((SUBLANES - 1, HALO + tm - SUBLANES, d), F32),
                        pltpu.VMEM((tm, d), F32)],
        compiler_params=_cparams(1),
        name="conv_dw",
    )(u, halo, wdw, bdw, g, b)


def _dims_keys(cache):
    nd = cache.ndim
    t = jnp.transpose(cache, (0, 1) + tuple(range(3, nd)) + (2,))
    return t.reshape(cache.shape[0], cache.shape[1], D_MODEL, cache.shape[2])


def kernel(x_prompt, x_sample, cache_diff_k, cache_diff_v, state_conv, cache_fox_k, cache_fox_v, cache_fox_logf, rel_bias, norm_g, final_g, diff_w_in, diff_w_out, diff_lq1, diff_lk1, diff_lq2, diff_lk2, diff_subln_g, conv_w_pw1, conv_b_pw1, conv_w_dw, conv_b_dw, conv_ln_g, conv_ln_b, conv_w_pw2, conv_b_pw2, fox_w_in, fox_b_f, fox_w_out, mlp_w1, mlp_w2):
    B, S, D = x_prompt.shape
    Bd, T, _ = x_sample.shape
    P = cache_diff_k.shape[2]
    depth = norm_g.shape[0]
    n_p, n_s = B * S, Bd * T
    assert D == D_MODEL and S % (2 * ATT_TQ) == 0 and P % SAMPLE_TK == 0 and n_p % ROW_TILE == 0
    assert n_s % ROW_TILE == 0 and T % 8 == 0 and T >= CONV_WIDTH - 1 and S % ROW_TILE == 0

    x = jnp.concatenate([x_prompt.reshape(n_p, D), x_sample.reshape(n_s, D)], axis=0)
    zero_bias = jnp.zeros((1, D), F32)
    final_gain = final_g.reshape(1, D)

    bt_diff, bt_fox = _prompt_tiles(rel_bias, ATT_TQ)
    sb_last, sb_new, sf_new = _sample_tiles(rel_bias, T, P, SAMPLE_TK)

    diff_kt = _dims_keys(cache_diff_k)
    diff_v = cache_diff_v.reshape(cache_diff_v.shape[0], Bd, P * DIFF_HEADS, 2 * DIFF_HEAD_DIM)
    fox_kt = _dims_keys(cache_fox_k)
    fox_vt = _dims_keys(cache_fox_v)

    n_diff, n_fox = diff_w_in.shape[0], fox_w_in.shape[0]
    dk_shape = (1, n_p, D)
    dv_shape = (n_diff, n_p, DIFF_HEADS, 2 * DIFF_HEAD_DIM)
    fkv_shape = (n_fox, n_p, FOX_HEADS, FOX_HEAD_DIM)
    dv_new, fk_new, fv_new = None, None, None
    dk_p, dk_s, dv_s = [], [], []
    cv_p, cv_s = [], []
    fl_p, fk_s, fv_s, fl_s = [], [], [], []
    y = None

    for i in range(depth):
        kind, j = i % N_MIXERS, i // N_MIXERS
        g1 = norm_g[i, 0].reshape(1, D)
        g2 = norm_g[i, 1].reshape(1, D)
        if kind == 0:
            lam_init = 0.8 - 0.6 * math.exp(-0.3 * i)
            q, k_p, k_s, dv_new, v_s, kb, vb = _qkv_inproj(
                x, n_p, g1, diff_w_in[j].astype(BF16), DIFF_SCALE, (dk_shape, 0), (dv_shape, j),
                prev=None if dv_new is None else {3: dv_new})
            dk_p.append(k_p.reshape(B, S, DIFF_HEADS, 2, DIFF_HEAD_DIM))
            lqk = jnp.stack([diff_lq1[j], diff_lk1[j], diff_lq2[j], diff_lk2[j]]).astype(F32)
            sg = diff_subln_g[j].reshape(1, 2 * DIFF_HEAD_DIM).astype(F32)
            o_p = _prompt_attn(False, B, S, q, kb, vb, bt_diff, (lqk, sg.reshape(2 * DIFF_HEAD_DIM, 1)), lam_init)
            o_s = _sample_attn(False, j, n_p, Bd, T, P, q, diff_kt, diff_v, kb, vb, sb_last, sb_new, (lqk, sg),
                               lam_init)
            wo, bo = diff_w_out[j].astype(BF16), zero_bias
            dk_s.append(k_s.reshape(Bd, T, DIFF_HEADS, 2, DIFF_HEAD_DIM))
            dv_s.append(v_s.reshape(Bd, T, DIFF_HEADS, 2 * DIFF_HEAD_DIM))
        elif kind == 1:
            u = _conv_inproj(x, g1, conv_w_pw1[j].astype(BF16), conv_b_pw1[j].reshape(1, 2 * D))
            wdw = jnp.pad(conv_w_dw[j], ((0, HALO - CONV_WIDTH), (0, 0)))
            args = (wdw, conv_b_dw[j].reshape(1, D), conv_ln_g[j].reshape(1, D), conv_ln_b[j].reshape(1, D))
            per_tile = ROW_TILE // HALO
            o_p = _conv_dw(u, 0, n_p, ROW_TILE, u, lambda t: jnp.maximum(t * per_tile - 1, 0), S // ROW_TILE, *args)
            state = jnp.pad(state_conv[j], ((0, 0), (HALO - (CONV_WIDTH - 1), 0), (0, 0))).reshape(Bd * HALO, D)
            o_s = _conv_dw(u, n_p, n_s, T, state, lambda t: t, 0, *args)
            wo, bo = conv_w_pw2[j].astype(BF16), conv_b_pw2[j].reshape(1, D)
            keep = CONV_WIDTH - 1
            cv_p.append(jnp.stack([u[(b + 1) * S - keep:(b + 1) * S] for b in range(B)]))
            cv_s.append(u[n_p:].reshape(Bd, T, D)[:, T - keep:])
        else:
            w_in = fox_w_in[j]
            pad_heads = lambda a: jnp.pad(a, [(0, 0)] * (a.ndim - 1) + [(0, LANES - FOX_HEADS)])
            wf = jnp.concatenate([_spread_heads(w_in[:, 3 * D:]), pad_heads(w_in[:, 3 * D:])], axis=1).astype(BF16)
            b_f = fox_b_f[j].astype(F32).reshape(1, FOX_HEADS)
            bf = jnp.concatenate([_spread_heads(b_f), pad_heads(b_f)], axis=1)
            w_qkv = w_in[:, :3 * D].astype(BF16)
            w_rot = [jnp.roll(w_qkv[:, c * D:(c + 1) * D], -FOX_HEAD_DIM, axis=1) for c in (1, 2)]
            q, fk_new, k_s, fv_new, v_s, kb, vb, lf, lf_heads = _qkv_inproj(
                x, n_p, g1, jnp.concatenate([w_qkv] + w_rot, axis=1), FOX_SCALE, (fkv_shape, j), (fkv_shape, j),
                prev=None if fk_new is None else {1: fk_new, 3: fv_new}, forget=(wf, bf))
            lf_p_cols = lf[:n_p].reshape(B, S, LANES)
            lf_s_cols = lf[n_p:].reshape(Bd, T, LANES)
            lf_p = lf_heads[:n_p].reshape(B, S, FOX_HEADS)
            lf_s = lf_heads[n_p:].reshape(Bd, T, FOX_HEADS)
            qa_p, ka_p = _decay_cols(lf_p_cols, False)
            cache_cols = _spread_heads(cache_fox_logf[j].astype(F32))
            decay_s = _decay_cols(jnp.concatenate([cache_cols, lf_s_cols], axis=1), True)
            o_p = _prompt_attn(True, B, S, q, kb, vb, bt_fox, (qa_p.reshape(n_p, LANES), ka_p.reshape(n_p, LANES)))
            o_s = _sample_attn(True, j, n_p, Bd, T, P, q, fox_kt, fox_vt, kb, vb, None, sf_new, decay_s)
            wo, bo = fox_w_out[j].astype(BF16), zero_bias
            fl_p.append(lf_p)
            fk_s.append(k_s.reshape(Bd, T, FOX_HEADS, FOX_HEAD_DIM))
            fv_s.append(v_s.reshape(Bd, T, FOX_HEADS, FOX_HEAD_DIM))
            fl_s.append(lf_s)
        final = i == depth - 1
        out = _mix_mlp(x, o_p, o_s, wo, bo, g2, mlp_w1[i].astype(BF16), mlp_w2[i].astype(BF16), final_gain, final)
        if final:
            y = out
        else:
            x = out[0]

    y_prompt = y[0].reshape(B, S, D)
    y_sample = y[1].reshape(Bd, T, D)
    new_diff_k_p = jnp.stack(dk_p)
    new_diff_v_p = dv_new.reshape(n_diff, B, S, DIFF_HEADS, 2 * DIFF_HEAD_DIM)
    new_fox_k_p = fk_new.reshape(n_fox, B, S, FOX_HEADS, FOX_HEAD_DIM)
    new_fox_v_p = fv_new.reshape(n_fox, B, S, FOX_HEADS, FOX_HEAD_DIM)
    return (y_prompt, y_sample, new_diff_k_p, new_diff_v_p, jnp.stack(cv_p), new_fox_k_p,
            new_fox_v_p, jnp.stack(fl_p), jnp.stack(dk_s), jnp.stack(dv_s), jnp.stack(cv_s),
            jnp.stack(fk_s), jnp.stack(fv_s), jnp.stack(fl_s))
```

```python
import functools
import math

import numpy as np
import jax
import jax.numpy as jnp
from jax import lax
from jax.experimental import pallas as pl
from jax.experimental.pallas import tpu as pltpu

F32 = jnp.float32
BF16 = jnp.bfloat16

D_MODEL = 1024
CHUNK = 64
EPS = 1e-6
DIFF_HEADS = 8
DIFF_HEAD_DIM = 64
LOG2E = math.log2(math.e)
DIFF_SCALE = DIFF_HEAD_DIM ** -0.5
CONV_WIDTH = 31
FOX_HEADS = 16
FOX_HEAD_DIM = 64
FOX_SCALE = FOX_HEAD_DIM ** -0.5
REL_BUCKETS = 32
N_MIXERS = 3

LANES = 128
SUBLANES = 8
NEG = -1e30
ROW_TILE = 512
ATT_TQ = 512
ATT_STRIP = 512
ATT_GROUPS = 2
REDUCE_ROWS = 64
SAMPLE_TK = 2048
SAMPLE_GROUP = 256
FF_CHUNK = 1024
CUM_BLOCK = 256
DECAY_COLS = 6
HALO = 32
CONV_RC = 32
CONV_LC = 256
VMEM_LIMIT = 56 * 1024 * 1024

_T5_STEPS = (12, 16, 23, 32, 46, 64, 91)
T5_NEAR = 128
assert _T5_STEPS[-1] <= T5_NEAR


def _cparams(n_axes):
    return pltpu.CompilerParams(dimension_semantics=("arbitrary",) * n_axes, vmem_limit_bytes=VMEM_LIMIT)


def _rms(x, g):
    return x * lax.rsqrt(jnp.mean(x * x, axis=-1, keepdims=True) + EPS) * g


def _full_spec(shape):
    nd = len(shape)
    return pl.BlockSpec(shape, lambda *_: (0,) * nd)


def _split_specs(n_prompt_tiles, tm, d):
    prompt = pl.BlockSpec((tm, d), lambda i: (jnp.minimum(i, n_prompt_tiles - 1), 0))
    sample = pl.BlockSpec((tm, d), lambda i: (jnp.maximum(i - n_prompt_tiles, 0), 0))
    return prompt, sample


def _split_store(n_prompt_tiles, value, prompt_ref, sample_ref):
    i = pl.program_id(0)

    @pl.when(i < n_prompt_tiles)
    def _():
        prompt_ref[...] = value

    @pl.when(i >= n_prompt_tiles)
    def _():
        sample_ref[...] = value


def _store_prompt_cache(value, prompt_ref, layer):
    layer = layer if prompt_ref.shape[0] > 1 else 0
    for other in range(prompt_ref.shape[0]):
        if other != layer:
            prompt_ref[other] = jnp.zeros(prompt_ref.shape[1:], prompt_ref.dtype)
    if len(prompt_ref.shape) == 3:
        prompt_ref[layer] = value
    else:
        prompt_ref[layer] = pltpu.einshape("m(hd)->mhd", value, h=prompt_ref.shape[2])


def _qkv_inproj_kernel(*refs, n_pt, scale, n_prev, forget, layers):
    x_ref, g_ref, w_ref = refs[:3]
    refs = refs[3:]
    if forget:
        wf_ref, bf_ref = refs[:2]
        refs = refs[2:]
    refs = refs[n_prev:]
    q_ref, kp_ref, ks_ref, vp_ref, vs_ref, kb_ref, vb_ref = refs[:7]
    d = D_MODEL

    def project(prompt):
        xn = _rms(x_ref[...], g_ref[...]).astype(BF16)
        q = jnp.dot(xn, w_ref[:, 0:d], preferred_element_type=F32)
        q_ref[...] = (q * (scale * LOG2E)).astype(BF16)
        for c, new_p, new_s, bf_ref_out in ((1, kp_ref, ks_ref, kb_ref), (2, vp_ref, vs_ref, vb_ref)):
            val = jnp.dot(xn, w_ref[:, c * d:(c + 1) * d], preferred_element_type=F32)
            if prompt:
                _store_prompt_cache(val, new_p, layers[c - 1])
            else:
                new_s[...] = val
            bf_ref_out[...] = val.astype(BF16)
        if forget:
            z = jnp.dot(xn, wf_ref[...], preferred_element_type=F32) + bf_ref[...]
            lf = jnp.minimum(z, 0.0) - jnp.log1p(jnp.exp(-jnp.abs(z)))
            refs[7][...] = lf[:, 0:LANES]
            refs[8][...] = lf[:, LANES:LANES + FOX_HEADS]

    pl.when(pl.program_id(0) < n_pt)(lambda: project(True))
    pl.when(pl.program_id(0) >= n_pt)(lambda: project(False))


def _qkv_inproj(x, n_p, g, w, scale, k_slab, v_slab, prev=None, forget=()):
    n, d = x.shape
    tm = ROW_TILE
    n_pt = n_p // tm
    row = pl.BlockSpec((tm, d), lambda i: (i, 0))
    _, sample = _split_specs(n_pt, tm, d)
    prev = prev or {}

    def slab(out_index, shape, layer):
        rest = tuple(shape[2:])
        zeros = (0,) * len(rest)
        if out_index in prev:
            return pl.BlockSpec((1, tm) + rest, lambda i: (layer, jnp.minimum(i, n_pt - 1)) + zeros)
        return pl.BlockSpec((shape[0], tm) + rest, lambda i: (0, jnp.minimum(i, n_pt - 1)) + zeros)

    f32_s = jax.ShapeDtypeStruct((n - n_p, d), F32)
    bf_all = jax.ShapeDtypeStruct((n, d), BF16)
    ins = [x, g, w] + list(forget) + list(prev.values())
    w_spec = pl.BlockSpec(w.shape, lambda i: (0, 0), pipeline_mode=pl.Buffered(1))
    in_specs = ([row, _full_spec((1, d)), w_spec] + [_full_spec(a.shape) for a in forget]
                + [pl.BlockSpec(memory_space=pl.ANY)] * len(prev))
    out_specs = [row, slab(1, *k_slab), sample, slab(3, *v_slab), sample, row, row]
    out_shape = [bf_all, jax.ShapeDtypeStruct(k_slab[0], F32), f32_s, jax.ShapeDtypeStruct(v_slab[0], F32), f32_s,
                 bf_all, bf_all]
    if forget:
        out_specs += [pl.BlockSpec((tm, LANES), lambda i: (i, 0)), pl.BlockSpec((tm, FOX_HEADS), lambda i: (i, 0))]
        out_shape += [jax.ShapeDtypeStruct((n, LANES), F32), jax.ShapeDtypeStruct((n, FOX_HEADS), F32)]
    first_prev = 3 + len(forget)
    aliases = {first_prev + pos: out_index for pos, out_index in enumerate(prev)}
    return pl.pallas_call(
        functools.partial(_qkv_inproj_kernel, n_pt=n_pt, scale=scale, n_prev=len(prev), forget=bool(forget),
                          layers=(k_slab[1], v_slab[1])),
        grid=(n // tm,),
        in_specs=in_specs,
        out_specs=out_specs,
        out_shape=out_shape,
        input_output_aliases=aliases,
        compiler_params=_cparams(1),
        name="fox_inproj" if forget else "diff_inproj",
    )(*ins)


def _conv_inproj_kernel(x_ref, g_ref, w_ref, b_ref, u_ref):
    d = D_MODEL
    xn = _rms(x_ref[...], g_ref[...]).astype(BF16)
    a = jnp.dot(xn, w_ref[:, 0:d], preferred_element_type=F32) + b_ref[:, 0:d]
    gate = jnp.dot(xn, w_ref[:, d:2 * d], preferred_element_type=F32) + b_ref[:, d:2 * d]
    u_ref[...] = a * (1.0 / (1.0 + jnp.exp(-gate)))


def _conv_inproj(x, g, w, b):
    n, d = x.shape
    tm = ROW_TILE
    row = pl.BlockSpec((tm, d), lambda i: (i, 0))
    return pl.pallas_call(
        _conv_inproj_kernel,
        grid=(n // tm,),
        in_specs=[row, _full_spec((1, d)), _full_spec(w.shape), _full_spec(b.shape)],
        out_specs=row,
        out_shape=jax.ShapeDtypeStruct((n, d), F32),
        compiler_params=_cparams(1),
        name="conv_inproj",
    )(x, g, w, b)


def _mix_mlp_kernel(x_ref, op_ref, os_ref, wo_ref, bo_ref, g_ref, w1_ref, w2_ref, gf_ref, *out_refs, n_pt, final):
    o = jnp.where(pl.program_id(0) < n_pt, op_ref[...], os_ref[...])
    x1 = x_ref[...] + jnp.dot(o, wo_ref[...], preferred_element_type=F32) + bo_ref[...]
    xn = _rms(x1, g_ref[...]).astype(BF16)
    acc = x1
    d_ff = w1_ref.shape[1]
    for c in range(d_ff // FF_CHUNK):
        h = jnp.dot(xn, w1_ref[:, c * FF_CHUNK:(c + 1) * FF_CHUNK], preferred_element_type=F32)
        h = jnp.square(jnp.maximum(h, 0.0)).astype(BF16)
        acc = acc + jnp.dot(h, w2_ref[c * FF_CHUNK:(c + 1) * FF_CHUNK, :], preferred_element_type=F32)
    if final:
        _split_store(n_pt, _rms(acc, gf_ref[...]), *out_refs)
    else:
        out_refs[0][...] = acc


def _mix_mlp(x, o_p, o_s, wo, bo, g, w1, w2, gf, final):
    n, d = x.shape
    tm = ROW_TILE
    n_p = o_p.shape[0]
    n_pt = n_p // tm
    row = pl.BlockSpec((tm, d), lambda i: (i, 0))
    prompt, sample = _split_specs(n_pt, tm, d)
    resident = lambda a: pl.BlockSpec(a.shape, lambda i: (0,) * a.ndim, pipeline_mode=pl.Buffered(1))
    if final:
        out_specs = [prompt, sample]
        out_shape = [jax.ShapeDtypeStruct((n_p, d), F32), jax.ShapeDtypeStruct((n - n_p, d), F32)]
    else:
        out_specs = [row]
        out_shape = [jax.ShapeDtypeStruct((n, d), F32)]
    return pl.pallas_call(
        functools.partial(_mix_mlp_kernel, n_pt=n_pt, final=final),
        grid=(n // tm,),
        in_specs=[row, prompt, sample, resident(wo), _full_spec(bo.shape), _full_spec(g.shape), resident(w1),
                  resident(w2), _full_spec(gf.shape)],
        out_specs=out_specs,
        out_shape=out_shape,
        compiler_params=_cparams(1),
        name="mix_mlp",
    )(x, o_p, o_s, wo, bo, g, w1, w2, gf)


def _t5_bias(tab_ref, h, qpos, kpos):
    half = REL_BUCKETS // 2
    n = qpos - kpos
    off = jnp.where(n < 0, half, 0)
    n = jnp.abs(n)
    large = jnp.full(n.shape, half // 2, jnp.int32)
    for t in _T5_STEPS:
        large = large + jnp.where(n >= t, 1, 0)
    bucket = off + jnp.where(n < half // 2, n, large)
    far = tab_ref[half - 1, h]
    out = jnp.zeros(n.shape, F32)
    for b in range(REL_BUCKETS):
        out = jnp.where(bucket == b, (tab_ref[b, h] - far) * LOG2E, out)
    visible = (kpos // CHUNK) <= (qpos // CHUNK)
    return jnp.where(visible, out, NEG)


def _prompt_tiles_kernel(tab_ref, bt_ref, cm_ref, *, tq):
    h = pl.program_id(0)
    for v in range(2):
        kpos = lax.broadcasted_iota(jnp.int32, (tq, tq), 0)
        qpos = lax.broadcasted_iota(jnp.int32, (tq, tq), 1) + (1 - v) * tq
        bt_ref[0, v] = _t5_bias(tab_ref, h, qpos, kpos)

    @pl.when(h == 0)
    def _():
        kpos = lax.broadcasted_iota(jnp.int32, (tq, tq), 0)
        qpos = lax.broadcasted_iota(jnp.int32, (tq, tq), 1)
        cm_ref[0, 0] = jnp.zeros((tq, tq), F32)
        cm_ref[0, 1] = jnp.where(kpos <= qpos, 0.0, NEG)


def _prompt_tiles(table, tq):
    return pl.pallas_call(
        functools.partial(_prompt_tiles_kernel, tq=tq),
        grid=(DIFF_HEADS,),
        in_specs=[pl.BlockSpec(memory_space=pltpu.SMEM)],
        out_specs=[pl.BlockSpec((1, 2, tq, tq), lambda h: (h, 0, 0, 0)),
                   pl.BlockSpec((1, 2, tq, tq), lambda h: (0, 0, 0, 0))],
        out_shape=[jax.ShapeDtypeStruct((DIFF_HEADS, 2, tq, tq), F32), jax.ShapeDtypeStruct((1, 2, tq, tq), F32)],
        compiler_params=_cparams(1),
        name="prompt_score_tiles",
    )(table)


def _sample_tiles_kernel(tab_ref, dl_ref, dn_ref, fn_ref, *, t_new, past, tk):
    rows = DIFF_HEADS * 2 * t_new
    for h in range(DIFF_HEADS):
        r0 = h * 2 * t_new
        q = lax.broadcasted_iota(jnp.int32, (2 * t_new, tk), 0) % t_new + past
        k = lax.broadcasted_iota(jnp.int32, (2 * t_new, tk), 1) + (past - tk)
        dl_ref[r0:r0 + 2 * t_new, :] = _t5_bias(tab_ref, h, q, k)
        q = lax.broadcasted_iota(jnp.int32, (2 * t_new, t_new), 0) % t_new + past
        k = lax.broadcasted_iota(jnp.int32, (2 * t_new, t_new), 1) + past
        dn_ref[r0:r0 + 2 * t_new, :] = _t5_bias(tab_ref, h, q, k)
    q = lax.broadcasted_iota(jnp.int32, (rows, t_new), 0) % t_new
    k = lax.broadcasted_iota(jnp.int32, (rows, t_new), 1)
    fn_ref[...] = jnp.where(k <= q, 0.0, NEG)


def _sample_tiles(table, t_new, past, tk):
    rows = DIFF_HEADS * 2 * t_new
    assert rows == FOX_HEADS * t_new
    return pl.pallas_call(
        functools.partial(_sample_tiles_kernel, t_new=t_new, past=past, tk=tk),
        in_specs=[pl.BlockSpec(memory_space=pltpu.SMEM)],
        out_shape=[jax.ShapeDtypeStruct((rows, tk), F32), jax.ShapeDtypeStruct((rows, t_new), F32),
                   jax.ShapeDtypeStruct((rows, t_new), F32)],
        name="sample_score_tiles",
    )(table)


def _lambda(lqk_ref, lam_init):
    a = jnp.sum(lqk_ref[0:1, :] * lqk_ref[1:2, :], axis=-1, keepdims=True)
    b = jnp.sum(lqk_ref[2:3, :] * lqk_ref[3:4, :], axis=-1, keepdims=True)
    return jnp.exp(a) - jnp.exp(b) + lam_init


def _scores_t(k, q2_ref, s_out):
    for c0 in range(0, q2_ref.shape[0], ATT_STRIP):
        cols = slice(c0, c0 + ATT_STRIP)
        s_out[:, cols] = lax.dot_general(k, q2_ref[cols, :], (((1,), (1,)), ((), ())), preferred_element_type=F32)


def _reduce_rows(op, x):
    w, n = x.shape
    part = op(x.reshape(w // REDUCE_ROWS, REDUCE_ROWS, n), axis=0)
    return op(part, axis=0, keepdims=True)


def _softmax_pv_t(s_in, v, bias_t, bias_rows, m_ref, l_ref, acc_ref):
    w = s_in.shape[0]
    for c0 in range(0, s_in.shape[1], ATT_STRIP):
        cols = slice(c0, c0 + ATT_STRIP)
        s = s_in[:, cols]
        if bias_t is not None:
            b0 = c0 % bias_t.shape[1]
            r0 = w - bias_rows
            biased = s[r0:, :] + bias_t[r0:, b0:b0 + ATT_STRIP]
            s = biased if r0 == 0 else jnp.concatenate([s[:r0, :], biased], axis=0)
        m_prev = m_ref[:, cols]
        m_new = jnp.maximum(m_prev, _reduce_rows(jnp.max, s))
        alpha = jnp.exp2(m_prev - m_new)
        p = jnp.exp2(s - m_new)
        l_ref[:, cols] = alpha * l_ref[:, cols] + _reduce_rows(jnp.sum, p)
        pv = lax.dot_general(v, p.astype(BF16), (((0,), (0,)), ((), ())), preferred_element_type=F32)
        acc_ref[:, cols] = alpha * acc_ref[:, cols] + pv
        m_ref[:, cols] = m_new


def _prompt_attn_kernel(*refs, fox, tq, seq, lam_init):
    if fox:
        (q_ref, qa_ref, k_ref, ka_ref, v_ref, bt_ref, o_ref, q2_ref, s0_ref, s1_ref, m_ref, l_ref, acc_ref) = refs
    else:
        (q_ref, k_ref, v_ref, bt_ref, lqk_ref, sg_ref, o_ref, q2_ref, s0_ref, s1_ref, m_ref, l_ref, acc_ref) = refs
    s_refs = (s0_ref, s1_ref)
    n_groups = q_ref.shape[1] // LANES
    lane = lax.broadcasted_iota(jnp.int32, (tq, LANES), 1)
    lo = lane < LANES // 2
    first_rows = lax.broadcasted_iota(jnp.int32, (LANES, tq), 0) < LANES // 2
    lanes = [slice(g * LANES, (g + 1) * LANES) for g in range(n_groups)]
    if fox:
        in_a, in_b = [], []
        for g in range(n_groups):
            c0 = 12 * (pl.program_id(1) * n_groups + g)
            in_a.append(jnp.logical_and(lane >= c0, lane < c0 + 6))
            in_b.append(jnp.logical_and(lane >= c0 + 6, lane < c0 + 12))

    def scores_g(g, j, slot):
        k0 = pl.multiple_of(jnp.maximum(j, 0) * tq, tq)
        k = k_ref[pl.ds(k0, tq), lanes[g]]
        if fox:
            k = jnp.concatenate([k, ka_ref[pl.ds(k0, tq), :]], axis=1)
        _scores_t(k, q2_ref.at[g], s_refs[slot].at[g])

    def softmax_pv_g(g, j, slot, bias_index):
        k0 = pl.multiple_of(j * tq, tq)
        if bias_index is None or (fox and bias_index == 0):
            bias_t, bias_rows = None, 0
        else:
            bias_t = bt_ref.at[0 if fox else g, bias_index]
            bias_rows = tq if bias_index == 1 else T5_NEAR
        _softmax_pv_t(s_refs[slot].at[g], v_ref[pl.ds(k0, tq), lanes[g]], bias_t, bias_rows, m_ref.at[g],
                      l_ref.at[g], acc_ref.at[g])

    def scores(j, slot):
        for g in range(n_groups):
            scores_g(g, j, slot)

    def softmax_pv(j, slot, bias_index):
        for g in range(n_groups):
            softmax_pv_g(g, j, slot, bias_index)

    def step(j_next, slot_next, j, slot, bias_index):
        for g in range(n_groups):
            scores_g(g, j_next, slot_next)
            softmax_pv_g(g, j, slot, bias_index)

    def q_block(i, carry):
        r0 = pl.multiple_of(i * tq, tq)
        for g in range(n_groups):
            q = q_ref[pl.ds(r0, tq), lanes[g]]
            zero = jnp.zeros_like(q)
            q2_ref[g, 0:tq, 0:LANES] = jnp.where(lo, q, zero)
            q2_ref[g, tq:2 * tq, 0:LANES] = jnp.where(lo, zero, q)
            if fox:
                qa = qa_ref[pl.ds(r0, tq), :]
                q2_ref[g, 0:tq, LANES:2 * LANES] = jnp.where(in_a[g], qa, zero)
                q2_ref[g, tq:2 * tq, LANES:2 * LANES] = jnp.where(in_b[g], qa, zero)
        m_ref[...] = jnp.full(m_ref.shape, NEG, F32)
        l_ref[...] = jnp.zeros(l_ref.shape, F32)
        acc_ref[...] = jnp.zeros(acc_ref.shape, F32)

        @pl.when(i == 0)
        def _():
            scores(i, 0)
            softmax_pv(i, 0, 1)

        @pl.when(i >= 1)
        def _():
            scores(i, 0)
            step(i - 1, 1, i, 0, 1)
            step(i - 2, 0, i - 1, 1, 0)
            n_far = i - 1

            def pair(u, c):
                a = i - 2 - 2 * u
                step(a - 1, 1, a, 0, None)
                step(a - 2, 0, a - 1, 1, None)
                return c

            lax.fori_loop(0, n_far // 2, pair, 0)

            @pl.when(n_far % 2 == 1)
            def _():
                softmax_pv(0, 0, None)

        for g in range(n_groups):
            top = acc_ref[g, :, 0:tq] / l_ref[g, :, 0:tq]
            bot = acc_ref[g, :, tq:2 * tq] / l_ref[g, :, tq:2 * tq]
            if fox:
                o_t = jnp.where(first_rows, top, bot)
            else:
                o_t = top - _lambda(lqk_ref, lam_init) * bot
                ms = jnp.mean(o_t * o_t, axis=0, keepdims=True)
                o_t = o_t * lax.rsqrt(ms + EPS) * sg_ref[...] * (1.0 - lam_init)
            o_ref[pl.ds(r0, tq), lanes[g]] = o_t.T.astype(BF16)
        return carry

    lax.fori_loop(0, seq // tq, q_block, 0)


def _prompt_attn(fox, batch, seq, q, k, v, bt, extra, lam_init=0.0):
    tq = ATT_TQ
    ng = ATT_GROUPS
    steps = D_MODEL // (LANES * ng)
    blk = pl.BlockSpec((seq, ng * LANES), lambda b, h: (b, h))
    if fox:
        qa, ka = extra
        aug = pl.BlockSpec((seq, LANES), lambda b, h: (b, 0))
        ins = [q, qa, k, ka, v, bt]
        in_specs = [blk, aug, blk, aug, blk, pl.BlockSpec((1, 2, tq, tq), lambda b, h: (0, 0, 0, 0))]
        kd = 2 * LANES
    else:
        lqk, sg = extra
        ins = [q, k, v, bt, lqk, sg]
        in_specs = [blk, blk, blk, pl.BlockSpec((ng, 2, tq, tq), lambda b, h: (h, 0, 0, 0)),
                    _full_spec(lqk.shape), _full_spec(sg.shape)]
        kd = LANES
    return pl.pallas_call(
        functools.partial(_prompt_attn_kernel, fox=fox, tq=tq, seq=seq, lam_init=lam_init),
        grid=(batch, steps),
        in_specs=in_specs,
        out_specs=blk,
        out_shape=jax.ShapeDtypeStruct((batch * seq, D_MODEL), BF16),
        scratch_shapes=[pltpu.VMEM((ng, 2 * tq, kd), BF16), pltpu.VMEM((ng, tq, 2 * tq), F32),
                        pltpu.VMEM((ng, tq, 2 * tq), F32), pltpu.VMEM((ng, 1, 2 * tq), F32),
                        pltpu.VMEM((ng, 1, 2 * tq), F32), pltpu.VMEM((ng, LANES, 2 * tq), F32)],
        compiler_params=_cparams(2),
        name="fox_prompt_attn" if fox else "diff_prompt_attn",
    )(*ins)


def _online_update(s, pv_fn, m_ref, l_ref, acc_ref):
    m_prev = m_ref[...]
    m_new = jnp.maximum(m_prev, jnp.max(s, axis=-1, keepdims=True))
    alpha = jnp.exp2(m_prev - m_new)
    p = jnp.exp2(s - m_new)
    l_ref[...] = alpha * l_ref[...] + jnp.sum(p, axis=-1, keepdims=True)
    acc_ref[...] = alpha * acc_ref[...] + pv_fn(p.astype(BF16))
    m_ref[...] = m_new


def _sample_attn_kernel(*refs, fox, t_new, tk, n_tiles, lam_init):
    if fox:
        (q_ref, qa_ref, kt_ref, kat_ref, vt_ref, kn_ref, kan_ref, vn_ref, bn_ref,
         o_ref, qbd_ref, m_ref, l_ref, acc_ref) = refs
    else:
        (q_ref, kt_ref, v_ref, kn_ref, vn_ref, bl_ref, bn_ref, lqk_ref, sg_ref,
         o_ref, qbd_ref, m_ref, l_ref, acc_ref) = refs
    t = pl.program_id(1)
    n_blocks = D_MODEL // 64
    rows_per_head = 2 * t_new

    @pl.when(t == 0)
    def _():
        q = q_ref[...]
        zero = jnp.zeros_like(q)
        cb = lax.broadcasted_iota(jnp.int32, q.shape, 1) // 64
        if fox:
            qa = qa_ref[0]
            la = lax.broadcasted_iota(jnp.int32, qa.shape, 1)
        for rb in range(n_blocks):
            qbd_ref[rb * t_new:(rb + 1) * t_new, 0:D_MODEL] = jnp.where(cb == rb, q, zero)
            if fox:
                sel = jnp.logical_and(la >= 6 * rb, la < 6 * rb + 6)
                qbd_ref[rb * t_new:(rb + 1) * t_new, D_MODEL:D_MODEL + LANES] = jnp.where(sel, qa, jnp.zeros_like(qa))
        m_ref[...] = jnp.full(m_ref.shape, NEG, F32)
        l_ref[...] = jnp.zeros(l_ref.shape, F32)
        acc_ref[...] = jnp.zeros(acc_ref.shape, F32)

    n_groups = D_MODEL // SAMPLE_GROUP
    group_rows = (SAMPLE_GROUP // 64) * t_new
    groups = [(slice(g * group_rows, (g + 1) * group_rows), slice(g * SAMPLE_GROUP, (g + 1) * SAMPLE_GROUP))
              for g in range(n_groups)]

    def cache_scores():
        s = jnp.concatenate([jnp.dot(qbd_ref[rows, dims], kt_ref[0, 0, dims, :].astype(BF16),
                                     preferred_element_type=F32) for rows, dims in groups], axis=0)
        if fox:
            s = s + jnp.dot(qbd_ref[:, D_MODEL:D_MODEL + LANES], kat_ref[0], preferred_element_type=F32)
        return s

    if fox:
        def cache_pv(p):
            return jnp.concatenate(
                [lax.dot_general(p[rows, :], vt_ref[0, 0, dims, :].astype(BF16), (((1,), (1,)), ((), ())),
                                 preferred_element_type=F32) for rows, dims in groups], axis=0)

        def new_pv(p):
            return jnp.concatenate([jnp.dot(p[rows, :], vn_ref[:, dims], preferred_element_type=F32)
                                    for rows, dims in groups], axis=0)
    else:
        def cache_pv(p):
            return jnp.concatenate(
                [jnp.dot(p[h * rows_per_head:(h + 1) * rows_per_head, :],
                         v_ref[0, 0, pl.ds(h, tk, stride=DIFF_HEADS), :].astype(BF16), preferred_element_type=F32)
                 for h in range(DIFF_HEADS)], axis=0)

        def new_pv(p):
            return jnp.concatenate(
                [jnp.dot(p[h * rows_per_head:(h + 1) * rows_per_head, :], vn_ref[:, h * LANES:(h + 1) * LANES],
                         preferred_element_type=F32) for h in range(DIFF_HEADS)], axis=0)

    @pl.when(t < n_tiles - 1)
    def _():
        _online_update(cache_scores(), cache_pv, m_ref, l_ref, acc_ref)

    @pl.when(t == n_tiles - 1)
    def _():
        s = cache_scores()
        if not fox:
            s = s + bl_ref[...]
        _online_update(s, cache_pv, m_ref, l_ref, acc_ref)
        kn = kn_ref[...]
        if fox:
            kn = jnp.concatenate([kn, kan_ref[0]], axis=1)
        s = lax.dot_general(qbd_ref[...], kn, (((1,), (1,)), ((), ())), preferred_element_type=F32) + bn_ref[...]
        _online_update(s, new_pv, m_ref, l_ref, acc_ref)

        lane = lax.broadcasted_iota(jnp.int32, (t_new, LANES), 1)
        lo = lane < LANES // 2
        for g in range(D_MODEL // LANES):
            ra, rb = 2 * g * t_new, (2 * g + 1) * t_new
            cols = slice(g * LANES, (g + 1) * LANES)
            c0 = (g * LANES) % SAMPLE_GROUP if fox else 0
            acc_cols = slice(c0, c0 + LANES)
            top = acc_ref[ra:ra + t_new, acc_cols] / l_ref[ra:ra + t_new, :]
            bot = acc_ref[rb:rb + t_new, acc_cols] / l_ref[rb:rb + t_new, :]
            if fox:
                o = jnp.where(lo, top, bot)
            else:
                o = _rms(top - _lambda(lqk_ref, lam_init) * bot, sg_ref[...]) * (1.0 - lam_init)
            o_ref[:, cols] = o.astype(BF16)


def _sample_attn(fox, layer, n_prompt, dec_batch, t_new, past, q, kt, v, kn, vn, bias_last, bias_new, extra,
                 lam_init=0.0):
    tk = SAMPLE_TK
    n_tiles = past // tk
    d = D_MODEL
    rows = (d // 64) * t_new
    s0 = n_prompt // t_new
    new = pl.BlockSpec((t_new, d), lambda b, t: (s0 + b, 0))
    dims_keys = pl.BlockSpec((1, 1, d, tk), lambda b, t: (layer, b, 0, t))
    out = pl.BlockSpec((t_new, d), lambda b, t: (b, 0))
    if fox:
        qa, ka, kat = extra
        new_aug = pl.BlockSpec((1, t_new, LANES), lambda b, t: (b, past // t_new, 0))
        ins = [q, qa, kt, kat, v, kn, ka, vn, bias_new]
        in_specs = [new, new_aug, dims_keys, pl.BlockSpec((1, LANES, tk), lambda b, t: (b, 0, t)), dims_keys,
                    new, new_aug, new, _full_spec(bias_new.shape)]
        kd, acc_w = d + LANES, SAMPLE_GROUP
    else:
        lqk, sg = extra
        ins = [q, kt, v, kn, vn, bias_last, bias_new, lqk, sg]
        in_specs = [new, dims_keys, pl.BlockSpec((1, 1, tk * DIFF_HEADS, LANES), lambda b, t: (layer, b, t, 0)),
                    new, new, _full_spec(bias_last.shape), _full_spec(bias_new.shape),
                    _full_spec(lqk.shape), _full_spec(sg.shape)]
        kd, acc_w = d, LANES
    return pl.pallas_call(
        functools.partial(_sample_attn_kernel, fox=fox, t_new=t_new, tk=tk, n_tiles=n_tiles, lam_init=lam_init),
        grid=(dec_batch, n_tiles),
        in_specs=in_specs,
        out_specs=out,
        out_shape=jax.ShapeDtypeStruct((dec_batch * t_new, d), BF16),
        scratch_shapes=[pltpu.VMEM((rows, kd), BF16), pltpu.VMEM((rows, 1), F32),
                        pltpu.VMEM((rows, 1), F32), pltpu.VMEM((rows, acc_w), F32)],
        compiler_params=_cparams(2),
        name="fox_sample_attn" if fox else "diff_sample_attn",
    )(*ins)


def _split3(c):
    hi = c.astype(BF16)
    r = c - hi.astype(F32)
    mid = r.astype(BF16)
    lo = (r - mid.astype(F32)).astype(BF16)
    return hi, mid, lo


def _decay_cols_kernel(lf_ref, part_ref, qa_ref, ka_ref, *maybe_kat_ref, length):
    part = part_ref[0:1, :]
    which = part_ref[1:2, :]
    carry = jnp.zeros((1, LANES), F32)
    for r in range(0, length, CUM_BLOCK):
        n = min(CUM_BLOCK, length - r)
        tri = (lax.broadcasted_iota(jnp.int32, (n, n), 1) <= lax.broadcasted_iota(jnp.int32, (n, n), 0))
        tri = jnp.where(tri, 1.0, 0.0).astype(BF16)
        c = carry
        for p in _split3(lf_ref[0, r:r + n, :]):
            c = c + jnp.dot(tri, p, preferred_element_type=F32)
        carry = c[n - 1:n, :]
        hi, mid, lo = (t.astype(F32) for t in _split3(c * LOG2E))
        terms = jnp.where(which == 0, hi, jnp.where(which == 1, mid, lo))
        qa = jnp.where(part < 0, 0.0, jnp.where(part < 3, terms, 1.0))
        ka = jnp.where(part < 0, 0.0, jnp.where(part < 3, 1.0, -terms))
        qa_ref[0, r:r + n, :] = qa.astype(BF16)
        ka_ref[0, r:r + n, :] = ka.astype(BF16)
        if maybe_kat_ref:
            maybe_kat_ref[0][0, :, r:r + n] = ka.T.astype(BF16)


def _decay_layout():
    part = np.full((2, LANES), -1, np.int32)
    used = np.arange(FOX_HEADS * DECAY_COLS) % DECAY_COLS
    part[0, :used.size] = used
    part[1, :used.size] = used % 3
    return jnp.asarray(part)


def _spread_heads(a):
    rep = jnp.repeat(a, DECAY_COLS, axis=-1)
    return jnp.pad(rep, [(0, 0)] * (a.ndim - 1) + [(0, LANES - rep.shape[-1])])


def _decay_cols(lf, keys_transposed):
    nb, length, _ = lf.shape
    part = _decay_layout()
    spec = pl.BlockSpec((1, length, LANES), lambda b: (b, 0, 0))
    out_specs = [spec, spec]
    out_shape = [jax.ShapeDtypeStruct((nb, length, LANES), BF16)] * 2
    if keys_transposed:
        out_specs.append(pl.BlockSpec((1, LANES, length), lambda b: (b, 0, 0)))
        out_shape.append(jax.ShapeDtypeStruct((nb, LANES, length), BF16))
    return pl.pallas_call(
        functools.partial(_decay_cols_kernel, length=length),
        grid=(nb,),
        in_specs=[spec, _full_spec(part.shape)],
        out_specs=out_specs,
        out_shape=out_shape,
        compiler_params=_cparams(1),
        name="fox_decay_cols",
    )(lf, part)


def _conv_dw_kernel(u_ref, halo_ref, wdw_ref, bdw_ref, g_ref, b_ref, o_ref, ext_ref, sh_ref, y_ref, *, tm,
                    steps_per_seq):
    halo = halo_ref[...]
    if steps_per_seq:
        halo = jnp.where(pl.program_id(0) % steps_per_seq == 0, 0.0, halo)
    ext_ref[0:HALO, :] = halo
    ext_ref[HALO:HALO + tm, :] = u_ref[...]
    span = sh_ref.shape[1]
    for s in range(1, SUBLANES):
        sh_ref[s - 1] = ext_ref[s:s + span, :]
    d = u_ref.shape[1]
    first = HALO - (CONV_WIDTH - 1)
    for rc in range(tm // CONV_RC):
        for lc in range(d // CONV_LC):
            cols = slice(lc * CONV_LC, (lc + 1) * CONV_LC)
            acc = jnp.zeros((CONV_RC, CONV_LC), F32)
            for w in range(CONV_WIDTH):
                a, s = divmod(first + w, SUBLANES)
                src = ext_ref if s == 0 else sh_ref.at[s - 1]
                r = rc * CONV_RC + a * SUBLANES
                acc = acc + src[r:r + CONV_RC, cols] * wdw_ref[w:w + 1, cols]
            y_ref[rc * CONV_RC:(rc + 1) * CONV_RC, cols] = acc
    y = y_ref[...] + bdw_ref[...]
    yc = y - jnp.mean(y, axis=-1, keepdims=True)
    yn = yc * lax.rsqrt(jnp.mean(yc * yc, axis=-1, keepdims=True) + EPS) * g_ref[...] + b_ref[...]
    o_ref[...] = (yn * (1.0 / (1.0 + jnp.exp(-yn)))).astype(BF16)


def _conv_dw(u, row0, n_rows, tm, halo, halo_index, steps_per_seq, wdw, bdw, g, b):
    d = u.shape[1]
    b0 = row0 // tm
    return pl.pallas_call(
        functools.partial(_conv_dw_kernel, tm=tm, steps_per_seq=steps_per_seq),
        grid=(n_rows // tm,),
        in_specs=[pl.BlockSpec((tm, d), lambda i: (b0 + i, 0)), pl.BlockSpec((HALO, d), lambda i: (halo_index(i), 0)),
                  _full_spec(wdw.shape), _full_spec(bdw.shape), _full_spec(g.shape), _full_spec(b.shape)],
        out_specs=pl.BlockSpec((tm, d), lambda i: (i, 0)),
        out_shape=jax.ShapeDtypeStruct((n_rows, d), BF16),
        scratch_shapes=[pltpu.VMEM((HALO + tm, d), F32), pltpu.VMEM((SUBLANES - 1, HALO + tm - SUBLANES, d), F32),
                        pltpu.VMEM((tm, d), F32)],
        compiler_params=_cparams(1),
        name="conv_dw",
    )(u, halo, wdw, bdw, g, b)


def _dims_keys(cache):
    nd = cache.ndim
    t = jnp.transpose(cache, (0, 1) + tuple(range(3, nd)) + (2,))
    return t.reshape(cache.shape[0], cache.shape[1], D_MODEL, cache.shape[2])


def kernel(x_prompt, x_sample, cache_diff_k, cache_diff_v, state_conv, cache_fox_k, cache_fox_v, cache_fox_logf, rel_bias, norm_g, final_g, diff_w_in, diff_w_out, diff_lq1, diff_lk1, diff_lq2, diff_lk2, diff_subln_g, conv_w_pw1, conv_b_pw1, conv_w_dw, conv_b_dw, conv_ln_g, conv_ln_b, conv_w_pw2, conv_b_pw2, fox_w_in, fox_b_f, fox_w_out, mlp_w1, mlp_w2):
    B, S, D = x_prompt.shape
    Bd, T, _ = x_sample.shape
    P = cache_diff_k.shape[2]
    depth = norm_g.shape[0]
    n_p, n_s = B * S, Bd * T
    assert D == D_MODEL and S % (2 * ATT_TQ) == 0 and P % SAMPLE_TK == 0 and n_p % ROW_TILE == 0
    assert n_s % ROW_TILE == 0 and T % 8 == 0 and T >= CONV_WIDTH - 1 and S % ROW_TILE == 0

    x = jnp.concatenate([x_prompt.reshape(n_p, D), x_sample.reshape(n_s, D)], axis=0)
    zero_bias = jnp.zeros((1, D), F32)
    final_gain = final_g.reshape(1, D)

    bt_diff, bt_fox = _prompt_tiles(rel_bias, ATT_TQ)
    sb_last, sb_new, sf_new = _sample_tiles(rel_bias, T, P, SAMPLE_TK)

    diff_kt = _dims_keys(cache_diff_k)
    diff_v = cache_diff_v.reshape(cache_diff_v.shape[0], Bd, P * DIFF_HEADS, 2 * DIFF_HEAD_DIM)
    fox_kt = _dims_keys(cache_fox_k)
    fox_vt = _dims_keys(cache_fox_v)

    n_diff, n_fox = diff_w_in.shape[0], fox_w_in.shape[0]
    dk_shape = (1, n_p, D)
    dv_shape = (n_diff, n_p, DIFF_HEADS, 2 * DIFF_HEAD_DIM)
    fkv_shape = (n_fox, n_p, FOX_HEADS, FOX_HEAD_DIM)
    dv_new, fk_new, fv_new = None, None, None
    dk_p, dk_s, dv_s = [], [], []
    cv_p, cv_s = [], []
    fl_p, fk_s, fv_s, fl_s = [], [], [], []
    y = None

    for i in range(depth):
        kind, j = i % N_MIXERS, i // N_MIXERS
        g1 = norm_g[i, 0].reshape(1, D)
        g2 = norm_g[i, 1].reshape(1, D)
        if kind == 0:
            lam_init = 0.8 - 0.6 * math.exp(-0.3 * i)
            q, k_p, k_s, dv_new, v_s, kb, vb = _qkv_inproj(
                x, n_p, g1, diff_w_in[j].astype(BF16), DIFF_SCALE, (dk_shape, 0), (dv_shape, j),
                prev=None if dv_new is None else {3: dv_new})
            dk_p.append(k_p.reshape(B, S, DIFF_HEADS, 2, DIFF_HEAD_DIM))
            lqk = jnp.stack([diff_lq1[j], diff_lk1[j], diff_lq2[j], diff_lk2[j]]).astype(F32)
            sg = diff_subln_g[j].reshape(1, 2 * DIFF_HEAD_DIM).astype(F32)
            o_p = _prompt_attn(False, B, S, q, kb, vb, bt_diff, (lqk, sg.reshape(2 * DIFF_HEAD_DIM, 1)), lam_init)
            o_s = _sample_attn(False, j, n_p, Bd, T, P, q, diff_kt, diff_v, kb, vb, sb_last, sb_new, (lqk, sg),
                               lam_init)
            wo, bo = diff_w_out[j].astype(BF16), zero_bias
            dk_s.append(k_s.reshape(Bd, T, DIFF_HEADS, 2, DIFF_HEAD_DIM))
            dv_s.append(v_s.reshape(Bd, T, DIFF_HEADS, 2 * DIFF_HEAD_DIM))
        elif kind == 1:
            u = _conv_inproj(x, g1, conv_w_pw1[j].astype(BF16), conv_b_pw1[j].reshape(1, 2 * D))
            wdw = jnp.pad(conv_w_dw[j], ((0, HALO - CONV_WIDTH), (0, 0)))
            args = (wdw, conv_b_dw[j].reshape(1, D), conv_ln_g[j].reshape(1, D), conv_ln_b[j].reshape(1, D))
            per_tile = ROW_TILE // HALO
            o_p = _conv_dw(u, 0, n_p, ROW_TILE, u, lambda t: jnp.maximum(t * per_tile - 1, 0), S // ROW_TILE, *args)
            state = jnp.pad(state_conv[j], ((0, 0), (HALO - (CONV_WIDTH - 1), 0), (0, 0))).reshape(Bd * HALO, D)
            o_s = _conv_dw(u, n_p, n_s, T, state, lambda t: t, 0, *args)
            wo, bo = conv_w_pw2[j].astype(BF16), conv_b_pw2[j].reshape(1, D)
            keep = CONV_WIDTH - 1
            cv_p.append(jnp.stack([u[(b + 1) * S - keep:(b + 1) * S] for b in range(B)]))
            cv_s.append(u[n_p:].reshape(Bd, T, D)[:, T - keep:])
        else:
            w_in = fox_w_in[j]
            pad_heads = lambda a: jnp.pad(a, [(0, 0)] * (a.ndim - 1) + [(0, LANES - FOX_HEADS)])
            wf = jnp.concatenate([_spread_heads(w_in[:, 3 * D:]), pad_heads(w_in[:, 3 * D:])], axis=1).astype(BF16)
            b_f = fox_b_f[j].astype(F32).reshape(1, FOX_HEADS)
            bf = jnp.concatenate([_spread_heads(b_f), pad_heads(b_f)], axis=1)
            q, fk_new, k_s, fv_new, v_s, kb, vb, lf, lf_heads = _qkv_inproj(
                x, n_p, g1, w_in[:, :3 * D].astype(BF16), FOX_SCALE, (fkv_shape, j), (fkv_shape, j),
                prev=None if fk_new is None else {1: fk_new, 3: fv_new}, forget=(wf, bf))
            lf_p_cols = lf[:n_p].reshape(B, S, LANES)
            lf_s_cols = lf[n_p:].reshape(Bd, T, LANES)
            lf_p = lf_heads[:n_p].reshape(B, S, FOX_HEADS)
            lf_s = lf_heads[n_p:].reshape(Bd, T, FOX_HEADS)
            qa_p, ka_p = _decay_cols(lf_p_cols, False)
            cache_cols = _spread_heads(cache_fox_logf[j].astype(F32))
            decay_s = _decay_cols(jnp.concatenate([cache_cols, lf_s_cols], axis=1), True)
            o_p = _prompt_attn(True, B, S, q, kb, vb, bt_fox, (qa_p.reshape(n_p, LANES), ka_p.reshape(n_p, LANES)))
            o_s = _sample_attn(True, j, n_p, Bd, T, P, q, fox_kt, fox_vt, kb, vb, None, sf_new, decay_s)
            wo, bo = fox_w_out[j].astype(BF16), zero_bias
            fl_p.append(lf_p)
            fk_s.append(k_s.reshape(Bd, T, FOX_HEADS, FOX_HEAD_DIM))
            fv_s.append(v_s.reshape(Bd, T, FOX_HEADS, FOX_HEAD_DIM))
            fl_s.append(lf_s)
        final = i == depth - 1
        out = _mix_mlp(x, o_p, o_s, wo, bo, g2, mlp_w1[i].astype(BF16), mlp_w2[i].astype(BF16), final_gain, final)
        if final:
            y = out
        else:
            x = out[0]

    y_prompt = y[0].reshape(B, S, D)
    y_sample = y[1].reshape(Bd, T, D)
    new_diff_k_p = jnp.stack(dk_p)
    new_diff_v_p = dv_new.reshape(n_diff, B, S, DIFF_HEADS, 2 * DIFF_HEAD_DIM)
    new_fox_k_p = fk_new.reshape(n_fox, B, S, FOX_HEADS, FOX_HEAD_DIM)
    new_fox_v_p = fv_new.reshape(n_fox, B, S, FOX_HEADS, FOX_HEAD_DIM)
    return (y_prompt, y_sample, new_diff_k_p, new_diff_v_p, jnp.stack(cv_p), new_fox_k_p,
            new_fox_v_p, jnp.stack(fl_p), jnp.stack(dk_s), jnp.stack(dv_s), jnp.stack(cv_s),
            jnp.stack(fk_s), jnp.stack(fv_s), jnp.stack(fl_s))
```

```python
import functools
import math

import numpy as np
import jax
import jax.numpy as jnp
from jax import lax
from jax.experimental import pallas as pl
from jax.experimental.pallas import tpu as pltpu

F32 = jnp.float32
BF16 = jnp.bfloat16

D_MODEL = 1024
CHUNK = 64
EPS = 1e-6
DIFF_HEADS = 8
DIFF_HEAD_DIM = 64
LOG2E = math.log2(math.e)
DIFF_SCALE = DIFF_HEAD_DIM ** -0.5
CONV_WIDTH = 31
FOX_HEADS = 16
FOX_HEAD_DIM = 64
FOX_SCALE = FOX_HEAD_DIM ** -0.5
REL_BUCKETS = 32
N_MIXERS = 3

LANES = 128
SUBLANES = 8
NEG = -1e30
ROW_TILE = 512
ATT_TQ = 512
ATT_STRIP = 512
ATT_GROUPS = 2
REDUCE_ROWS = 64
SAMPLE_TK = 2048
SAMPLE_GROUP = 256
FF_CHUNK = 1024
KEYS_TS = 256
CUM_BLOCK = 256
DECAY_COLS = 6
HALO = 32
CONV_RC = 32
CONV_LC = 256
VMEM_LIMIT = 56 * 1024 * 1024

_T5_STEPS = (12, 16, 23, 32, 46, 64, 91)
T5_NEAR = 128
assert _T5_STEPS[-1] <= T5_NEAR


def _cparams(n_axes):
    return pltpu.CompilerParams(dimension_semantics=("arbitrary",) * n_axes, vmem_limit_bytes=VMEM_LIMIT)


def _rms(x, g):
    return x * lax.rsqrt(jnp.mean(x * x, axis=-1, keepdims=True) + EPS) * g


def _full_spec(shape):
    nd = len(shape)
    return pl.BlockSpec(shape, lambda *_: (0,) * nd)


def _split_specs(n_prompt_tiles, tm, d):
    prompt = pl.BlockSpec((tm, d), lambda i: (jnp.minimum(i, n_prompt_tiles - 1), 0))
    sample = pl.BlockSpec((tm, d), lambda i: (jnp.maximum(i - n_prompt_tiles, 0), 0))
    return prompt, sample


def _split_store(n_prompt_tiles, value, prompt_ref, sample_ref):
    i = pl.program_id(0)

    @pl.when(i < n_prompt_tiles)
    def _():
        prompt_ref[...] = value

    @pl.when(i >= n_prompt_tiles)
    def _():
        sample_ref[...] = value


def _store_prompt_cache(value, prompt_ref, layer):
    layer = layer if prompt_ref.shape[0] > 1 else 0
    for other in range(prompt_ref.shape[0]):
        if other != layer:
            prompt_ref[other] = jnp.zeros(prompt_ref.shape[1:], prompt_ref.dtype)
    if len(prompt_ref.shape) == 3:
        prompt_ref[layer] = value
    else:
        prompt_ref[layer] = pltpu.einshape("m(hd)->mhd", value, h=prompt_ref.shape[2])


def _qkv_inproj_kernel(*refs, n_pt, scale, n_prev, forget, layers):
    x_ref, g_ref, w_ref = refs[:3]
    refs = refs[3:]
    if forget:
        wf_ref, bf_ref = refs[:2]
        refs = refs[2:]
    refs = refs[n_prev:]
    q_ref, kp_ref, ks_ref, vp_ref, vs_ref, kb_ref, vb_ref = refs[:7]
    d = D_MODEL

    def project(prompt):
        xn = _rms(x_ref[...], g_ref[...]).astype(BF16)
        q = jnp.dot(xn, w_ref[:, 0:d], preferred_element_type=F32)
        q_ref[...] = (q * (scale * LOG2E)).astype(BF16)
        for c, new_p, new_s, bf_ref_out in ((1, kp_ref, ks_ref, kb_ref), (2, vp_ref, vs_ref, vb_ref)):
            val = jnp.dot(xn, w_ref[:, c * d:(c + 1) * d], preferred_element_type=F32)
            if prompt:
                _store_prompt_cache(val, new_p, layers[c - 1])
            else:
                new_s[...] = val
            bf_ref_out[...] = val.astype(BF16)
        if forget:
            z = jnp.dot(xn, wf_ref[...], preferred_element_type=F32) + bf_ref[...]
            lf = jnp.minimum(z, 0.0) - jnp.log1p(jnp.exp(-jnp.abs(z)))
            refs[7][...] = lf[:, 0:LANES]
            refs[8][...] = lf[:, LANES:LANES + FOX_HEADS]

    pl.when(pl.program_id(0) < n_pt)(lambda: project(True))
    pl.when(pl.program_id(0) >= n_pt)(lambda: project(False))


def _qkv_inproj(x, n_p, g, w, scale, k_slab, v_slab, prev=None, forget=()):
    n, d = x.shape
    tm = ROW_TILE
    n_pt = n_p // tm
    row = pl.BlockSpec((tm, d), lambda i: (i, 0))
    _, sample = _split_specs(n_pt, tm, d)
    prev = prev or {}

    def slab(out_index, shape, layer):
        rest = tuple(shape[2:])
        zeros = (0,) * len(rest)
        if out_index in prev:
            return pl.BlockSpec((1, tm) + rest, lambda i: (layer, jnp.minimum(i, n_pt - 1)) + zeros)
        return pl.BlockSpec((shape[0], tm) + rest, lambda i: (0, jnp.minimum(i, n_pt - 1)) + zeros)

    f32_s = jax.ShapeDtypeStruct((n - n_p, d), F32)
    bf_all = jax.ShapeDtypeStruct((n, d), BF16)
    ins = [x, g, w] + list(forget) + list(prev.values())
    w_spec = pl.BlockSpec(w.shape, lambda i: (0, 0), pipeline_mode=pl.Buffered(1))
    in_specs = ([row, _full_spec((1, d)), w_spec] + [_full_spec(a.shape) for a in forget]
                + [pl.BlockSpec(memory_space=pl.ANY)] * len(prev))
    out_specs = [row, slab(1, *k_slab), sample, slab(3, *v_slab), sample, row, row]
    out_shape = [bf_all, jax.ShapeDtypeStruct(k_slab[0], F32), f32_s, jax.ShapeDtypeStruct(v_slab[0], F32), f32_s,
                 bf_all, bf_all]
    if forget:
        out_specs += [pl.BlockSpec((tm, LANES), lambda i: (i, 0)), pl.BlockSpec((tm, FOX_HEADS), lambda i: (i, 0))]
        out_shape += [jax.ShapeDtypeStruct((n, LANES), F32), jax.ShapeDtypeStruct((n, FOX_HEADS), F32)]
    first_prev = 3 + len(forget)
    aliases = {first_prev + pos: out_index for pos, out_index in enumerate(prev)}
    return pl.pallas_call(
        functools.partial(_qkv_inproj_kernel, n_pt=n_pt, scale=scale, n_prev=len(prev), forget=bool(forget),
                          layers=(k_slab[1], v_slab[1])),
        grid=(n // tm,),
        in_specs=in_specs,
        out_specs=out_specs,
        out_shape=out_shape,
        input_output_aliases=aliases,
        compiler_params=_cparams(1),
        name="fox_inproj" if forget else "diff_inproj",
    )(*ins)


def _conv_inproj_kernel(x_ref, g_ref, w_ref, b_ref, u_ref):
    d = D_MODEL
    xn = _rms(x_ref[...], g_ref[...]).astype(BF16)
    a = jnp.dot(xn, w_ref[:, 0:d], preferred_element_type=F32) + b_ref[:, 0:d]
    gate = jnp.dot(xn, w_ref[:, d:2 * d], preferred_element_type=F32) + b_ref[:, d:2 * d]
    u_ref[...] = a * (1.0 / (1.0 + jnp.exp(-gate)))


def _conv_inproj(x, g, w, b):
    n, d = x.shape
    tm = ROW_TILE
    row = pl.BlockSpec((tm, d), lambda i: (i, 0))
    return pl.pallas_call(
        _conv_inproj_kernel,
        grid=(n // tm,),
        in_specs=[row, _full_spec((1, d)), _full_spec(w.shape), _full_spec(b.shape)],
        out_specs=row,
        out_shape=jax.ShapeDtypeStruct((n, d), F32),
        compiler_params=_cparams(1),
        name="conv_inproj",
    )(x, g, w, b)


def _mix_mlp_kernel(x_ref, op_ref, os_ref, wo_ref, bo_ref, g_ref, w1_ref, w2_ref, gf_ref, *out_refs, n_pt, final):
    o = jnp.where(pl.program_id(0) < n_pt, op_ref[...], os_ref[...])
    x1 = x_ref[...] + jnp.dot(o, wo_ref[...], preferred_element_type=F32) + bo_ref[...]
    xn = _rms(x1, g_ref[...]).astype(BF16)
    acc = x1
    d_ff = w1_ref.shape[1]
    for c in range(d_ff // FF_CHUNK):
        h = jnp.dot(xn, w1_ref[:, c * FF_CHUNK:(c + 1) * FF_CHUNK], preferred_element_type=F32)
        h = jnp.square(jnp.maximum(h, 0.0)).astype(BF16)
        acc = acc + jnp.dot(h, w2_ref[c * FF_CHUNK:(c + 1) * FF_CHUNK, :], preferred_element_type=F32)
    if final:
        _split_store(n_pt, _rms(acc, gf_ref[...]), *out_refs)
    else:
        out_refs[0][...] = acc


def _mix_mlp(x, o_p, o_s, wo, bo, g, w1, w2, gf, final):
    n, d = x.shape
    tm = ROW_TILE
    n_p = o_p.shape[0]
    n_pt = n_p // tm
    row = pl.BlockSpec((tm, d), lambda i: (i, 0))
    prompt, sample = _split_specs(n_pt, tm, d)
    resident = lambda a: pl.BlockSpec(a.shape, lambda i: (0,) * a.ndim, pipeline_mode=pl.Buffered(1))
    if final:
        out_specs = [prompt, sample]
        out_shape = [jax.ShapeDtypeStruct((n_p, d), F32), jax.ShapeDtypeStruct((n - n_p, d), F32)]
    else:
        out_specs = [row]
        out_shape = [jax.ShapeDtypeStruct((n, d), F32)]
    return pl.pallas_call(
        functools.partial(_mix_mlp_kernel, n_pt=n_pt, final=final),
        grid=(n // tm,),
        in_specs=[row, prompt, sample, resident(wo), _full_spec(bo.shape), _full_spec(g.shape), resident(w1),
                  resident(w2), _full_spec(gf.shape)],
        out_specs=out_specs,
        out_shape=out_shape,
        compiler_params=_cparams(1),
        name="mix_mlp",
    )(x, o_p, o_s, wo, bo, g, w1, w2, gf)


def _t5_bias(tab_ref, h, qpos, kpos):
    half = REL_BUCKETS // 2
    n = qpos - kpos
    off = jnp.where(n < 0, half, 0)
    n = jnp.abs(n)
    large = jnp.full(n.shape, half // 2, jnp.int32)
    for t in _T5_STEPS:
        large = large + jnp.where(n >= t, 1, 0)
    bucket = off + jnp.where(n < half // 2, n, large)
    far = tab_ref[half - 1, h]
    out = jnp.zeros(n.shape, F32)
    for b in range(REL_BUCKETS):
        out = jnp.where(bucket == b, (tab_ref[b, h] - far) * LOG2E, out)
    visible = (kpos // CHUNK) <= (qpos // CHUNK)
    return jnp.where(visible, out, NEG)


def _prompt_tiles_kernel(tab_ref, bt_ref, cm_ref, *, tq):
    h = pl.program_id(0)
    for v in range(2):
        kpos = lax.broadcasted_iota(jnp.int32, (tq, tq), 0)
        qpos = lax.broadcasted_iota(jnp.int32, (tq, tq), 1) + (1 - v) * tq
        bt_ref[0, v] = _t5_bias(tab_ref, h, qpos, kpos)

    @pl.when(h == 0)
    def _():
        kpos = lax.broadcasted_iota(jnp.int32, (tq, tq), 0)
        qpos = lax.broadcasted_iota(jnp.int32, (tq, tq), 1)
        cm_ref[0, 0] = jnp.zeros((tq, tq), F32)
        cm_ref[0, 1] = jnp.where(kpos <= qpos, 0.0, NEG)


def _prompt_tiles(table, tq):
    return pl.pallas_call(
        functools.partial(_prompt_tiles_kernel, tq=tq),
        grid=(DIFF_HEADS,),
        in_specs=[pl.BlockSpec(memory_space=pltpu.SMEM)],
        out_specs=[pl.BlockSpec((1, 2, tq, tq), lambda h: (h, 0, 0, 0)),
                   pl.BlockSpec((1, 2, tq, tq), lambda h: (0, 0, 0, 0))],
        out_shape=[jax.ShapeDtypeStruct((DIFF_HEADS, 2, tq, tq), F32), jax.ShapeDtypeStruct((1, 2, tq, tq), F32)],
        compiler_params=_cparams(1),
        name="prompt_score_tiles",
    )(table)


def _sample_tiles_kernel(tab_ref, dl_ref, dn_ref, fn_ref, *, t_new, past, tk):
    rows = DIFF_HEADS * 2 * t_new
    for h in range(DIFF_HEADS):
        r0 = h * 2 * t_new
        q = lax.broadcasted_iota(jnp.int32, (2 * t_new, tk), 0) % t_new + past
        k = lax.broadcasted_iota(jnp.int32, (2 * t_new, tk), 1) + (past - tk)
        dl_ref[r0:r0 + 2 * t_new, :] = _t5_bias(tab_ref, h, q, k)
        q = lax.broadcasted_iota(jnp.int32, (2 * t_new, t_new), 0) % t_new + past
        k = lax.broadcasted_iota(jnp.int32, (2 * t_new, t_new), 1) + past
        dn_ref[r0:r0 + 2 * t_new, :] = _t5_bias(tab_ref, h, q, k)
    q = lax.broadcasted_iota(jnp.int32, (rows, t_new), 0) % t_new
    k = lax.broadcasted_iota(jnp.int32, (rows, t_new), 1)
    fn_ref[...] = jnp.where(k <= q, 0.0, NEG)


def _sample_tiles(table, t_new, past, tk):
    rows = DIFF_HEADS * 2 * t_new
    assert rows == FOX_HEADS * t_new
    return pl.pallas_call(
        functools.partial(_sample_tiles_kernel, t_new=t_new, past=past, tk=tk),
        in_specs=[pl.BlockSpec(memory_space=pltpu.SMEM)],
        out_shape=[jax.ShapeDtypeStruct((rows, tk), F32), jax.ShapeDtypeStruct((rows, t_new), F32),
                   jax.ShapeDtypeStruct((rows, t_new), F32)],
        name="sample_score_tiles",
    )(table)


def _lambda(lqk_ref, lam_init):
    a = jnp.sum(lqk_ref[0:1, :] * lqk_ref[1:2, :], axis=-1, keepdims=True)
    b = jnp.sum(lqk_ref[2:3, :] * lqk_ref[3:4, :], axis=-1, keepdims=True)
    return jnp.exp(a) - jnp.exp(b) + lam_init


def _scores_t(k, q2_ref, s_out):
    for c0 in range(0, q2_ref.shape[0], ATT_STRIP):
        cols = slice(c0, c0 + ATT_STRIP)
        s_out[:, cols] = lax.dot_general(k, q2_ref[cols, :], (((1,), (1,)), ((), ())), preferred_element_type=F32)


def _reduce_rows(op, x):
    w, n = x.shape
    part = op(x.reshape(w // REDUCE_ROWS, REDUCE_ROWS, n), axis=0)
    return op(part, axis=0, keepdims=True)


def _softmax_pv_t(s_in, v, bias_t, bias_rows, m_ref, l_ref, acc_ref):
    w = s_in.shape[0]
    for c0 in range(0, s_in.shape[1], ATT_STRIP):
        cols = slice(c0, c0 + ATT_STRIP)
        s = s_in[:, cols]
        if bias_t is not None:
            b0 = c0 % bias_t.shape[1]
            r0 = w - bias_rows
            biased = s[r0:, :] + bias_t[r0:, b0:b0 + ATT_STRIP]
            s = biased if r0 == 0 else jnp.concatenate([s[:r0, :], biased], axis=0)
        m_prev = m_ref[:, cols]
        m_new = jnp.maximum(m_prev, _reduce_rows(jnp.max, s))
        alpha = jnp.exp2(m_prev - m_new)
        p = jnp.exp2(s - m_new)
        l_ref[:, cols] = alpha * l_ref[:, cols] + _reduce_rows(jnp.sum, p)
        pv = lax.dot_general(v, p.astype(BF16), (((0,), (0,)), ((), ())), preferred_element_type=F32)
        acc_ref[:, cols] = alpha * acc_ref[:, cols] + pv
        m_ref[:, cols] = m_new


def _prompt_attn_kernel(*refs, fox, tq, seq, lam_init):
    if fox:
        (q_ref, qa_ref, k_ref, ka_ref, v_ref, bt_ref, o_ref, q2_ref, s0_ref, s1_ref, m_ref, l_ref, acc_ref) = refs
    else:
        (q_ref, k_ref, v_ref, bt_ref, lqk_ref, sg_ref, o_ref, q2_ref, s0_ref, s1_ref, m_ref, l_ref, acc_ref) = refs
    s_refs = (s0_ref, s1_ref)
    n_groups = q_ref.shape[1] // LANES
    lane = lax.broadcasted_iota(jnp.int32, (tq, LANES), 1)
    lo = lane < LANES // 2
    first_rows = lax.broadcasted_iota(jnp.int32, (LANES, tq), 0) < LANES // 2
    lanes = [slice(g * LANES, (g + 1) * LANES) for g in range(n_groups)]
    if fox:
        in_a, in_b = [], []
        for g in range(n_groups):
            c0 = 12 * (pl.program_id(1) * n_groups + g)
            in_a.append(jnp.logical_and(lane >= c0, lane < c0 + 6))
            in_b.append(jnp.logical_and(lane >= c0 + 6, lane < c0 + 12))

    def scores_g(g, j, slot):
        k0 = pl.multiple_of(jnp.maximum(j, 0) * tq, tq)
        k = k_ref[pl.ds(k0, tq), lanes[g]]
        if fox:
            k = jnp.concatenate([k, ka_ref[pl.ds(k0, tq), :]], axis=1)
        _scores_t(k, q2_ref.at[g], s_refs[slot].at[g])

    def softmax_pv_g(g, j, slot, bias_index):
        k0 = pl.multiple_of(j * tq, tq)
        if bias_index is None or (fox and bias_index == 0):
            bias_t, bias_rows = None, 0
        else:
            bias_t = bt_ref.at[0 if fox else g, bias_index]
            bias_rows = tq if bias_index == 1 else T5_NEAR
        _softmax_pv_t(s_refs[slot].at[g], v_ref[pl.ds(k0, tq), lanes[g]], bias_t, bias_rows, m_ref.at[g],
                      l_ref.at[g], acc_ref.at[g])

    def scores(j, slot):
        for g in range(n_groups):
            scores_g(g, j, slot)

    def softmax_pv(j, slot, bias_index):
        for g in range(n_groups):
            softmax_pv_g(g, j, slot, bias_index)

    def step(j_next, slot_next, j, slot, bias_index):
        for g in range(n_groups):
            scores_g(g, j_next, slot_next)
            softmax_pv_g(g, j, slot, bias_index)

    def q_block(i, carry):
        r0 = pl.multiple_of(i * tq, tq)
        for g in range(n_groups):
            q = q_ref[pl.ds(r0, tq), lanes[g]]
            zero = jnp.zeros_like(q)
            q2_ref[g, 0:tq, 0:LANES] = jnp.where(lo, q, zero)
            q2_ref[g, tq:2 * tq, 0:LANES] = jnp.where(lo, zero, q)
            if fox:
                qa = qa_ref[pl.ds(r0, tq), :]
                q2_ref[g, 0:tq, LANES:2 * LANES] = jnp.where(in_a[g], qa, zero)
                q2_ref[g, tq:2 * tq, LANES:2 * LANES] = jnp.where(in_b[g], qa, zero)
        m_ref[...] = jnp.full(m_ref.shape, NEG, F32)
        l_ref[...] = jnp.zeros(l_ref.shape, F32)
        acc_ref[...] = jnp.zeros(acc_ref.shape, F32)

        @pl.when(i == 0)
        def _():
            scores(i, 0)
            softmax_pv(i, 0, 1)

        @pl.when(i >= 1)
        def _():
            scores(i, 0)
            step(i - 1, 1, i, 0, 1)
            step(i - 2, 0, i - 1, 1, 0)
            n_far = i - 1

            def pair(u, c):
                a = i - 2 - 2 * u
                step(a - 1, 1, a, 0, None)
                step(a - 2, 0, a - 1, 1, None)
                return c

            lax.fori_loop(0, n_far // 2, pair, 0)

            @pl.when(n_far % 2 == 1)
            def _():
                softmax_pv(0, 0, None)

        for g in range(n_groups):
            top = acc_ref[g, :, 0:tq] / l_ref[g, :, 0:tq]
            bot = acc_ref[g, :, tq:2 * tq] / l_ref[g, :, tq:2 * tq]
            if fox:
                o_t = jnp.where(first_rows, top, bot)
            else:
                o_t = top - _lambda(lqk_ref, lam_init) * bot
                ms = jnp.mean(o_t * o_t, axis=0, keepdims=True)
                o_t = o_t * lax.rsqrt(ms + EPS) * sg_ref[...] * (1.0 - lam_init)
            o_ref[pl.ds(r0, tq), lanes[g]] = o_t.T.astype(BF16)
        return carry

    lax.fori_loop(0, seq // tq, q_block, 0)


def _prompt_attn(fox, batch, seq, q, k, v, bt, extra, lam_init=0.0):
    tq = ATT_TQ
    ng = ATT_GROUPS
    steps = D_MODEL // (LANES * ng)
    blk = pl.BlockSpec((seq, ng * LANES), lambda b, h: (b, h))
    if fox:
        qa, ka = extra
        aug = pl.BlockSpec((seq, LANES), lambda b, h: (b, 0))
        ins = [q, qa, k, ka, v, bt]
        in_specs = [blk, aug, blk, aug, blk, pl.BlockSpec((1, 2, tq, tq), lambda b, h: (0, 0, 0, 0))]
        kd = 2 * LANES
    else:
        lqk, sg = extra
        ins = [q, k, v, bt, lqk, sg]
        in_specs = [blk, blk, blk, pl.BlockSpec((ng, 2, tq, tq), lambda b, h: (h, 0, 0, 0)),
                    _full_spec(lqk.shape), _full_spec(sg.shape)]
        kd = LANES
    return pl.pallas_call(
        functools.partial(_prompt_attn_kernel, fox=fox, tq=tq, seq=seq, lam_init=lam_init),
        grid=(batch, steps),
        in_specs=in_specs,
        out_specs=blk,
        out_shape=jax.ShapeDtypeStruct((batch * seq, D_MODEL), BF16),
        scratch_shapes=[pltpu.VMEM((ng, 2 * tq, kd), BF16), pltpu.VMEM((ng, tq, 2 * tq), F32),
                        pltpu.VMEM((ng, tq, 2 * tq), F32), pltpu.VMEM((ng, 1, 2 * tq), F32),
                        pltpu.VMEM((ng, 1, 2 * tq), F32), pltpu.VMEM((ng, LANES, 2 * tq), F32)],
        compiler_params=_cparams(2),
        name="fox_prompt_attn" if fox else "diff_prompt_attn",
    )(*ins)


def _online_update(s, pv_fn, m_ref, l_ref, acc_ref):
    m_prev = m_ref[...]
    m_new = jnp.maximum(m_prev, jnp.max(s, axis=-1, keepdims=True))
    alpha = jnp.exp2(m_prev - m_new)
    p = jnp.exp2(s - m_new)
    l_ref[...] = alpha * l_ref[...] + jnp.sum(p, axis=-1, keepdims=True)
    acc_ref[...] = alpha * acc_ref[...] + pv_fn(p.astype(BF16))
    m_ref[...] = m_new


def _sample_attn_kernel(*refs, fox, t_new, tk, n_tiles, lam_init):
    if fox:
        (q_ref, qa_ref, kt_ref, kat_ref, vt_ref, kn_ref, kan_ref, vn_ref, bn_ref,
         o_ref, qbd_ref, m_ref, l_ref, acc_ref) = refs
    else:
        (q_ref, kt_ref, v_ref, kn_ref, vn_ref, bl_ref, bn_ref, lqk_ref, sg_ref,
         o_ref, qbd_ref, m_ref, l_ref, acc_ref) = refs
    t = pl.program_id(1)
    n_blocks = D_MODEL // 64
    rows_per_head = 2 * t_new

    @pl.when(t == 0)
    def _():
        q = q_ref[...]
        zero = jnp.zeros_like(q)
        cb = lax.broadcasted_iota(jnp.int32, q.shape, 1) // 64
        if fox:
            qa = qa_ref[0]
            la = lax.broadcasted_iota(jnp.int32, qa.shape, 1)
        for rb in range(n_blocks):
            qbd_ref[rb * t_new:(rb + 1) * t_new, 0:D_MODEL] = jnp.where(cb == rb, q, zero)
            if fox:
                sel = jnp.logical_and(la >= 6 * rb, la < 6 * rb + 6)
                qbd_ref[rb * t_new:(rb + 1) * t_new, D_MODEL:D_MODEL + LANES] = jnp.where(sel, qa, jnp.zeros_like(qa))
        m_ref[...] = jnp.full(m_ref.shape, NEG, F32)
        l_ref[...] = jnp.zeros(l_ref.shape, F32)
        acc_ref[...] = jnp.zeros(acc_ref.shape, F32)

    n_groups = D_MODEL // SAMPLE_GROUP
    group_rows = (SAMPLE_GROUP // 64) * t_new
    groups = [(slice(g * group_rows, (g + 1) * group_rows), slice(g * SAMPLE_GROUP, (g + 1) * SAMPLE_GROUP))
              for g in range(n_groups)]

    def cache_scores():
        s = jnp.concatenate([jnp.dot(qbd_ref[rows, dims], kt_ref[0, 0, dims, :].astype(BF16),
                                     preferred_element_type=F32) for rows, dims in groups], axis=0)
        if fox:
            s = s + jnp.dot(qbd_ref[:, D_MODEL:D_MODEL + LANES], kat_ref[0], preferred_element_type=F32)
        return s

    if fox:
        def cache_pv(p):
            return jnp.concatenate(
                [lax.dot_general(p[rows, :], vt_ref[0, 0, dims, :].astype(BF16), (((1,), (1,)), ((), ())),
                                 preferred_element_type=F32) for rows, dims in groups], axis=0)

        def new_pv(p):
            return jnp.concatenate([jnp.dot(p[rows, :], vn_ref[:, dims], preferred_element_type=F32)
                                    for rows, dims in groups], axis=0)
    else:
        def cache_pv(p):
            return jnp.concatenate(
                [jnp.dot(p[h * rows_per_head:(h + 1) * rows_per_head, :],
                         v_ref[0, 0, pl.ds(h, tk, stride=DIFF_HEADS), :].astype(BF16), preferred_element_type=F32)
                 for h in range(DIFF_HEADS)], axis=0)

        def new_pv(p):
            return jnp.concatenate(
                [jnp.dot(p[h * rows_per_head:(h + 1) * rows_per_head, :], vn_ref[:, h * LANES:(h + 1) * LANES],
                         preferred_element_type=F32) for h in range(DIFF_HEADS)], axis=0)

    @pl.when(t < n_tiles - 1)
    def _():
        _online_update(cache_scores(), cache_pv, m_ref, l_ref, acc_ref)

    @pl.when(t == n_tiles - 1)
    def _():
        s = cache_scores()
        if not fox:
            s = s + bl_ref[...]
        _online_update(s, cache_pv, m_ref, l_ref, acc_ref)
        kn = kn_ref[...]
        if fox:
            kn = jnp.concatenate([kn, kan_ref[0]], axis=1)
        s = lax.dot_general(qbd_ref[...], kn, (((1,), (1,)), ((), ())), preferred_element_type=F32) + bn_ref[...]
        _online_update(s, new_pv, m_ref, l_ref, acc_ref)

        lane = lax.broadcasted_iota(jnp.int32, (t_new, LANES), 1)
        lo = lane < LANES // 2
        for g in range(D_MODEL // LANES):
            ra, rb = 2 * g * t_new, (2 * g + 1) * t_new
            cols = slice(g * LANES, (g + 1) * LANES)
            c0 = (g * LANES) % SAMPLE_GROUP if fox else 0
            acc_cols = slice(c0, c0 + LANES)
            top = acc_ref[ra:ra + t_new, acc_cols] / l_ref[ra:ra + t_new, :]
            bot = acc_ref[rb:rb + t_new, acc_cols] / l_ref[rb:rb + t_new, :]
            if fox:
                o = jnp.where(lo, top, bot)
            else:
                o = _rms(top - _lambda(lqk_ref, lam_init) * bot, sg_ref[...]) * (1.0 - lam_init)
            o_ref[:, cols] = o.astype(BF16)


def _sample_attn(fox, layer, n_prompt, dec_batch, t_new, past, q, kt, v, kn, vn, bias_last, bias_new, extra,
                 lam_init=0.0):
    tk = SAMPLE_TK
    n_tiles = past // tk
    d = D_MODEL
    rows = (d // 64) * t_new
    s0 = n_prompt // t_new
    new = pl.BlockSpec((t_new, d), lambda b, t: (s0 + b, 0))
    dims_keys = pl.BlockSpec((1, 1, d, tk), lambda b, t: (layer, b, 0, t))
    out = pl.BlockSpec((t_new, d), lambda b, t: (b, 0))
    if fox:
        qa, ka, kat = extra
        new_aug = pl.BlockSpec((1, t_new, LANES), lambda b, t: (b, past // t_new, 0))
        ins = [q, qa, kt, kat, v, kn, ka, vn, bias_new]
        in_specs = [new, new_aug, dims_keys, pl.BlockSpec((1, LANES, tk), lambda b, t: (b, 0, t)), dims_keys,
                    new, new_aug, new, _full_spec(bias_new.shape)]
        kd, acc_w = d + LANES, SAMPLE_GROUP
    else:
        lqk, sg = extra
        ins = [q, kt, v, kn, vn, bias_last, bias_new, lqk, sg]
        in_specs = [new, dims_keys, pl.BlockSpec((1, 1, tk * DIFF_HEADS, LANES), lambda b, t: (layer, b, t, 0)),
                    new, new, _full_spec(bias_last.shape), _full_spec(bias_new.shape),
                    _full_spec(lqk.shape), _full_spec(sg.shape)]
        kd, acc_w = d, LANES
    return pl.pallas_call(
        functools.partial(_sample_attn_kernel, fox=fox, t_new=t_new, tk=tk, n_tiles=n_tiles, lam_init=lam_init),
        grid=(dec_batch, n_tiles),
        in_specs=in_specs,
        out_specs=out,
        out_shape=jax.ShapeDtypeStruct((dec_batch * t_new, d), BF16),
        scratch_shapes=[pltpu.VMEM((rows, kd), BF16), pltpu.VMEM((rows, 1), F32),
                        pltpu.VMEM((rows, 1), F32), pltpu.VMEM((rows, acc_w), F32)],
        compiler_params=_cparams(2),
        name="fox_sample_attn" if fox else "diff_sample_attn",
    )(*ins)


def _split3(c):
    hi = c.astype(BF16)
    r = c - hi.astype(F32)
    mid = r.astype(BF16)
    lo = (r - mid.astype(F32)).astype(BF16)
    return hi, mid, lo


def _decay_cols_kernel(lf_ref, part_ref, qa_ref, ka_ref, *maybe_kat_ref, length):
    part = part_ref[0:1, :]
    which = part_ref[1:2, :]
    carry = jnp.zeros((1, LANES), F32)
    for r in range(0, length, CUM_BLOCK):
        n = min(CUM_BLOCK, length - r)
        tri = (lax.broadcasted_iota(jnp.int32, (n, n), 1) <= lax.broadcasted_iota(jnp.int32, (n, n), 0))
        tri = jnp.where(tri, 1.0, 0.0).astype(BF16)
        c = carry
        for p in _split3(lf_ref[0, r:r + n, :]):
            c = c + jnp.dot(tri, p, preferred_element_type=F32)
        carry = c[n - 1:n, :]
        hi, mid, lo = (t.astype(F32) for t in _split3(c * LOG2E))
        terms = jnp.where(which == 0, hi, jnp.where(which == 1, mid, lo))
        qa = jnp.where(part < 0, 0.0, jnp.where(part < 3, terms, 1.0))
        ka = jnp.where(part < 0, 0.0, jnp.where(part < 3, 1.0, -terms))
        qa_ref[0, r:r + n, :] = qa.astype(BF16)
        ka_ref[0, r:r + n, :] = ka.astype(BF16)
        if maybe_kat_ref:
            maybe_kat_ref[0][0, :, r:r + n] = ka.T.astype(BF16)


def _decay_layout():
    part = np.full((2, LANES), -1, np.int32)
    used = np.arange(FOX_HEADS * DECAY_COLS) % DECAY_COLS
    part[0, :used.size] = used
    part[1, :used.size] = used % 3
    return jnp.asarray(part)


def _spread_heads(a):
    rep = jnp.repeat(a, DECAY_COLS, axis=-1)
    return jnp.pad(rep, [(0, 0)] * (a.ndim - 1) + [(0, LANES - rep.shape[-1])])


def _decay_cols(lf, keys_transposed):
    nb, length, _ = lf.shape
    part = _decay_layout()
    spec = pl.BlockSpec((1, length, LANES), lambda b: (b, 0, 0))
    out_specs = [spec, spec]
    out_shape = [jax.ShapeDtypeStruct((nb, length, LANES), BF16)] * 2
    if keys_transposed:
        out_specs.append(pl.BlockSpec((1, LANES, length), lambda b: (b, 0, 0)))
        out_shape.append(jax.ShapeDtypeStruct((nb, LANES, length), BF16))
    return pl.pallas_call(
        functools.partial(_decay_cols_kernel, length=length),
        grid=(nb,),
        in_specs=[spec, _full_spec(part.shape)],
        out_specs=out_specs,
        out_shape=out_shape,
        compiler_params=_cparams(1),
        name="fox_decay_cols",
    )(lf, part)


def _conv_dw_kernel(u_ref, halo_ref, wdw_ref, bdw_ref, g_ref, b_ref, o_ref, ext_ref, sh_ref, y_ref, *, tm,
                    steps_per_seq):
    halo = halo_ref[...]
    if steps_per_seq:
        halo = jnp.where(pl.program_id(0) % steps_per_seq == 0, 0.0, halo)
    ext_ref[0:HALO, :] = halo
    ext_ref[HALO:HALO + tm, :] = u_ref[...]
    span = sh_ref.shape[1]
    for s in range(1, SUBLANES):
        sh_ref[s - 1] = ext_ref[s:s + span, :]
    d = u_ref.shape[1]
    first = HALO - (CONV_WIDTH - 1)
    for rc in range(tm // CONV_RC):
        for lc in range(d // CONV_LC):
            cols = slice(lc * CONV_LC, (lc + 1) * CONV_LC)
            acc = jnp.zeros((CONV_RC, CONV_LC), F32)
            for w in range(CONV_WIDTH):
                a, s = divmod(first + w, SUBLANES)
                src = ext_ref if s == 0 else sh_ref.at[s - 1]
                r = rc * CONV_RC + a * SUBLANES
                acc = acc + src[r:r + CONV_RC, cols] * wdw_ref[w:w + 1, cols]
            y_ref[rc * CONV_RC:(rc + 1) * CONV_RC, cols] = acc
    y = y_ref[...] + bdw_ref[...]
    yc = y - jnp.mean(y, axis=-1, keepdims=True)
    yn = yc * lax.rsqrt(jnp.mean(yc * yc, axis=-1, keepdims=True) + EPS) * g_ref[...] + b_ref[...]
    o_ref[...] = (yn * (1.0 / (1.0 + jnp.exp(-yn)))).astype(BF16)


def _conv_dw(u, row0, n_rows, tm, halo, halo_index, steps_per_seq, wdw, bdw, g, b):
    d = u.shape[1]
    b0 = row0 // tm
    return pl.pallas_call(
        functools.partial(_conv_dw_kernel, tm=tm, steps_per_seq=steps_per_seq),
        grid=(n_rows // tm,),
        in_specs=[pl.BlockSpec((tm, d), lambda i: (b0 + i, 0)), pl.BlockSpec((HALO, d), lambda i: (halo_index(i), 0)),
                  _full_spec(wdw.shape), _full_spec(bdw.shape), _full_spec(g.shape), _full_spec(b.shape)],
        out_specs=pl.BlockSpec((tm, d), lambda i: (i, 0)),
        out_shape=jax.ShapeDtypeStruct((n_rows, d), BF16),
        scratch_shapes=[pltpu.VMEM((HALO + tm, d), F32), pltpu.VMEM((SUBLANES - 1, HALO + tm - SUBLANES, d), F32),
                        pltpu.VMEM((tm, d), F32)],
        compiler_params=_cparams(1),
        name="conv_dw",
    )(u, halo, wdw, bdw, g, b)


def _keys_dims_major_kernel(*refs):
    k_refs, o_ref = refs[:-1], refs[-1]
    for layer, k_ref in enumerate(k_refs):
        @pl.when(pl.program_id(0) == layer)
        def _(k_ref=k_ref):
            o_ref[0] = pltpu.einshape("bsd->dbs", k_ref[...])


def _keys_dims_major(keys, batch, seq):
    n_layers, d, ts = len(keys), keys[0].shape[-1], KEYS_TS
    n_chunks = seq // ts

    def in_spec(layer):
        park = 0 if layer else n_chunks - 1
        return pl.BlockSpec((batch, ts, d), lambda l, j: (0, jnp.where(l == layer, j, park), 0))

    return pl.pallas_call(
        _keys_dims_major_kernel,
        grid=(n_layers, n_chunks),
        in_specs=[in_spec(layer) for layer in range(n_layers)],
        out_specs=pl.BlockSpec((1, d, batch, ts), lambda l, j: (l, 0, 0, j)),
        out_shape=jax.ShapeDtypeStruct((n_layers, d, batch, seq), F32),
        compiler_params=_cparams(2),
        name="diff_keys_dims_major",
    )(*[k.reshape(batch, seq, d) for k in keys])


def _dims_keys(cache):
    nd = cache.ndim
    t = jnp.transpose(cache, (0, 1) + tuple(range(3, nd)) + (2,))
    return t.reshape(cache.shape[0], cache.shape[1], D_MODEL, cache.shape[2])


def kernel(x_prompt, x_sample, cache_diff_k, cache_diff_v, state_conv, cache_fox_k, cache_fox_v, cache_fox_logf, rel_bias, norm_g, final_g, diff_w_in, diff_w_out, diff_lq1, diff_lk1, diff_lq2, diff_lk2, diff_subln_g, conv_w_pw1, conv_b_pw1, conv_w_dw, conv_b_dw, conv_ln_g, conv_ln_b, conv_w_pw2, conv_b_pw2, fox_w_in, fox_b_f, fox_w_out, mlp_w1, mlp_w2):
    B, S, D = x_prompt.shape
    Bd, T, _ = x_sample.shape
    P = cache_diff_k.shape[2]
    depth = norm_g.shape[0]
    n_p, n_s = B * S, Bd * T
    assert D == D_MODEL and S % (2 * ATT_TQ) == 0 and P % SAMPLE_TK == 0 and n_p % ROW_TILE == 0
    assert n_s % ROW_TILE == 0 and T % 8 == 0 and T >= CONV_WIDTH - 1 and S % ROW_TILE == 0

    x = jnp.concatenate([x_prompt.reshape(n_p, D), x_sample.reshape(n_s, D)], axis=0)
    zero_bias = jnp.zeros((1, D), F32)
    final_gain = final_g.reshape(1, D)

    bt_diff, bt_fox = _prompt_tiles(rel_bias, ATT_TQ)
    sb_last, sb_new, sf_new = _sample_tiles(rel_bias, T, P, SAMPLE_TK)

    diff_kt = _dims_keys(cache_diff_k)
    diff_v = cache_diff_v.reshape(cache_diff_v.shape[0], Bd, P * DIFF_HEADS, 2 * DIFF_HEAD_DIM)
    fox_kt = _dims_keys(cache_fox_k)
    fox_vt = _dims_keys(cache_fox_v)

    n_diff, n_fox = diff_w_in.shape[0], fox_w_in.shape[0]
    dk_shape = (1, n_p, D)
    dv_shape = (n_diff, n_p, DIFF_HEADS, 2 * DIFF_HEAD_DIM)
    fkv_shape = (n_fox, n_p, FOX_HEADS, FOX_HEAD_DIM)
    dv_new, fk_new, fv_new = None, None, None
    dk_p, dk_s, dv_s = [], [], []
    cv_p, cv_s = [], []
    fl_p, fk_s, fv_s, fl_s = [], [], [], []
    y = None

    for i in range(depth):
        kind, j = i % N_MIXERS, i // N_MIXERS
        g1 = norm_g[i, 0].reshape(1, D)
        g2 = norm_g[i, 1].reshape(1, D)
        if kind == 0:
            lam_init = 0.8 - 0.6 * math.exp(-0.3 * i)
            q, k_p, k_s, dv_new, v_s, kb, vb = _qkv_inproj(
                x, n_p, g1, diff_w_in[j].astype(BF16), DIFF_SCALE, (dk_shape, 0), (dv_shape, j),
                prev=None if dv_new is None else {3: dv_new})
            dk_p.append(k_p)
            lqk = jnp.stack([diff_lq1[j], diff_lk1[j], diff_lq2[j], diff_lk2[j]]).astype(F32)
            sg = diff_subln_g[j].reshape(1, 2 * DIFF_HEAD_DIM).astype(F32)
            o_p = _prompt_attn(False, B, S, q, kb, vb, bt_diff, (lqk, sg.reshape(2 * DIFF_HEAD_DIM, 1)), lam_init)
            o_s = _sample_attn(False, j, n_p, Bd, T, P, q, diff_kt, diff_v, kb, vb, sb_last, sb_new, (lqk, sg),
                               lam_init)
            wo, bo = diff_w_out[j].astype(BF16), zero_bias
            dk_s.append(k_s.reshape(Bd, T, DIFF_HEADS, 2, DIFF_HEAD_DIM))
            dv_s.append(v_s.reshape(Bd, T, DIFF_HEADS, 2 * DIFF_HEAD_DIM))
        elif kind == 1:
            u = _conv_inproj(x, g1, conv_w_pw1[j].astype(BF16), conv_b_pw1[j].reshape(1, 2 * D))
            wdw = jnp.pad(conv_w_dw[j], ((0, HALO - CONV_WIDTH), (0, 0)))
            args = (wdw, conv_b_dw[j].reshape(1, D), conv_ln_g[j].reshape(1, D), conv_ln_b[j].reshape(1, D))
            per_tile = ROW_TILE // HALO
            o_p = _conv_dw(u, 0, n_p, ROW_TILE, u, lambda t: jnp.maximum(t * per_tile - 1, 0), S // ROW_TILE, *args)
            state = jnp.pad(state_conv[j], ((0, 0), (HALO - (CONV_WIDTH - 1), 0), (0, 0))).reshape(Bd * HALO, D)
            o_s = _conv_dw(u, n_p, n_s, T, state, lambda t: t, 0, *args)
            wo, bo = conv_w_pw2[j].astype(BF16), conv_b_pw2[j].reshape(1, D)
            keep = CONV_WIDTH - 1
            cv_p.append(jnp.stack([u[(b + 1) * S - keep:(b + 1) * S] for b in range(B)]))
            cv_s.append(u[n_p:].reshape(Bd, T, D)[:, T - keep:])
        else:
            w_in = fox_w_in[j]
            pad_heads = lambda a: jnp.pad(a, [(0, 0)] * (a.ndim - 1) + [(0, LANES - FOX_HEADS)])
            wf = jnp.concatenate([_spread_heads(w_in[:, 3 * D:]), pad_heads(w_in[:, 3 * D:])], axis=1).astype(BF16)
            b_f = fox_b_f[j].astype(F32).reshape(1, FOX_HEADS)
            bf = jnp.concatenate([_spread_heads(b_f), pad_heads(b_f)], axis=1)
            q, fk_new, k_s, fv_new, v_s, kb, vb, lf, lf_heads = _qkv_inproj(
                x, n_p, g1, w_in[:, :3 * D].astype(BF16), FOX_SCALE, (fkv_shape, j), (fkv_shape, j),
                prev=None if fk_new is None else {1: fk_new, 3: fv_new}, forget=(wf, bf))
            lf_p_cols = lf[:n_p].reshape(B, S, LANES)
            lf_s_cols = lf[n_p:].reshape(Bd, T, LANES)
            lf_p = lf_heads[:n_p].reshape(B, S, FOX_HEADS)
            lf_s = lf_heads[n_p:].reshape(Bd, T, FOX_HEADS)
            qa_p, ka_p = _decay_cols(lf_p_cols, False)
            cache_cols = _spread_heads(cache_fox_logf[j].astype(F32))
            decay_s = _decay_cols(jnp.concatenate([cache_cols, lf_s_cols], axis=1), True)
            o_p = _prompt_attn(True, B, S, q, kb, vb, bt_fox, (qa_p.reshape(n_p, LANES), ka_p.reshape(n_p, LANES)))
            o_s = _sample_attn(True, j, n_p, Bd, T, P, q, fox_kt, fox_vt, kb, vb, None, sf_new, decay_s)
            wo, bo = fox_w_out[j].astype(BF16), zero_bias
            fl_p.append(lf_p)
            fk_s.append(k_s.reshape(Bd, T, FOX_HEADS, FOX_HEAD_DIM))
            fv_s.append(v_s.reshape(Bd, T, FOX_HEADS, FOX_HEAD_DIM))
            fl_s.append(lf_s)
        final = i == depth - 1
        out = _mix_mlp(x, o_p, o_s, wo, bo, g2, mlp_w1[i].astype(BF16), mlp_w2[i].astype(BF16), final_gain, final)
        if final:
            y = out
        else:
            x = out[0]

    y_prompt = y[0].reshape(B, S, D)
    y_sample = y[1].reshape(Bd, T, D)
    dk_dims = _keys_dims_major(dk_p, B, S).reshape(n_diff, DIFF_HEADS, 2, DIFF_HEAD_DIM, B, S)
    new_diff_k_p = jnp.transpose(dk_dims, (0, 4, 5, 1, 2, 3))
    new_diff_v_p = dv_new.reshape(n_diff, B, S, DIFF_HEADS, 2 * DIFF_HEAD_DIM)
    new_fox_k_p = fk_new.reshape(n_fox, B, S, FOX_HEADS, FOX_HEAD_DIM)
    new_fox_v_p = fv_new.reshape(n_fox, B, S, FOX_HEADS, FOX_HEAD_DIM)
    return (y_prompt, y_sample, new_diff_k_p, new_diff_v_p, jnp.stack(cv_p), new_fox_k_p,
            new_fox_v_p, jnp.stack(fl_p), jnp.stack(dk_s), jnp.stack(dv_s), jnp.stack(cv_s),
            jnp.stack(fk_s), jnp.stack(fv_s), jnp.stack(fl_s))
```

```python
import functools
import math

import numpy as np
import jax
import jax.numpy as jnp
from jax import lax
from jax.experimental import pallas as pl
from jax.experimental.pallas import tpu as pltpu

F32 = jnp.float32
BF16 = jnp.bfloat16

D_MODEL = 1024
CHUNK = 64
EPS = 1e-6
DIFF_HEADS = 8
DIFF_HEAD_DIM = 64
LOG2E = math.log2(math.e)
DIFF_SCALE = DIFF_HEAD_DIM ** -0.5
CONV_WIDTH = 31
FOX_HEADS = 16
FOX_HEAD_DIM = 64
FOX_SCALE = FOX_HEAD_DIM ** -0.5
REL_BUCKETS = 32
N_MIXERS = 3

LANES = 128
SUBLANES = 8
NEG = -1e30
ROW_TILE = 512
ATT_TQ = 512
ATT_STRIP = 512
ATT_GROUPS = 2
REDUCE_ROWS = 64
SAMPLE_TK = 2048
SAMPLE_GROUP = 256
FF_CHUNK = 1024
KEYS_TS = 1024
CUM_BLOCK = 256
DECAY_COLS = 6
HALO = 32
CONV_RC = 32
CONV_LC = 256
VMEM_LIMIT = 56 * 1024 * 1024

_T5_STEPS = (12, 16, 23, 32, 46, 64, 91)
T5_NEAR = 128
assert _T5_STEPS[-1] <= T5_NEAR


def _cparams(n_axes):
    return pltpu.CompilerParams(dimension_semantics=("arbitrary",) * n_axes, vmem_limit_bytes=VMEM_LIMIT)


def _rms(x, g):
    return x * lax.rsqrt(jnp.mean(x * x, axis=-1, keepdims=True) + EPS) * g


def _full_spec(shape):
    nd = len(shape)
    return pl.BlockSpec(shape, lambda *_: (0,) * nd)


def _split_specs(n_prompt_tiles, tm, d):
    prompt = pl.BlockSpec((tm, d), lambda i: (jnp.minimum(i, n_prompt_tiles - 1), 0))
    sample = pl.BlockSpec((tm, d), lambda i: (jnp.maximum(i - n_prompt_tiles, 0), 0))
    return prompt, sample


def _split_store(n_prompt_tiles, value, prompt_ref, sample_ref):
    i = pl.program_id(0)

    @pl.when(i < n_prompt_tiles)
    def _():
        prompt_ref[...] = value

    @pl.when(i >= n_prompt_tiles)
    def _():
        sample_ref[...] = value


def _store_prompt_cache(value, prompt_ref, layer):
    layer = layer if prompt_ref.shape[0] > 1 else 0
    for other in range(prompt_ref.shape[0]):
        if other != layer:
            prompt_ref[other] = jnp.zeros(prompt_ref.shape[1:], prompt_ref.dtype)
    if len(prompt_ref.shape) == 3:
        prompt_ref[layer] = value
    else:
        prompt_ref[layer] = pltpu.einshape("m(hd)->mhd", value, h=prompt_ref.shape[2])


def _qkv_inproj_kernel(*refs, n_pt, scale, n_prev, forget, layers):
    x_ref, g_ref, w_ref = refs[:3]
    refs = refs[3:]
    if forget:
        wf_ref, bf_ref = refs[:2]
        refs = refs[2:]
    refs = refs[n_prev:]
    q_ref, kp_ref, ks_ref, vp_ref, vs_ref, kb_ref, vb_ref = refs[:7]
    d = D_MODEL

    def project(prompt):
        xn = _rms(x_ref[...], g_ref[...]).astype(BF16)
        q = jnp.dot(xn, w_ref[:, 0:d], preferred_element_type=F32)
        q_ref[...] = (q * (scale * LOG2E)).astype(BF16)
        for c, new_p, new_s, bf_ref_out in ((1, kp_ref, ks_ref, kb_ref), (2, vp_ref, vs_ref, vb_ref)):
            val = jnp.dot(xn, w_ref[:, c * d:(c + 1) * d], preferred_element_type=F32)
            if prompt:
                _store_prompt_cache(val, new_p, layers[c - 1])
            else:
                new_s[...] = val
            bf_ref_out[...] = val.astype(BF16)
        if forget:
            z = jnp.dot(xn, wf_ref[...], preferred_element_type=F32) + bf_ref[...]
            lf = jnp.minimum(z, 0.0) - jnp.log1p(jnp.exp(-jnp.abs(z)))
            refs[7][...] = lf[:, 0:LANES]
            refs[8][...] = lf[:, LANES:LANES + FOX_HEADS]

    pl.when(pl.program_id(0) < n_pt)(lambda: project(True))
    pl.when(pl.program_id(0) >= n_pt)(lambda: project(False))


def _qkv_inproj(x, n_p, g, w, scale, k_slab, v_slab, prev=None, forget=()):
    n, d = x.shape
    tm = ROW_TILE
    n_pt = n_p // tm
    row = pl.BlockSpec((tm, d), lambda i: (i, 0))
    _, sample = _split_specs(n_pt, tm, d)
    prev = prev or {}

    def slab(out_index, shape, layer):
        rest = tuple(shape[2:])
        zeros = (0,) * len(rest)
        if out_index in prev:
            return pl.BlockSpec((1, tm) + rest, lambda i: (layer, jnp.minimum(i, n_pt - 1)) + zeros)
        return pl.BlockSpec((shape[0], tm) + rest, lambda i: (0, jnp.minimum(i, n_pt - 1)) + zeros)

    f32_s = jax.ShapeDtypeStruct((n - n_p, d), F32)
    bf_all = jax.ShapeDtypeStruct((n, d), BF16)
    ins = [x, g, w] + list(forget) + list(prev.values())
    w_spec = pl.BlockSpec(w.shape, lambda i: (0, 0), pipeline_mode=pl.Buffered(1))
    in_specs = ([row, _full_spec((1, d)), w_spec] + [_full_spec(a.shape) for a in forget]
                + [pl.BlockSpec(memory_space=pl.ANY)] * len(prev))
    out_specs = [row, slab(1, *k_slab), sample, slab(3, *v_slab), sample, row, row]
    out_shape = [bf_all, jax.ShapeDtypeStruct(k_slab[0], F32), f32_s, jax.ShapeDtypeStruct(v_slab[0], F32), f32_s,
                 bf_all, bf_all]
    if forget:
        out_specs += [pl.BlockSpec((tm, LANES), lambda i: (i, 0)), pl.BlockSpec((tm, FOX_HEADS), lambda i: (i, 0))]
        out_shape += [jax.ShapeDtypeStruct((n, LANES), F32), jax.ShapeDtypeStruct((n, FOX_HEADS), F32)]
    first_prev = 3 + len(forget)
    aliases = {first_prev + pos: out_index for pos, out_index in enumerate(prev)}
    return pl.pallas_call(
        functools.partial(_qkv_inproj_kernel, n_pt=n_pt, scale=scale, n_prev=len(prev), forget=bool(forget),
                          layers=(k_slab[1], v_slab[1])),
        grid=(n // tm,),
        in_specs=in_specs,
        out_specs=out_specs,
        out_shape=out_shape,
        input_output_aliases=aliases,
        compiler_params=_cparams(1),
        name="fox_inproj" if forget else "diff_inproj",
    )(*ins)


def _conv_inproj_kernel(x_ref, g_ref, w_ref, b_ref, u_ref):
    d = D_MODEL
    xn = _rms(x_ref[...], g_ref[...]).astype(BF16)
    a = jnp.dot(xn, w_ref[:, 0:d], preferred_element_type=F32) + b_ref[:, 0:d]
    gate = jnp.dot(xn, w_ref[:, d:2 * d], preferred_element_type=F32) + b_ref[:, d:2 * d]
    u_ref[...] = a * (1.0 / (1.0 + jnp.exp(-gate)))


def _conv_inproj(x, g, w, b):
    n, d = x.shape
    tm = ROW_TILE
    row = pl.BlockSpec((tm, d), lambda i: (i, 0))
    return pl.pallas_call(
        _conv_inproj_kernel,
        grid=(n // tm,),
        in_specs=[row, _full_spec((1, d)), _full_spec(w.shape), _full_spec(b.shape)],
        out_specs=row,
        out_shape=jax.ShapeDtypeStruct((n, d), F32),
        compiler_params=_cparams(1),
        name="conv_inproj",
    )(x, g, w, b)


def _mix_mlp_kernel(x_ref, op_ref, os_ref, wo_ref, bo_ref, g_ref, w1_ref, w2_ref, gf_ref, *out_refs, n_pt, final):
    o = jnp.where(pl.program_id(0) < n_pt, op_ref[...], os_ref[...])
    x1 = x_ref[...] + jnp.dot(o, wo_ref[...], preferred_element_type=F32) + bo_ref[...]
    xn = _rms(x1, g_ref[...]).astype(BF16)
    acc = x1
    d_ff = w1_ref.shape[1]
    for c in range(d_ff // FF_CHUNK):
        h = jnp.dot(xn, w1_ref[:, c * FF_CHUNK:(c + 1) * FF_CHUNK], preferred_element_type=F32)
        h = jnp.square(jnp.maximum(h, 0.0)).astype(BF16)
        acc = acc + jnp.dot(h, w2_ref[c * FF_CHUNK:(c + 1) * FF_CHUNK, :], preferred_element_type=F32)
    if final:
        _split_store(n_pt, _rms(acc, gf_ref[...]), *out_refs)
    else:
        out_refs[0][...] = acc


def _mix_mlp(x, o_p, o_s, wo, bo, g, w1, w2, gf, final):
    n, d = x.shape
    tm = ROW_TILE
    n_p = o_p.shape[0]
    n_pt = n_p // tm
    row = pl.BlockSpec((tm, d), lambda i: (i, 0))
    prompt, sample = _split_specs(n_pt, tm, d)
    resident = lambda a: pl.BlockSpec(a.shape, lambda i: (0,) * a.ndim, pipeline_mode=pl.Buffered(1))
    if final:
        out_specs = [prompt, sample]
        out_shape = [jax.ShapeDtypeStruct((n_p, d), F32), jax.ShapeDtypeStruct((n - n_p, d), F32)]
    else:
        out_specs = [row]
        out_shape = [jax.ShapeDtypeStruct((n, d), F32)]
    return pl.pallas_call(
        functools.partial(_mix_mlp_kernel, n_pt=n_pt, final=final),
        grid=(n // tm,),
        in_specs=[row, prompt, sample, resident(wo), _full_spec(bo.shape), _full_spec(g.shape), resident(w1),
                  resident(w2), _full_spec(gf.shape)],
        out_specs=out_specs,
        out_shape=out_shape,
        compiler_params=_cparams(1),
        name="mix_mlp",
    )(x, o_p, o_s, wo, bo, g, w1, w2, gf)


def _t5_bias(tab_ref, h, qpos, kpos):
    half = REL_BUCKETS // 2
    n = qpos - kpos
    off = jnp.where(n < 0, half, 0)
    n = jnp.abs(n)
    large = jnp.full(n.shape, half // 2, jnp.int32)
    for t in _T5_STEPS:
        large = large + jnp.where(n >= t, 1, 0)
    bucket = off + jnp.where(n < half // 2, n, large)
    far = tab_ref[half - 1, h]
    out = jnp.zeros(n.shape, F32)
    for b in range(REL_BUCKETS):
        out = jnp.where(bucket == b, (tab_ref[b, h] - far) * LOG2E, out)
    visible = (kpos // CHUNK) <= (qpos // CHUNK)
    return jnp.where(visible, out, NEG)


def _prompt_tiles_kernel(tab_ref, bt_ref, cm_ref, *, tq):
    h = pl.program_id(0)
    for v in range(2):
        kpos = lax.broadcasted_iota(jnp.int32, (tq, tq), 0)
        qpos = lax.broadcasted_iota(jnp.int32, (tq, tq), 1) + (1 - v) * tq
        bt_ref[0, v] = _t5_bias(tab_ref, h, qpos, kpos)

    @pl.when(h == 0)
    def _():
        kpos = lax.broadcasted_iota(jnp.int32, (tq, tq), 0)
        qpos = lax.broadcasted_iota(jnp.int32, (tq, tq), 1)
        cm_ref[0, 0] = jnp.zeros((tq, tq), F32)
        cm_ref[0, 1] = jnp.where(kpos <= qpos, 0.0, NEG)


def _prompt_tiles(table, tq):
    return pl.pallas_call(
        functools.partial(_prompt_tiles_kernel, tq=tq),
        grid=(DIFF_HEADS,),
        in_specs=[pl.BlockSpec(memory_space=pltpu.SMEM)],
        out_specs=[pl.BlockSpec((1, 2, tq, tq), lambda h: (h, 0, 0, 0)),
                   pl.BlockSpec((1, 2, tq, tq), lambda h: (0, 0, 0, 0))],
        out_shape=[jax.ShapeDtypeStruct((DIFF_HEADS, 2, tq, tq), F32), jax.ShapeDtypeStruct((1, 2, tq, tq), F32)],
        compiler_params=_cparams(1),
        name="prompt_score_tiles",
    )(table)


def _sample_tiles_kernel(tab_ref, dl_ref, dn_ref, fn_ref, *, t_new, past, tk):
    rows = DIFF_HEADS * 2 * t_new
    for h in range(DIFF_HEADS):
        r0 = h * 2 * t_new
        q = lax.broadcasted_iota(jnp.int32, (2 * t_new, tk), 0) % t_new + past
        k = lax.broadcasted_iota(jnp.int32, (2 * t_new, tk), 1) + (past - tk)
        dl_ref[r0:r0 + 2 * t_new, :] = _t5_bias(tab_ref, h, q, k)
        q = lax.broadcasted_iota(jnp.int32, (2 * t_new, t_new), 0) % t_new + past
        k = lax.broadcasted_iota(jnp.int32, (2 * t_new, t_new), 1) + past
        dn_ref[r0:r0 + 2 * t_new, :] = _t5_bias(tab_ref, h, q, k)
    q = lax.broadcasted_iota(jnp.int32, (rows, t_new), 0) % t_new
    k = lax.broadcasted_iota(jnp.int32, (rows, t_new), 1)
    fn_ref[...] = jnp.where(k <= q, 0.0, NEG)


def _sample_tiles(table, t_new, past, tk):
    rows = DIFF_HEADS * 2 * t_new
    assert rows == FOX_HEADS * t_new
    return pl.pallas_call(
        functools.partial(_sample_tiles_kernel, t_new=t_new, past=past, tk=tk),
        in_specs=[pl.BlockSpec(memory_space=pltpu.SMEM)],
        out_shape=[jax.ShapeDtypeStruct((rows, tk), F32), jax.ShapeDtypeStruct((rows, t_new), F32),
                   jax.ShapeDtypeStruct((rows, t_new), F32)],
        name="sample_score_tiles",
    )(table)


def _lambda(lqk_ref, lam_init):
    a = jnp.sum(lqk_ref[0:1, :] * lqk_ref[1:2, :], axis=-1, keepdims=True)
    b = jnp.sum(lqk_ref[2:3, :] * lqk_ref[3:4, :], axis=-1, keepdims=True)
    return jnp.exp(a) - jnp.exp(b) + lam_init


def _scores_t(k, q2_ref, s_out):
    for c0 in range(0, q2_ref.shape[0], ATT_STRIP):
        cols = slice(c0, c0 + ATT_STRIP)
        s_out[:, cols] = lax.dot_general(k, q2_ref[cols, :], (((1,), (1,)), ((), ())), preferred_element_type=F32)


def _reduce_rows(op, x):
    w, n = x.shape
    part = op(x.reshape(w // REDUCE_ROWS, REDUCE_ROWS, n), axis=0)
    return op(part, axis=0, keepdims=True)


def _softmax_pv_t(s_in, v, bias_t, bias_rows, m_ref, l_ref, acc_ref):
    w = s_in.shape[0]
    for c0 in range(0, s_in.shape[1], ATT_STRIP):
        cols = slice(c0, c0 + ATT_STRIP)
        s = s_in[:, cols]
        if bias_t is not None:
            b0 = c0 % bias_t.shape[1]
            r0 = w - bias_rows
            biased = s[r0:, :] + bias_t[r0:, b0:b0 + ATT_STRIP]
            s = biased if r0 == 0 else jnp.concatenate([s[:r0, :], biased], axis=0)
        m_prev = m_ref[:, cols]
        m_new = jnp.maximum(m_prev, _reduce_rows(jnp.max, s))
        alpha = jnp.exp2(m_prev - m_new)
        p = jnp.exp2(s - m_new)
        l_ref[:, cols] = alpha * l_ref[:, cols] + _reduce_rows(jnp.sum, p)
        pv = lax.dot_general(v, p.astype(BF16), (((0,), (0,)), ((), ())), preferred_element_type=F32)
        acc_ref[:, cols] = alpha * acc_ref[:, cols] + pv
        m_ref[:, cols] = m_new


def _prompt_attn_kernel(*refs, fox, tq, seq, lam_init):
    if fox:
        (q_ref, qa_ref, k_ref, ka_ref, v_ref, bt_ref, o_ref, q2_ref, s0_ref, s1_ref, m_ref, l_ref, acc_ref) = refs
    else:
        (q_ref, k_ref, v_ref, bt_ref, lqk_ref, sg_ref, o_ref, q2_ref, s0_ref, s1_ref, m_ref, l_ref, acc_ref) = refs
    s_refs = (s0_ref, s1_ref)
    n_groups = q_ref.shape[1] // LANES
    lane = lax.broadcasted_iota(jnp.int32, (tq, LANES), 1)
    lo = lane < LANES // 2
    first_rows = lax.broadcasted_iota(jnp.int32, (LANES, tq), 0) < LANES // 2
    lanes = [slice(g * LANES, (g + 1) * LANES) for g in range(n_groups)]
    if fox:
        in_a, in_b = [], []
        for g in range(n_groups):
            c0 = 12 * (pl.program_id(1) * n_groups + g)
            in_a.append(jnp.logical_and(lane >= c0, lane < c0 + 6))
            in_b.append(jnp.logical_and(lane >= c0 + 6, lane < c0 + 12))

    def scores_g(g, j, slot):
        k0 = pl.multiple_of(jnp.maximum(j, 0) * tq, tq)
        k = k_ref[pl.ds(k0, tq), lanes[g]]
        if fox:
            k = jnp.concatenate([k, ka_ref[pl.ds(k0, tq), :]], axis=1)
        _scores_t(k, q2_ref.at[g], s_refs[slot].at[g])

    def softmax_pv_g(g, j, slot, bias_index):
        k0 = pl.multiple_of(j * tq, tq)
        if bias_index is None or (fox and bias_index == 0):
            bias_t, bias_rows = None, 0
        else:
            bias_t = bt_ref.at[0 if fox else g, bias_index]
            bias_rows = tq if bias_index == 1 else T5_NEAR
        _softmax_pv_t(s_refs[slot].at[g], v_ref[pl.ds(k0, tq), lanes[g]], bias_t, bias_rows, m_ref.at[g],
                      l_ref.at[g], acc_ref.at[g])

    def scores(j, slot):
        for g in range(n_groups):
            scores_g(g, j, slot)

    def softmax_pv(j, slot, bias_index):
        for g in range(n_groups):
            softmax_pv_g(g, j, slot, bias_index)

    def step(j_next, slot_next, j, slot, bias_index):
        for g in range(n_groups):
            scores_g(g, j_next, slot_next)
            softmax_pv_g(g, j, slot, bias_index)

    def q_block(i, carry):
        r0 = pl.multiple_of(i * tq, tq)
        for g in range(n_groups):
            q = q_ref[pl.ds(r0, tq), lanes[g]]
            zero = jnp.zeros_like(q)
            q2_ref[g, 0:tq, 0:LANES] = jnp.where(lo, q, zero)
            q2_ref[g, tq:2 * tq, 0:LANES] = jnp.where(lo, zero, q)
            if fox:
                qa = qa_ref[pl.ds(r0, tq), :]
                q2_ref[g, 0:tq, LANES:2 * LANES] = jnp.where(in_a[g], qa, zero)
                q2_ref[g, tq:2 * tq, LANES:2 * LANES] = jnp.where(in_b[g], qa, zero)
        m_ref[...] = jnp.full(m_ref.shape, NEG, F32)
        l_ref[...] = jnp.zeros(l_ref.shape, F32)
        acc_ref[...] = jnp.zeros(acc_ref.shape, F32)

        @pl.when(i == 0)
        def _():
            scores(i, 0)
            softmax_pv(i, 0, 1)

        @pl.when(i >= 1)
        def _():
            scores(i, 0)
            step(i - 1, 1, i, 0, 1)
            step(i - 2, 0, i - 1, 1, 0)
            n_far = i - 1

            def pair(u, c):
                a = i - 2 - 2 * u
                step(a - 1, 1, a, 0, None)
                step(a - 2, 0, a - 1, 1, None)
                return c

            lax.fori_loop(0, n_far // 2, pair, 0)

            @pl.when(n_far % 2 == 1)
            def _():
                softmax_pv(0, 0, None)

        for g in range(n_groups):
            top = acc_ref[g, :, 0:tq] / l_ref[g, :, 0:tq]
            bot = acc_ref[g, :, tq:2 * tq] / l_ref[g, :, tq:2 * tq]
            if fox:
                o_t = jnp.where(first_rows, top, bot)
            else:
                o_t = top - _lambda(lqk_ref, lam_init) * bot
                ms = jnp.mean(o_t * o_t, axis=0, keepdims=True)
                o_t = o_t * lax.rsqrt(ms + EPS) * sg_ref[...] * (1.0 - lam_init)
            o_ref[pl.ds(r0, tq), lanes[g]] = o_t.T.astype(BF16)
        return carry

    lax.fori_loop(0, seq // tq, q_block, 0)


def _prompt_attn(fox, batch, seq, q, k, v, bt, extra, lam_init=0.0):
    tq = ATT_TQ
    ng = ATT_GROUPS
    steps = D_MODEL // (LANES * ng)
    blk = pl.BlockSpec((seq, ng * LANES), lambda b, h: (b, h))
    if fox:
        qa, ka = extra
        aug = pl.BlockSpec((seq, LANES), lambda b, h: (b, 0))
        ins = [q, qa, k, ka, v, bt]
        in_specs = [blk, aug, blk, aug, blk, pl.BlockSpec((1, 2, tq, tq), lambda b, h: (0, 0, 0, 0))]
        kd = 2 * LANES
    else:
        lqk, sg = extra
        ins = [q, k, v, bt, lqk, sg]
        in_specs = [blk, blk, blk, pl.BlockSpec((ng, 2, tq, tq), lambda b, h: (h, 0, 0, 0)),
                    _full_spec(lqk.shape), _full_spec(sg.shape)]
        kd = LANES
    return pl.pallas_call(
        functools.partial(_prompt_attn_kernel, fox=fox, tq=tq, seq=seq, lam_init=lam_init),
        grid=(batch, steps),
        in_specs=in_specs,
        out_specs=blk,
        out_shape=jax.ShapeDtypeStruct((batch * seq, D_MODEL), BF16),
        scratch_shapes=[pltpu.VMEM((ng, 2 * tq, kd), BF16), pltpu.VMEM((ng, tq, 2 * tq), F32),
                        pltpu.VMEM((ng, tq, 2 * tq), F32), pltpu.VMEM((ng, 1, 2 * tq), F32),
                        pltpu.VMEM((ng, 1, 2 * tq), F32), pltpu.VMEM((ng, LANES, 2 * tq), F32)],
        compiler_params=_cparams(2),
        name="fox_prompt_attn" if fox else "diff_prompt_attn",
    )(*ins)


def _online_update(s, pv_fn, m_ref, l_ref, acc_ref):
    m_prev = m_ref[...]
    m_new = jnp.maximum(m_prev, jnp.max(s, axis=-1, keepdims=True))
    alpha = jnp.exp2(m_prev - m_new)
    p = jnp.exp2(s - m_new)
    l_ref[...] = alpha * l_ref[...] + jnp.sum(p, axis=-1, keepdims=True)
    acc_ref[...] = alpha * acc_ref[...] + pv_fn(p.astype(BF16))
    m_ref[...] = m_new


def _sample_attn_kernel(*refs, fox, t_new, tk, n_tiles, lam_init):
    if fox:
        (q_ref, qa_ref, kt_ref, kat_ref, vt_ref, kn_ref, kan_ref, vn_ref, bn_ref,
         o_ref, qbd_ref, m_ref, l_ref, acc_ref) = refs
    else:
        (q_ref, kt_ref, v_ref, kn_ref, vn_ref, bl_ref, bn_ref, lqk_ref, sg_ref,
         o_ref, qbd_ref, m_ref, l_ref, acc_ref) = refs
    t = pl.program_id(1)
    n_blocks = D_MODEL // 64
    rows_per_head = 2 * t_new

    @pl.when(t == 0)
    def _():
        q = q_ref[...]
        zero = jnp.zeros_like(q)
        cb = lax.broadcasted_iota(jnp.int32, q.shape, 1) // 64
        if fox:
            qa = qa_ref[0]
            la = lax.broadcasted_iota(jnp.int32, qa.shape, 1)
        for rb in range(n_blocks):
            qbd_ref[rb * t_new:(rb + 1) * t_new, 0:D_MODEL] = jnp.where(cb == rb, q, zero)
            if fox:
                sel = jnp.logical_and(la >= 6 * rb, la < 6 * rb + 6)
                qbd_ref[rb * t_new:(rb + 1) * t_new, D_MODEL:D_MODEL + LANES] = jnp.where(sel, qa, jnp.zeros_like(qa))
        m_ref[...] = jnp.full(m_ref.shape, NEG, F32)
        l_ref[...] = jnp.zeros(l_ref.shape, F32)
        acc_ref[...] = jnp.zeros(acc_ref.shape, F32)

    n_groups = D_MODEL // SAMPLE_GROUP
    group_rows = (SAMPLE_GROUP // 64) * t_new
    groups = [(slice(g * group_rows, (g + 1) * group_rows), slice(g * SAMPLE_GROUP, (g + 1) * SAMPLE_GROUP))
              for g in range(n_groups)]

    def cache_scores():
        s = jnp.concatenate([jnp.dot(qbd_ref[rows, dims], kt_ref[0, 0, dims, :].astype(BF16),
                                     preferred_element_type=F32) for rows, dims in groups], axis=0)
        if fox:
            s = s + jnp.dot(qbd_ref[:, D_MODEL:D_MODEL + LANES], kat_ref[0], preferred_element_type=F32)
        return s

    if fox:
        def cache_pv(p):
            return jnp.concatenate(
                [lax.dot_general(p[rows, :], vt_ref[0, 0, dims, :].astype(BF16), (((1,), (1,)), ((), ())),
                                 preferred_element_type=F32) for rows, dims in groups], axis=0)

        def new_pv(p):
            return jnp.concatenate([jnp.dot(p[rows, :], vn_ref[:, dims], preferred_element_type=F32)
                                    for rows, dims in groups], axis=0)
    else:
        def cache_pv(p):
            return jnp.concatenate(
                [jnp.dot(p[h * rows_per_head:(h + 1) * rows_per_head, :],
                         v_ref[0, 0, pl.ds(h, tk, stride=DIFF_HEADS), :].astype(BF16), preferred_element_type=F32)
                 for h in range(DIFF_HEADS)], axis=0)

        def new_pv(p):
            return jnp.concatenate(
                [jnp.dot(p[h * rows_per_head:(h + 1) * rows_per_head, :], vn_ref[:, h * LANES:(h + 1) * LANES],
                         preferred_element_type=F32) for h in range(DIFF_HEADS)], axis=0)

    @pl.when(t < n_tiles - 1)
    def _():
        _online_update(cache_scores(), cache_pv, m_ref, l_ref, acc_ref)

    @pl.when(t == n_tiles - 1)
    def _():
        s = cache_scores()
        if not fox:
            s = s + bl_ref[...]
        _online_update(s, cache_pv, m_ref, l_ref, acc_ref)
        kn = kn_ref[...]
        if fox:
            kn = jnp.concatenate([kn, kan_ref[0]], axis=1)
        s = lax.dot_general(qbd_ref[...], kn, (((1,), (1,)), ((), ())), preferred_element_type=F32) + bn_ref[...]
        _online_update(s, new_pv, m_ref, l_ref, acc_ref)

        lane = lax.broadcasted_iota(jnp.int32, (t_new, LANES), 1)
        lo = lane < LANES // 2
        for g in range(D_MODEL // LANES):
            ra, rb = 2 * g * t_new, (2 * g + 1) * t_new
            cols = slice(g * LANES, (g + 1) * LANES)
            c0 = (g * LANES) % SAMPLE_GROUP if fox else 0
            acc_cols = slice(c0, c0 + LANES)
            top = acc_ref[ra:ra + t_new, acc_cols] / l_ref[ra:ra + t_new, :]
            bot = acc_ref[rb:rb + t_new, acc_cols] / l_ref[rb:rb + t_new, :]
            if fox:
                o = jnp.where(lo, top, bot)
            else:
                o = _rms(top - _lambda(lqk_ref, lam_init) * bot, sg_ref[...]) * (1.0 - lam_init)
            o_ref[:, cols] = o.astype(BF16)


def _sample_attn(fox, layer, n_prompt, dec_batch, t_new, past, q, kt, v, kn, vn, bias_last, bias_new, extra,
                 lam_init=0.0):
    tk = SAMPLE_TK
    n_tiles = past // tk
    d = D_MODEL
    rows = (d // 64) * t_new
    s0 = n_prompt // t_new
    new = pl.BlockSpec((t_new, d), lambda b, t: (s0 + b, 0))
    dims_keys = pl.BlockSpec((1, 1, d, tk), lambda b, t: (layer, b, 0, t))
    out = pl.BlockSpec((t_new, d), lambda b, t: (b, 0))
    if fox:
        qa, ka, kat = extra
        new_aug = pl.BlockSpec((1, t_new, LANES), lambda b, t: (b, past // t_new, 0))
        ins = [q, qa, kt, kat, v, kn, ka, vn, bias_new]
        in_specs = [new, new_aug, dims_keys, pl.BlockSpec((1, LANES, tk), lambda b, t: (b, 0, t)), dims_keys,
                    new, new_aug, new, _full_spec(bias_new.shape)]
        kd, acc_w = d + LANES, SAMPLE_GROUP
    else:
        lqk, sg = extra
        ins = [q, kt, v, kn, vn, bias_last, bias_new, lqk, sg]
        in_specs = [new, dims_keys, pl.BlockSpec((1, 1, tk * DIFF_HEADS, LANES), lambda b, t: (layer, b, t, 0)),
                    new, new, _full_spec(bias_last.shape), _full_spec(bias_new.shape),
                    _full_spec(lqk.shape), _full_spec(sg.shape)]
        kd, acc_w = d, LANES
    return pl.pallas_call(
        functools.partial(_sample_attn_kernel, fox=fox, t_new=t_new, tk=tk, n_tiles=n_tiles, lam_init=lam_init),
        grid=(dec_batch, n_tiles),
        in_specs=in_specs,
        out_specs=out,
        out_shape=jax.ShapeDtypeStruct((dec_batch * t_new, d), BF16),
        scratch_shapes=[pltpu.VMEM((rows, kd), BF16), pltpu.VMEM((rows, 1), F32),
                        pltpu.VMEM((rows, 1), F32), pltpu.VMEM((rows, acc_w), F32)],
        compiler_params=_cparams(2),
        name="fox_sample_attn" if fox else "diff_sample_attn",
    )(*ins)


def _split3(c):
    hi = c.astype(BF16)
    r = c - hi.astype(F32)
    mid = r.astype(BF16)
    lo = (r - mid.astype(F32)).astype(BF16)
    return hi, mid, lo


def _decay_cols_kernel(lf_ref, part_ref, qa_ref, ka_ref, *maybe_kat_ref, length):
    part = part_ref[0:1, :]
    which = part_ref[1:2, :]
    carry = jnp.zeros((1, LANES), F32)
    for r in range(0, length, CUM_BLOCK):
        n = min(CUM_BLOCK, length - r)
        tri = (lax.broadcasted_iota(jnp.int32, (n, n), 1) <= lax.broadcasted_iota(jnp.int32, (n, n), 0))
        tri = jnp.where(tri, 1.0, 0.0).astype(BF16)
        c = carry
        for p in _split3(lf_ref[0, r:r + n, :]):
            c = c + jnp.dot(tri, p, preferred_element_type=F32)
        carry = c[n - 1:n, :]
        hi, mid, lo = (t.astype(F32) for t in _split3(c * LOG2E))
        terms = jnp.where(which == 0, hi, jnp.where(which == 1, mid, lo))
        qa = jnp.where(part < 0, 0.0, jnp.where(part < 3, terms, 1.0))
        ka = jnp.where(part < 0, 0.0, jnp.where(part < 3, 1.0, -terms))
        qa_ref[0, r:r + n, :] = qa.astype(BF16)
        ka_ref[0, r:r + n, :] = ka.astype(BF16)
        if maybe_kat_ref:
            maybe_kat_ref[0][0, :, r:r + n] = ka.T.astype(BF16)


def _decay_layout():
    part = np.full((2, LANES), -1, np.int32)
    used = np.arange(FOX_HEADS * DECAY_COLS) % DECAY_COLS
    part[0, :used.size] = used
    part[1, :used.size] = used % 3
    return jnp.asarray(part)


def _spread_heads(a):
    rep = jnp.repeat(a, DECAY_COLS, axis=-1)
    return jnp.pad(rep, [(0, 0)] * (a.ndim - 1) + [(0, LANES - rep.shape[-1])])


def _decay_cols(lf, keys_transposed):
    nb, length, _ = lf.shape
    part = _decay_layout()
    spec = pl.BlockSpec((1, length, LANES), lambda b: (b, 0, 0))
    out_specs = [spec, spec]
    out_shape = [jax.ShapeDtypeStruct((nb, length, LANES), BF16)] * 2
    if keys_transposed:
        out_specs.append(pl.BlockSpec((1, LANES, length), lambda b: (b, 0, 0)))
        out_shape.append(jax.ShapeDtypeStruct((nb, LANES, length), BF16))
    return pl.pallas_call(
        functools.partial(_decay_cols_kernel, length=length),
        grid=(nb,),
        in_specs=[spec, _full_spec(part.shape)],
        out_specs=out_specs,
        out_shape=out_shape,
        compiler_params=_cparams(1),
        name="fox_decay_cols",
    )(lf, part)


def _conv_dw_kernel(u_ref, halo_ref, wdw_ref, bdw_ref, g_ref, b_ref, o_ref, ext_ref, sh_ref, y_ref, *, tm,
                    steps_per_seq):
    halo = halo_ref[...]
    if steps_per_seq:
        halo = jnp.where(pl.program_id(0) % steps_per_seq == 0, 0.0, halo)
    ext_ref[0:HALO, :] = halo
    ext_ref[HALO:HALO + tm, :] = u_ref[...]
    span = sh_ref.shape[1]
    for s in range(1, SUBLANES):
        sh_ref[s - 1] = ext_ref[s:s + span, :]
    d = u_ref.shape[1]
    first = HALO - (CONV_WIDTH - 1)
    for rc in range(tm // CONV_RC):
        for lc in range(d // CONV_LC):
            cols = slice(lc * CONV_LC, (lc + 1) * CONV_LC)
            acc = jnp.zeros((CONV_RC, CONV_LC), F32)
            for w in range(CONV_WIDTH):
                a, s = divmod(first + w, SUBLANES)
                src = ext_ref if s == 0 else sh_ref.at[s - 1]
                r = rc * CONV_RC + a * SUBLANES
                acc = acc + src[r:r + CONV_RC, cols] * wdw_ref[w:w + 1, cols]
            y_ref[rc * CONV_RC:(rc + 1) * CONV_RC, cols] = acc
    y = y_ref[...] + bdw_ref[...]
    yc = y - jnp.mean(y, axis=-1, keepdims=True)
    yn = yc * lax.rsqrt(jnp.mean(yc * yc, axis=-1, keepdims=True) + EPS) * g_ref[...] + b_ref[...]
    o_ref[...] = (yn * (1.0 / (1.0 + jnp.exp(-yn)))).astype(BF16)


def _conv_dw(u, row0, n_rows, tm, halo, halo_index, steps_per_seq, wdw, bdw, g, b):
    d = u.shape[1]
    b0 = row0 // tm
    return pl.pallas_call(
        functools.partial(_conv_dw_kernel, tm=tm, steps_per_seq=steps_per_seq),
        grid=(n_rows // tm,),
        in_specs=[pl.BlockSpec((tm, d), lambda i: (b0 + i, 0)), pl.BlockSpec((HALO, d), lambda i: (halo_index(i), 0)),
                  _full_spec(wdw.shape), _full_spec(bdw.shape), _full_spec(g.shape), _full_spec(b.shape)],
        out_specs=pl.BlockSpec((tm, d), lambda i: (i, 0)),
        out_shape=jax.ShapeDtypeStruct((n_rows, d), BF16),
        scratch_shapes=[pltpu.VMEM((HALO + tm, d), F32), pltpu.VMEM((SUBLANES - 1, HALO + tm - SUBLANES, d), F32),
                        pltpu.VMEM((tm, d), F32)],
        compiler_params=_cparams(1),
        name="conv_dw",
    )(u, halo, wdw, bdw, g, b)


def _keys_dims_major_kernel(*refs):
    k_refs, o_ref = refs[:-1], refs[-1]
    for layer, k_ref in enumerate(k_refs):
        @pl.when(pl.program_id(0) == layer)
        def _(k_ref=k_ref):
            o_ref[0, 0] = k_ref[...].T


def _keys_dims_major(keys, batch, seq):
    n_layers, d, ts = len(keys), keys[0].shape[-1], KEYS_TS
    n_chunks = batch * seq // ts
    per_seq = seq // ts

    def in_spec(layer):
        park = 0 if layer else n_chunks - 1
        return pl.BlockSpec((ts, d), lambda l, j: (jnp.where(l == layer, j, park), 0))

    return pl.pallas_call(
        _keys_dims_major_kernel,
        grid=(n_layers, n_chunks),
        in_specs=[in_spec(layer) for layer in range(n_layers)],
        out_specs=pl.BlockSpec((1, 1, d, ts), lambda l, j: (l, j // per_seq, 0, j % per_seq)),
        out_shape=jax.ShapeDtypeStruct((n_layers, batch, d, seq), F32),
        compiler_params=_cparams(2),
        name="diff_keys_dims_major",
    )(*[k.reshape(batch * seq, d) for k in keys])


def _dims_keys(cache):
    nd = cache.ndim
    t = jnp.transpose(cache, (0, 1) + tuple(range(3, nd)) + (2,))
    return t.reshape(cache.shape[0], cache.shape[1], D_MODEL, cache.shape[2])


def kernel(x_prompt, x_sample, cache_diff_k, cache_diff_v, state_conv, cache_fox_k, cache_fox_v, cache_fox_logf, rel_bias, norm_g, final_g, diff_w_in, diff_w_out, diff_lq1, diff_lk1, diff_lq2, diff_lk2, diff_subln_g, conv_w_pw1, conv_b_pw1, conv_w_dw, conv_b_dw, conv_ln_g, conv_ln_b, conv_w_pw2, conv_b_pw2, fox_w_in, fox_b_f, fox_w_out, mlp_w1, mlp_w2):
    B, S, D = x_prompt.shape
    Bd, T, _ = x_sample.shape
    P = cache_diff_k.shape[2]
    depth = norm_g.shape[0]
    n_p, n_s = B * S, Bd * T
    assert D == D_MODEL and S % (2 * ATT_TQ) == 0 and P % SAMPLE_TK == 0 and n_p % ROW_TILE == 0
    assert n_s % ROW_TILE == 0 and T % 8 == 0 and T >= CONV_WIDTH - 1 and S % ROW_TILE == 0

    x = jnp.concatenate([x_prompt.reshape(n_p, D), x_sample.reshape(n_s, D)], axis=0)
    zero_bias = jnp.zeros((1, D), F32)
    final_gain = final_g.reshape(1, D)

    bt_diff, bt_fox = _prompt_tiles(rel_bias, ATT_TQ)
    sb_last, sb_new, sf_new = _sample_tiles(rel_bias, T, P, SAMPLE_TK)

    diff_kt = _dims_keys(cache_diff_k)
    diff_v = cache_diff_v.reshape(cache_diff_v.shape[0], Bd, P * DIFF_HEADS, 2 * DIFF_HEAD_DIM)
    fox_kt = _dims_keys(cache_fox_k)
    fox_vt = _dims_keys(cache_fox_v)

    n_diff, n_fox = diff_w_in.shape[0], fox_w_in.shape[0]
    dk_shape = (1, n_p, D)
    dv_shape = (n_diff, n_p, DIFF_HEADS, 2 * DIFF_HEAD_DIM)
    fkv_shape = (n_fox, n_p, FOX_HEADS, FOX_HEAD_DIM)
    dv_new, fk_new, fv_new = None, None, None
    dk_p, dk_s, dv_s = [], [], []
    cv_p, cv_s = [], []
    fl_p, fk_s, fv_s, fl_s = [], [], [], []
    y = None

    for i in range(depth):
        kind, j = i % N_MIXERS, i // N_MIXERS
        g1 = norm_g[i, 0].reshape(1, D)
        g2 = norm_g[i, 1].reshape(1, D)
        if kind == 0:
            lam_init = 0.8 - 0.6 * math.exp(-0.3 * i)
            q, k_p, k_s, dv_new, v_s, kb, vb = _qkv_inproj(
                x, n_p, g1, diff_w_in[j].astype(BF16), DIFF_SCALE, (dk_shape, 0), (dv_shape, j),
                prev=None if dv_new is None else {3: dv_new})
            dk_p.append(k_p)
            lqk = jnp.stack([diff_lq1[j], diff_lk1[j], diff_lq2[j], diff_lk2[j]]).astype(F32)
            sg = diff_subln_g[j].reshape(1, 2 * DIFF_HEAD_DIM).astype(F32)
            o_p = _prompt_attn(False, B, S, q, kb, vb, bt_diff, (lqk, sg.reshape(2 * DIFF_HEAD_DIM, 1)), lam_init)
            o_s = _sample_attn(False, j, n_p, Bd, T, P, q, diff_kt, diff_v, kb, vb, sb_last, sb_new, (lqk, sg),
                               lam_init)
            wo, bo = diff_w_out[j].astype(BF16), zero_bias
            dk_s.append(k_s.reshape(Bd, T, DIFF_HEADS, 2, DIFF_HEAD_DIM))
            dv_s.append(v_s.reshape(Bd, T, DIFF_HEADS, 2 * DIFF_HEAD_DIM))
        elif kind == 1:
            u = _conv_inproj(x, g1, conv_w_pw1[j].astype(BF16), conv_b_pw1[j].reshape(1, 2 * D))
            wdw = jnp.pad(conv_w_dw[j], ((0, HALO - CONV_WIDTH), (0, 0)))
            args = (wdw, conv_b_dw[j].reshape(1, D), conv_ln_g[j].reshape(1, D), conv_ln_b[j].reshape(1, D))
            per_tile = ROW_TILE // HALO
            o_p = _conv_dw(u, 0, n_p, ROW_TILE, u, lambda t: jnp.maximum(t * per_tile - 1, 0), S // ROW_TILE, *args)
            state = jnp.pad(state_conv[j], ((0, 0), (HALO - (CONV_WIDTH - 1), 0), (0, 0))).reshape(Bd * HALO, D)
            o_s = _conv_dw(u, n_p, n_s, T, state, lambda t: t, 0, *args)
            wo, bo = conv_w_pw2[j].astype(BF16), conv_b_pw2[j].reshape(1, D)
            keep = CONV_WIDTH - 1
            cv_p.append(jnp.stack([u[(b + 1) * S - keep:(b + 1) * S] for b in range(B)]))
            cv_s.append(u[n_p:].reshape(Bd, T, D)[:, T - keep:])
        else:
            w_in = fox_w_in[j]
            pad_heads = lambda a: jnp.pad(a, [(0, 0)] * (a.ndim - 1) + [(0, LANES - FOX_HEADS)])
            wf = jnp.concatenate([_spread_heads(w_in[:, 3 * D:]), pad_heads(w_in[:, 3 * D:])], axis=1).astype(BF16)
            b_f = fox_b_f[j].astype(F32).reshape(1, FOX_HEADS)
            bf = jnp.concatenate([_spread_heads(b_f), pad_heads(b_f)], axis=1)
            q, fk_new, k_s, fv_new, v_s, kb, vb, lf, lf_heads = _qkv_inproj(
                x, n_p, g1, w_in[:, :3 * D].astype(BF16), FOX_SCALE, (fkv_shape, j), (fkv_shape, j),
                prev=None if fk_new is None else {1: fk_new, 3: fv_new}, forget=(wf, bf))
            lf_p_cols = lf[:n_p].reshape(B, S, LANES)
            lf_s_cols = lf[n_p:].reshape(Bd, T, LANES)
            lf_p = lf_heads[:n_p].reshape(B, S, FOX_HEADS)
            lf_s = lf_heads[n_p:].reshape(Bd, T, FOX_HEADS)
            qa_p, ka_p = _decay_cols(lf_p_cols, False)
            cache_cols = _spread_heads(cache_fox_logf[j].astype(F32))
            decay_s = _decay_cols(jnp.concatenate([cache_cols, lf_s_cols], axis=1), True)
            o_p = _prompt_attn(True, B, S, q, kb, vb, bt_fox, (qa_p.reshape(n_p, LANES), ka_p.reshape(n_p, LANES)))
            o_s = _sample_attn(True, j, n_p, Bd, T, P, q, fox_kt, fox_vt, kb, vb, None, sf_new, decay_s)
            wo, bo = fox_w_out[j].astype(BF16), zero_bias
            fl_p.append(lf_p)
            fk_s.append(k_s.reshape(Bd, T, FOX_HEADS, FOX_HEAD_DIM))
            fv_s.append(v_s.reshape(Bd, T, FOX_HEADS, FOX_HEAD_DIM))
            fl_s.append(lf_s)
        final = i == depth - 1
        out = _mix_mlp(x, o_p, o_s, wo, bo, g2, mlp_w1[i].astype(BF16), mlp_w2[i].astype(BF16), final_gain, final)
        if final:
            y = out
        else:
            x = out[0]

    y_prompt = y[0].reshape(B, S, D)
    y_sample = y[1].reshape(Bd, T, D)
    dk_dims = _keys_dims_major(dk_p, B, S).reshape(n_diff, B, DIFF_HEADS, 2, DIFF_HEAD_DIM, S)
    new_diff_k_p = jnp.transpose(dk_dims, (0, 1, 5, 2, 3, 4))
    new_diff_v_p = dv_new.reshape(n_diff, B, S, DIFF_HEADS, 2 * DIFF_HEAD_DIM)
    new_fox_k_p = fk_new.reshape(n_fox, B, S, FOX_HEADS, FOX_HEAD_DIM)
    new_fox_v_p = fv_new.reshape(n_fox, B, S, FOX_HEADS, FOX_HEAD_DIM)
    return (y_prompt, y_sample, new_diff_k_p, new_diff_v_p, jnp.stack(cv_p), new_fox_k_p,
            new_fox_v_p, jnp.stack(fl_p), jnp.stack(dk_s), jnp.stack(dv_s), jnp.stack(cv_s),
            jnp.stack(fk_s), jnp.stack(fv_s), jnp.stack(fl_s))
```

```python
import functools
import math

import numpy as np
import jax
import jax.numpy as jnp
from jax import lax
from jax.experimental import pallas as pl
from jax.experimental.pallas import tpu as pltpu

F32 = jnp.float32
BF16 = jnp.bfloat16

D_MODEL = 1024
CHUNK = 64
EPS = 1e-6
DIFF_HEADS = 8
DIFF_HEAD_DIM = 64
LOG2E = math.log2(math.e)
DIFF_SCALE = DIFF_HEAD_DIM ** -0.5
CONV_WIDTH = 31
FOX_HEADS = 16
FOX_HEAD_DIM = 64
FOX_SCALE = FOX_HEAD_DIM ** -0.5
REL_BUCKETS = 32
N_MIXERS = 3

LANES = 128
SUBLANES = 8
NEG = -1e30
ROW_TILE = 512
ATT_TQ = 512
ATT_STRIP = 512
ATT_GROUPS = 2
REDUCE_ROWS = 64
SAMPLE_TK = 2048
SAMPLE_GROUP = 256
FF_CHUNK = 1024
CUM_BLOCK = 256
DECAY_COLS = 6
HALO = 32
CONV_RC = 32
CONV_LC = 256
VMEM_LIMIT = 56 * 1024 * 1024

_T5_STEPS = (12, 16, 23, 32, 46, 64, 91)
T5_NEAR = 128
assert _T5_STEPS[-1] <= T5_NEAR


def _cparams(n_axes):
    return pltpu.CompilerParams(dimension_semantics=("arbitrary",) * n_axes, vmem_limit_bytes=VMEM_LIMIT)


def _rms(x, g):
    return x * lax.rsqrt(jnp.mean(x * x, axis=-1, keepdims=True) + EPS) * g


def _full_spec(shape):
    nd = len(shape)
    return pl.BlockSpec(shape, lambda *_: (0,) * nd)


def _split_specs(n_prompt_tiles, tm, d):
    prompt = pl.BlockSpec((tm, d), lambda i: (jnp.minimum(i, n_prompt_tiles - 1), 0))
    sample = pl.BlockSpec((tm, d), lambda i: (jnp.maximum(i - n_prompt_tiles, 0), 0))
    return prompt, sample


def _split_store(n_prompt_tiles, value, prompt_ref, sample_ref):
    i = pl.program_id(0)

    @pl.when(i < n_prompt_tiles)
    def _():
        prompt_ref[...] = value

    @pl.when(i >= n_prompt_tiles)
    def _():
        sample_ref[...] = value


def _store_prompt_cache(value, prompt_ref, layer, kind):
    layer = layer if prompt_ref.shape[0] > 1 else 0
    for other in range(prompt_ref.shape[0]):
        if other != layer:
            prompt_ref[other] = jnp.zeros(prompt_ref.shape[1:], prompt_ref.dtype)
    if kind == "dims":
        prompt_ref[layer, 0] = value.T
    else:
        prompt_ref[layer] = pltpu.einshape("m(hd)->mhd", value, h=prompt_ref.shape[2])


def _qkv_inproj_kernel(*refs, n_pt, scale, n_prev, forget, layers, kinds):
    x_ref, g_ref, w_ref = refs[:3]
    refs = refs[3:]
    if forget:
        wf_ref, bf_ref = refs[:2]
        refs = refs[2:]
    refs = refs[n_prev:]
    q_ref, kp_ref, ks_ref, vp_ref, vs_ref, kb_ref, vb_ref = refs[:7]
    d = D_MODEL

    def project(prompt):
        xn = _rms(x_ref[...], g_ref[...]).astype(BF16)
        q = jnp.dot(xn, w_ref[:, 0:d], preferred_element_type=F32)
        q_ref[...] = (q * (scale * LOG2E)).astype(BF16)
        for c, new_p, new_s, bf_ref_out in ((1, kp_ref, ks_ref, kb_ref), (2, vp_ref, vs_ref, vb_ref)):
            val = jnp.dot(xn, w_ref[:, c * d:(c + 1) * d], preferred_element_type=F32)
            if prompt:
                _store_prompt_cache(val, new_p, layers[c - 1], kinds[c - 1])
            else:
                new_s[...] = val
            bf_ref_out[...] = val.astype(BF16)
        if forget:
            z = jnp.dot(xn, wf_ref[...], preferred_element_type=F32) + bf_ref[...]
            lf = jnp.minimum(z, 0.0) - jnp.log1p(jnp.exp(-jnp.abs(z)))
            refs[7][...] = lf[:, 0:LANES]
            refs[8][...] = lf[:, LANES:LANES + FOX_HEADS]

    pl.when(pl.program_id(0) < n_pt)(lambda: project(True))
    pl.when(pl.program_id(0) >= n_pt)(lambda: project(False))


def _qkv_inproj(x, n_p, g, w, scale, k_slab, v_slab, prev=None, forget=()):
    n, d = x.shape
    tm = ROW_TILE
    n_pt = n_p // tm
    row = pl.BlockSpec((tm, d), lambda i: (i, 0))
    _, sample = _split_specs(n_pt, tm, d)
    prev = prev or {}

    def slab(out_index, shape, layer, kind):
        n_layers, first = (1, layer) if out_index in prev else (shape[0], 0)
        if kind == "dims":
            per_seq = shape[3] // tm
            return pl.BlockSpec((n_layers, 1, d, tm), lambda i: (first, jnp.minimum(i, n_pt - 1) // per_seq, 0,
                                                                 jnp.minimum(i, n_pt - 1) % per_seq))
        return pl.BlockSpec((n_layers, tm) + tuple(shape[2:]), lambda i: (first, jnp.minimum(i, n_pt - 1), 0, 0))

    f32_s = jax.ShapeDtypeStruct((n - n_p, d), F32)
    bf_all = jax.ShapeDtypeStruct((n, d), BF16)
    ins = [x, g, w] + list(forget) + list(prev.values())
    w_spec = pl.BlockSpec(w.shape, lambda i: (0, 0), pipeline_mode=pl.Buffered(1))
    in_specs = ([row, _full_spec((1, d)), w_spec] + [_full_spec(a.shape) for a in forget]
                + [pl.BlockSpec(memory_space=pl.ANY)] * len(prev))
    out_specs = [row, slab(1, *k_slab), sample, slab(3, *v_slab), sample, row, row]
    out_shape = [bf_all, jax.ShapeDtypeStruct(k_slab[0], F32), f32_s, jax.ShapeDtypeStruct(v_slab[0], F32), f32_s,
                 bf_all, bf_all]
    if forget:
        out_specs += [pl.BlockSpec((tm, LANES), lambda i: (i, 0)), pl.BlockSpec((tm, FOX_HEADS), lambda i: (i, 0))]
        out_shape += [jax.ShapeDtypeStruct((n, LANES), F32), jax.ShapeDtypeStruct((n, FOX_HEADS), F32)]
    first_prev = 3 + len(forget)
    aliases = {first_prev + pos: out_index for pos, out_index in enumerate(prev)}
    return pl.pallas_call(
        functools.partial(_qkv_inproj_kernel, n_pt=n_pt, scale=scale, n_prev=len(prev), forget=bool(forget),
                          layers=(k_slab[1], v_slab[1]), kinds=(k_slab[2], v_slab[2])),
        grid=(n // tm,),
        in_specs=in_specs,
        out_specs=out_specs,
        out_shape=out_shape,
        input_output_aliases=aliases,
        compiler_params=_cparams(1),
        name="fox_inproj" if forget else "diff_inproj",
    )(*ins)


def _conv_inproj_kernel(x_ref, g_ref, w_ref, b_ref, u_ref):
    d = D_MODEL
    xn = _rms(x_ref[...], g_ref[...]).astype(BF16)
    a = jnp.dot(xn, w_ref[:, 0:d], preferred_element_type=F32) + b_ref[:, 0:d]
    gate = jnp.dot(xn, w_ref[:, d:2 * d], preferred_element_type=F32) + b_ref[:, d:2 * d]
    u_ref[...] = a * (1.0 / (1.0 + jnp.exp(-gate)))


def _conv_inproj(x, g, w, b):
    n, d = x.shape
    tm = ROW_TILE
    row = pl.BlockSpec((tm, d), lambda i: (i, 0))
    return pl.pallas_call(
        _conv_inproj_kernel,
        grid=(n // tm,),
        in_specs=[row, _full_spec((1, d)), _full_spec(w.shape), _full_spec(b.shape)],
        out_specs=row,
        out_shape=jax.ShapeDtypeStruct((n, d), F32),
        compiler_params=_cparams(1),
        name="conv_inproj",
    )(x, g, w, b)


def _mix_mlp_kernel(x_ref, op_ref, os_ref, wo_ref, bo_ref, g_ref, w1_ref, w2_ref, gf_ref, *out_refs, n_pt, final):
    o = jnp.where(pl.program_id(0) < n_pt, op_ref[...], os_ref[...])
    x1 = x_ref[...] + jnp.dot(o, wo_ref[...], preferred_element_type=F32) + bo_ref[...]
    xn = _rms(x1, g_ref[...]).astype(BF16)
    acc = x1
    d_ff = w1_ref.shape[1]
    for c in range(d_ff // FF_CHUNK):
        h = jnp.dot(xn, w1_ref[:, c * FF_CHUNK:(c + 1) * FF_CHUNK], preferred_element_type=F32)
        h = jnp.square(jnp.maximum(h, 0.0)).astype(BF16)
        acc = acc + jnp.dot(h, w2_ref[c * FF_CHUNK:(c + 1) * FF_CHUNK, :], preferred_element_type=F32)
    if final:
        _split_store(n_pt, _rms(acc, gf_ref[...]), *out_refs)
    else:
        out_refs[0][...] = acc


def _mix_mlp(x, o_p, o_s, wo, bo, g, w1, w2, gf, final):
    n, d = x.shape
    tm = ROW_TILE
    n_p = o_p.shape[0]
    n_pt = n_p // tm
    row = pl.BlockSpec((tm, d), lambda i: (i, 0))
    prompt, sample = _split_specs(n_pt, tm, d)
    resident = lambda a: pl.BlockSpec(a.shape, lambda i: (0,) * a.ndim, pipeline_mode=pl.Buffered(1))
    if final:
        out_specs = [prompt, sample]
        out_shape = [jax.ShapeDtypeStruct((n_p, d), F32), jax.ShapeDtypeStruct((n - n_p, d), F32)]
    else:
        out_specs = [row]
        out_shape = [jax.ShapeDtypeStruct((n, d), F32)]
    return pl.pallas_call(
        functools.partial(_mix_mlp_kernel, n_pt=n_pt, final=final),
        grid=(n // tm,),
        in_specs=[row, prompt, sample, resident(wo), _full_spec(bo.shape), _full_spec(g.shape), resident(w1),
                  resident(w2), _full_spec(gf.shape)],
        out_specs=out_specs,
        out_shape=out_shape,
        compiler_params=_cparams(1),
        name="mix_mlp",
    )(x, o_p, o_s, wo, bo, g, w1, w2, gf)


def _t5_bias(tab_ref, h, qpos, kpos):
    half = REL_BUCKETS // 2
    n = qpos - kpos
    off = jnp.where(n < 0, half, 0)
    n = jnp.abs(n)
    large = jnp.full(n.shape, half // 2, jnp.int32)
    for t in _T5_STEPS:
        large = large + jnp.where(n >= t, 1, 0)
    bucket = off + jnp.where(n < half // 2, n, large)
    far = tab_ref[half - 1, h]
    out = jnp.zeros(n.shape, F32)
    for b in range(REL_BUCKETS):
        out = jnp.where(bucket == b, (tab_ref[b, h] - far) * LOG2E, out)
    visible = (kpos // CHUNK) <= (qpos // CHUNK)
    return jnp.where(visible, out, NEG)


def _prompt_tiles_kernel(tab_ref, bt_ref, cm_ref, *, tq):
    h = pl.program_id(0)
    for v in range(2):
        kpos = lax.broadcasted_iota(jnp.int32, (tq, tq), 0)
        qpos = lax.broadcasted_iota(jnp.int32, (tq, tq), 1) + (1 - v) * tq
        bt_ref[0, v] = _t5_bias(tab_ref, h, qpos, kpos)

    @pl.when(h == 0)
    def _():
        kpos = lax.broadcasted_iota(jnp.int32, (tq, tq), 0)
        qpos = lax.broadcasted_iota(jnp.int32, (tq, tq), 1)
        cm_ref[0, 0] = jnp.zeros((tq, tq), F32)
        cm_ref[0, 1] = jnp.where(kpos <= qpos, 0.0, NEG)


def _prompt_tiles(table, tq):
    return pl.pallas_call(
        functools.partial(_prompt_tiles_kernel, tq=tq),
        grid=(DIFF_HEADS,),
        in_specs=[pl.BlockSpec(memory_space=pltpu.SMEM)],
        out_specs=[pl.BlockSpec((1, 2, tq, tq), lambda h: (h, 0, 0, 0)),
                   pl.BlockSpec((1, 2, tq, tq), lambda h: (0, 0, 0, 0))],
        out_shape=[jax.ShapeDtypeStruct((DIFF_HEADS, 2, tq, tq), F32), jax.ShapeDtypeStruct((1, 2, tq, tq), F32)],
        compiler_params=_cparams(1),
        name="prompt_score_tiles",
    )(table)


def _sample_tiles_kernel(tab_ref, dl_ref, dn_ref, fn_ref, *, t_new, past, tk):
    rows = DIFF_HEADS * 2 * t_new
    for h in range(DIFF_HEADS):
        r0 = h * 2 * t_new
        q = lax.broadcasted_iota(jnp.int32, (2 * t_new, tk), 0) % t_new + past
        k = lax.broadcasted_iota(jnp.int32, (2 * t_new, tk), 1) + (past - tk)
        dl_ref[r0:r0 + 2 * t_new, :] = _t5_bias(tab_ref, h, q, k)
        q = lax.broadcasted_iota(jnp.int32, (2 * t_new, t_new), 0) % t_new + past
        k = lax.broadcasted_iota(jnp.int32, (2 * t_new, t_new), 1) + past
        dn_ref[r0:r0 + 2 * t_new, :] = _t5_bias(tab_ref, h, q, k)
    q = lax.broadcasted_iota(jnp.int32, (rows, t_new), 0) % t_new
    k = lax.broadcasted_iota(jnp.int32, (rows, t_new), 1)
    fn_ref[...] = jnp.where(k <= q, 0.0, NEG)


def _sample_tiles(table, t_new, past, tk):
    rows = DIFF_HEADS * 2 * t_new
    assert rows == FOX_HEADS * t_new
    return pl.pallas_call(
        functools.partial(_sample_tiles_kernel, t_new=t_new, past=past, tk=tk),
        in_specs=[pl.BlockSpec(memory_space=pltpu.SMEM)],
        out_shape=[jax.ShapeDtypeStruct((rows, tk), F32), jax.ShapeDtypeStruct((rows, t_new), F32),
                   jax.ShapeDtypeStruct((rows, t_new), F32)],
        name="sample_score_tiles",
    )(table)


def _lambda(lqk_ref, lam_init):
    a = jnp.sum(lqk_ref[0:1, :] * lqk_ref[1:2, :], axis=-1, keepdims=True)
    b = jnp.sum(lqk_ref[2:3, :] * lqk_ref[3:4, :], axis=-1, keepdims=True)
    return jnp.exp(a) - jnp.exp(b) + lam_init


def _scores_t(k, q2_ref, s_out):
    for c0 in range(0, q2_ref.shape[0], ATT_STRIP):
        cols = slice(c0, c0 + ATT_STRIP)
        s_out[:, cols] = lax.dot_general(k, q2_ref[cols, :], (((1,), (1,)), ((), ())), preferred_element_type=F32)


def _reduce_rows(op, x):
    w, n = x.shape
    part = op(x.reshape(w // REDUCE_ROWS, REDUCE_ROWS, n), axis=0)
    return op(part, axis=0, keepdims=True)


def _softmax_pv_t(s_in, v, bias_t, bias_rows, m_ref, l_ref, acc_ref):
    w = s_in.shape[0]
    for c0 in range(0, s_in.shape[1], ATT_STRIP):
        cols = slice(c0, c0 + ATT_STRIP)
        s = s_in[:, cols]
        if bias_t is not None:
            b0 = c0 % bias_t.shape[1]
            r0 = w - bias_rows
            biased = s[r0:, :] + bias_t[r0:, b0:b0 + ATT_STRIP]
            s = biased if r0 == 0 else jnp.concatenate([s[:r0, :], biased], axis=0)
        m_prev = m_ref[:, cols]
        m_new = jnp.maximum(m_prev, _reduce_rows(jnp.max, s))
        alpha = jnp.exp2(m_prev - m_new)
        p = jnp.exp2(s - m_new)
        l_ref[:, cols] = alpha * l_ref[:, cols] + _reduce_rows(jnp.sum, p)
        pv = lax.dot_general(v, p.astype(BF16), (((0,), (0,)), ((), ())), preferred_element_type=F32)
        acc_ref[:, cols] = alpha * acc_ref[:, cols] + pv
        m_ref[:, cols] = m_new


def _prompt_attn_kernel(*refs, fox, tq, seq, lam_init):
    if fox:
        (q_ref, qa_ref, k_ref, ka_ref, v_ref, bt_ref, o_ref, q2_ref, s0_ref, s1_ref, m_ref, l_ref, acc_ref) = refs
    else:
        (q_ref, k_ref, v_ref, bt_ref, lqk_ref, sg_ref, o_ref, q2_ref, s0_ref, s1_ref, m_ref, l_ref, acc_ref) = refs
    s_refs = (s0_ref, s1_ref)
    n_groups = q_ref.shape[1] // LANES
    lane = lax.broadcasted_iota(jnp.int32, (tq, LANES), 1)
    lo = lane < LANES // 2
    first_rows = lax.broadcasted_iota(jnp.int32, (LANES, tq), 0) < LANES // 2
    lanes = [slice(g * LANES, (g + 1) * LANES) for g in range(n_groups)]
    if fox:
        in_a, in_b = [], []
        for g in range(n_groups):
            c0 = 12 * (pl.program_id(1) * n_groups + g)
            in_a.append(jnp.logical_and(lane >= c0, lane < c0 + 6))
            in_b.append(jnp.logical_and(lane >= c0 + 6, lane < c0 + 12))

    def scores_g(g, j, slot):
        k0 = pl.multiple_of(jnp.maximum(j, 0) * tq, tq)
        k = k_ref[pl.ds(k0, tq), lanes[g]]
        if fox:
            k = jnp.concatenate([k, ka_ref[pl.ds(k0, tq), :]], axis=1)
        _scores_t(k, q2_ref.at[g], s_refs[slot].at[g])

    def softmax_pv_g(g, j, slot, bias_index):
        k0 = pl.multiple_of(j * tq, tq)
        if bias_index is None or (fox and bias_index == 0):
            bias_t, bias_rows = None, 0
        else:
            bias_t = bt_ref.at[0 if fox else g, bias_index]
            bias_rows = tq if bias_index == 1 else T5_NEAR
        _softmax_pv_t(s_refs[slot].at[g], v_ref[pl.ds(k0, tq), lanes[g]], bias_t, bias_rows, m_ref.at[g],
                      l_ref.at[g], acc_ref.at[g])

    def scores(j, slot):
        for g in range(n_groups):
            scores_g(g, j, slot)

    def softmax_pv(j, slot, bias_index):
        for g in range(n_groups):
            softmax_pv_g(g, j, slot, bias_index)

    def step(j_next, slot_next, j, slot, bias_index):
        for g in range(n_groups):
            scores_g(g, j_next, slot_next)
            softmax_pv_g(g, j, slot, bias_index)

    def q_block(i, carry):
        r0 = pl.multiple_of(i * tq, tq)
        for g in range(n_groups):
            q = q_ref[pl.ds(r0, tq), lanes[g]]
            zero = jnp.zeros_like(q)
            q2_ref[g, 0:tq, 0:LANES] = jnp.where(lo, q, zero)
            q2_ref[g, tq:2 * tq, 0:LANES] = jnp.where(lo, zero, q)
            if fox:
                qa = qa_ref[pl.ds(r0, tq), :]
                q2_ref[g, 0:tq, LANES:2 * LANES] = jnp.where(in_a[g], qa, zero)
                q2_ref[g, tq:2 * tq, LANES:2 * LANES] = jnp.where(in_b[g], qa, zero)
        m_ref[...] = jnp.full(m_ref.shape, NEG, F32)
        l_ref[...] = jnp.zeros(l_ref.shape, F32)
        acc_ref[...] = jnp.zeros(acc_ref.shape, F32)

        @pl.when(i == 0)
        def _():
            scores(i, 0)
            softmax_pv(i, 0, 1)

        @pl.when(i >= 1)
        def _():
            scores(i, 0)
            step(i - 1, 1, i, 0, 1)
            step(i - 2, 0, i - 1, 1, 0)
            n_far = i - 1

            def pair(u, c):
                a = i - 2 - 2 * u
                step(a - 1, 1, a, 0, None)
                step(a - 2, 0, a - 1, 1, None)
                return c

            lax.fori_loop(0, n_far // 2, pair, 0)

            @pl.when(n_far % 2 == 1)
            def _():
                softmax_pv(0, 0, None)

        for g in range(n_groups):
            top = acc_ref[g, :, 0:tq] / l_ref[g, :, 0:tq]
            bot = acc_ref[g, :, tq:2 * tq] / l_ref[g, :, tq:2 * tq]
            if fox:
                o_t = jnp.where(first_rows, top, bot)
            else:
                o_t = top - _lambda(lqk_ref, lam_init) * bot
                ms = jnp.mean(o_t * o_t, axis=0, keepdims=True)
                o_t = o_t * lax.rsqrt(ms + EPS) * sg_ref[...] * (1.0 - lam_init)
            o_ref[pl.ds(r0, tq), lanes[g]] = o_t.T.astype(BF16)
        return carry

    lax.fori_loop(0, seq // tq, q_block, 0)


def _prompt_attn(fox, batch, seq, q, k, v, bt, extra, lam_init=0.0):
    tq = ATT_TQ
    ng = ATT_GROUPS
    steps = D_MODEL // (LANES * ng)
    blk = pl.BlockSpec((seq, ng * LANES), lambda b, h: (b, h))
    if fox:
        qa, ka = extra
        aug = pl.BlockSpec((seq, LANES), lambda b, h: (b, 0))
        ins = [q, qa, k, ka, v, bt]
        in_specs = [blk, aug, blk, aug, blk, pl.BlockSpec((1, 2, tq, tq), lambda b, h: (0, 0, 0, 0))]
        kd = 2 * LANES
    else:
        lqk, sg = extra
        ins = [q, k, v, bt, lqk, sg]
        in_specs = [blk, blk, blk, pl.BlockSpec((ng, 2, tq, tq), lambda b, h: (h, 0, 0, 0)),
                    _full_spec(lqk.shape), _full_spec(sg.shape)]
        kd = LANES
    return pl.pallas_call(
        functools.partial(_prompt_attn_kernel, fox=fox, tq=tq, seq=seq, lam_init=lam_init),
        grid=(batch, steps),
        in_specs=in_specs,
        out_specs=blk,
        out_shape=jax.ShapeDtypeStruct((batch * seq, D_MODEL), BF16),
        scratch_shapes=[pltpu.VMEM((ng, 2 * tq, kd), BF16), pltpu.VMEM((ng, tq, 2 * tq), F32),
                        pltpu.VMEM((ng, tq, 2 * tq), F32), pltpu.VMEM((ng, 1, 2 * tq), F32),
                        pltpu.VMEM((ng, 1, 2 * tq), F32), pltpu.VMEM((ng, LANES, 2 * tq), F32)],
        compiler_params=_cparams(2),
        name="fox_prompt_attn" if fox else "diff_prompt_attn",
    )(*ins)


def _online_update(s, pv_fn, m_ref, l_ref, acc_ref):
    m_prev = m_ref[...]
    m_new = jnp.maximum(m_prev, jnp.max(s, axis=-1, keepdims=True))
    alpha = jnp.exp2(m_prev - m_new)
    p = jnp.exp2(s - m_new)
    l_ref[...] = alpha * l_ref[...] + jnp.sum(p, axis=-1, keepdims=True)
    acc_ref[...] = alpha * acc_ref[...] + pv_fn(p.astype(BF16))
    m_ref[...] = m_new


def _sample_attn_kernel(*refs, fox, t_new, tk, n_tiles, lam_init):
    if fox:
        (q_ref, qa_ref, kt_ref, kat_ref, vt_ref, kn_ref, kan_ref, vn_ref, bn_ref,
         o_ref, qbd_ref, m_ref, l_ref, acc_ref) = refs
    else:
        (q_ref, kt_ref, v_ref, kn_ref, vn_ref, bl_ref, bn_ref, lqk_ref, sg_ref,
         o_ref, qbd_ref, m_ref, l_ref, acc_ref) = refs
    t = pl.program_id(1)
    n_blocks = D_MODEL // 64
    rows_per_head = 2 * t_new

    @pl.when(t == 0)
    def _():
        q = q_ref[...]
        zero = jnp.zeros_like(q)
        cb = lax.broadcasted_iota(jnp.int32, q.shape, 1) // 64
        if fox:
            qa = qa_ref[0]
            la = lax.broadcasted_iota(jnp.int32, qa.shape, 1)
        for rb in range(n_blocks):
            qbd_ref[rb * t_new:(rb + 1) * t_new, 0:D_MODEL] = jnp.where(cb == rb, q, zero)
            if fox:
                sel = jnp.logical_and(la >= 6 * rb, la < 6 * rb + 6)
                qbd_ref[rb * t_new:(rb + 1) * t_new, D_MODEL:D_MODEL + LANES] = jnp.where(sel, qa, jnp.zeros_like(qa))
        m_ref[...] = jnp.full(m_ref.shape, NEG, F32)
        l_ref[...] = jnp.zeros(l_ref.shape, F32)
        acc_ref[...] = jnp.zeros(acc_ref.shape, F32)

    n_groups = D_MODEL // SAMPLE_GROUP
    group_rows = (SAMPLE_GROUP // 64) * t_new
    groups = [(slice(g * group_rows, (g + 1) * group_rows), slice(g * SAMPLE_GROUP, (g + 1) * SAMPLE_GROUP))
              for g in range(n_groups)]

    def cache_scores():
        s = jnp.concatenate([jnp.dot(qbd_ref[rows, dims], kt_ref[0, 0, dims, :].astype(BF16),
                                     preferred_element_type=F32) for rows, dims in groups], axis=0)
        if fox:
            s = s + jnp.dot(qbd_ref[:, D_MODEL:D_MODEL + LANES], kat_ref[0], preferred_element_type=F32)
        return s

    if fox:
        def cache_pv(p):
            return jnp.concatenate(
                [lax.dot_general(p[rows, :], vt_ref[0, 0, dims, :].astype(BF16), (((1,), (1,)), ((), ())),
                                 preferred_element_type=F32) for rows, dims in groups], axis=0)

        def new_pv(p):
            return jnp.concatenate([jnp.dot(p[rows, :], vn_ref[:, dims], preferred_element_type=F32)
                                    for rows, dims in groups], axis=0)
    else:
        def cache_pv(p):
            return jnp.concatenate(
                [jnp.dot(p[h * rows_per_head:(h + 1) * rows_per_head, :],
                         v_ref[0, 0, pl.ds(h, tk, stride=DIFF_HEADS), :].astype(BF16), preferred_element_type=F32)
                 for h in range(DIFF_HEADS)], axis=0)

        def new_pv(p):
            return jnp.concatenate(
                [jnp.dot(p[h * rows_per_head:(h + 1) * rows_per_head, :], vn_ref[:, h * LANES:(h + 1) * LANES],
                         preferred_element_type=F32) for h in range(DIFF_HEADS)], axis=0)

    @pl.when(t < n_tiles - 1)
    def _():
        _online_update(cache_scores(), cache_pv, m_ref, l_ref, acc_ref)

    @pl.when(t == n_tiles - 1)
    def _():
        s = cache_scores()
        if not fox:
            s = s + bl_ref[...]
        _online_update(s, cache_pv, m_ref, l_ref, acc_ref)
        kn = kn_ref[...]
        if fox:
            kn = jnp.concatenate([kn, kan_ref[0]], axis=1)
        s = lax.dot_general(qbd_ref[...], kn, (((1,), (1,)), ((), ())), preferred_element_type=F32) + bn_ref[...]
        _online_update(s, new_pv, m_ref, l_ref, acc_ref)

        lane = lax.broadcasted_iota(jnp.int32, (t_new, LANES), 1)
        lo = lane < LANES // 2
        for g in range(D_MODEL // LANES):
            ra, rb = 2 * g * t_new, (2 * g + 1) * t_new
            cols = slice(g * LANES, (g + 1) * LANES)
            c0 = (g * LANES) % SAMPLE_GROUP if fox else 0
            acc_cols = slice(c0, c0 + LANES)
            top = acc_ref[ra:ra + t_new, acc_cols] / l_ref[ra:ra + t_new, :]
            bot = acc_ref[rb:rb + t_new, acc_cols] / l_ref[rb:rb + t_new, :]
            if fox:
                o = jnp.where(lo, top, bot)
            else:
                o = _rms(top - _lambda(lqk_ref, lam_init) * bot, sg_ref[...]) * (1.0 - lam_init)
            o_ref[:, cols] = o.astype(BF16)


def _sample_attn(fox, layer, n_prompt, dec_batch, t_new, past, q, kt, v, kn, vn, bias_last, bias_new, extra,
                 lam_init=0.0):
    tk = SAMPLE_TK
    n_tiles = past // tk
    d = D_MODEL
    rows = (d // 64) * t_new
    s0 = n_prompt // t_new
    new = pl.BlockSpec((t_new, d), lambda b, t: (s0 + b, 0))
    dims_keys = pl.BlockSpec((1, 1, d, tk), lambda b, t: (layer, b, 0, t))
    out = pl.BlockSpec((t_new, d), lambda b, t: (b, 0))
    if fox:
        qa, ka, kat = extra
        new_aug = pl.BlockSpec((1, t_new, LANES), lambda b, t: (b, past // t_new, 0))
        ins = [q, qa, kt, kat, v, kn, ka, vn, bias_new]
        in_specs = [new, new_aug, dims_keys, pl.BlockSpec((1, LANES, tk), lambda b, t: (b, 0, t)), dims_keys,
                    new, new_aug, new, _full_spec(bias_new.shape)]
        kd, acc_w = d + LANES, SAMPLE_GROUP
    else:
        lqk, sg = extra
        ins = [q, kt, v, kn, vn, bias_last, bias_new, lqk, sg]
        in_specs = [new, dims_keys, pl.BlockSpec((1, 1, tk * DIFF_HEADS, LANES), lambda b, t: (layer, b, t, 0)),
                    new, new, _full_spec(bias_last.shape), _full_spec(bias_new.shape),
                    _full_spec(lqk.shape), _full_spec(sg.shape)]
        kd, acc_w = d, LANES
    return pl.pallas_call(
        functools.partial(_sample_attn_kernel, fox=fox, t_new=t_new, tk=tk, n_tiles=n_tiles, lam_init=lam_init),
        grid=(dec_batch, n_tiles),
        in_specs=in_specs,
        out_specs=out,
        out_shape=jax.ShapeDtypeStruct((dec_batch * t_new, d), BF16),
        scratch_shapes=[pltpu.VMEM((rows, kd), BF16), pltpu.VMEM((rows, 1), F32),
                        pltpu.VMEM((rows, 1), F32), pltpu.VMEM((rows, acc_w), F32)],
        compiler_params=_cparams(2),
        name="fox_sample_attn" if fox else "diff_sample_attn",
    )(*ins)


def _split3(c):
    hi = c.astype(BF16)
    r = c - hi.astype(F32)
    mid = r.astype(BF16)
    lo = (r - mid.astype(F32)).astype(BF16)
    return hi, mid, lo


def _decay_cols_kernel(lf_ref, part_ref, qa_ref, ka_ref, *maybe_kat_ref, length):
    part = part_ref[0:1, :]
    which = part_ref[1:2, :]
    carry = jnp.zeros((1, LANES), F32)
    for r in range(0, length, CUM_BLOCK):
        n = min(CUM_BLOCK, length - r)
        tri = (lax.broadcasted_iota(jnp.int32, (n, n), 1) <= lax.broadcasted_iota(jnp.int32, (n, n), 0))
        tri = jnp.where(tri, 1.0, 0.0).astype(BF16)
        c = carry
        for p in _split3(lf_ref[0, r:r + n, :]):
            c = c + jnp.dot(tri, p, preferred_element_type=F32)
        carry = c[n - 1:n, :]
        hi, mid, lo = (t.astype(F32) for t in _split3(c * LOG2E))
        terms = jnp.where(which == 0, hi, jnp.where(which == 1, mid, lo))
        qa = jnp.where(part < 0, 0.0, jnp.where(part < 3, terms, 1.0))
        ka = jnp.where(part < 0, 0.0, jnp.where(part < 3, 1.0, -terms))
        qa_ref[0, r:r + n, :] = qa.astype(BF16)
        ka_ref[0, r:r + n, :] = ka.astype(BF16)
        if maybe_kat_ref:
            maybe_kat_ref[0][0, :, r:r + n] = ka.T.astype(BF16)


def _decay_layout():
    part = np.full((2, LANES), -1, np.int32)
    used = np.arange(FOX_HEADS * DECAY_COLS) % DECAY_COLS
    part[0, :used.size] = used
    part[1, :used.size] = used % 3
    return jnp.asarray(part)


def _spread_heads(a):
    rep = jnp.repeat(a, DECAY_COLS, axis=-1)
    return jnp.pad(rep, [(0, 0)] * (a.ndim - 1) + [(0, LANES - rep.shape[-1])])


def _decay_cols(lf, keys_transposed):
    nb, length, _ = lf.shape
    part = _decay_layout()
    spec = pl.BlockSpec((1, length, LANES), lambda b: (b, 0, 0))
    out_specs = [spec, spec]
    out_shape = [jax.ShapeDtypeStruct((nb, length, LANES), BF16)] * 2
    if keys_transposed:
        out_specs.append(pl.BlockSpec((1, LANES, length), lambda b: (b, 0, 0)))
        out_shape.append(jax.ShapeDtypeStruct((nb, LANES, length), BF16))
    return pl.pallas_call(
        functools.partial(_decay_cols_kernel, length=length),
        grid=(nb,),
        in_specs=[spec, _full_spec(part.shape)],
        out_specs=out_specs,
        out_shape=out_shape,
        compiler_params=_cparams(1),
        name="fox_decay_cols",
    )(lf, part)


def _conv_dw_kernel(u_ref, halo_ref, wdw_ref, bdw_ref, g_ref, b_ref, o_ref, ext_ref, sh_ref, y_ref, *, tm,
                    steps_per_seq):
    halo = halo_ref[...]
    if steps_per_seq:
        halo = jnp.where(pl.program_id(0) % steps_per_seq == 0, 0.0, halo)
    ext_ref[0:HALO, :] = halo
    ext_ref[HALO:HALO + tm, :] = u_ref[...]
    span = sh_ref.shape[1]
    for s in range(1, SUBLANES):
        sh_ref[s - 1] = ext_ref[s:s + span, :]
    d = u_ref.shape[1]
    first = HALO - (CONV_WIDTH - 1)
    for rc in range(tm // CONV_RC):
        for lc in range(d // CONV_LC):
            cols = slice(lc * CONV_LC, (lc + 1) * CONV_LC)
            acc = jnp.zeros((CONV_RC, CONV_LC), F32)
            for w in range(CONV_WIDTH):
                a, s = divmod(first + w, SUBLANES)
                src = ext_ref if s == 0 else sh_ref.at[s - 1]
                r = rc * CONV_RC + a * SUBLANES
                acc = acc + src[r:r + CONV_RC, cols] * wdw_ref[w:w + 1, cols]
            y_ref[rc * CONV_RC:(rc + 1) * CONV_RC, cols] = acc
    y = y_ref[...] + bdw_ref[...]
    yc = y - jnp.mean(y, axis=-1, keepdims=True)
    yn = yc * lax.rsqrt(jnp.mean(yc * yc, axis=-1, keepdims=True) + EPS) * g_ref[...] + b_ref[...]
    o_ref[...] = (yn * (1.0 / (1.0 + jnp.exp(-yn)))).astype(BF16)


def _conv_dw(u, row0, n_rows, tm, halo, halo_index, steps_per_seq, wdw, bdw, g, b):
    d = u.shape[1]
    b0 = row0 // tm
    return pl.pallas_call(
        functools.partial(_conv_dw_kernel, tm=tm, steps_per_seq=steps_per_seq),
        grid=(n_rows // tm,),
        in_specs=[pl.BlockSpec((tm, d), lambda i: (b0 + i, 0)), pl.BlockSpec((HALO, d), lambda i: (halo_index(i), 0)),
                  _full_spec(wdw.shape), _full_spec(bdw.shape), _full_spec(g.shape), _full_spec(b.shape)],
        out_specs=pl.BlockSpec((tm, d), lambda i: (i, 0)),
        out_shape=jax.ShapeDtypeStruct((n_rows, d), BF16),
        scratch_shapes=[pltpu.VMEM((HALO + tm, d), F32), pltpu.VMEM((SUBLANES - 1, HALO + tm - SUBLANES, d), F32),
                        pltpu.VMEM((tm, d), F32)],
        compiler_params=_cparams(1),
        name="conv_dw",
    )(u, halo, wdw, bdw, g, b)


def _dims_keys(cache):
    nd = cache.ndim
    t = jnp.transpose(cache, (0, 1) + tuple(range(3, nd)) + (2,))
    return t.reshape(cache.shape[0], cache.shape[1], D_MODEL, cache.shape[2])


def kernel(x_prompt, x_sample, cache_diff_k, cache_diff_v, state_conv, cache_fox_k, cache_fox_v, cache_fox_logf, rel_bias, norm_g, final_g, diff_w_in, diff_w_out, diff_lq1, diff_lk1, diff_lq2, diff_lk2, diff_subln_g, conv_w_pw1, conv_b_pw1, conv_w_dw, conv_b_dw, conv_ln_g, conv_ln_b, conv_w_pw2, conv_b_pw2, fox_w_in, fox_b_f, fox_w_out, mlp_w1, mlp_w2):
    B, S, D = x_prompt.shape
    Bd, T, _ = x_sample.shape
    P = cache_diff_k.shape[2]
    depth = norm_g.shape[0]
    n_p, n_s = B * S, Bd * T
    assert D == D_MODEL and S % (2 * ATT_TQ) == 0 and P % SAMPLE_TK == 0 and n_p % ROW_TILE == 0
    assert n_s % ROW_TILE == 0 and T % 8 == 0 and T >= CONV_WIDTH - 1 and S % ROW_TILE == 0

    x = jnp.concatenate([x_prompt.reshape(n_p, D), x_sample.reshape(n_s, D)], axis=0)
    zero_bias = jnp.zeros((1, D), F32)
    final_gain = final_g.reshape(1, D)

    bt_diff, bt_fox = _prompt_tiles(rel_bias, ATT_TQ)
    sb_last, sb_new, sf_new = _sample_tiles(rel_bias, T, P, SAMPLE_TK)

    diff_kt = _dims_keys(cache_diff_k)
    diff_v = cache_diff_v.reshape(cache_diff_v.shape[0], Bd, P * DIFF_HEADS, 2 * DIFF_HEAD_DIM)
    fox_kt = _dims_keys(cache_fox_k)
    fox_vt = _dims_keys(cache_fox_v)

    n_diff, n_fox = diff_w_in.shape[0], fox_w_in.shape[0]
    dk_shape = (n_diff, B, D, S)
    dv_shape = (n_diff, n_p, DIFF_HEADS, 2 * DIFF_HEAD_DIM)
    fkv_shape = (n_fox, B, D, S)
    dk_new, dv_new, fk_new, fv_new = None, None, None, None
    dk_s, dv_s = [], []
    cv_p, cv_s = [], []
    fl_p, fk_s, fv_s, fl_s = [], [], [], []
    y = None

    for i in range(depth):
        kind, j = i % N_MIXERS, i // N_MIXERS
        g1 = norm_g[i, 0].reshape(1, D)
        g2 = norm_g[i, 1].reshape(1, D)
        if kind == 0:
            lam_init = 0.8 - 0.6 * math.exp(-0.3 * i)
            q, dk_new, k_s, dv_new, v_s, kb, vb = _qkv_inproj(
                x, n_p, g1, diff_w_in[j].astype(BF16), DIFF_SCALE, (dk_shape, j, "dims"), (dv_shape, j, "heads"),
                prev=None if dv_new is None else {1: dk_new, 3: dv_new})
            lqk = jnp.stack([diff_lq1[j], diff_lk1[j], diff_lq2[j], diff_lk2[j]]).astype(F32)
            sg = diff_subln_g[j].reshape(1, 2 * DIFF_HEAD_DIM).astype(F32)
            o_p = _prompt_attn(False, B, S, q, kb, vb, bt_diff, (lqk, sg.reshape(2 * DIFF_HEAD_DIM, 1)), lam_init)
            o_s = _sample_attn(False, j, n_p, Bd, T, P, q, diff_kt, diff_v, kb, vb, sb_last, sb_new, (lqk, sg),
                               lam_init)
            wo, bo = diff_w_out[j].astype(BF16), zero_bias
            dk_s.append(k_s.reshape(Bd, T, DIFF_HEADS, 2, DIFF_HEAD_DIM))
            dv_s.append(v_s.reshape(Bd, T, DIFF_HEADS, 2 * DIFF_HEAD_DIM))
        elif kind == 1:
            u = _conv_inproj(x, g1, conv_w_pw1[j].astype(BF16), conv_b_pw1[j].reshape(1, 2 * D))
            wdw = jnp.pad(conv_w_dw[j], ((0, HALO - CONV_WIDTH), (0, 0)))
            args = (wdw, conv_b_dw[j].reshape(1, D), conv_ln_g[j].reshape(1, D), conv_ln_b[j].reshape(1, D))
            per_tile = ROW_TILE // HALO
            o_p = _conv_dw(u, 0, n_p, ROW_TILE, u, lambda t: jnp.maximum(t * per_tile - 1, 0), S // ROW_TILE, *args)
            state = jnp.pad(state_conv[j], ((0, 0), (HALO - (CONV_WIDTH - 1), 0), (0, 0))).reshape(Bd * HALO, D)
            o_s = _conv_dw(u, n_p, n_s, T, state, lambda t: t, 0, *args)
            wo, bo = conv_w_pw2[j].astype(BF16), conv_b_pw2[j].reshape(1, D)
            keep = CONV_WIDTH - 1
            cv_p.append(jnp.stack([u[(b + 1) * S - keep:(b + 1) * S] for b in range(B)]))
            cv_s.append(u[n_p:].reshape(Bd, T, D)[:, T - keep:])
        else:
            w_in = fox_w_in[j]
            pad_heads = lambda a: jnp.pad(a, [(0, 0)] * (a.ndim - 1) + [(0, LANES - FOX_HEADS)])
            wf = jnp.concatenate([_spread_heads(w_in[:, 3 * D:]), pad_heads(w_in[:, 3 * D:])], axis=1).astype(BF16)
            b_f = fox_b_f[j].astype(F32).reshape(1, FOX_HEADS)
            bf = jnp.concatenate([_spread_heads(b_f), pad_heads(b_f)], axis=1)
            q, fk_new, k_s, fv_new, v_s, kb, vb, lf, lf_heads = _qkv_inproj(
                x, n_p, g1, w_in[:, :3 * D].astype(BF16), FOX_SCALE, (fkv_shape, j, "dims"), (fkv_shape, j, "dims"),
                prev=None if fk_new is None else {1: fk_new, 3: fv_new}, forget=(wf, bf))
            lf_p_cols = lf[:n_p].reshape(B, S, LANES)
            lf_s_cols = lf[n_p:].reshape(Bd, T, LANES)
            lf_p = lf_heads[:n_p].reshape(B, S, FOX_HEADS)
            lf_s = lf_heads[n_p:].reshape(Bd, T, FOX_HEADS)
            qa_p, ka_p = _decay_cols(lf_p_cols, False)
            cache_cols = _spread_heads(cache_fox_logf[j].astype(F32))
            decay_s = _decay_cols(jnp.concatenate([cache_cols, lf_s_cols], axis=1), True)
            o_p = _prompt_attn(True, B, S, q, kb, vb, bt_fox, (qa_p.reshape(n_p, LANES), ka_p.reshape(n_p, LANES)))
            o_s = _sample_attn(True, j, n_p, Bd, T, P, q, fox_kt, fox_vt, kb, vb, None, sf_new, decay_s)
            wo, bo = fox_w_out[j].astype(BF16), zero_bias
            fl_p.append(lf_p)
            fk_s.append(k_s.reshape(Bd, T, FOX_HEADS, FOX_HEAD_DIM))
            fv_s.append(v_s.reshape(Bd, T, FOX_HEADS, FOX_HEAD_DIM))
            fl_s.append(lf_s)
        final = i == depth - 1
        out = _mix_mlp(x, o_p, o_s, wo, bo, g2, mlp_w1[i].astype(BF16), mlp_w2[i].astype(BF16), final_gain, final)
        if final:
            y = out
        else:
            x = out[0]

    y_prompt = y[0].reshape(B, S, D)
    y_sample = y[1].reshape(Bd, T, D)
    new_diff_k_p = jnp.transpose(dk_new.reshape(n_diff, B, DIFF_HEADS, 2, DIFF_HEAD_DIM, S), (0, 1, 5, 2, 3, 4))
    new_diff_v_p = dv_new.reshape(n_diff, B, S, DIFF_HEADS, 2 * DIFF_HEAD_DIM)
    new_fox_k_p = jnp.transpose(fk_new.reshape(n_fox, B, FOX_HEADS, FOX_HEAD_DIM, S), (0, 1, 4, 2, 3))
    new_fox_v_p = jnp.transpose(fv_new.reshape(n_fox, B, FOX_HEADS, FOX_HEAD_DIM, S), (0, 1, 4, 2, 3))
    return (y_prompt, y_sample, new_diff_k_p, new_diff_v_p, jnp.stack(cv_p), new_fox_k_p,
            new_fox_v_p, jnp.stack(fl_p), jnp.stack(dk_s), jnp.stack(dv_s), jnp.stack(cv_s),
            jnp.stack(fk_s), jnp.stack(fv_s), jnp.stack(fl_s))
```

```python
import functools
import math

import numpy as np
import jax
import jax.numpy as jnp
from jax import lax
from jax.experimental import pallas as pl
from jax.experimental.pallas import tpu as pltpu

F32 = jnp.float32
BF16 = jnp.bfloat16

D_MODEL = 1024
CHUNK = 64
EPS = 1e-6
DIFF_HEADS = 8
DIFF_HEAD_DIM = 64
LOG2E = math.log2(math.e)
DIFF_SCALE = DIFF_HEAD_DIM ** -0.5
CONV_WIDTH = 31
FOX_HEADS = 16
FOX_HEAD_DIM = 64
FOX_SCALE = FOX_HEAD_DIM ** -0.5
REL_BUCKETS = 32
N_MIXERS = 3

LANES = 128
SUBLANES = 8
NEG = -1e30
ROW_TILE = 512
ATT_TQ = 512
ATT_STRIP = 512
ATT_GROUPS = 2
REDUCE_ROWS = 64
SAMPLE_TK = 2048
SAMPLE_GROUP = 256
FF_CHUNK = 1024
CUM_BLOCK = 256
DECAY_COLS = 6
HALO = 32
CONV_RC = 32
CONV_LC = 256
VMEM_LIMIT = 56 * 1024 * 1024

_T5_STEPS = (12, 16, 23, 32, 46, 64, 91)
T5_NEAR = 128
assert _T5_STEPS[-1] <= T5_NEAR


def _cparams(n_axes):
    return pltpu.CompilerParams(dimension_semantics=("arbitrary",) * n_axes, vmem_limit_bytes=VMEM_LIMIT)


def _rms(x, g):
    return x * lax.rsqrt(jnp.mean(x * x, axis=-1, keepdims=True) + EPS) * g


def _full_spec(shape):
    nd = len(shape)
    return pl.BlockSpec(shape, lambda *_: (0,) * nd)


def _split_specs(n_prompt_tiles, tm, d):
    prompt = pl.BlockSpec((tm, d), lambda i: (jnp.minimum(i, n_prompt_tiles - 1), 0))
    sample = pl.BlockSpec((tm, d), lambda i: (jnp.maximum(i - n_prompt_tiles, 0), 0))
    return prompt, sample


def _row_tile(n_prompt_tiles, prompt_ref, sample_ref):
    return jnp.where(pl.program_id(0) < n_prompt_tiles, prompt_ref[...], sample_ref[...])


def _split_store(n_prompt_tiles, value, prompt_ref, sample_ref):
    i = pl.program_id(0)

    @pl.when(i < n_prompt_tiles)
    def _():
        prompt_ref[...] = value

    @pl.when(i >= n_prompt_tiles)
    def _():
        sample_ref[...] = value


def _store_prompt_cache(value, prompt_ref, layer, kind):
    layer = layer if prompt_ref.shape[0] > 1 else 0
    for other in range(prompt_ref.shape[0]):
        if other != layer:
            prompt_ref[other] = jnp.zeros(prompt_ref.shape[1:], prompt_ref.dtype)
    if kind == "dims":
        prompt_ref[layer, 0] = value.T
    else:
        prompt_ref[layer] = pltpu.einshape("m(hd)->mhd", value, h=prompt_ref.shape[2])


def _qkv_inproj_kernel(*refs, n_pt, scale, n_prev, forget, layers, kinds):
    xp_ref, xs_ref, g_ref, w_ref = refs[:4]
    refs = refs[4:]
    if forget:
        wf_ref, bf_ref = refs[:2]
        refs = refs[2:]
    refs = refs[n_prev:]
    q_ref, kp_ref, ks_ref, vp_ref, vs_ref, kb_ref, vb_ref = refs[:7]
    d = D_MODEL

    def project(prompt):
        xn = _rms((xp_ref if prompt else xs_ref)[...], g_ref[...]).astype(BF16)
        q = jnp.dot(xn, w_ref[:, 0:d], preferred_element_type=F32)
        q_ref[...] = (q * (scale * LOG2E)).astype(BF16)
        for c, new_p, new_s, bf_ref_out in ((1, kp_ref, ks_ref, kb_ref), (2, vp_ref, vs_ref, vb_ref)):
            val = jnp.dot(xn, w_ref[:, c * d:(c + 1) * d], preferred_element_type=F32)
            if prompt:
                _store_prompt_cache(val, new_p, layers[c - 1], kinds[c - 1])
            else:
                new_s[...] = val
            bf_ref_out[...] = val.astype(BF16)
        if forget:
            z = jnp.dot(xn, wf_ref[...], preferred_element_type=F32) + bf_ref[...]
            lf = jnp.minimum(z, 0.0) - jnp.log1p(jnp.exp(-jnp.abs(z)))
            refs[7][...] = lf[:, 0:LANES]
            refs[8][...] = lf[:, LANES:LANES + FOX_HEADS]

    pl.when(pl.program_id(0) < n_pt)(lambda: project(True))
    pl.when(pl.program_id(0) >= n_pt)(lambda: project(False))


def _qkv_inproj(x, g, w, scale, k_slab, v_slab, prev=None, forget=()):
    n_p, d = x[0].shape
    n = n_p + x[1].shape[0]
    tm = ROW_TILE
    n_pt = n_p // tm
    row = pl.BlockSpec((tm, d), lambda i: (i, 0))
    prompt, sample = _split_specs(n_pt, tm, d)
    prev = prev or {}

    def slab(out_index, shape, layer, kind):
        n_layers, first = (1, layer) if out_index in prev else (shape[0], 0)
        if kind == "dims":
            per_seq = shape[3] // tm
            return pl.BlockSpec((n_layers, 1, d, tm), lambda i: (first, jnp.minimum(i, n_pt - 1) // per_seq, 0,
                                                                 jnp.minimum(i, n_pt - 1) % per_seq))
        return pl.BlockSpec((n_layers, tm) + tuple(shape[2:]), lambda i: (first, jnp.minimum(i, n_pt - 1), 0, 0))

    f32_s = jax.ShapeDtypeStruct((n - n_p, d), F32)
    bf_all = jax.ShapeDtypeStruct((n, d), BF16)
    ins = [*x, g, w] + list(forget) + list(prev.values())
    w_spec = pl.BlockSpec(w.shape, lambda i: (0, 0), pipeline_mode=pl.Buffered(1))
    in_specs = ([prompt, sample, _full_spec((1, d)), w_spec] + [_full_spec(a.shape) for a in forget]
                + [pl.BlockSpec(memory_space=pl.ANY)] * len(prev))
    out_specs = [row, slab(1, *k_slab), sample, slab(3, *v_slab), sample, row, row]
    out_shape = [bf_all, jax.ShapeDtypeStruct(k_slab[0], F32), f32_s, jax.ShapeDtypeStruct(v_slab[0], F32), f32_s,
                 bf_all, bf_all]
    if forget:
        out_specs += [pl.BlockSpec((tm, LANES), lambda i: (i, 0)), pl.BlockSpec((tm, FOX_HEADS), lambda i: (i, 0))]
        out_shape += [jax.ShapeDtypeStruct((n, LANES), F32), jax.ShapeDtypeStruct((n, FOX_HEADS), F32)]
    first_prev = 4 + len(forget)
    aliases = {first_prev + pos: out_index for pos, out_index in enumerate(prev)}
    return pl.pallas_call(
        functools.partial(_qkv_inproj_kernel, n_pt=n_pt, scale=scale, n_prev=len(prev), forget=bool(forget),
                          layers=(k_slab[1], v_slab[1]), kinds=(k_slab[2], v_slab[2])),
        grid=(n // tm,),
        in_specs=in_specs,
        out_specs=out_specs,
        out_shape=out_shape,
        input_output_aliases=aliases,
        compiler_params=_cparams(1),
        name="fox_inproj" if forget else "diff_inproj",
    )(*ins)


def _conv_inproj_kernel(xp_ref, xs_ref, g_ref, w_ref, b_ref, u_ref, *, n_pt):
    d = D_MODEL
    xn = _rms(_row_tile(n_pt, xp_ref, xs_ref), g_ref[...]).astype(BF16)
    a = jnp.dot(xn, w_ref[:, 0:d], preferred_element_type=F32) + b_ref[:, 0:d]
    gate = jnp.dot(xn, w_ref[:, d:2 * d], preferred_element_type=F32) + b_ref[:, d:2 * d]
    u_ref[...] = a * (1.0 / (1.0 + jnp.exp(-gate)))


def _conv_inproj(x, g, w, b):
    (n_p, d), n = x[0].shape, x[0].shape[0] + x[1].shape[0]
    tm = ROW_TILE
    row = pl.BlockSpec((tm, d), lambda i: (i, 0))
    return pl.pallas_call(
        functools.partial(_conv_inproj_kernel, n_pt=n_p // tm),
        grid=(n // tm,),
        in_specs=[*_split_specs(n_p // tm, tm, d), _full_spec((1, d)), _full_spec(w.shape), _full_spec(b.shape)],
        out_specs=row,
        out_shape=jax.ShapeDtypeStruct((n, d), F32),
        compiler_params=_cparams(1),
        name="conv_inproj",
    )(*x, g, w, b)


def _mix_mlp_kernel(xp_ref, xs_ref, op_ref, os_ref, wo_ref, bo_ref, g_ref, w1_ref, w2_ref, gf_ref, yp_ref, ys_ref, *,
                    n_pt, final):
    o = _row_tile(n_pt, op_ref, os_ref)
    x1 = _row_tile(n_pt, xp_ref, xs_ref) + jnp.dot(o, wo_ref[...], preferred_element_type=F32) + bo_ref[...]
    xn = _rms(x1, g_ref[...]).astype(BF16)
    acc = x1
    d_ff = w1_ref.shape[1]
    for c in range(d_ff // FF_CHUNK):
        h = jnp.dot(xn, w1_ref[:, c * FF_CHUNK:(c + 1) * FF_CHUNK], preferred_element_type=F32)
        h = jnp.square(jnp.maximum(h, 0.0)).astype(BF16)
        acc = acc + jnp.dot(h, w2_ref[c * FF_CHUNK:(c + 1) * FF_CHUNK, :], preferred_element_type=F32)
    _split_store(n_pt, _rms(acc, gf_ref[...]) if final else acc, yp_ref, ys_ref)


def _mix_mlp(x, o_p, o_s, wo, bo, g, w1, w2, gf, final):
    n_p, d = x[0].shape
    n = n_p + x[1].shape[0]
    tm = ROW_TILE
    n_pt = n_p // tm
    prompt, sample = _split_specs(n_pt, tm, d)
    resident = lambda a: pl.BlockSpec(a.shape, lambda i: (0,) * a.ndim, pipeline_mode=pl.Buffered(1))
    return pl.pallas_call(
        functools.partial(_mix_mlp_kernel, n_pt=n_pt, final=final),
        grid=(n // tm,),
        in_specs=[prompt, sample, prompt, sample, resident(wo), _full_spec(bo.shape), _full_spec(g.shape),
                  resident(w1), resident(w2), _full_spec(gf.shape)],
        out_specs=[prompt, sample],
        out_shape=[jax.ShapeDtypeStruct((n_p, d), F32), jax.ShapeDtypeStruct((n - n_p, d), F32)],
        compiler_params=_cparams(1),
        name="mix_mlp",
    )(*x, o_p, o_s, wo, bo, g, w1, w2, gf)


def _t5_bias(tab_ref, h, qpos, kpos):
    half = REL_BUCKETS // 2
    n = qpos - kpos
    off = jnp.where(n < 0, half, 0)
    n = jnp.abs(n)
    large = jnp.full(n.shape, half // 2, jnp.int32)
    for t in _T5_STEPS:
        large = large + jnp.where(n >= t, 1, 0)
    bucket = off + jnp.where(n < half // 2, n, large)
    far = tab_ref[half - 1, h]
    out = jnp.zeros(n.shape, F32)
    for b in range(REL_BUCKETS):
        out = jnp.where(bucket == b, (tab_ref[b, h] - far) * LOG2E, out)
    visible = (kpos // CHUNK) <= (qpos // CHUNK)
    return jnp.where(visible, out, NEG)


def _prompt_tiles_kernel(tab_ref, bt_ref, cm_ref, *, tq):
    h = pl.program_id(0)
    for v in range(2):
        kpos = lax.broadcasted_iota(jnp.int32, (tq, tq), 0)
        qpos = lax.broadcasted_iota(jnp.int32, (tq, tq), 1) + (1 - v) * tq
        bt_ref[0, v] = _t5_bias(tab_ref, h, qpos, kpos)

    @pl.when(h == 0)
    def _():
        kpos = lax.broadcasted_iota(jnp.int32, (tq, tq), 0)
        qpos = lax.broadcasted_iota(jnp.int32, (tq, tq), 1)
        cm_ref[0, 0] = jnp.zeros((tq, tq), F32)
        cm_ref[0, 1] = jnp.where(kpos <= qpos, 0.0, NEG)


def _prompt_tiles(table, tq):
    return pl.pallas_call(
        functools.partial(_prompt_tiles_kernel, tq=tq),
        grid=(DIFF_HEADS,),
        in_specs=[pl.BlockSpec(memory_space=pltpu.SMEM)],
        out_specs=[pl.BlockSpec((1, 2, tq, tq), lambda h: (h, 0, 0, 0)),
                   pl.BlockSpec((1, 2, tq, tq), lambda h: (0, 0, 0, 0))],
        out_shape=[jax.ShapeDtypeStruct((DIFF_HEADS, 2, tq, tq), F32), jax.ShapeDtypeStruct((1, 2, tq, tq), F32)],
        compiler_params=_cparams(1),
        name="prompt_score_tiles",
    )(table)


def _sample_tiles_kernel(tab_ref, dl_ref, dn_ref, fn_ref, *, t_new, past, tk):
    rows = DIFF_HEADS * 2 * t_new
    for h in range(DIFF_HEADS):
        r0 = h * 2 * t_new
        q = lax.broadcasted_iota(jnp.int32, (2 * t_new, tk), 0) % t_new + past
        k = lax.broadcasted_iota(jnp.int32, (2 * t_new, tk), 1) + (past - tk)
        dl_ref[r0:r0 + 2 * t_new, :] = _t5_bias(tab_ref, h, q, k)
        q = lax.broadcasted_iota(jnp.int32, (2 * t_new, t_new), 0) % t_new + past
        k = lax.broadcasted_iota(jnp.int32, (2 * t_new, t_new), 1) + past
        dn_ref[r0:r0 + 2 * t_new, :] = _t5_bias(tab_ref, h, q, k)
    q = lax.broadcasted_iota(jnp.int32, (rows, t_new), 0) % t_new
    k = lax.broadcasted_iota(jnp.int32, (rows, t_new), 1)
    fn_ref[...] = jnp.where(k <= q, 0.0, NEG)


def _sample_tiles(table, t_new, past, tk):
    rows = DIFF_HEADS * 2 * t_new
    assert rows == FOX_HEADS * t_new
    return pl.pallas_call(
        functools.partial(_sample_tiles_kernel, t_new=t_new, past=past, tk=tk),
        in_specs=[pl.BlockSpec(memory_space=pltpu.SMEM)],
        out_shape=[jax.ShapeDtypeStruct((rows, tk), F32), jax.ShapeDtypeStruct((rows, t_new), F32),
                   jax.ShapeDtypeStruct((rows, t_new), F32)],
        name="sample_score_tiles",
    )(table)


def _lambda(lqk_ref, lam_init):
    a = jnp.sum(lqk_ref[0:1, :] * lqk_ref[1:2, :], axis=-1, keepdims=True)
    b = jnp.sum(lqk_ref[2:3, :] * lqk_ref[3:4, :], axis=-1, keepdims=True)
    return jnp.exp(a) - jnp.exp(b) + lam_init


def _scores_t(k, q2_ref, s_out):
    for c0 in range(0, q2_ref.shape[0], ATT_STRIP):
        cols = slice(c0, c0 + ATT_STRIP)
        s_out[:, cols] = lax.dot_general(k, q2_ref[cols, :], (((1,), (1,)), ((), ())), preferred_element_type=F32)


def _reduce_rows(op, x):
    w, n = x.shape
    part = op(x.reshape(w // REDUCE_ROWS, REDUCE_ROWS, n), axis=0)
    return op(part, axis=0, keepdims=True)


def _softmax_pv_t(s_in, v, bias_t, bias_rows, m_ref, l_ref, acc_ref):
    w = s_in.shape[0]
    for c0 in range(0, s_in.shape[1], ATT_STRIP):
        cols = slice(c0, c0 + ATT_STRIP)
        s = s_in[:, cols]
        if bias_t is not None:
            b0 = c0 % bias_t.shape[1]
            r0 = w - bias_rows
            biased = s[r0:, :] + bias_t[r0:, b0:b0 + ATT_STRIP]
            s = biased if r0 == 0 else jnp.concatenate([s[:r0, :], biased], axis=0)
        m_prev = m_ref[:, cols]
        m_new = jnp.maximum(m_prev, _reduce_rows(jnp.max, s))
        alpha = jnp.exp2(m_prev - m_new)
        p = jnp.exp2(s - m_new)
        l_ref[:, cols] = alpha * l_ref[:, cols] + _reduce_rows(jnp.sum, p)
        pv = lax.dot_general(v, p.astype(BF16), (((0,), (0,)), ((), ())), preferred_element_type=F32)
        acc_ref[:, cols] = alpha * acc_ref[:, cols] + pv
        m_ref[:, cols] = m_new


def _prompt_attn_kernel(*refs, fox, tq, seq, lam_init):
    if fox:
        (q_ref, qa_ref, k_ref, ka_ref, v_ref, bt_ref, o_ref, q2_ref, s0_ref, s1_ref, m_ref, l_ref, acc_ref) = refs
    else:
        (q_ref, k_ref, v_ref, bt_ref, lqk_ref, sg_ref, o_ref, q2_ref, s0_ref, s1_ref, m_ref, l_ref, acc_ref) = refs
    s_refs = (s0_ref, s1_ref)
    n_groups = q_ref.shape[1] // LANES
    lane = lax.broadcasted_iota(jnp.int32, (tq, LANES), 1)
    lo = lane < LANES // 2
    first_rows = lax.broadcasted_iota(jnp.int32, (LANES, tq), 0) < LANES // 2
    lanes = [slice(g * LANES, (g + 1) * LANES) for g in range(n_groups)]
    if fox:
        in_a, in_b = [], []
        for g in range(n_groups):
            c0 = 12 * (pl.program_id(1) * n_groups + g)
            in_a.append(jnp.logical_and(lane >= c0, lane < c0 + 6))
            in_b.append(jnp.logical_and(lane >= c0 + 6, lane < c0 + 12))

    def scores_g(g, j, slot):
        k0 = pl.multiple_of(jnp.maximum(j, 0) * tq, tq)
        k = k_ref[pl.ds(k0, tq), lanes[g]]
        if fox:
            k = jnp.concatenate([k, ka_ref[pl.ds(k0, tq), :]], axis=1)
        _scores_t(k, q2_ref.at[g], s_refs[slot].at[g])

    def softmax_pv_g(g, j, slot, bias_index):
        k0 = pl.multiple_of(j * tq, tq)
        if bias_index is None or (fox and bias_index == 0):
            bias_t, bias_rows = None, 0
        else:
            bias_t = bt_ref.at[0 if fox else g, bias_index]
            bias_rows = tq if bias_index == 1 else T5_NEAR
        _softmax_pv_t(s_refs[slot].at[g], v_ref[pl.ds(k0, tq), lanes[g]], bias_t, bias_rows, m_ref.at[g],
                      l_ref.at[g], acc_ref.at[g])

    def scores(j, slot):
        for g in range(n_groups):
            scores_g(g, j, slot)

    def softmax_pv(j, slot, bias_index):
        for g in range(n_groups):
            softmax_pv_g(g, j, slot, bias_index)

    def step(j_next, slot_next, j, slot, bias_index):
        for g in range(n_groups):
            scores_g(g, j_next, slot_next)
            softmax_pv_g(g, j, slot, bias_index)

    def q_block(i, carry):
        r0 = pl.multiple_of(i * tq, tq)
        for g in range(n_groups):
            q = q_ref[pl.ds(r0, tq), lanes[g]]
            zero = jnp.zeros_like(q)
            q2_ref[g, 0:tq, 0:LANES] = jnp.where(lo, q, zero)
            q2_ref[g, tq:2 * tq, 0:LANES] = jnp.where(lo, zero, q)
            if fox:
                qa = qa_ref[pl.ds(r0, tq), :]
                q2_ref[g, 0:tq, LANES:2 * LANES] = jnp.where(in_a[g], qa, zero)
                q2_ref[g, tq:2 * tq, LANES:2 * LANES] = jnp.where(in_b[g], qa, zero)
        m_ref[...] = jnp.full(m_ref.shape, NEG, F32)
        l_ref[...] = jnp.zeros(l_ref.shape, F32)
        acc_ref[...] = jnp.zeros(acc_ref.shape, F32)

        @pl.when(i == 0)
        def _():
            scores(i, 0)
            softmax_pv(i, 0, 1)

        @pl.when(i >= 1)
        def _():
            scores(i, 0)
            step(i - 1, 1, i, 0, 1)
            step(i - 2, 0, i - 1, 1, 0)
            n_far = i - 1

            def pair(u, c):
                a = i - 2 - 2 * u
                step(a - 1, 1, a, 0, None)
                step(a - 2, 0, a - 1, 1, None)
                return c

            lax.fori_loop(0, n_far // 2, pair, 0)

            @pl.when(n_far % 2 == 1)
            def _():
                softmax_pv(0, 0, None)

        for g in range(n_groups):
            top = acc_ref[g, :, 0:tq] / l_ref[g, :, 0:tq]
            bot = acc_ref[g, :, tq:2 * tq] / l_ref[g, :, tq:2 * tq]
            if fox:
                o_t = jnp.where(first_rows, top, bot)
            else:
                o_t = top - _lambda(lqk_ref, lam_init) * bot
                ms = jnp.mean(o_t * o_t, axis=0, keepdims=True)
                o_t = o_t * lax.rsqrt(ms + EPS) * sg_ref[...] * (1.0 - lam_init)
            o_ref[pl.ds(r0, tq), lanes[g]] = o_t.T.astype(BF16)
        return carry

    lax.fori_loop(0, seq // tq, q_block, 0)


def _prompt_attn(fox, batch, seq, q, k, v, bt, extra, lam_init=0.0):
    tq = ATT_TQ
    ng = ATT_GROUPS
    steps = D_MODEL // (LANES * ng)
    blk = pl.BlockSpec((seq, ng * LANES), lambda b, h: (b, h))
    if fox:
        qa, ka = extra
        aug = pl.BlockSpec((seq, LANES), lambda b, h: (b, 0))
        ins = [q, qa, k, ka, v, bt]
        in_specs = [blk, aug, blk, aug, blk, pl.BlockSpec((1, 2, tq, tq), lambda b, h: (0, 0, 0, 0))]
        kd = 2 * LANES
    else:
        lqk, sg = extra
        ins = [q, k, v, bt, lqk, sg]
        in_specs = [blk, blk, blk, pl.BlockSpec((ng, 2, tq, tq), lambda b, h: (h, 0, 0, 0)),
                    _full_spec(lqk.shape), _full_spec(sg.shape)]
        kd = LANES
    return pl.pallas_call(
        functools.partial(_prompt_attn_kernel, fox=fox, tq=tq, seq=seq, lam_init=lam_init),
        grid=(batch, steps),
        in_specs=in_specs,
        out_specs=blk,
        out_shape=jax.ShapeDtypeStruct((batch * seq, D_MODEL), BF16),
        scratch_shapes=[pltpu.VMEM((ng, 2 * tq, kd), BF16), pltpu.VMEM((ng, tq, 2 * tq), F32),
                        pltpu.VMEM((ng, tq, 2 * tq), F32), pltpu.VMEM((ng, 1, 2 * tq), F32),
                        pltpu.VMEM((ng, 1, 2 * tq), F32), pltpu.VMEM((ng, LANES, 2 * tq), F32)],
        compiler_params=_cparams(2),
        name="fox_prompt_attn" if fox else "diff_prompt_attn",
    )(*ins)


def _online_update(s, pv_fn, m_ref, l_ref, acc_ref):
    m_prev = m_ref[...]
    m_new = jnp.maximum(m_prev, jnp.max(s, axis=-1, keepdims=True))
    alpha = jnp.exp2(m_prev - m_new)
    p = jnp.exp2(s - m_new)
    l_ref[...] = alpha * l_ref[...] + jnp.sum(p, axis=-1, keepdims=True)
    acc_ref[...] = alpha * acc_ref[...] + pv_fn(p.astype(BF16))
    m_ref[...] = m_new


def _sample_attn_kernel(*refs, fox, t_new, tk, n_tiles, lam_init):
    if fox:
        (q_ref, qa_ref, kt_ref, kat_ref, vt_ref, kn_ref, kan_ref, vn_ref, bn_ref,
         o_ref, qbd_ref, m_ref, l_ref, acc_ref) = refs
    else:
        (q_ref, kt_ref, v_ref, kn_ref, vn_ref, bl_ref, bn_ref, lqk_ref, sg_ref,
         o_ref, qbd_ref, m_ref, l_ref, acc_ref) = refs
    t = pl.program_id(1)
    n_blocks = D_MODEL // 64
    rows_per_head = 2 * t_new

    @pl.when(t == 0)
    def _():
        q = q_ref[...]
        zero = jnp.zeros_like(q)
        cb = lax.broadcasted_iota(jnp.int32, q.shape, 1) // 64
        if fox:
            qa = qa_ref[0]
            la = lax.broadcasted_iota(jnp.int32, qa.shape, 1)
        for rb in range(n_blocks):
            qbd_ref[rb * t_new:(rb + 1) * t_new, 0:D_MODEL] = jnp.where(cb == rb, q, zero)
            if fox:
                sel = jnp.logical_and(la >= 6 * rb, la < 6 * rb + 6)
                qbd_ref[rb * t_new:(rb + 1) * t_new, D_MODEL:D_MODEL + LANES] = jnp.where(sel, qa, jnp.zeros_like(qa))
        m_ref[...] = jnp.full(m_ref.shape, NEG, F32)
        l_ref[...] = jnp.zeros(l_ref.shape, F32)
        acc_ref[...] = jnp.zeros(acc_ref.shape, F32)

    n_groups = D_MODEL // SAMPLE_GROUP
    group_rows = (SAMPLE_GROUP // 64) * t_new
    groups = [(slice(g * group_rows, (g + 1) * group_rows), slice(g * SAMPLE_GROUP, (g + 1) * SAMPLE_GROUP))
              for g in range(n_groups)]

    def cache_scores():
        s = jnp.concatenate([jnp.dot(qbd_ref[rows, dims], kt_ref[0, 0, dims, :].astype(BF16),
                                     preferred_element_type=F32) for rows, dims in groups], axis=0)
        if fox:
            s = s + jnp.dot(qbd_ref[:, D_MODEL:D_MODEL + LANES], kat_ref[0], preferred_element_type=F32)
        return s

    if fox:
        def cache_pv(p):
            return jnp.concatenate(
                [lax.dot_general(p[rows, :], vt_ref[0, 0, dims, :].astype(BF16), (((1,), (1,)), ((), ())),
                                 preferred_element_type=F32) for rows, dims in groups], axis=0)

        def new_pv(p):
            return jnp.concatenate([jnp.dot(p[rows, :], vn_ref[:, dims], preferred_element_type=F32)
                                    for rows, dims in groups], axis=0)
    else:
        def cache_pv(p):
            return jnp.concatenate(
                [jnp.dot(p[h * rows_per_head:(h + 1) * rows_per_head, :],
                         v_ref[0, 0, pl.ds(h, tk, stride=DIFF_HEADS), :].astype(BF16), preferred_element_type=F32)
                 for h in range(DIFF_HEADS)], axis=0)

        def new_pv(p):
            return jnp.concatenate(
                [jnp.dot(p[h * rows_per_head:(h + 1) * rows_per_head, :], vn_ref[:, h * LANES:(h + 1) * LANES],
                         preferred_element_type=F32) for h in range(DIFF_HEADS)], axis=0)

    @pl.when(t < n_tiles - 1)
    def _():
        _online_update(cache_scores(), cache_pv, m_ref, l_ref, acc_ref)

    @pl.when(t == n_tiles - 1)
    def _():
        s = cache_scores()
        if not fox:
            s = s + bl_ref[...]
        _online_update(s, cache_pv, m_ref, l_ref, acc_ref)
        kn = kn_ref[...]
        if fox:
            kn = jnp.concatenate([kn, kan_ref[0]], axis=1)
        s = lax.dot_general(qbd_ref[...], kn, (((1,), (1,)), ((), ())), preferred_element_type=F32) + bn_ref[...]
        _online_update(s, new_pv, m_ref, l_ref, acc_ref)

        lane = lax.broadcasted_iota(jnp.int32, (t_new, LANES), 1)
        lo = lane < LANES // 2
        for g in range(D_MODEL // LANES):
            ra, rb = 2 * g * t_new, (2 * g + 1) * t_new
            cols = slice(g * LANES, (g + 1) * LANES)
            c0 = (g * LANES) % SAMPLE_GROUP if fox else 0
            acc_cols = slice(c0, c0 + LANES)
            top = acc_ref[ra:ra + t_new, acc_cols] / l_ref[ra:ra + t_new, :]
            bot = acc_ref[rb:rb + t_new, acc_cols] / l_ref[rb:rb + t_new, :]
            if fox:
                o = jnp.where(lo, top, bot)
            else:
                o = _rms(top - _lambda(lqk_ref, lam_init) * bot, sg_ref[...]) * (1.0 - lam_init)
            o_ref[:, cols] = o.astype(BF16)


def _sample_attn(fox, layer, n_prompt, dec_batch, t_new, past, q, kt, v, kn, vn, bias_last, bias_new, extra,
                 lam_init=0.0):
    tk = SAMPLE_TK
    n_tiles = past // tk
    d = D_MODEL
    rows = (d // 64) * t_new
    s0 = n_prompt // t_new
    new = pl.BlockSpec((t_new, d), lambda b, t: (s0 + b, 0))
    dims_keys = pl.BlockSpec((1, 1, d, tk), lambda b, t: (layer, b, 0, t))
    out = pl.BlockSpec((t_new, d), lambda b, t: (b, 0))
    if fox:
        qa, ka, kat = extra
        new_aug = pl.BlockSpec((1, t_new, LANES), lambda b, t: (b, past // t_new, 0))
        ins = [q, qa, kt, kat, v, kn, ka, vn, bias_new]
        in_specs = [new, new_aug, dims_keys, pl.BlockSpec((1, LANES, tk), lambda b, t: (b, 0, t)), dims_keys,
                    new, new_aug, new, _full_spec(bias_new.shape)]
        kd, acc_w = d + LANES, SAMPLE_GROUP
    else:
        lqk, sg = extra
        ins = [q, kt, v, kn, vn, bias_last, bias_new, lqk, sg]
        in_specs = [new, dims_keys, pl.BlockSpec((1, 1, tk * DIFF_HEADS, LANES), lambda b, t: (layer, b, t, 0)),
                    new, new, _full_spec(bias_last.shape), _full_spec(bias_new.shape),
                    _full_spec(lqk.shape), _full_spec(sg.shape)]
        kd, acc_w = d, LANES
    return pl.pallas_call(
        functools.partial(_sample_attn_kernel, fox=fox, t_new=t_new, tk=tk, n_tiles=n_tiles, lam_init=lam_init),
        grid=(dec_batch, n_tiles),
        in_specs=in_specs,
        out_specs=out,
        out_shape=jax.ShapeDtypeStruct((dec_batch * t_new, d), BF16),
        scratch_shapes=[pltpu.VMEM((rows, kd), BF16), pltpu.VMEM((rows, 1), F32),
                        pltpu.VMEM((rows, 1), F32), pltpu.VMEM((rows, acc_w), F32)],
        compiler_params=_cparams(2),
        name="fox_sample_attn" if fox else "diff_sample_attn",
    )(*ins)


def _split3(c):
    hi = c.astype(BF16)
    r = c - hi.astype(F32)
    mid = r.astype(BF16)
    lo = (r - mid.astype(F32)).astype(BF16)
    return hi, mid, lo


def _decay_cols_kernel(lf_ref, part_ref, qa_ref, ka_ref, *maybe_kat_ref, length):
    part = part_ref[0:1, :]
    which = part_ref[1:2, :]
    carry = jnp.zeros((1, LANES), F32)
    for r in range(0, length, CUM_BLOCK):
        n = min(CUM_BLOCK, length - r)
        tri = (lax.broadcasted_iota(jnp.int32, (n, n), 1) <= lax.broadcasted_iota(jnp.int32, (n, n), 0))
        tri = jnp.where(tri, 1.0, 0.0).astype(BF16)
        c = carry
        for p in _split3(lf_ref[0, r:r + n, :]):
            c = c + jnp.dot(tri, p, preferred_element_type=F32)
        carry = c[n - 1:n, :]
        hi, mid, lo = (t.astype(F32) for t in _split3(c * LOG2E))
        terms = jnp.where(which == 0, hi, jnp.where(which == 1, mid, lo))
        qa = jnp.where(part < 0, 0.0, jnp.where(part < 3, terms, 1.0))
        ka = jnp.where(part < 0, 0.0, jnp.where(part < 3, 1.0, -terms))
        qa_ref[0, r:r + n, :] = qa.astype(BF16)
        ka_ref[0, r:r + n, :] = ka.astype(BF16)
        if maybe_kat_ref:
            maybe_kat_ref[0][0, :, r:r + n] = ka.T.astype(BF16)


def _decay_layout():
    part = np.full((2, LANES), -1, np.int32)
    used = np.arange(FOX_HEADS * DECAY_COLS) % DECAY_COLS
    part[0, :used.size] = used
    part[1, :used.size] = used % 3
    return jnp.asarray(part)


def _spread_heads(a):
    rep = jnp.repeat(a, DECAY_COLS, axis=-1)
    return jnp.pad(rep, [(0, 0)] * (a.ndim - 1) + [(0, LANES - rep.shape[-1])])


def _decay_cols(lf, keys_transposed):
    nb, length, _ = lf.shape
    part = _decay_layout()
    spec = pl.BlockSpec((1, length, LANES), lambda b: (b, 0, 0))
    out_specs = [spec, spec]
    out_shape = [jax.ShapeDtypeStruct((nb, length, LANES), BF16)] * 2
    if keys_transposed:
        out_specs.append(pl.BlockSpec((1, LANES, length), lambda b: (b, 0, 0)))
        out_shape.append(jax.ShapeDtypeStruct((nb, LANES, length), BF16))
    return pl.pallas_call(
        functools.partial(_decay_cols_kernel, length=length),
        grid=(nb,),
        in_specs=[spec, _full_spec(part.shape)],
        out_specs=out_specs,
        out_shape=out_shape,
        compiler_params=_cparams(1),
        name="fox_decay_cols",
    )(lf, part)


def _conv_dw_kernel(u_ref, halo_ref, wdw_ref, bdw_ref, g_ref, b_ref, o_ref, ext_ref, sh_ref, y_ref, *, tm,
                    steps_per_seq):
    halo = halo_ref[...]
    if steps_per_seq:
        halo = jnp.where(pl.program_id(0) % steps_per_seq == 0, 0.0, halo)
    ext_ref[0:HALO, :] = halo
    ext_ref[HALO:HALO + tm, :] = u_ref[...]
    span = sh_ref.shape[1]
    for s in range(1, SUBLANES):
        sh_ref[s - 1] = ext_ref[s:s + span, :]
    d = u_ref.shape[1]
    first = HALO - (CONV_WIDTH - 1)
    for rc in range(tm // CONV_RC):
        for lc in range(d // CONV_LC):
            cols = slice(lc * CONV_LC, (lc + 1) * CONV_LC)
            acc = jnp.zeros((CONV_RC, CONV_LC), F32)
            for w in range(CONV_WIDTH):
                a, s = divmod(first + w, SUBLANES)
                src = ext_ref if s == 0 else sh_ref.at[s - 1]
                r = rc * CONV_RC + a * SUBLANES
                acc = acc + src[r:r + CONV_RC, cols] * wdw_ref[w:w + 1, cols]
            y_ref[rc * CONV_RC:(rc + 1) * CONV_RC, cols] = acc
    y = y_ref[...] + bdw_ref[...]
    yc = y - jnp.mean(y, axis=-1, keepdims=True)
    yn = yc * lax.rsqrt(jnp.mean(yc * yc, axis=-1, keepdims=True) + EPS) * g_ref[...] + b_ref[...]
    o_ref[...] = (yn * (1.0 / (1.0 + jnp.exp(-yn)))).astype(BF16)


def _conv_dw(u, row0, n_rows, tm, halo, halo_index, steps_per_seq, wdw, bdw, g, b):
    d = u.shape[1]
    b0 = row0 // tm
    return pl.pallas_call(
        functools.partial(_conv_dw_kernel, tm=tm, steps_per_seq=steps_per_seq),
        grid=(n_rows // tm,),
        in_specs=[pl.BlockSpec((tm, d), lambda i: (b0 + i, 0)), pl.BlockSpec((HALO, d), lambda i: (halo_index(i), 0)),
                  _full_spec(wdw.shape), _full_spec(bdw.shape), _full_spec(g.shape), _full_spec(b.shape)],
        out_specs=pl.BlockSpec((tm, d), lambda i: (i, 0)),
        out_shape=jax.ShapeDtypeStruct((n_rows, d), BF16),
        scratch_shapes=[pltpu.VMEM((HALO + tm, d), F32), pltpu.VMEM((SUBLANES - 1, HALO + tm - SUBLANES, d), F32),
                        pltpu.VMEM((tm, d), F32)],
        compiler_params=_cparams(1),
        name="conv_dw",
    )(u, halo, wdw, bdw, g, b)


def _dims_keys(cache):
    nd = cache.ndim
    t = jnp.transpose(cache, (0, 1) + tuple(range(3, nd)) + (2,))
    return t.reshape(cache.shape[0], cache.shape[1], D_MODEL, cache.shape[2])


def kernel(x_prompt, x_sample, cache_diff_k, cache_diff_v, state_conv, cache_fox_k, cache_fox_v, cache_fox_logf, rel_bias, norm_g, final_g, diff_w_in, diff_w_out, diff_lq1, diff_lk1, diff_lq2, diff_lk2, diff_subln_g, conv_w_pw1, conv_b_pw1, conv_w_dw, conv_b_dw, conv_ln_g, conv_ln_b, conv_w_pw2, conv_b_pw2, fox_w_in, fox_b_f, fox_w_out, mlp_w1, mlp_w2):
    B, S, D = x_prompt.shape
    Bd, T, _ = x_sample.shape
    P = cache_diff_k.shape[2]
    depth = norm_g.shape[0]
    n_p, n_s = B * S, Bd * T
    assert D == D_MODEL and S % (2 * ATT_TQ) == 0 and P % SAMPLE_TK == 0 and n_p % ROW_TILE == 0
    assert n_s % ROW_TILE == 0 and T % 8 == 0 and T >= CONV_WIDTH - 1 and S % ROW_TILE == 0

    x = (x_prompt.reshape(n_p, D), x_sample.reshape(n_s, D))
    zero_bias = jnp.zeros((1, D), F32)
    final_gain = final_g.reshape(1, D)

    bt_diff, bt_fox = _prompt_tiles(rel_bias, ATT_TQ)
    sb_last, sb_new, sf_new = _sample_tiles(rel_bias, T, P, SAMPLE_TK)

    diff_kt = _dims_keys(cache_diff_k)
    diff_v = cache_diff_v.reshape(cache_diff_v.shape[0], Bd, P * DIFF_HEADS, 2 * DIFF_HEAD_DIM)
    fox_kt = _dims_keys(cache_fox_k)
    fox_vt = _dims_keys(cache_fox_v)

    n_diff, n_fox = diff_w_in.shape[0], fox_w_in.shape[0]
    dk_shape = (n_diff, B, D, S)
    dv_shape = (n_diff, n_p, DIFF_HEADS, 2 * DIFF_HEAD_DIM)
    fkv_shape = (n_fox, B, D, S)
    dk_new, dv_new, fk_new, fv_new = None, None, None, None
    dk_s, dv_s = [], []
    cv_p, cv_s = [], []
    fl_p, fk_s, fv_s, fl_s = [], [], [], []

    for i in range(depth):
        kind, j = i % N_MIXERS, i // N_MIXERS
        g1 = norm_g[i, 0].reshape(1, D)
        g2 = norm_g[i, 1].reshape(1, D)
        if kind == 0:
            lam_init = 0.8 - 0.6 * math.exp(-0.3 * i)
            q, dk_new, k_s, dv_new, v_s, kb, vb = _qkv_inproj(
                x, g1, diff_w_in[j].astype(BF16), DIFF_SCALE, (dk_shape, j, "dims"), (dv_shape, j, "heads"),
                prev=None if dv_new is None else {1: dk_new, 3: dv_new})
            lqk = jnp.stack([diff_lq1[j], diff_lk1[j], diff_lq2[j], diff_lk2[j]]).astype(F32)
            sg = diff_subln_g[j].reshape(1, 2 * DIFF_HEAD_DIM).astype(F32)
            o_p = _prompt_attn(False, B, S, q, kb, vb, bt_diff, (lqk, sg.reshape(2 * DIFF_HEAD_DIM, 1)), lam_init)
            o_s = _sample_attn(False, j, n_p, Bd, T, P, q, diff_kt, diff_v, kb, vb, sb_last, sb_new, (lqk, sg),
                               lam_init)
            wo, bo = diff_w_out[j].astype(BF16), zero_bias
            dk_s.append(k_s.reshape(Bd, T, DIFF_HEADS, 2, DIFF_HEAD_DIM))
            dv_s.append(v_s.reshape(Bd, T, DIFF_HEADS, 2 * DIFF_HEAD_DIM))
        elif kind == 1:
            u = _conv_inproj(x, g1, conv_w_pw1[j].astype(BF16), conv_b_pw1[j].reshape(1, 2 * D))
            wdw = jnp.pad(conv_w_dw[j], ((0, HALO - CONV_WIDTH), (0, 0)))
            args = (wdw, conv_b_dw[j].reshape(1, D), conv_ln_g[j].reshape(1, D), conv_ln_b[j].reshape(1, D))
            per_tile = ROW_TILE // HALO
            o_p = _conv_dw(u, 0, n_p, ROW_TILE, u, lambda t: jnp.maximum(t * per_tile - 1, 0), S // ROW_TILE, *args)
            state = jnp.pad(state_conv[j], ((0, 0), (HALO - (CONV_WIDTH - 1), 0), (0, 0))).reshape(Bd * HALO, D)
            o_s = _conv_dw(u, n_p, n_s, T, state, lambda t: t, 0, *args)
            wo, bo = conv_w_pw2[j].astype(BF16), conv_b_pw2[j].reshape(1, D)
            keep = CONV_WIDTH - 1
            cv_p.append(jnp.stack([u[(b + 1) * S - keep:(b + 1) * S] for b in range(B)]))
            cv_s.append(u[n_p:].reshape(Bd, T, D)[:, T - keep:])
        else:
            w_in = fox_w_in[j]
            pad_heads = lambda a: jnp.pad(a, [(0, 0)] * (a.ndim - 1) + [(0, LANES - FOX_HEADS)])
            wf = jnp.concatenate([_spread_heads(w_in[:, 3 * D:]), pad_heads(w_in[:, 3 * D:])], axis=1).astype(BF16)
            b_f = fox_b_f[j].astype(F32).reshape(1, FOX_HEADS)
            bf = jnp.concatenate([_spread_heads(b_f), pad_heads(b_f)], axis=1)
            q, fk_new, k_s, fv_new, v_s, kb, vb, lf, lf_heads = _qkv_inproj(
                x, g1, w_in[:, :3 * D].astype(BF16), FOX_SCALE, (fkv_shape, j, "dims"), (fkv_shape, j, "dims"),
                prev=None if fk_new is None else {1: fk_new, 3: fv_new}, forget=(wf, bf))
            lf_p_cols = lf[:n_p].reshape(B, S, LANES)
            lf_s_cols = lf[n_p:].reshape(Bd, T, LANES)
            lf_p = lf_heads[:n_p].reshape(B, S, FOX_HEADS)
            lf_s = lf_heads[n_p:].reshape(Bd, T, FOX_HEADS)
            qa_p, ka_p = _decay_cols(lf_p_cols, False)
            cache_cols = _spread_heads(cache_fox_logf[j].astype(F32))
            decay_s = _decay_cols(jnp.concatenate([cache_cols, lf_s_cols], axis=1), True)
            o_p = _prompt_attn(True, B, S, q, kb, vb, bt_fox, (qa_p.reshape(n_p, LANES), ka_p.reshape(n_p, LANES)))
            o_s = _sample_attn(True, j, n_p, Bd, T, P, q, fox_kt, fox_vt, kb, vb, None, sf_new, decay_s)
            wo, bo = fox_w_out[j].astype(BF16), zero_bias
            fl_p.append(lf_p)
            fk_s.append(k_s.reshape(Bd, T, FOX_HEADS, FOX_HEAD_DIM))
            fv_s.append(v_s.reshape(Bd, T, FOX_HEADS, FOX_HEAD_DIM))
            fl_s.append(lf_s)
        final = i == depth - 1
        x = _mix_mlp(x, o_p, o_s, wo, bo, g2, mlp_w1[i].astype(BF16), mlp_w2[i].astype(BF16), final_gain,
                     final=(i == depth - 1))

    y_prompt = x[0].reshape(B, S, D)
    y_sample = x[1].reshape(Bd, T, D)
    new_diff_k_p = jnp.transpose(dk_new.reshape(n_diff, B, DIFF_HEADS, 2, DIFF_HEAD_DIM, S), (0, 1, 5, 2, 3, 4))
    new_diff_v_p = dv_new.reshape(n_diff, B, S, DIFF_HEADS, 2 * DIFF_HEAD_DIM)
    new_fox_k_p = jnp.transpose(fk_new.reshape(n_fox, B, FOX_HEADS, FOX_HEAD_DIM, S), (0, 1, 4, 2, 3))
    new_fox_v_p = jnp.transpose(fv_new.reshape(n_fox, B, FOX_HEADS, FOX_HEAD_DIM, S), (0, 1, 4, 2, 3))
    return (y_prompt, y_sample, new_diff_k_p, new_diff_v_p, jnp.stack(cv_p), new_fox_k_p,
            new_fox_v_p, jnp.stack(fl_p), jnp.stack(dk_s), jnp.stack(dv_s), jnp.stack(cv_s),
            jnp.stack(fk_s), jnp.stack(fv_s), jnp.stack(fl_s))
```

```python
import functools
import math

import numpy as np
import jax
import jax.numpy as jnp
from jax import lax
from jax.experimental import pallas as pl
from jax.experimental.pallas import tpu as pltpu

F32 = jnp.float32
BF16 = jnp.bfloat16

D_MODEL = 1024
CHUNK = 64
EPS = 1e-6
DIFF_HEADS = 8
DIFF_HEAD_DIM = 64
LOG2E = math.log2(math.e)
DIFF_SCALE = DIFF_HEAD_DIM ** -0.5
CONV_WIDTH = 31
FOX_HEADS = 16
FOX_HEAD_DIM = 64
FOX_SCALE = FOX_HEAD_DIM ** -0.5
REL_BUCKETS = 32
N_MIXERS = 3

LANES = 128
SUBLANES = 8
NEG = -1e30
ROW_TILE = 512
ATT_TQ = 512
ATT_STRIP = 512
ATT_GROUPS = 2
REDUCE_ROWS = 64
SAMPLE_TK = 2048
SAMPLE_GROUP = 256
FF_CHUNK = 1024
CUM_BLOCK = 256
DECAY_COLS = 6
HALO = 32
CONV_RC = 32
CONV_LC = 256
VMEM_LIMIT = 56 * 1024 * 1024

_T5_STEPS = (12, 16, 23, 32, 46, 64, 91)
T5_NEAR = 128
assert _T5_STEPS[-1] <= T5_NEAR


def _cparams(n_axes):
    return pltpu.CompilerParams(dimension_semantics=("arbitrary",) * n_axes, vmem_limit_bytes=VMEM_LIMIT)


def _rms(x, g):
    return x * lax.rsqrt(jnp.mean(x * x, axis=-1, keepdims=True) + EPS) * g


def _full_spec(shape):
    nd = len(shape)
    return pl.BlockSpec(shape, lambda *_: (0,) * nd)


def _split_specs(n_prompt_tiles, tm, d):
    prompt = pl.BlockSpec((tm, d), lambda i: (jnp.minimum(i, n_prompt_tiles - 1), 0))
    sample = pl.BlockSpec((tm, d), lambda i: (jnp.maximum(i - n_prompt_tiles, 0), 0))
    return prompt, sample


def _row_tile(n_prompt_tiles, prompt_ref, sample_ref):
    return jnp.where(pl.program_id(0) < n_prompt_tiles, prompt_ref[...], sample_ref[...])


def _split_store(n_prompt_tiles, value, prompt_ref, sample_ref):
    i = pl.program_id(0)

    @pl.when(i < n_prompt_tiles)
    def _():
        prompt_ref[...] = value

    @pl.when(i >= n_prompt_tiles)
    def _():
        sample_ref[...] = value


def _store_prompt_cache(value, prompt_ref, layer, kind):
    layer = layer if prompt_ref.shape[0] > 1 else 0
    for other in range(prompt_ref.shape[0]):
        if other != layer:
            prompt_ref[other] = jnp.zeros(prompt_ref.shape[1:], prompt_ref.dtype)
    if kind == "dims":
        prompt_ref[layer, 0] = value.T
    else:
        prompt_ref[layer] = pltpu.einshape("m(hd)->mhd", value, h=prompt_ref.shape[2])


def _qkv_inproj_kernel(*refs, n_pt, scale, n_prev, forget, layers, kinds):
    xp_ref, xs_ref, g_ref, w_ref = refs[:4]
    refs = refs[4:]
    if forget:
        wf_ref, bf_ref = refs[:2]
        refs = refs[2:]
    refs = refs[n_prev:]
    q_ref, kp_ref, ks_ref, vp_ref, vs_ref, kb_ref, vb_ref = refs[:7]
    d = D_MODEL

    def project(prompt):
        xn = _rms((xp_ref if prompt else xs_ref)[...], g_ref[...]).astype(BF16)
        q = jnp.dot(xn, w_ref[:, 0:d], preferred_element_type=F32)
        q_ref[...] = (q * (scale * LOG2E)).astype(BF16)
        for c, new_p, new_s, bf_ref_out in ((1, kp_ref, ks_ref, kb_ref), (2, vp_ref, vs_ref, vb_ref)):
            val = jnp.dot(xn, w_ref[:, c * d:(c + 1) * d], preferred_element_type=F32)
            if prompt:
                _store_prompt_cache(val, new_p, layers[c - 1], kinds[c - 1])
            else:
                new_s[...] = val
            bf_ref_out[...] = val.astype(BF16)
        if forget:
            z = jnp.dot(xn, wf_ref[...], preferred_element_type=F32) + bf_ref[...]
            lf = jnp.minimum(z, 0.0) - jnp.log1p(jnp.exp(-jnp.abs(z)))
            refs[7][...] = lf[:, 0:LANES]
            refs[8][...] = lf[:, LANES:LANES + FOX_HEADS]

    pl.when(pl.program_id(0) < n_pt)(lambda: project(True))
    pl.when(pl.program_id(0) >= n_pt)(lambda: project(False))


def _qkv_inproj(x, g, w, scale, k_slab, v_slab, prev=None, forget=()):
    n_p, d = x[0].shape
    n = n_p + x[1].shape[0]
    tm = ROW_TILE
    n_pt = n_p // tm
    row = pl.BlockSpec((tm, d), lambda i: (i, 0))
    prompt, sample = _split_specs(n_pt, tm, d)
    prev = prev or {}

    def slab(out_index, shape, layer, kind):
        n_layers, first = (1, layer) if out_index in prev else (shape[0], 0)
        if kind == "dims":
            per_seq = shape[3] // tm
            return pl.BlockSpec((n_layers, 1, d, tm), lambda i: (first, jnp.minimum(i, n_pt - 1) // per_seq, 0,
                                                                 jnp.minimum(i, n_pt - 1) % per_seq))
        return pl.BlockSpec((n_layers, tm) + tuple(shape[2:]), lambda i: (first, jnp.minimum(i, n_pt - 1), 0, 0))

    f32_s = jax.ShapeDtypeStruct((n - n_p, d), F32)
    bf_all = jax.ShapeDtypeStruct((n, d), BF16)
    ins = [*x, g, w] + list(forget) + list(prev.values())
    w_spec = pl.BlockSpec(w.shape, lambda i: (0, 0), pipeline_mode=pl.Buffered(1))
    in_specs = ([prompt, sample, _full_spec((1, d)), w_spec] + [_full_spec(a.shape) for a in forget]
                + [pl.BlockSpec(memory_space=pl.ANY)] * len(prev))
    out_specs = [row, slab(1, *k_slab), sample, slab(3, *v_slab), sample, row, row]
    out_shape = [bf_all, jax.ShapeDtypeStruct(k_slab[0], F32), f32_s, jax.ShapeDtypeStruct(v_slab[0], F32), f32_s,
                 bf_all, bf_all]
    if forget:
        out_specs += [pl.BlockSpec((tm, LANES), lambda i: (i, 0)), pl.BlockSpec((tm, FOX_HEADS), lambda i: (i, 0))]
        out_shape += [jax.ShapeDtypeStruct((n, LANES), F32), jax.ShapeDtypeStruct((n, FOX_HEADS), F32)]
    first_prev = 4 + len(forget)
    aliases = {first_prev + pos: out_index for pos, out_index in enumerate(prev)}
    return pl.pallas_call(
        functools.partial(_qkv_inproj_kernel, n_pt=n_pt, scale=scale, n_prev=len(prev), forget=bool(forget),
                          layers=(k_slab[1], v_slab[1]), kinds=(k_slab[2], v_slab[2])),
        grid=(n // tm,),
        in_specs=in_specs,
        out_specs=out_specs,
        out_shape=out_shape,
        input_output_aliases=aliases,
        compiler_params=_cparams(1),
        name="fox_inproj" if forget else "diff_inproj",
    )(*ins)


def _conv_inproj_kernel(xp_ref, xs_ref, g_ref, w_ref, b_ref, u_ref, *, n_pt):
    d = D_MODEL
    xn = _rms(_row_tile(n_pt, xp_ref, xs_ref), g_ref[...]).astype(BF16)
    a = jnp.dot(xn, w_ref[:, 0:d], preferred_element_type=F32) + b_ref[:, 0:d]
    gate = jnp.dot(xn, w_ref[:, d:2 * d], preferred_element_type=F32) + b_ref[:, d:2 * d]
    u_ref[...] = a * (1.0 / (1.0 + jnp.exp(-gate)))


def _conv_inproj(x, g, w, b):
    (n_p, d), n = x[0].shape, x[0].shape[0] + x[1].shape[0]
    tm = ROW_TILE
    row = pl.BlockSpec((tm, d), lambda i: (i, 0))
    return pl.pallas_call(
        functools.partial(_conv_inproj_kernel, n_pt=n_p // tm),
        grid=(n // tm,),
        in_specs=[*_split_specs(n_p // tm, tm, d), _full_spec((1, d)), _full_spec(w.shape), _full_spec(b.shape)],
        out_specs=row,
        out_shape=jax.ShapeDtypeStruct((n, d), F32),
        compiler_params=_cparams(1),
        name="conv_inproj",
    )(*x, g, w, b)


def _mix_mlp_kernel(xp_ref, xs_ref, op_ref, os_ref, wo_ref, bo_ref, g_ref, w1_ref, w2_ref, gf_ref, yp_ref, ys_ref, *,
                    n_pt, final):
    def rows(x_ref, o_ref, y_ref):
        x1 = x_ref[...] + jnp.dot(o_ref[...], wo_ref[...], preferred_element_type=F32) + bo_ref[...]
        xn = _rms(x1, g_ref[...]).astype(BF16)
        acc = x1
        d_ff = w1_ref.shape[1]
        for c in range(d_ff // FF_CHUNK):
            h = jnp.dot(xn, w1_ref[:, c * FF_CHUNK:(c + 1) * FF_CHUNK], preferred_element_type=F32)
            h = jnp.square(jnp.maximum(h, 0.0)).astype(BF16)
            acc = acc + jnp.dot(h, w2_ref[c * FF_CHUNK:(c + 1) * FF_CHUNK, :], preferred_element_type=F32)
        y_ref[...] = _rms(acc, gf_ref[...]) if final else acc

    pl.when(pl.program_id(0) < n_pt)(lambda: rows(xp_ref, op_ref, yp_ref))
    pl.when(pl.program_id(0) >= n_pt)(lambda: rows(xs_ref, os_ref, ys_ref))


def _mix_mlp(x, o_p, o_s, wo, bo, g, w1, w2, gf, final):
    n_p, d = x[0].shape
    n = n_p + x[1].shape[0]
    tm = ROW_TILE
    n_pt = n_p // tm
    prompt, sample = _split_specs(n_pt, tm, d)
    resident = lambda a: pl.BlockSpec(a.shape, lambda i: (0,) * a.ndim, pipeline_mode=pl.Buffered(1))
    return pl.pallas_call(
        functools.partial(_mix_mlp_kernel, n_pt=n_pt, final=final),
        grid=(n // tm,),
        in_specs=[prompt, sample, prompt, sample, resident(wo), _full_spec(bo.shape), _full_spec(g.shape),
                  resident(w1), resident(w2), _full_spec(gf.shape)],
        out_specs=[prompt, sample],
        out_shape=[jax.ShapeDtypeStruct((n_p, d), F32), jax.ShapeDtypeStruct((n - n_p, d), F32)],
        compiler_params=_cparams(1),
        name="mix_mlp",
    )(*x, o_p, o_s, wo, bo, g, w1, w2, gf)


def _t5_bias(tab_ref, h, qpos, kpos):
    half = REL_BUCKETS // 2
    n = qpos - kpos
    off = jnp.where(n < 0, half, 0)
    n = jnp.abs(n)
    large = jnp.full(n.shape, half // 2, jnp.int32)
    for t in _T5_STEPS:
        large = large + jnp.where(n >= t, 1, 0)
    bucket = off + jnp.where(n < half // 2, n, large)
    far = tab_ref[half - 1, h]
    out = jnp.zeros(n.shape, F32)
    for b in range(REL_BUCKETS):
        out = jnp.where(bucket == b, (tab_ref[b, h] - far) * LOG2E, out)
    visible = (kpos // CHUNK) <= (qpos // CHUNK)
    return jnp.where(visible, out, NEG)


def _prompt_tiles_kernel(tab_ref, bt_ref, cm_ref, *, tq):
    h = pl.program_id(0)
    for v in range(2):
        kpos = lax.broadcasted_iota(jnp.int32, (tq, tq), 0)
        qpos = lax.broadcasted_iota(jnp.int32, (tq, tq), 1) + (1 - v) * tq
        bt_ref[0, v] = _t5_bias(tab_ref, h, qpos, kpos)

    @pl.when(h == 0)
    def _():
        kpos = lax.broadcasted_iota(jnp.int32, (tq, tq), 0)
        qpos = lax.broadcasted_iota(jnp.int32, (tq, tq), 1)
        cm_ref[0, 0] = jnp.zeros((tq, tq), F32)
        cm_ref[0, 1] = jnp.where(kpos <= qpos, 0.0, NEG)


def _prompt_tiles(table, tq):
    return pl.pallas_call(
        functools.partial(_prompt_tiles_kernel, tq=tq),
        grid=(DIFF_HEADS,),
        in_specs=[pl.BlockSpec(memory_space=pltpu.SMEM)],
        out_specs=[pl.BlockSpec((1, 2, tq, tq), lambda h: (h, 0, 0, 0)),
                   pl.BlockSpec((1, 2, tq, tq), lambda h: (0, 0, 0, 0))],
        out_shape=[jax.ShapeDtypeStruct((DIFF_HEADS, 2, tq, tq), F32), jax.ShapeDtypeStruct((1, 2, tq, tq), F32)],
        compiler_params=_cparams(1),
        name="prompt_score_tiles",
    )(table)


def _sample_tiles_kernel(tab_ref, dl_ref, dn_ref, fn_ref, *, t_new, past, tk):
    rows = DIFF_HEADS * 2 * t_new
    for h in range(DIFF_HEADS):
        r0 = h * 2 * t_new
        q = lax.broadcasted_iota(jnp.int32, (2 * t_new, tk), 0) % t_new + past
        k = lax.broadcasted_iota(jnp.int32, (2 * t_new, tk), 1) + (past - tk)
        dl_ref[r0:r0 + 2 * t_new, :] = _t5_bias(tab_ref, h, q, k)
        q = lax.broadcasted_iota(jnp.int32, (2 * t_new, t_new), 0) % t_new + past
        k = lax.broadcasted_iota(jnp.int32, (2 * t_new, t_new), 1) + past
        dn_ref[r0:r0 + 2 * t_new, :] = _t5_bias(tab_ref, h, q, k)
    q = lax.broadcasted_iota(jnp.int32, (rows, t_new), 0) % t_new
    k = lax.broadcasted_iota(jnp.int32, (rows, t_new), 1)
    fn_ref[...] = jnp.where(k <= q, 0.0, NEG)


def _sample_tiles(table, t_new, past, tk):
    rows = DIFF_HEADS * 2 * t_new
    assert rows == FOX_HEADS * t_new
    return pl.pallas_call(
        functools.partial(_sample_tiles_kernel, t_new=t_new, past=past, tk=tk),
        in_specs=[pl.BlockSpec(memory_space=pltpu.SMEM)],
        out_shape=[jax.ShapeDtypeStruct((rows, tk), F32), jax.ShapeDtypeStruct((rows, t_new), F32),
                   jax.ShapeDtypeStruct((rows, t_new), F32)],
        name="sample_score_tiles",
    )(table)


def _lambda(lqk_ref, lam_init):
    a = jnp.sum(lqk_ref[0:1, :] * lqk_ref[1:2, :], axis=-1, keepdims=True)
    b = jnp.sum(lqk_ref[2:3, :] * lqk_ref[3:4, :], axis=-1, keepdims=True)
    return jnp.exp(a) - jnp.exp(b) + lam_init


def _scores_t(k, q2_ref, s_out):
    for c0 in range(0, q2_ref.shape[0], ATT_STRIP):
        cols = slice(c0, c0 + ATT_STRIP)
        s_out[:, cols] = lax.dot_general(k, q2_ref[cols, :], (((1,), (1,)), ((), ())), preferred_element_type=F32)


def _reduce_rows(op, x):
    w, n = x.shape
    part = op(x.reshape(w // REDUCE_ROWS, REDUCE_ROWS, n), axis=0)
    return op(part, axis=0, keepdims=True)


def _softmax_pv_t(s_in, v, bias_t, bias_rows, m_ref, l_ref, acc_ref):
    w = s_in.shape[0]
    for c0 in range(0, s_in.shape[1], ATT_STRIP):
        cols = slice(c0, c0 + ATT_STRIP)
        s = s_in[:, cols]
        if bias_t is not None:
            b0 = c0 % bias_t.shape[1]
            r0 = w - bias_rows
            biased = s[r0:, :] + bias_t[r0:, b0:b0 + ATT_STRIP]
            s = biased if r0 == 0 else jnp.concatenate([s[:r0, :], biased], axis=0)
        m_prev = m_ref[:, cols]
        m_new = jnp.maximum(m_prev, _reduce_rows(jnp.max, s))
        alpha = jnp.exp2(m_prev - m_new)
        p = jnp.exp2(s - m_new)
        l_ref[:, cols] = alpha * l_ref[:, cols] + _reduce_rows(jnp.sum, p)
        pv = lax.dot_general(v, p.astype(BF16), (((0,), (0,)), ((), ())), preferred_element_type=F32)
        acc_ref[:, cols] = alpha * acc_ref[:, cols] + pv
        m_ref[:, cols] = m_new


def _prompt_attn_kernel(*refs, fox, tq, seq, lam_init):
    if fox:
        (q_ref, qa_ref, k_ref, ka_ref, v_ref, bt_ref, o_ref, q2_ref, s0_ref, s1_ref, m_ref, l_ref, acc_ref) = refs
    else:
        (q_ref, k_ref, v_ref, bt_ref, lqk_ref, sg_ref, o_ref, q2_ref, s0_ref, s1_ref, m_ref, l_ref, acc_ref) = refs
    s_refs = (s0_ref, s1_ref)
    n_groups = q_ref.shape[1] // LANES
    lane = lax.broadcasted_iota(jnp.int32, (tq, LANES), 1)
    lo = lane < LANES // 2
    first_rows = lax.broadcasted_iota(jnp.int32, (LANES, tq), 0) < LANES // 2
    lanes = [slice(g * LANES, (g + 1) * LANES) for g in range(n_groups)]
    if fox:
        in_a, in_b = [], []
        for g in range(n_groups):
            c0 = 12 * (pl.program_id(1) * n_groups + g)
            in_a.append(jnp.logical_and(lane >= c0, lane < c0 + 6))
            in_b.append(jnp.logical_and(lane >= c0 + 6, lane < c0 + 12))

    def scores_g(g, j, slot):
        k0 = pl.multiple_of(jnp.maximum(j, 0) * tq, tq)
        k = k_ref[pl.ds(k0, tq), lanes[g]]
        if fox:
            k = jnp.concatenate([k, ka_ref[pl.ds(k0, tq), :]], axis=1)
        _scores_t(k, q2_ref.at[g], s_refs[slot].at[g])

    def softmax_pv_g(g, j, slot, bias_index):
        k0 = pl.multiple_of(j * tq, tq)
        if bias_index is None or (fox and bias_index == 0):
            bias_t, bias_rows = None, 0
        else:
            bias_t = bt_ref.at[0 if fox else g, bias_index]
            bias_rows = tq if bias_index == 1 else T5_NEAR
        _softmax_pv_t(s_refs[slot].at[g], v_ref[pl.ds(k0, tq), lanes[g]], bias_t, bias_rows, m_ref.at[g],
                      l_ref.at[g], acc_ref.at[g])

    def scores(j, slot):
        for g in range(n_groups):
            scores_g(g, j, slot)

    def softmax_pv(j, slot, bias_index):
        for g in range(n_groups):
            softmax_pv_g(g, j, slot, bias_index)

    def step(j_next, slot_next, j, slot, bias_index):
        for g in range(n_groups):
            scores_g(g, j_next, slot_next)
            softmax_pv_g(g, j, slot, bias_index)

    def q_block(i, carry):
        r0 = pl.multiple_of(i * tq, tq)
        for g in range(n_groups):
            q = q_ref[pl.ds(r0, tq), lanes[g]]
            zero = jnp.zeros_like(q)
            q2_ref[g, 0:tq, 0:LANES] = jnp.where(lo, q, zero)
            q2_ref[g, tq:2 * tq, 0:LANES] = jnp.where(lo, zero, q)
            if fox:
                qa = qa_ref[pl.ds(r0, tq), :]
                q2_ref[g, 0:tq, LANES:2 * LANES] = jnp.where(in_a[g], qa, zero)
                q2_ref[g, tq:2 * tq, LANES:2 * LANES] = jnp.where(in_b[g], qa, zero)
        m_ref[...] = jnp.full(m_ref.shape, NEG, F32)
        l_ref[...] = jnp.zeros(l_ref.shape, F32)
        acc_ref[...] = jnp.zeros(acc_ref.shape, F32)

        @pl.when(i == 0)
        def _():
            scores(i, 0)
            softmax_pv(i, 0, 1)

        @pl.when(i >= 1)
        def _():
            scores(i, 0)
            step(i - 1, 1, i, 0, 1)
            step(i - 2, 0, i - 1, 1, 0)
            n_far = i - 1

            def pair(u, c):
                a = i - 2 - 2 * u
                step(a - 1, 1, a, 0, None)
                step(a - 2, 0, a - 1, 1, None)
                return c

            lax.fori_loop(0, n_far // 2, pair, 0)

            @pl.when(n_far % 2 == 1)
            def _():
                softmax_pv(0, 0, None)

        for g in range(n_groups):
            top = acc_ref[g, :, 0:tq] / l_ref[g, :, 0:tq]
            bot = acc_ref[g, :, tq:2 * tq] / l_ref[g, :, tq:2 * tq]
            if fox:
                o_t = jnp.where(first_rows, top, bot)
            else:
                o_t = top - _lambda(lqk_ref, lam_init) * bot
                ms = jnp.mean(o_t * o_t, axis=0, keepdims=True)
                o_t = o_t * lax.rsqrt(ms + EPS) * sg_ref[...] * (1.0 - lam_init)
            o_ref[pl.ds(r0, tq), lanes[g]] = o_t.T.astype(BF16)
        return carry

    lax.fori_loop(0, seq // tq, q_block, 0)


def _prompt_attn(fox, batch, seq, q, k, v, bt, extra, lam_init=0.0):
    tq = ATT_TQ
    ng = ATT_GROUPS
    steps = D_MODEL // (LANES * ng)
    blk = pl.BlockSpec((seq, ng * LANES), lambda b, h: (b, h))
    if fox:
        qa, ka = extra
        aug = pl.BlockSpec((seq, LANES), lambda b, h: (b, 0))
        ins = [q, qa, k, ka, v, bt]
        in_specs = [blk, aug, blk, aug, blk, pl.BlockSpec((1, 2, tq, tq), lambda b, h: (0, 0, 0, 0))]
        kd = 2 * LANES
    else:
        lqk, sg = extra
        ins = [q, k, v, bt, lqk, sg]
        in_specs = [blk, blk, blk, pl.BlockSpec((ng, 2, tq, tq), lambda b, h: (h, 0, 0, 0)),
                    _full_spec(lqk.shape), _full_spec(sg.shape)]
        kd = LANES
    return pl.pallas_call(
        functools.partial(_prompt_attn_kernel, fox=fox, tq=tq, seq=seq, lam_init=lam_init),
        grid=(batch, steps),
        in_specs=in_specs,
        out_specs=blk,
        out_shape=jax.ShapeDtypeStruct((batch * seq, D_MODEL), BF16),
        scratch_shapes=[pltpu.VMEM((ng, 2 * tq, kd), BF16), pltpu.VMEM((ng, tq, 2 * tq), F32),
                        pltpu.VMEM((ng, tq, 2 * tq), F32), pltpu.VMEM((ng, 1, 2 * tq), F32),
                        pltpu.VMEM((ng, 1, 2 * tq), F32), pltpu.VMEM((ng, LANES, 2 * tq), F32)],
        compiler_params=_cparams(2),
        name="fox_prompt_attn" if fox else "diff_prompt_attn",
    )(*ins)


def _online_update(s, pv_fn, m_ref, l_ref, acc_ref):
    m_prev = m_ref[...]
    m_new = jnp.maximum(m_prev, jnp.max(s, axis=-1, keepdims=True))
    alpha = jnp.exp2(m_prev - m_new)
    p = jnp.exp2(s - m_new)
    l_ref[...] = alpha * l_ref[...] + jnp.sum(p, axis=-1, keepdims=True)
    acc_ref[...] = alpha * acc_ref[...] + pv_fn(p.astype(BF16))
    m_ref[...] = m_new


def _sample_attn_kernel(*refs, fox, t_new, tk, n_tiles, lam_init):
    if fox:
        (q_ref, qa_ref, kt_ref, kat_ref, vt_ref, kn_ref, kan_ref, vn_ref, bn_ref,
         o_ref, qbd_ref, m_ref, l_ref, acc_ref) = refs
    else:
        (q_ref, kt_ref, v_ref, kn_ref, vn_ref, bl_ref, bn_ref, lqk_ref, sg_ref,
         o_ref, qbd_ref, m_ref, l_ref, acc_ref) = refs
    t = pl.program_id(1)
    n_blocks = D_MODEL // 64
    rows_per_head = 2 * t_new

    @pl.when(t == 0)
    def _():
        q = q_ref[...]
        zero = jnp.zeros_like(q)
        cb = lax.broadcasted_iota(jnp.int32, q.shape, 1) // 64
        if fox:
            qa = qa_ref[0]
            la = lax.broadcasted_iota(jnp.int32, qa.shape, 1)
        for rb in range(n_blocks):
            qbd_ref[rb * t_new:(rb + 1) * t_new, 0:D_MODEL] = jnp.where(cb == rb, q, zero)
            if fox:
                sel = jnp.logical_and(la >= 6 * rb, la < 6 * rb + 6)
                qbd_ref[rb * t_new:(rb + 1) * t_new, D_MODEL:D_MODEL + LANES] = jnp.where(sel, qa, jnp.zeros_like(qa))
        m_ref[...] = jnp.full(m_ref.shape, NEG, F32)
        l_ref[...] = jnp.zeros(l_ref.shape, F32)
        acc_ref[...] = jnp.zeros(acc_ref.shape, F32)

    n_groups = D_MODEL // SAMPLE_GROUP
    group_rows = (SAMPLE_GROUP // 64) * t_new
    groups = [(slice(g * group_rows, (g + 1) * group_rows), slice(g * SAMPLE_GROUP, (g + 1) * SAMPLE_GROUP))
              for g in range(n_groups)]

    def cache_scores():
        s = jnp.concatenate([jnp.dot(qbd_ref[rows, dims], kt_ref[0, 0, dims, :].astype(BF16),
                                     preferred_element_type=F32) for rows, dims in groups], axis=0)
        if fox:
            s = s + jnp.dot(qbd_ref[:, D_MODEL:D_MODEL + LANES], kat_ref[0], preferred_element_type=F32)
        return s

    if fox:
        def cache_pv(p):
            return jnp.concatenate(
                [lax.dot_general(p[rows, :], vt_ref[0, 0, dims, :].astype(BF16), (((1,), (1,)), ((), ())),
                                 preferred_element_type=F32) for rows, dims in groups], axis=0)

        def new_pv(p):
            return jnp.concatenate([jnp.dot(p[rows, :], vn_ref[:, dims], preferred_element_type=F32)
                                    for rows, dims in groups], axis=0)
    else:
        def cache_pv(p):
            return jnp.concatenate(
                [jnp.dot(p[h * rows_per_head:(h + 1) * rows_per_head, :],
                         v_ref[0, 0, pl.ds(h, tk, stride=DIFF_HEADS), :].astype(BF16), preferred_element_type=F32)
                 for h in range(DIFF_HEADS)], axis=0)

        def new_pv(p):
            return jnp.concatenate(
                [jnp.dot(p[h * rows_per_head:(h + 1) * rows_per_head, :], vn_ref[:, h * LANES:(h + 1) * LANES],
                         preferred_element_type=F32) for h in range(DIFF_HEADS)], axis=0)

    @pl.when(t < n_tiles - 1)
    def _():
        _online_update(cache_scores(), cache_pv, m_ref, l_ref, acc_ref)

    @pl.when(t == n_tiles - 1)
    def _():
        s = cache_scores()
        if not fox:
            s = s + bl_ref[...]
        _online_update(s, cache_pv, m_ref, l_ref, acc_ref)
        kn = kn_ref[...]
        if fox:
            kn = jnp.concatenate([kn, kan_ref[0]], axis=1)
        s = lax.dot_general(qbd_ref[...], kn, (((1,), (1,)), ((), ())), preferred_element_type=F32) + bn_ref[...]
        _online_update(s, new_pv, m_ref, l_ref, acc_ref)

        lane = lax.broadcasted_iota(jnp.int32, (t_new, LANES), 1)
        lo = lane < LANES // 2
        for g in range(D_MODEL // LANES):
            ra, rb = 2 * g * t_new, (2 * g + 1) * t_new
            cols = slice(g * LANES, (g + 1) * LANES)
            c0 = (g * LANES) % SAMPLE_GROUP if fox else 0
            acc_cols = slice(c0, c0 + LANES)
            top = acc_ref[ra:ra + t_new, acc_cols] / l_ref[ra:ra + t_new, :]
            bot = acc_ref[rb:rb + t_new, acc_cols] / l_ref[rb:rb + t_new, :]
            if fox:
                o = jnp.where(lo, top, bot)
            else:
                o = _rms(top - _lambda(lqk_ref, lam_init) * bot, sg_ref[...]) * (1.0 - lam_init)
            o_ref[:, cols] = o.astype(BF16)


def _sample_attn(fox, layer, n_prompt, dec_batch, t_new, past, q, kt, v, kn, vn, bias_last, bias_new, extra,
                 lam_init=0.0):
    tk = SAMPLE_TK
    n_tiles = past // tk
    d = D_MODEL
    rows = (d // 64) * t_new
    s0 = n_prompt // t_new
    new = pl.BlockSpec((t_new, d), lambda b, t: (s0 + b, 0))
    dims_keys = pl.BlockSpec((1, 1, d, tk), lambda b, t: (layer, b, 0, t))
    out = pl.BlockSpec((t_new, d), lambda b, t: (b, 0))
    if fox:
        qa, ka, kat = extra
        new_aug = pl.BlockSpec((1, t_new, LANES), lambda b, t: (b, past // t_new, 0))
        ins = [q, qa, kt, kat, v, kn, ka, vn, bias_new]
        in_specs = [new, new_aug, dims_keys, pl.BlockSpec((1, LANES, tk), lambda b, t: (b, 0, t)), dims_keys,
                    new, new_aug, new, _full_spec(bias_new.shape)]
        kd, acc_w = d + LANES, SAMPLE_GROUP
    else:
        lqk, sg = extra
        ins = [q, kt, v, kn, vn, bias_last, bias_new, lqk, sg]
        in_specs = [new, dims_keys, pl.BlockSpec((1, 1, tk * DIFF_HEADS, LANES), lambda b, t: (layer, b, t, 0)),
                    new, new, _full_spec(bias_last.shape), _full_spec(bias_new.shape),
                    _full_spec(lqk.shape), _full_spec(sg.shape)]
        kd, acc_w = d, LANES
    return pl.pallas_call(
        functools.partial(_sample_attn_kernel, fox=fox, t_new=t_new, tk=tk, n_tiles=n_tiles, lam_init=lam_init),
        grid=(dec_batch, n_tiles),
        in_specs=in_specs,
        out_specs=out,
        out_shape=jax.ShapeDtypeStruct((dec_batch * t_new, d), BF16),
        scratch_shapes=[pltpu.VMEM((rows, kd), BF16), pltpu.VMEM((rows, 1), F32),
                        pltpu.VMEM((rows, 1), F32), pltpu.VMEM((rows, acc_w), F32)],
        compiler_params=_cparams(2),
        name="fox_sample_attn" if fox else "diff_sample_attn",
    )(*ins)


def _split3(c):
    hi = c.astype(BF16)
    r = c - hi.astype(F32)
    mid = r.astype(BF16)
    lo = (r - mid.astype(F32)).astype(BF16)
    return hi, mid, lo


def _decay_cols_kernel(lf_ref, part_ref, qa_ref, ka_ref, *maybe_kat_ref, length):
    part = part_ref[0:1, :]
    which = part_ref[1:2, :]
    carry = jnp.zeros((1, LANES), F32)
    for r in range(0, length, CUM_BLOCK):
        n = min(CUM_BLOCK, length - r)
        tri = (lax.broadcasted_iota(jnp.int32, (n, n), 1) <= lax.broadcasted_iota(jnp.int32, (n, n), 0))
        tri = jnp.where(tri, 1.0, 0.0).astype(BF16)
        c = carry
        for p in _split3(lf_ref[0, r:r + n, :]):
            c = c + jnp.dot(tri, p, preferred_element_type=F32)
        carry = c[n - 1:n, :]
        hi, mid, lo = (t.astype(F32) for t in _split3(c * LOG2E))
        terms = jnp.where(which == 0, hi, jnp.where(which == 1, mid, lo))
        qa = jnp.where(part < 0, 0.0, jnp.where(part < 3, terms, 1.0))
        ka = jnp.where(part < 0, 0.0, jnp.where(part < 3, 1.0, -terms))
        qa_ref[0, r:r + n, :] = qa.astype(BF16)
        ka_ref[0, r:r + n, :] = ka.astype(BF16)
        if maybe_kat_ref:
            maybe_kat_ref[0][0, :, r:r + n] = ka.T.astype(BF16)


def _decay_layout():
    part = np.full((2, LANES), -1, np.int32)
    used = np.arange(FOX_HEADS * DECAY_COLS) % DECAY_COLS
    part[0, :used.size] = used
    part[1, :used.size] = used % 3
    return jnp.asarray(part)


def _spread_heads(a):
    rep = jnp.repeat(a, DECAY_COLS, axis=-1)
    return jnp.pad(rep, [(0, 0)] * (a.ndim - 1) + [(0, LANES - rep.shape[-1])])


def _decay_cols(lf, keys_transposed):
    nb, length, _ = lf.shape
    part = _decay_layout()
    spec = pl.BlockSpec((1, length, LANES), lambda b: (b, 0, 0))
    out_specs = [spec, spec]
    out_shape = [jax.ShapeDtypeStruct((nb, length, LANES), BF16)] * 2
    if keys_transposed:
        out_specs.append(pl.BlockSpec((1, LANES, length), lambda b: (b, 0, 0)))
        out_shape.append(jax.ShapeDtypeStruct((nb, LANES, length), BF16))
    return pl.pallas_call(
        functools.partial(_decay_cols_kernel, length=length),
        grid=(nb,),
        in_specs=[spec, _full_spec(part.shape)],
        out_specs=out_specs,
        out_shape=out_shape,
        compiler_params=_cparams(1),
        name="fox_decay_cols",
    )(lf, part)


def _conv_dw_kernel(u_ref, halo_ref, wdw_ref, bdw_ref, g_ref, b_ref, o_ref, ext_ref, sh_ref, y_ref, *, tm,
                    steps_per_seq):
    halo = halo_ref[...]
    if steps_per_seq:
        halo = jnp.where(pl.program_id(0) % steps_per_seq == 0, 0.0, halo)
    ext_ref[0:HALO, :] = halo
    ext_ref[HALO:HALO + tm, :] = u_ref[...]
    span = sh_ref.shape[1]
    for s in range(1, SUBLANES):
        sh_ref[s - 1] = ext_ref[s:s + span, :]
    d = u_ref.shape[1]
    first = HALO - (CONV_WIDTH - 1)
    for rc in range(tm // CONV_RC):
        for lc in range(d // CONV_LC):
            cols = slice(lc * CONV_LC, (lc + 1) * CONV_LC)
            acc = jnp.zeros((CONV_RC, CONV_LC), F32)
            for w in range(CONV_WIDTH):
                a, s = divmod(first + w, SUBLANES)
                src = ext_ref if s == 0 else sh_ref.at[s - 1]
                r = rc * CONV_RC + a * SUBLANES
                acc = acc + src[r:r + CONV_RC, cols] * wdw_ref[w:w + 1, cols]
            y_ref[rc * CONV_RC:(rc + 1) * CONV_RC, cols] = acc
    y = y_ref[...] + bdw_ref[...]
    yc = y - jnp.mean(y, axis=-1, keepdims=True)
    yn = yc * lax.rsqrt(jnp.mean(yc * yc, axis=-1, keepdims=True) + EPS) * g_ref[...] + b_ref[...]
    o_ref[...] = (yn * (1.0 / (1.0 + jnp.exp(-yn)))).astype(BF16)


def _conv_dw(u, row0, n_rows, tm, halo, halo_index, steps_per_seq, wdw, bdw, g, b):
    d = u.shape[1]
    b0 = row0 // tm
    return pl.pallas_call(
        functools.partial(_conv_dw_kernel, tm=tm, steps_per_seq=steps_per_seq),
        grid=(n_rows // tm,),
        in_specs=[pl.BlockSpec((tm, d), lambda i: (b0 + i, 0)), pl.BlockSpec((HALO, d), lambda i: (halo_index(i), 0)),
                  _full_spec(wdw.shape), _full_spec(bdw.shape), _full_spec(g.shape), _full_spec(b.shape)],
        out_specs=pl.BlockSpec((tm, d), lambda i: (i, 0)),
        out_shape=jax.ShapeDtypeStruct((n_rows, d), BF16),
        scratch_shapes=[pltpu.VMEM((HALO + tm, d), F32), pltpu.VMEM((SUBLANES - 1, HALO + tm - SUBLANES, d), F32),
                        pltpu.VMEM((tm, d), F32)],
        compiler_params=_cparams(1),
        name="conv_dw",
    )(u, halo, wdw, bdw, g, b)


def _dims_keys(cache):
    nd = cache.ndim
    t = jnp.transpose(cache, (0, 1) + tuple(range(3, nd)) + (2,))
    return t.reshape(cache.shape[0], cache.shape[1], D_MODEL, cache.shape[2])


def kernel(x_prompt, x_sample, cache_diff_k, cache_diff_v, state_conv, cache_fox_k, cache_fox_v, cache_fox_logf, rel_bias, norm_g, final_g, diff_w_in, diff_w_out, diff_lq1, diff_lk1, diff_lq2, diff_lk2, diff_subln_g, conv_w_pw1, conv_b_pw1, conv_w_dw, conv_b_dw, conv_ln_g, conv_ln_b, conv_w_pw2, conv_b_pw2, fox_w_in, fox_b_f, fox_w_out, mlp_w1, mlp_w2):
    B, S, D = x_prompt.shape
    Bd, T, _ = x_sample.shape
    P = cache_diff_k.shape[2]
    depth = norm_g.shape[0]
    n_p, n_s = B * S, Bd * T
    assert D == D_MODEL and S % (2 * ATT_TQ) == 0 and P % SAMPLE_TK == 0 and n_p % ROW_TILE == 0
    assert n_s % ROW_TILE == 0 and T % 8 == 0 and T >= CONV_WIDTH - 1 and S % ROW_TILE == 0

    x = (x_prompt.reshape(n_p, D), x_sample.reshape(n_s, D))
    zero_bias = jnp.zeros((1, D), F32)
    final_gain = final_g.reshape(1, D)

    bt_diff, bt_fox = _prompt_tiles(rel_bias, ATT_TQ)
    sb_last, sb_new, sf_new = _sample_tiles(rel_bias, T, P, SAMPLE_TK)

    diff_kt = _dims_keys(cache_diff_k)
    diff_v = cache_diff_v.reshape(cache_diff_v.shape[0], Bd, P * DIFF_HEADS, 2 * DIFF_HEAD_DIM)
    fox_kt = _dims_keys(cache_fox_k)
    fox_vt = _dims_keys(cache_fox_v)

    n_diff, n_fox = diff_w_in.shape[0], fox_w_in.shape[0]
    dk_shape = (n_diff, B, D, S)
    dv_shape = (n_diff, n_p, DIFF_HEADS, 2 * DIFF_HEAD_DIM)
    fkv_shape = (n_fox, B, D, S)
    dk_new, dv_new, fk_new, fv_new = None, None, None, None
    dk_s, dv_s = [], []
    cv_p, cv_s = [], []
    fl_p, fk_s, fv_s, fl_s = [], [], [], []

    for i in range(depth):
        kind, j = i % N_MIXERS, i // N_MIXERS
        g1 = norm_g[i, 0].reshape(1, D)
        g2 = norm_g[i, 1].reshape(1, D)
        if kind == 0:
            lam_init = 0.8 - 0.6 * math.exp(-0.3 * i)
            q, dk_new, k_s, dv_new, v_s, kb, vb = _qkv_inproj(
                x, g1, diff_w_in[j].astype(BF16), DIFF_SCALE, (dk_shape, j, "dims"), (dv_shape, j, "heads"),
                prev=None if dv_new is None else {1: dk_new, 3: dv_new})
            lqk = jnp.stack([diff_lq1[j], diff_lk1[j], diff_lq2[j], diff_lk2[j]]).astype(F32)
            sg = diff_subln_g[j].reshape(1, 2 * DIFF_HEAD_DIM).astype(F32)
            o_p = _prompt_attn(False, B, S, q, kb, vb, bt_diff, (lqk, sg.reshape(2 * DIFF_HEAD_DIM, 1)), lam_init)
            o_s = _sample_attn(False, j, n_p, Bd, T, P, q, diff_kt, diff_v, kb, vb, sb_last, sb_new, (lqk, sg),
                               lam_init)
            wo, bo = diff_w_out[j].astype(BF16), zero_bias
            dk_s.append(k_s.reshape(Bd, T, DIFF_HEADS, 2, DIFF_HEAD_DIM))
            dv_s.append(v_s.reshape(Bd, T, DIFF_HEADS, 2 * DIFF_HEAD_DIM))
        elif kind == 1:
            u = _conv_inproj(x, g1, conv_w_pw1[j].astype(BF16), conv_b_pw1[j].reshape(1, 2 * D))
            wdw = jnp.pad(conv_w_dw[j], ((0, HALO - CONV_WIDTH), (0, 0)))
            args = (wdw, conv_b_dw[j].reshape(1, D), conv_ln_g[j].reshape(1, D), conv_ln_b[j].reshape(1, D))
            per_tile = ROW_TILE // HALO
            o_p = _conv_dw(u, 0, n_p, ROW_TILE, u, lambda t: jnp.maximum(t * per_tile - 1, 0), S // ROW_TILE, *args)
            state = jnp.pad(state_conv[j], ((0, 0), (HALO - (CONV_WIDTH - 1), 0), (0, 0))).reshape(Bd * HALO, D)
            o_s = _conv_dw(u, n_p, n_s, T, state, lambda t: t, 0, *args)
            wo, bo = conv_w_pw2[j].astype(BF16), conv_b_pw2[j].reshape(1, D)
            keep = CONV_WIDTH - 1
            cv_p.append(jnp.stack([u[(b + 1) * S - keep:(b + 1) * S] for b in range(B)]))
            cv_s.append(u[n_p:].reshape(Bd, T, D)[:, T - keep:])
        else:
            w_in = fox_w_in[j]
            pad_heads = lambda a: jnp.pad(a, [(0, 0)] * (a.ndim - 1) + [(0, LANES - FOX_HEADS)])
            wf = jnp.concatenate([_spread_heads(w_in[:, 3 * D:]), pad_heads(w_in[:, 3 * D:])], axis=1).astype(BF16)
            b_f = fox_b_f[j].astype(F32).reshape(1, FOX_HEADS)
            bf = jnp.concatenate([_spread_heads(b_f), pad_heads(b_f)], axis=1)
            q, fk_new, k_s, fv_new, v_s, kb, vb, lf, lf_heads = _qkv_inproj(
                x, g1, w_in[:, :3 * D].astype(BF16), FOX_SCALE, (fkv_shape, j, "dims"), (fkv_shape, j, "dims"),
                prev=None if fk_new is None else {1: fk_new, 3: fv_new}, forget=(wf, bf))
            lf_p_cols = lf[:n_p].reshape(B, S, LANES)
            lf_s_cols = lf[n_p:].reshape(Bd, T, LANES)
            lf_p = lf_heads[:n_p].reshape(B, S, FOX_HEADS)
            lf_s = lf_heads[n_p:].reshape(Bd, T, FOX_HEADS)
            qa_p, ka_p = _decay_cols(lf_p_cols, False)
            cache_cols = _spread_heads(cache_fox_logf[j].astype(F32))
            decay_s = _decay_cols(jnp.concatenate([cache_cols, lf_s_cols], axis=1), True)
            o_p = _prompt_attn(True, B, S, q, kb, vb, bt_fox, (qa_p.reshape(n_p, LANES), ka_p.reshape(n_p, LANES)))
            o_s = _sample_attn(True, j, n_p, Bd, T, P, q, fox_kt, fox_vt, kb, vb, None, sf_new, decay_s)
            wo, bo = fox_w_out[j].astype(BF16), zero_bias
            fl_p.append(lf_p)
            fk_s.append(k_s.reshape(Bd, T, FOX_HEADS, FOX_HEAD_DIM))
            fv_s.append(v_s.reshape(Bd, T, FOX_HEADS, FOX_HEAD_DIM))
            fl_s.append(lf_s)
        final = i == depth - 1
        x = _mix_mlp(x, o_p, o_s, wo, bo, g2, mlp_w1[i].astype(BF16), mlp_w2[i].astype(BF16), final_gain,
                     final=(i == depth - 1))

    y_prompt = x[0].reshape(B, S, D)
    y_sample = x[1].reshape(Bd, T, D)
    new_diff_k_p = jnp.transpose(dk_new.reshape(n_diff, B, DIFF_HEADS, 2, DIFF_HEAD_DIM, S), (0, 1, 5, 2, 3, 4))
    new_diff_v_p = dv_new.reshape(n_diff, B, S, DIFF_HEADS, 2 * DIFF_HEAD_DIM)
    new_fox_k_p = jnp.transpose(fk_new.reshape(n_fox, B, FOX_HEADS, FOX_HEAD_DIM, S), (0, 1, 4, 2, 3))
    new_fox_v_p = jnp.transpose(fv_new.reshape(n_fox, B, FOX_HEADS, FOX_HEAD_DIM, S), (0, 1, 4, 2, 3))
    return (y_prompt, y_sample, new_diff_k_p, new_diff_v_p, jnp.stack(cv_p), new_fox_k_p,
            new_fox_v_p, jnp.stack(fl_p), jnp.stack(dk_s), jnp.stack(dv_s), jnp.stack(cv_s),
            jnp.stack(fk_s), jnp.stack(fv_s), jnp.stack(fl_s))
```
